```python
import jax, jax.numpy as jnp
from jax import lax
import numpy as np

D_MODEL = 1024
BATCH = 16
SEQ = 4096
DEPTH = 1

PLE_DIM = 256
N_HEADS = 16
N_KV_HEADS = 2
HEAD_DIM = 64
GROUP = N_HEADS // N_KV_HEADS
ATTN_WIDTH = N_HEADS * HEAD_DIM
KV_WIDTH = N_KV_HEADS * HEAD_DIM
CONV_WIDTH = D_MODEL
CONV_KERNEL = 31
WINDOW = 128
BLOCK = 128
ROPE_DIM = HEAD_DIM // 4
ROPE_THETA = 500000.0
N_BRANCH = 2
EPS = 1e-6
MAX_POS_OFFSET = 1024

COL_SIZES = (CONV_WIDTH, CONV_WIDTH, CONV_WIDTH,
             ATTN_WIDTH, KV_WIDTH, KV_WIDTH, ATTN_WIDTH,
             D_MODEL, D_MODEL)
IN_WIDTH = sum(COL_SIZES)

kernel_name = "hybrid_conformer_conv_swa_sink_gated_merge"


def rmsnorm(x, g):
    xf = x.astype(jnp.float32)
    y = xf * lax.rsqrt(jnp.mean(xf * xf, axis=-1, keepdims=True) + EPS)
    return (y * g.astype(jnp.float32)).astype(x.dtype)


def layernorm(x, g, b):
    xf = x.astype(jnp.float32)
    mu = jnp.mean(xf, axis=-1, keepdims=True)
    var = jnp.mean(jnp.square(xf - mu), axis=-1, keepdims=True)
    y = (xf - mu) * lax.rsqrt(var + EPS)
    return (y * g.astype(jnp.float32) + b.astype(jnp.float32)).astype(x.dtype)


def split_columns(z):
    outs, start = [], 0
    for size in COL_SIZES:
        outs.append(z[..., start:start + size])
        start += size
    return outs


def conformer_conv(val, glu_gate, w_dw, b_dw, ln_g, ln_b, w_pw):
    u = val * jax.nn.sigmoid(glu_gate)
    c = lax.conv_general_dilated(
        u, w_dw[:, None, :].astype(u.dtype), window_strides=(1,),
        padding=[(CONV_KERNEL - 1, 0)],
        dimension_numbers=('NWC', 'WIO', 'NWC'),
        feature_group_count=CONV_WIDTH) + b_dw
    c = jax.nn.silu(layernorm(c, ln_g, ln_b))
    return c @ w_pw


def rope_tables(positions, dtype):
    inv = jnp.power(ROPE_THETA, -jnp.arange(0, ROPE_DIM, 2, dtype=jnp.float32) / ROPE_DIM)
    ang = positions.astype(jnp.float32)[..., None] * inv
    return jnp.cos(ang)[:, :, None, :].astype(dtype), jnp.sin(ang)[:, :, None, :].astype(dtype)


def partial_rope(t, cos, sin):
    half = ROPE_DIM // 2
    t1 = t[..., :half]
    t2 = t[..., half:ROPE_DIM]
    return jnp.concatenate([t1 * cos - t2 * sin, t2 * cos + t1 * sin, t[..., ROPE_DIM:]], axis=-1)


def sliding_window_sink_attention(q, k, v, sinks, positions):
    B, S = q.shape[0], q.shape[1]
    nb = S // BLOCK
    q = q.reshape(B, S, N_HEADS, HEAD_DIM)
    k = k.reshape(B, S, N_KV_HEADS, HEAD_DIM)
    v = v.reshape(B, S, N_KV_HEADS, HEAD_DIM)
    cos, sin = rope_tables(positions, q.dtype)
    q = partial_rope(q, cos, sin)
    k = partial_rope(k, cos, sin)
    q = q.reshape(B, nb, BLOCK, N_KV_HEADS, GROUP, HEAD_DIM)
    k = k.reshape(B, nb, BLOCK, N_KV_HEADS, HEAD_DIM)
    v = v.reshape(B, nb, BLOCK, N_KV_HEADS, HEAD_DIM)
    pad = ((0, 0), (1, 0), (0, 0), (0, 0), (0, 0))
    kb = jnp.concatenate([jnp.pad(k[:, :-1], pad), k], axis=2)
    vb = jnp.concatenate([jnp.pad(v[:, :-1], pad), v], axis=2)
    s = jnp.einsum('bnqkgd,bnskd->bnkgqs', q, kb).astype(jnp.float32) * (HEAD_DIM ** -0.5)
    qi = jnp.arange(BLOCK)[:, None]
    sj = jnp.arange(2 * BLOCK)[None, :]
    band = (sj <= qi + BLOCK) & (sj > qi + BLOCK - WINDOW)
    blk = jnp.arange(nb)[:, None, None]
    mask = band[None] & ((blk > 0) | (sj[None] >= BLOCK))
    s = jnp.where(mask[None, :, None, None], s, -jnp.inf)
    sink = jnp.broadcast_to(sinks.astype(jnp.float32).reshape(1, 1, N_KV_HEADS, GROUP, 1, 1),
                            s.shape[:-1] + (1,))
    probs = jax.nn.softmax(jnp.concatenate([s, sink], axis=-1), axis=-1)[..., :-1]
    o = jnp.einsum('bnkgqs,bnskd->bnqkgd', probs.astype(vb.dtype), vb)
    return o.reshape(B, S, ATTN_WIDTH)


def _fwd_setup_inputs(seed: int = 0) -> dict:
    key = jax.random.key(seed)
    ks = jax.random.split(key, 20)
    f32 = jnp.float32
    x = jax.random.normal(ks[0], (BATCH, SEQ, D_MODEL), f32)
    p = jax.random.normal(ks[1], (DEPTH, BATCH, SEQ, PLE_DIM), f32)
    offsets = jax.random.randint(ks[2], (BATCH, 1), 0, MAX_POS_OFFSET, dtype=jnp.int32)
    positions = offsets + jnp.arange(SEQ, dtype=jnp.int32)[None, :]
    w_in = jax.random.normal(ks[3], (DEPTH, D_MODEL, IN_WIDTH), f32) * D_MODEL ** -0.5
    ln_pre = 1.0 + 0.05 * jax.random.normal(ks[4], (DEPTH, D_MODEL), f32)
    ln_post = 1.0 + 0.05 * jax.random.normal(ks[5], (DEPTH, D_MODEL), f32)
    w_dw = jax.random.normal(ks[6], (DEPTH, CONV_KERNEL, CONV_WIDTH), f32) * CONV_KERNEL ** -0.5
    b_dw = 0.02 * jax.random.normal(ks[7], (DEPTH, CONV_WIDTH), f32)
    conv_ln_g = 1.0 + 0.05 * jax.random.normal(ks[8], (DEPTH, CONV_WIDTH), f32)
    conv_ln_b = 0.02 * jax.random.normal(ks[9], (DEPTH, CONV_WIDTH), f32)
    w_pw = jax.random.normal(ks[10], (DEPTH, CONV_WIDTH, CONV_WIDTH), f32) * CONV_WIDTH ** -0.5
    sinks = 0.5 * jax.random.normal(ks[11], (DEPTH, N_HEADS), f32)
    w_br_conv = jax.random.normal(ks[12], (DEPTH, CONV_WIDTH, D_MODEL), f32) * CONV_WIDTH ** -0.5
    w_br_attn = jax.random.normal(ks[13], (DEPTH, ATTN_WIDTH, D_MODEL), f32) * ATTN_WIDTH ** -0.5
    w_out = jax.random.normal(ks[14], (DEPTH, D_MODEL, D_MODEL), f32) * D_MODEL ** -0.5
    w_ple_gate = jax.random.normal(ks[15], (DEPTH, D_MODEL, D_MODEL), f32) * D_MODEL ** -0.5
    w_ple_proj = jax.random.normal(ks[16], (DEPTH, PLE_DIM, D_MODEL), f32) * PLE_DIM ** -0.5
    return {"x": x, "p": p, "positions": positions, "w_in": w_in, "ln_pre": ln_pre,
            "ln_post": ln_post, "w_dw": w_dw, "b_dw": b_dw, "conv_ln_g": conv_ln_g,
            "conv_ln_b": conv_ln_b, "w_pw": w_pw, "sinks": sinks, "w_br_conv": w_br_conv,
            "w_br_attn": w_br_attn, "w_out": w_out, "w_ple_gate": w_ple_gate,
            "w_ple_proj": w_ple_proj}


def _fwd_reference(x, p, positions, w_in, ln_pre, ln_post, w_dw, b_dw, conv_ln_g, conv_ln_b,
              w_pw, sinks, w_br_conv, w_br_attn, w_out, w_ple_gate, w_ple_proj):
    for i in range(DEPTH):
        h = rmsnorm(x, ln_pre[i])
        z = h @ w_in[i]
        (c_val, c_glu, c_gate, q, k, v, a_gate, g_conv, g_attn) = split_columns(z)
        ya = conformer_conv(c_val, c_glu, w_dw[i], b_dw[i], conv_ln_g[i], conv_ln_b[i], w_pw[i])
        ya = (ya * jax.nn.silu(c_gate)) @ w_br_conv[i]
        yb = sliding_window_sink_attention(q, k, v, sinks[i], positions)
        yb = (yb * jax.nn.silu(a_gate)) @ w_br_attn[i]
        m = jax.nn.sigmoid(g_conv) * ya + jax.nn.sigmoid(g_attn) * yb
        x = x + rmsnorm(m @ w_out[i], ln_post[i])
        x = x + jax.nn.sigmoid(x @ w_ple_gate[i]) * (p[i] @ w_ple_proj[i])
    return x


import jax as _jax
import jax.numpy as _jnp

TWIN_FORMAT = 'train_step'
FWD_PARAMS = ['x', 'p', 'positions', 'w_in', 'ln_pre', 'ln_post', 'w_dw', 'b_dw', 'conv_ln_g', 'conv_ln_b', 'w_pw', 'sinks', 'w_br_conv', 'w_br_attn', 'w_out', 'w_ple_gate', 'w_ple_proj']
TWIN_WEIGHTS = ['w_in', 'ln_pre', 'ln_post', 'w_dw', 'b_dw', 'conv_ln_g', 'conv_ln_b', 'w_pw', 'sinks', 'w_br_conv', 'w_br_attn', 'w_out', 'w_ple_gate', 'w_ple_proj']
TWIN_DIFF_INPUT = 'x'
TWIN_INPUTS = ['x', 'p', 'positions', 'w_in', 'ln_pre', 'ln_post', 'w_dw', 'b_dw', 'conv_ln_g', 'conv_ln_b', 'w_pw', 'sinks', 'w_br_conv', 'w_br_attn', 'w_out', 'w_ple_gate', 'w_ple_proj', 'loss_target', 'm_w_in', 'm_ln_pre', 'm_ln_post', 'm_w_dw', 'm_b_dw', 'm_conv_ln_g', 'm_conv_ln_b', 'm_w_pw', 'm_sinks', 'm_w_br_conv', 'm_w_br_attn', 'm_w_out', 'm_w_ple_gate', 'm_w_ple_proj', 'v_w_in', 'v_ln_pre', 'v_ln_post', 'v_w_dw', 'v_b_dw', 'v_conv_ln_g', 'v_conv_ln_b', 'v_w_pw', 'v_sinks', 'v_w_br_conv', 'v_w_br_attn', 'v_w_out', 'v_w_ple_gate', 'v_w_ple_proj']
TWIN_OUTPUTS = ['loss', 'grad_x', 'grad_w_in', 'grad_ln_pre', 'grad_ln_post', 'grad_w_dw', 'grad_b_dw', 'grad_conv_ln_g', 'grad_conv_ln_b', 'grad_w_pw', 'grad_sinks', 'grad_w_br_conv', 'grad_w_br_attn', 'grad_w_out', 'grad_w_ple_gate', 'grad_w_ple_proj', 'delta_w_in', 'delta_ln_pre', 'delta_ln_post', 'delta_w_dw', 'delta_b_dw', 'delta_conv_ln_g', 'delta_conv_ln_b', 'delta_w_pw', 'delta_sinks', 'delta_w_br_conv', 'delta_w_br_attn', 'delta_w_out', 'delta_w_ple_gate', 'delta_w_ple_proj', 'new_m_w_in', 'new_m_ln_pre', 'new_m_ln_post', 'new_m_w_dw', 'new_m_b_dw', 'new_m_conv_ln_g', 'new_m_conv_ln_b', 'new_m_w_pw', 'new_m_sinks', 'new_m_w_br_conv', 'new_m_w_br_attn', 'new_m_w_out', 'new_m_w_ple_gate', 'new_m_w_ple_proj', 'new_v_w_in', 'new_v_ln_pre', 'new_v_ln_post', 'new_v_w_dw', 'new_v_b_dw', 'new_v_conv_ln_g', 'new_v_conv_ln_b', 'new_v_w_pw', 'new_v_sinks', 'new_v_w_br_conv', 'new_v_w_br_attn', 'new_v_w_out', 'new_v_w_ple_gate', 'new_v_w_ple_proj']
TWIN_LEAF_KINDS = {'loss': 'loss', 'grad_x': 'grad_x', 'grad_w_in': 'grad_w', 'grad_ln_pre': 'grad_w', 'grad_ln_post': 'grad_w', 'grad_w_dw': 'grad_w', 'grad_b_dw': 'grad_w', 'grad_conv_ln_g': 'grad_w', 'grad_conv_ln_b': 'grad_w', 'grad_w_pw': 'grad_w', 'grad_sinks': 'grad_w', 'grad_w_br_conv': 'grad_w', 'grad_w_br_attn': 'grad_w', 'grad_w_out': 'grad_w', 'grad_w_ple_gate': 'grad_w', 'grad_w_ple_proj': 'grad_w', 'delta_w_in': 'delta_w', 'delta_ln_pre': 'delta_w', 'delta_ln_post': 'delta_w', 'delta_w_dw': 'delta_w', 'delta_b_dw': 'delta_w', 'delta_conv_ln_g': 'delta_w', 'delta_conv_ln_b': 'delta_w', 'delta_w_pw': 'delta_w', 'delta_sinks': 'delta_w', 'delta_w_br_conv': 'delta_w', 'delta_w_br_attn': 'delta_w', 'delta_w_out': 'delta_w', 'delta_w_ple_gate': 'delta_w', 'delta_w_ple_proj': 'delta_w', 'new_m_w_in': 'new_m', 'new_m_ln_pre': 'new_m', 'new_m_ln_post': 'new_m', 'new_m_w_dw': 'new_m', 'new_m_b_dw': 'new_m', 'new_m_conv_ln_g': 'new_m', 'new_m_conv_ln_b': 'new_m', 'new_m_w_pw': 'new_m', 'new_m_sinks': 'new_m', 'new_m_w_br_conv': 'new_m', 'new_m_w_br_attn': 'new_m', 'new_m_w_out': 'new_m', 'new_m_w_ple_gate': 'new_m', 'new_m_w_ple_proj': 'new_m', 'new_v_w_in': 'new_v', 'new_v_ln_pre': 'new_v', 'new_v_ln_post': 'new_v', 'new_v_w_dw': 'new_v', 'new_v_b_dw': 'new_v', 'new_v_conv_ln_g': 'new_v', 'new_v_conv_ln_b': 'new_v', 'new_v_w_pw': 'new_v', 'new_v_sinks': 'new_v', 'new_v_w_br_conv': 'new_v', 'new_v_w_br_attn': 'new_v', 'new_v_w_out': 'new_v', 'new_v_w_ple_gate': 'new_v', 'new_v_w_ple_proj': 'new_v'}


def _forward(args):
    return _fwd_reference(*[args[k] for k in FWD_PARAMS])


def _output_shape():
    out = _jax.eval_shape(lambda: _forward(_fwd_setup_inputs(0)))
    return out.shape, out.dtype

N_MICROBATCH = 1
ADAM_LR = 0.001
ADAM_B1 = 0.9
ADAM_B2 = 0.999
ADAM_EPS = 1e-08
ADAM_WD = 0.01
ADAM_STEP = 10
PER_EXAMPLE_BATCH_AXIS = {'x': 0, 'p': 1, 'positions': 0, 'loss_target': 0}
SHARED_INPUTS = []
_WEIGHT_DTYPES = {'w_in': _jnp.float32, 'ln_pre': _jnp.float32, 'ln_post': _jnp.float32, 'w_dw': _jnp.float32, 'b_dw': _jnp.float32, 'conv_ln_g': _jnp.float32, 'conv_ln_b': _jnp.float32, 'w_pw': _jnp.float32, 'sinks': _jnp.float32, 'w_br_conv': _jnp.float32, 'w_br_attn': _jnp.float32, 'w_out': _jnp.float32, 'w_ple_gate': _jnp.float32, 'w_ple_proj': _jnp.float32}
MOMENT_SCALE = {'w_in': 2.458777e-01, 'ln_pre': 6.622606e-01, 'ln_post': 6.521595e+01, 'w_dw': 4.228865e-01, 'b_dw': 2.024845e+00, 'conv_ln_g': 8.536594e-01, 'conv_ln_b': 1.233241e+00, 'w_pw': 5.527938e-01, 'sinks': 7.727743e-02, 'w_br_conv': 5.411086e-01, 'w_br_attn': 1.253103e-01, 'w_out': 5.420549e-01, 'w_ple_gate': 3.473425e-01, 'w_ple_proj': 7.435581e-01}


def _to_microbatches(a, axis):
    t = _jnp.moveaxis(a, axis, 0)
    t = t.reshape((N_MICROBATCH, t.shape[0] // N_MICROBATCH) + t.shape[1:])
    return _jnp.moveaxis(t, 1, axis + 1)


def setup_inputs(seed: int = 0) -> dict:
    inp = _fwd_setup_inputs(seed)
    key = _jax.random.fold_in(_jax.random.key(seed), 7919)
    shape, _ = _output_shape()
    out = dict(inp)
    out["loss_target"] = _jax.random.normal(_jax.random.fold_in(key, 0), shape, _jnp.float32)
    for i, name in enumerate(TWIN_WEIGHTS):
        w = inp[name].astype(_jnp.float32)
        if MOMENT_SCALE is None:
            s = _jnp.sqrt(_jnp.mean(_jnp.square(w)) + 1e-30)
        else:
            s = MOMENT_SCALE[name]
        km, kv = _jax.random.split(_jax.random.fold_in(key, i + 1))
        out[name] = w
        out["m_" + name] = s * _jax.random.normal(km, w.shape, _jnp.float32)
        out["v_" + name] = (s * s) * _jax.random.uniform(kv, w.shape, _jnp.float32, 0.5, 1.5)
    if N_MICROBATCH > 1:
        for name, axis in PER_EXAMPLE_BATCH_AXIS.items():
            out[name] = _to_microbatches(out[name], axis)
    return {'x': out['x'], 'p': out['p'], 'positions': out['positions'], 'w_in': out['w_in'], 'ln_pre': out['ln_pre'], 'ln_post': out['ln_post'], 'w_dw': out['w_dw'], 'b_dw': out['b_dw'], 'conv_ln_g': out['conv_ln_g'], 'conv_ln_b': out['conv_ln_b'], 'w_pw': out['w_pw'], 'sinks': out['sinks'], 'w_br_conv': out['w_br_conv'], 'w_br_attn': out['w_br_attn'], 'w_out': out['w_out'], 'w_ple_gate': out['w_ple_gate'], 'w_ple_proj': out['w_ple_proj'], 'loss_target': out['loss_target'], 'm_w_in': out['m_w_in'], 'm_ln_pre': out['m_ln_pre'], 'm_ln_post': out['m_ln_post'], 'm_w_dw': out['m_w_dw'], 'm_b_dw': out['m_b_dw'], 'm_conv_ln_g': out['m_conv_ln_g'], 'm_conv_ln_b': out['m_conv_ln_b'], 'm_w_pw': out['m_w_pw'], 'm_sinks': out['m_sinks'], 'm_w_br_conv': out['m_w_br_conv'], 'm_w_br_attn': out['m_w_br_attn'], 'm_w_out': out['m_w_out'], 'm_w_ple_gate': out['m_w_ple_gate'], 'm_w_ple_proj': out['m_w_ple_proj'], 'v_w_in': out['v_w_in'], 'v_ln_pre': out['v_ln_pre'], 'v_ln_post': out['v_ln_post'], 'v_w_dw': out['v_w_dw'], 'v_b_dw': out['v_b_dw'], 'v_conv_ln_g': out['v_conv_ln_g'], 'v_conv_ln_b': out['v_conv_ln_b'], 'v_w_pw': out['v_w_pw'], 'v_sinks': out['v_sinks'], 'v_w_br_conv': out['v_w_br_conv'], 'v_w_br_attn': out['v_w_br_attn'], 'v_w_out': out['v_w_out'], 'v_w_ple_gate': out['v_w_ple_gate'], 'v_w_ple_proj': out['v_w_ple_proj']}


def _loss(weights, diff, rest, loss_target):
    with _jax.named_scope("forward"):
        args = {**rest, TWIN_DIFF_INPUT: diff, **{k: w.astype(_WEIGHT_DTYPES[k]) for k, w in weights.items()}}
        y = _forward(args)
    with _jax.named_scope("loss_head"):
        err = _jnp.square(y.astype(_jnp.float32) - loss_target)
        return 0.5 * _jnp.sum(_jnp.mean(err, axis=-1)) if err.ndim else 0.5 * err


def _adamw(w, g, m, v):
    m = ADAM_B1 * m + (1.0 - ADAM_B1) * g
    v = ADAM_B2 * v + (1.0 - ADAM_B2) * _jnp.square(g)
    m_hat = m / (1.0 - ADAM_B1 ** ADAM_STEP)
    v_hat = v / (1.0 - ADAM_B2 ** ADAM_STEP)
    delta = -ADAM_LR * (m_hat / (_jnp.sqrt(v_hat) + ADAM_EPS) + ADAM_WD * w)
    return delta, m, v


def reference(x, p, positions, w_in, ln_pre, ln_post, w_dw, b_dw, conv_ln_g, conv_ln_b, w_pw, sinks, w_br_conv, w_br_attn, w_out, w_ple_gate, w_ple_proj, loss_target, m_w_in, m_ln_pre, m_ln_post, m_w_dw, m_b_dw, m_conv_ln_g, m_conv_ln_b, m_w_pw, m_sinks, m_w_br_conv, m_w_br_attn, m_w_out, m_w_ple_gate, m_w_ple_proj, v_w_in, v_ln_pre, v_ln_post, v_w_dw, v_b_dw, v_conv_ln_g, v_conv_ln_b, v_w_pw, v_sinks, v_w_br_conv, v_w_br_attn, v_w_out, v_w_ple_gate, v_w_ple_proj):
    given = dict(x=x, p=p, positions=positions, w_in=w_in, ln_pre=ln_pre, ln_post=ln_post, w_dw=w_dw, b_dw=b_dw, conv_ln_g=conv_ln_g, conv_ln_b=conv_ln_b, w_pw=w_pw, sinks=sinks, w_br_conv=w_br_conv, w_br_attn=w_br_attn, w_out=w_out, w_ple_gate=w_ple_gate, w_ple_proj=w_ple_proj, loss_target=loss_target, m_w_in=m_w_in, m_ln_pre=m_ln_pre, m_ln_post=m_ln_post, m_w_dw=m_w_dw, m_b_dw=m_b_dw, m_conv_ln_g=m_conv_ln_g, m_conv_ln_b=m_conv_ln_b, m_w_pw=m_w_pw, m_sinks=m_sinks, m_w_br_conv=m_w_br_conv, m_w_br_attn=m_w_br_attn, m_w_out=m_w_out, m_w_ple_gate=m_w_ple_gate, m_w_ple_proj=m_w_ple_proj, v_w_in=v_w_in, v_ln_pre=v_ln_pre, v_ln_post=v_ln_post, v_w_dw=v_w_dw, v_b_dw=v_b_dw, v_conv_ln_g=v_conv_ln_g, v_conv_ln_b=v_conv_ln_b, v_w_pw=v_w_pw, v_sinks=v_sinks, v_w_br_conv=v_w_br_conv, v_w_br_attn=v_w_br_attn, v_w_out=v_w_out, v_w_ple_gate=v_w_ple_gate, v_w_ple_proj=v_w_ple_proj)
    weights = {n: given[n] for n in TWIN_WEIGHTS}
    shared = {n: given[n] for n in SHARED_INPUTS}
    per_example = {n: given[n] for n in ['x', 'p', 'positions']}
    grad_fn = _jax.value_and_grad(_loss, argnums=(0, 1))

    def one_microbatch(ex, loss_target):
        ex = dict(ex)
        diff = ex.pop(TWIN_DIFF_INPUT)
        return grad_fn(weights, diff, {**shared, **ex}, loss_target)

    if N_MICROBATCH == 1:
        loss, (grad_w, grad_x) = one_microbatch(per_example, given["loss_target"])
    else:
        def body(carry, xs):
            loss_sum, grad_sum = carry
            l_k, (gw_k, gx_k) = one_microbatch(xs[0], xs[1])
            with _jax.named_scope("update"):
                return (loss_sum + l_k, _jax.tree.map(_jnp.add, grad_sum, gw_k)), gx_k

        init = (_jnp.zeros((), _jnp.float32), _jax.tree.map(_jnp.zeros_like, weights))
        (loss, grad_w), grad_x = _jax.lax.scan(body, init, (per_example, given["loss_target"]))
    with _jax.named_scope("update"):
        delta_w, new_m, new_v = {}, {}, {}
        for n in TWIN_WEIGHTS:
            delta_w[n], new_m[n], new_v[n] = _adamw(weights[n], grad_w[n], given["m_" + n], given["v_" + n])
    return (loss, grad_x, *[grad_w[n] for n in TWIN_WEIGHTS], *[delta_w[n] for n in TWIN_WEIGHTS],
            *[new_m[n] for n in TWIN_WEIGHTS], *[new_v[n] for n in TWIN_WEIGHTS])
```

```python
import functools

import numpy as np

import jax
import jax.numpy as jnp
from jax import lax
from jax.experimental import pallas as pl
from jax.experimental.pallas import tpu as pltpu

F32 = jnp.float32
BF16 = jnp.bfloat16

D = 1024
N_HEADS = 16
N_KV = 2
HEAD_DIM = 64
GROUP = N_HEADS // N_KV
KV_W = N_KV * HEAD_DIM
CONV_K = 31
CONV_HALO = 32
BLK = 128
ROPE_DIM = 16
ROPE_THETA = 500000.0
EPS = 1e-6
PLE = 256
NW = 7 * D + 2 * KV_W
N_DEV = 8
MESH_AXES = ("x", "y", "c")

CB_VAL, CB_GLU, CB_CGATE, CB_Q, CB_AGATE, CB_GCONV, CB_GATTN = range(7)
CB_K = 7 * D // KV_W
CB_V = CB_K + 1

ADAM_LR, ADAM_B1, ADAM_B2, ADAM_EPS, ADAM_WD, ADAM_STEP = 0.001, 0.9, 0.999, 1e-08, 0.01, 10

VMEM_LIMIT = 56 * 1024 * 1024
ROW_TILE = 256


def _cparams(*sem):
    return pltpu.CompilerParams(dimension_semantics=sem if sem else None, vmem_limit_bytes=VMEM_LIMIT)


def _sig(v):
    return 1.0 / (1.0 + jnp.exp(-v))


def _rowsum8(a):
    return a.reshape(a.shape[0] // 8, 8, a.shape[1]).sum(axis=0)


def _dot(a, b):
    return jnp.dot(a, b, preferred_element_type=F32)


def _dot_nt(a, b):
    return lax.dot_general(a, b, (((1,), (1,)), ((), ())), preferred_element_type=F32)


def _dot_tn(a, b):
    return lax.dot_general(a, b, (((0,), (0,)), ((), ())), preferred_element_type=F32)


def _to_perm(w):
    return jnp.concatenate([w[..., : 4 * D], w[..., 4 * D + 2 * KV_W :], w[..., 4 * D : 4 * D + 2 * KV_W]], axis=-1)


def _from_perm(g):
    return jnp.concatenate([g[..., : 4 * D], g[..., 7 * D :], g[..., 4 * D : 7 * D]], axis=-1)


def _my_place():
    return lax.axis_index("x"), lax.axis_index("y"), lax.axis_index("c")


def _slot(px, py, pc):
    return 4 * px + 2 * py + pc


def _all_gather(shards):
    n = len(shards)

    def body(*refs):
        ins, outs = refs[:n], refs[n : 2 * n]
        send_sems, recv_sems, local_sems = refs[2 * n :]
        x, y, c = _my_place()
        me, sibling = (x, y, c), (x, y, 1 - c)
        chips = [(1 - x, y), (x, 1 - y), (1 - x, 1 - y)]

        def copy(a, k, block, to, src=None):
            rows = outs[a].at[_slot(*block)]
            return pltpu.make_async_remote_copy(
                src_ref=rows if src is None else src, dst_ref=rows, send_sem=send_sems.at[a, k],
                recv_sem=recv_sems.at[a, k], device_id=to, device_id_type=pl.DeviceIdType.MESH)

        mine, first, passed = [], [], []
        for a in range(n):
            cp = pltpu.make_async_copy(ins[a], outs[a].at[_slot(*me)], local_sems.at[a])
            cp.start()
            mine.append(cp)
            fa = [copy(a, 0, me, sibling, src=ins[a])]
            fa += [copy(a, 1 + j, me, (*chip, c), src=ins[a]) for j, chip in enumerate(chips)]
            for cp in fa:
                cp.start()
            first += fa
        for j, chip in enumerate(chips):
            for a in range(n):
                copy(a, 1 + j, (*chip, c), me).wait_recv()
                cp = copy(a, 4 + j, (*chip, c), sibling)
                cp.start()
                passed.append(cp)
        for a in range(n):
            copy(a, 0, sibling, me).wait_recv()
            for j, chip in enumerate(chips):
                copy(a, 4 + j, (*chip, 1 - c), me).wait_recv()
        for cp in first + passed:
            cp.wait_send()
        for cp in mine:
            cp.wait()

    any_spec = pl.BlockSpec(memory_space=pl.ANY)
    return pl.pallas_call(
        body, name="weight_all_gather",
        out_shape=[jax.ShapeDtypeStruct((N_DEV,) + s.shape, s.dtype) for s in shards],
        in_specs=[any_spec] * n, out_specs=[any_spec] * n,
        scratch_shapes=[pltpu.SemaphoreType.DMA((n, 7)), pltpu.SemaphoreType.DMA((n, 7)), pltpu.SemaphoreType.DMA((n,))],
    )(*shards)


def _grad_exchange(scatter, bcast):
    arrs = list(scatter) + list(bcast)
    n, n_sc = len(arrs), len(scatter)

    def body(*refs):
        ins, outs = refs[:n], refs[n : 2 * n]
        send_sems, recv_sems, local_sems = refs[2 * n :]
        x, y, c = _my_place()
        me = _slot(x, y, c)
        mine, sends = [], []
        for a in range(n):
            src = ins[a].at[me] if a < n_sc else ins[a]
            cp = pltpu.make_async_copy(src, outs[a].at[me], local_sems.at[a])
            cp.start()
            mine.append(cp)
        peers = []
        for k in range(1, N_DEV):
            px = 1 - x if k & 4 else x
            py = 1 - y if k & 2 else y
            pc = 1 - c if k & 1 else c
            peers.append((px, py, pc))

        def copy(a, k, peer):
            src = ins[a].at[_slot(*peer)] if a < n_sc else ins[a]
            return pltpu.make_async_remote_copy(
                src_ref=src, dst_ref=outs[a].at[me], send_sem=send_sems.at[a, k], recv_sem=recv_sems.at[a, k],
                device_id=peer, device_id_type=pl.DeviceIdType.MESH)

        def arrival(a, k, peer):
            rows = outs[a].at[_slot(*peer)]
            return pltpu.make_async_remote_copy(
                src_ref=rows, dst_ref=rows, send_sem=send_sems.at[a, k], recv_sem=recv_sems.at[a, k],
                device_id=peer, device_id_type=pl.DeviceIdType.MESH)

        for k, peer in enumerate(peers):
            for a in range(n):
                cp = copy(a, k, peer)
                cp.start()
                sends.append(cp)
        for k, peer in enumerate(peers):
            for a in range(n):
                arrival(a, k, peer).wait_recv()
        for cp in sends:
            cp.wait_send()
        for cp in mine:
            cp.wait()

    any_spec = pl.BlockSpec(memory_space=pl.ANY)
    out_shape = [jax.ShapeDtypeStruct(a.shape, a.dtype) for a in scatter]
    out_shape += [jax.ShapeDtypeStruct((N_DEV,) + a.shape, a.dtype) for a in bcast]
    return pl.pallas_call(
        body, name="grad_exchange", out_shape=out_shape, in_specs=[any_spec] * n, out_specs=[any_spec] * n,
        scratch_shapes=[pltpu.SemaphoreType.DMA((n, 7)), pltpu.SemaphoreType.DMA((n, 7)), pltpu.SemaphoreType.DMA((n,))],
    )(*arrs)


def _rope_tables(pos):
    T = pos.shape[0]
    tm = min(1024, T)
    lane = np.arange(128) % HEAD_DIM
    inv = np.power(np.float32(ROPE_THETA), -np.arange(0, ROPE_DIM, 2, dtype=np.float32) / np.float32(ROPE_DIM)).astype(np.float32)
    half = ROPE_DIM // 2
    invf = np.where(lane < ROPE_DIM, inv[lane % half], 0.0).astype(np.float32)[None, :]
    m_a = (lane < half).astype(np.float32)[None, :]
    m_b = ((lane >= half) & (lane < ROPE_DIM)).astype(np.float32)[None, :]

    def body(pos_ref, invf_ref, ma_ref, mb_ref, cos_ref, sa_ref, sb_ref):
        ang = pos_ref[...].astype(F32) * invf_ref[...]
        sn = jnp.sin(ang)
        cos_ref[...] = jnp.cos(ang)
        sa_ref[...] = -sn * ma_ref[...]
        sb_ref[...] = sn * mb_ref[...]

    row = pl.BlockSpec((tm, 128), lambda i: (i, 0))
    cst = pl.BlockSpec((1, 128), lambda i: (0, 0))
    return pl.pallas_call(
        body, name="rope_tables", grid=(T // tm,), out_shape=[jax.ShapeDtypeStruct((T, 128), F32)] * 3,
        in_specs=[pl.BlockSpec((tm, 1), lambda i: (i, 0)), cst, cst, cst], out_specs=[row] * 3,
        compiler_params=_cparams("parallel"),
    )(pos, jnp.asarray(invf), jnp.asarray(m_a), jnp.asarray(m_b))


def _rope(t, cos, sa, sb, sign=1.0):
    parts = []
    for i in range(t.shape[1] // 128):
        ti = t[:, 128 * i : 128 * (i + 1)]
        up = pltpu.roll(ti, 128 - ROPE_DIM // 2, 1)
        dn = pltpu.roll(ti, ROPE_DIM // 2, 1)
        parts.append(ti * cos + sign * (up * sa + dn * sb))
    return parts[0] if len(parts) == 1 else jnp.concatenate(parts, axis=-1)


def _in_proj(x2, ln_pre, w_in):
    T = x2.shape[0]
    tm = min(512, T)
    tn = NW // 2
    chunk = 512

    def body(x_ref, g_ref, w_ref, z_ref, h_ref, hs):
        @pl.when(pl.program_id(1) == 0)
        def _():
            xv = x_ref[...]
            r = lax.rsqrt(jnp.mean(xv * xv, axis=-1, keepdims=True) + EPS)
            h = (xv * r * g_ref[...]).astype(BF16)
            hs[...] = h
            h_ref[...] = h

        for c0 in range(0, tn, chunk):
            cw = min(chunk, tn - c0)
            z_ref[:, c0 : c0 + cw] = _dot(hs[...], w_ref[:, c0 : c0 + cw]).astype(BF16)

    return pl.pallas_call(
        body, name="in_proj", grid=(T // tm, NW // tn),
        out_shape=[jax.ShapeDtypeStruct((T, NW), BF16), jax.ShapeDtypeStruct((T, D), BF16)],
        in_specs=[pl.BlockSpec((tm, D), lambda i, j: (i, 0)), pl.BlockSpec((1, D), lambda i, j: (0, 0)),
                  pl.BlockSpec((D, tn), lambda i, j: (0, j))],
        out_specs=[pl.BlockSpec((tm, tn), lambda i, j: (i, j)), pl.BlockSpec((tm, D), lambda i, j: (i, 0))],
        scratch_shapes=[pltpu.VMEM((tm, D), BF16)],
        compiler_params=_cparams("parallel", "arbitrary"),
    )(x2, ln_pre, w_in)


def _conv_tiles(S):
    tm = min(ROW_TILE, S)
    return tm, S // tm, tm // CONV_HALO


CONV_ROWS = 32


def _conv_fwd(z3, w_dw, b_dw, ln_g, ln_b):
    NS, S, _ = z3.shape
    tm, nt, r = _conv_tiles(S)

    def body(val_ref, glu_ref, hval_ref, hglu_ref, w_ref, b_ref, g_ref, bb_ref, c_ref, cs_ref, ubuf, cbuf):
        i = pl.program_id(1)
        ubuf[CONV_HALO:, :] = val_ref[...].astype(F32) * _sig(glu_ref[...].astype(F32))
        uh = hval_ref[...].astype(F32) * _sig(hglu_ref[...].astype(F32))
        ubuf[0:CONV_HALO, :] = jnp.where(i > 0, uh, 0.0)
        for r0 in range(0, tm, CONV_ROWS):
            acc = jnp.zeros((CONV_ROWS, D), F32)
            for k in range(CONV_K):
                o = r0 + CONV_HALO - (CONV_K - 1) + k
                acc = acc + w_ref[k : k + 1, :] * ubuf[o : o + CONV_ROWS, :]
            cbuf[r0 : r0 + CONV_ROWS, :] = acc + b_ref[...]
        cv = cbuf[...]
        mu = jnp.mean(cv, axis=-1, keepdims=True)
        xc = cv - mu
        var = jnp.mean(xc * xc, axis=-1, keepdims=True)
        cl = xc * lax.rsqrt(var + EPS) * g_ref[...] + bb_ref[...]
        c_ref[...] = cv.astype(BF16)
        cs_ref[...] = (cl * _sig(cl)).astype(BF16)

    def cur(cb):
        return pl.BlockSpec((None, tm, D), lambda s, i: (s, i, cb))

    def halo(cb):
        return pl.BlockSpec((None, CONV_HALO, D), lambda s, i: (s, jnp.maximum(i * r - 1, 0), cb))

    vec = pl.BlockSpec((1, D), lambda s, i: (0, 0))
    out = pl.BlockSpec((None, tm, D), lambda s, i: (s, i, 0))
    return pl.pallas_call(
        body, name="conv_fwd", grid=(NS, nt),
        out_shape=[jax.ShapeDtypeStruct((NS, S, D), BF16)] * 2,
        in_specs=[cur(CB_VAL), cur(CB_GLU), halo(CB_VAL), halo(CB_GLU),
                  pl.BlockSpec((CONV_HALO, D), lambda s, i: (0, 0)), vec, vec, vec],
        out_specs=[out, out],
        scratch_shapes=[pltpu.VMEM((tm + CONV_HALO, D), F32), pltpu.VMEM((tm, D), F32)],
        compiler_params=_cparams("parallel", "parallel"),
    )(z3, z3, z3, z3, w_dw, b_dw, ln_g, ln_b)


def _rope_qk(z2, cos, sa, sb):
    T = z2.shape[0]
    tm = min(ROW_TILE, T)

    def body(q_ref, k_ref, cos_ref, sa_ref, sb_ref, qr_ref, kr_ref):
        tabs = cos_ref[...], sa_ref[...], sb_ref[...]
        qr_ref[...] = (_rope(q_ref[...].astype(F32), *tabs) * (HEAD_DIM ** -0.5)).astype(BF16)
        kr_ref[...] = _rope(k_ref[...].astype(F32), *tabs).astype(BF16)

    tab = pl.BlockSpec((tm, 128), lambda i: (i, 0))
    return pl.pallas_call(
        body, name="rope_qk", grid=(T // tm,),
        out_shape=[jax.ShapeDtypeStruct((T, D), BF16), jax.ShapeDtypeStruct((T, KV_W), BF16)],
        in_specs=[pl.BlockSpec((tm, D), lambda i: (i, CB_Q)), pl.BlockSpec((tm, KV_W), lambda i: (i, CB_K)), tab, tab, tab],
        out_specs=[pl.BlockSpec((tm, D), lambda i: (i, 0)), pl.BlockSpec((tm, KV_W), lambda i: (i, 0))],
        compiler_params=_cparams("parallel"),
    )(z2, z2, cos, sa, sb)


def _attn_mask(has_prev):
    qi = lax.broadcasted_iota(jnp.int32, (BLK, 2 * BLK), 0)
    kj = lax.broadcasted_iota(jnp.int32, (BLK, 2 * BLK), 1)
    first_key = jnp.where(has_prev, 0, BLK)
    return (kj > qi) & (kj <= qi + BLK) & (kj >= first_key)


NEG_BIG = -1e30


def _attn_specs():
    q = pl.BlockSpec((None, BLK, D), lambda s, n: (s, n, 0))
    kc = pl.BlockSpec((None, BLK, KV_W), lambda s, n: (s, n, 0))
    kp = pl.BlockSpec((None, BLK, KV_W), lambda s, n: (s, jnp.maximum(n - 1, 0), 0))
    vc = pl.BlockSpec((None, BLK, KV_W), lambda s, n: (s, n, CB_V))
    vp = pl.BlockSpec((None, BLK, KV_W), lambda s, n: (s, jnp.maximum(n - 1, 0), CB_V))
    sink = pl.BlockSpec(memory_space=pltpu.SMEM)
    return sink, q, kc, kp, vc, vp


def _attn_fwd(qr3, kr3, z3, sinks):
    NS, S, _ = qr3.shape

    def body(sink_ref, q_ref, kc_ref, kp_ref, vc_ref, vp_ref, o_ref, lse_ref):
        has_prev = pl.program_id(1) > 0
        kcat = jnp.concatenate([jnp.where(has_prev, kp_ref[...], jnp.zeros(kp_ref.shape, kp_ref.dtype)), kc_ref[...]], axis=0)
        vcat = jnp.concatenate([jnp.where(has_prev, vp_ref[...], jnp.zeros(vp_ref.shape, vp_ref.dtype)), vc_ref[...]], axis=0)
        mask = _attn_mask(has_prev)
        outs = []
        for h in range(N_HEADS):
            g = h // GROUP
            qh = q_ref[:, h * HEAD_DIM : (h + 1) * HEAD_DIM]
            kg = kcat[:, g * HEAD_DIM : (g + 1) * HEAD_DIM]
            vg = vcat[:, g * HEAD_DIM : (g + 1) * HEAD_DIM]
            s = jnp.where(mask, _dot_nt(qh, kg), NEG_BIG)
            sk = sink_ref[h]
            mx = jnp.maximum(jnp.max(s, axis=-1, keepdims=True), sk)
            e = jnp.exp(s - mx)
            den = jnp.sum(e, axis=-1, keepdims=True) + jnp.exp(sk - mx)
            outs.append(_dot((e / den).astype(BF16), vg))
            lse_ref[:, h : h + 1] = mx + jnp.log(den)
        o_ref[...] = jnp.concatenate(outs, axis=-1).astype(BF16)

    return pl.pallas_call(
        body, name="attn_fwd", grid=(NS, S // BLK),
        out_shape=[jax.ShapeDtypeStruct((NS, S, D), BF16), jax.ShapeDtypeStruct((NS, S, N_HEADS), F32)],
        in_specs=list(_attn_specs()),
        out_specs=[pl.BlockSpec((None, BLK, D), lambda s, n: (s, n, 0)),
                   pl.BlockSpec((None, BLK, N_HEADS), lambda s, n: (s, n, 0))],
        compiler_params=_cparams("parallel", "parallel"),
    )(sinks, qr3, kr3, kr3, z3, z3)


def _weight_spec(shape):
    return pl.BlockSpec(shape, lambda i: (0,) * len(shape), pipeline_mode=pl.Buffered(1))


def _branch_a_fwd(cs, z2, w_pw, w_brc):
    T = cs.shape[0]
    tm = min(ROW_TILE, T)

    def body(cs_ref, cg_ref, wpw_ref, wbr_ref, ya0_ref, ya_ref):
        ya0 = _dot(cs_ref[...], wpw_ref[...])
        cg = cg_ref[...].astype(F32)
        ya0_ref[...] = ya0.astype(BF16)
        ya_ref[...] = _dot((ya0 * (cg * _sig(cg))).astype(BF16), wbr_ref[...]).astype(BF16)

    row = pl.BlockSpec((tm, D), lambda i: (i, 0))
    return pl.pallas_call(
        body, name="branch_a_fwd", grid=(T // tm,),
        out_shape=[jax.ShapeDtypeStruct((T, D), BF16)] * 2,
        in_specs=[row, pl.BlockSpec((tm, D), lambda i: (i, CB_CGATE)), _weight_spec((D, D)), _weight_spec((D, D))],
        out_specs=[row, row],
        compiler_params=_cparams("parallel"),
    )(cs, z2, w_pw, w_brc)


def _merge_fwd(o, z2, ya, x2, p2, tgt, ln_post, w_bra, w_out, w_pg, w_pp):
    T = o.shape[0]
    tm = min(ROW_TILE, T)
    nt = T // tm

    def body(o_ref, ag_ref, gc_ref, ga_ref, ya_ref, x_ref, p_ref, t_ref, g_ref, wbra_ref, wout_ref, wpg_ref, wpp_ref,
             yb_ref, m_ref, mo_ref, x1_ref, gate_ref, pp_ref, dy_ref, loss_ref, lacc):
        i = pl.program_id(0)

        @pl.when(i == 0)
        def _():
            lacc[...] = jnp.zeros_like(lacc)

        ag = ag_ref[...].astype(F32)
        yb = _dot((o_ref[...].astype(F32) * (ag * _sig(ag))).astype(BF16), wbra_ref[...])
        m = _sig(gc_ref[...].astype(F32)) * ya_ref[...].astype(F32) + _sig(ga_ref[...].astype(F32)) * yb
        mb = m.astype(BF16)
        mo = _dot(mb, wout_ref[...])
        r2 = lax.rsqrt(jnp.mean(mo * mo, axis=-1, keepdims=True) + EPS)
        x1 = x_ref[...] + mo * r2 * g_ref[...]
        x1b = x1.astype(BF16)
        gate = _sig(_dot(x1b, wpg_ref[...]))
        pp = _dot(p_ref[...].astype(BF16), wpp_ref[...])
        e = x1 + gate * pp - t_ref[...]
        yb_ref[...] = yb.astype(BF16)
        m_ref[...] = mb
        mo_ref[...] = mo
        x1_ref[...] = x1b
        gate_ref[...] = gate.astype(BF16)
        pp_ref[...] = pp.astype(BF16)
        dy_ref[...] = e * (1.0 / D)
        lacc[...] += _rowsum8(e * e)

        @pl.when(i == nt - 1)
        def _():
            loss_ref[...] = jnp.full(loss_ref.shape, jnp.sum(lacc[...]) * (0.5 / D), F32)

    row = pl.BlockSpec((tm, D), lambda i: (i, 0))

    def zcol(cb):
        return pl.BlockSpec((tm, D), lambda i: (i, cb))

    bf = jax.ShapeDtypeStruct((T, D), BF16)
    f32 = jax.ShapeDtypeStruct((T, D), F32)
    return pl.pallas_call(
        body, name="merge_fwd", grid=(nt,),
        out_shape=[bf, bf, f32, bf, bf, bf, f32, jax.ShapeDtypeStruct((8, 128), F32)],
        in_specs=[row, zcol(CB_AGATE), zcol(CB_GCONV), zcol(CB_GATTN), row, row, pl.BlockSpec((tm, PLE), lambda i: (i, 0)), row,
                  _weight_spec((1, D)), _weight_spec((D, D)), _weight_spec((D, D)), _weight_spec((D, D)), _weight_spec((PLE, D))],
        out_specs=[row] * 7 + [pl.BlockSpec((8, 128), lambda i: (0, 0))],
        scratch_shapes=[pltpu.VMEM((8, D), F32)],
        compiler_params=_cparams("arbitrary"),
    )(o, z2, z2, z2, ya, x2, p2, tgt, ln_post, w_bra, w_out, w_pg, w_pp)


def _merge_bwd(dy, gate, pp, mo, ya, yb, z2, ln_post, w_pg_t, w_out_t):
    T = dy.shape[0]
    tm = min(ROW_TILE, T)

    def body(dy_ref, gate_ref, pp_ref, mo_ref, ya_ref, yb_ref, gc_ref, ga_ref, g_ref, wpgt_ref, woutt_ref,
             dgl_ref, dpp_ref, dx1_ref, dmo_ref, dya_ref, dyb_ref, dgc_ref, dga_ref, glp_ref):
        @pl.when(pl.program_id(0) == 0)
        def _():
            glp_ref[...] = jnp.zeros_like(glp_ref)

        dyv = dy_ref[...]
        gate = gate_ref[...].astype(F32)
        dgl = (dyv * pp_ref[...].astype(F32) * gate * (1.0 - gate)).astype(BF16)
        dgl_ref[...] = dgl
        dpp_ref[...] = (dyv * gate).astype(BF16)
        dx1 = dyv + _dot(dgl, wpgt_ref[...])
        dx1_ref[...] = dx1
        mo = mo_ref[...]
        r2 = lax.rsqrt(jnp.mean(mo * mo, axis=-1, keepdims=True) + EPS)
        nrm = mo * r2
        glp_ref[...] += _rowsum8(dx1 * nrm)
        dn = dx1 * g_ref[...]
        dmo = (r2 * (dn - nrm * jnp.mean(dn * nrm, axis=-1, keepdims=True))).astype(BF16)
        dmo_ref[...] = dmo
        dm = _dot(dmo, woutt_ref[...])
        sg = _sig(gc_ref[...].astype(F32))
        sa = _sig(ga_ref[...].astype(F32))
        dya_ref[...] = (dm * sg).astype(BF16)
        dyb_ref[...] = (dm * sa).astype(BF16)
        dgc_ref[...] = (dm * ya_ref[...].astype(F32) * sg * (1.0 - sg)).astype(BF16)
        dga_ref[...] = (dm * yb_ref[...].astype(F32) * sa * (1.0 - sa)).astype(BF16)

    row = pl.BlockSpec((tm, D), lambda i: (i, 0))
    bf = jax.ShapeDtypeStruct((T, D), BF16)
    return pl.pallas_call(
        body, name="merge_bwd", grid=(T // tm,),
        out_shape=[bf, bf, jax.ShapeDtypeStruct((T, D), F32), bf, bf, bf, bf, bf, jax.ShapeDtypeStruct((8, D), F32)],
        in_specs=[row] * 6 + [pl.BlockSpec((tm, D), lambda i: (i, CB_GCONV)), pl.BlockSpec((tm, D), lambda i: (i, CB_GATTN)),
                              _weight_spec((1, D)), _weight_spec((D, D)), _weight_spec((D, D))],
        out_specs=[row] * 8 + [pl.BlockSpec((8, D), lambda i: (0, 0))],
        compiler_params=_cparams("arbitrary"),
    )(dy, gate, pp, mo, ya, yb, z2, z2, ln_post, w_pg_t, w_out_t)


def _dsilu(v, s):
    return s * (1.0 + v * (1.0 - s))


def _branches_bwd(dya, dyb, ya0, o, z2, w_brc_t, w_pw_t, w_bra_t):
    T = dya.shape[0]
    tm = min(ROW_TILE, T)

    def body(dya_ref, dyb_ref, ya0_ref, o_ref, cg_ref, ag_ref, wbrct_ref, wpwt_ref, wbrat_ref,
             dya0_ref, dcg_ref, ya1_ref, dcs_ref, do_ref, dag_ref, yb0_ref):
        cg = cg_ref[...].astype(F32)
        s = _sig(cg)
        silu = cg * s
        ya0 = ya0_ref[...].astype(F32)
        dya1 = _dot(dya_ref[...], wbrct_ref[...])
        dya0 = (dya1 * silu).astype(BF16)
        dya0_ref[...] = dya0
        dcg_ref[...] = (dya1 * ya0 * _dsilu(cg, s)).astype(BF16)
        ya1_ref[...] = (ya0 * silu).astype(BF16)
        dcs_ref[...] = _dot(dya0, wpwt_ref[...]).astype(BF16)
        ag = ag_ref[...].astype(F32)
        sa = _sig(ag)
        silua = ag * sa
        ov = o_ref[...].astype(F32)
        dyb0 = _dot(dyb_ref[...], wbrat_ref[...])
        do_ref[...] = (dyb0 * silua).astype(BF16)
        dag_ref[...] = (dyb0 * ov * _dsilu(ag, sa)).astype(BF16)
        yb0_ref[...] = (ov * silua).astype(BF16)

    row = pl.BlockSpec((tm, D), lambda i: (i, 0))
    return pl.pallas_call(
        body, name="branches_bwd", grid=(T // tm,),
        out_shape=[jax.ShapeDtypeStruct((T, D), BF16)] * 7,
        in_specs=[row] * 4 + [pl.BlockSpec((tm, D), lambda i: (i, CB_CGATE)), pl.BlockSpec((tm, D), lambda i: (i, CB_AGATE)),
                              _weight_spec((D, D)), _weight_spec((D, D)), _weight_spec((D, D))],
        out_specs=[row] * 7,
        compiler_params=_cparams("parallel"),
    )(dya, dyb, ya0, o, z2, z2, w_brc_t, w_pw_t, w_bra_t)


def _conv_bwd(z3, c3, dcs3, w_dw, ln_g, ln_b):
    NS, S, _ = z3.shape
    tm, nt, r = _conv_tiles(S)

    def body(val_ref, glu_ref, hval_ref, hglu_ref, c_ref, dcs_ref, hc_ref, hdcs_ref, w_ref, g_ref, bb_ref,
             dz_ref, gw_ref, gvec_ref, ubuf, dcbuf, dubuf):
        i = pl.program_id(1)

        @pl.when((pl.program_id(0) == 0) & (i == 0))
        def _():
            gw_ref[...] = jnp.zeros_like(gw_ref)
            gvec_ref[...] = jnp.zeros_like(gvec_ref)

        val = val_ref[...].astype(F32)
        sg = _sig(glu_ref[...].astype(F32))
        ubuf[CONV_HALO:, :] = val * sg
        uh = hval_ref[...].astype(F32) * _sig(hglu_ref[...].astype(F32))
        ubuf[0:CONV_HALO, :] = jnp.where(i > 0, uh, 0.0)

        def ln_bwd(cv, dcs):
            cv = cv.astype(F32)
            mu = jnp.mean(cv, axis=-1, keepdims=True)
            xc = cv - mu
            rstd = lax.rsqrt(jnp.mean(xc * xc, axis=-1, keepdims=True) + EPS)
            xhat = xc * rstd
            cl = xhat * g_ref[...] + bb_ref[...]
            s = _sig(cl)
            dcl = dcs.astype(F32) * _dsilu(cl, s)
            dxh = dcl * g_ref[...]
            dc = rstd * (dxh - jnp.mean(dxh, axis=-1, keepdims=True) - xhat * jnp.mean(dxh * xhat, axis=-1, keepdims=True))
            return dc, dcl, xhat

        dc, dcl, xhat = ln_bwd(c_ref[...], dcs_ref[...])
        dcbuf[0:tm, :] = dc
        dch, _, _ = ln_bwd(hc_ref[...], hdcs_ref[...])
        dcbuf[tm:, :] = jnp.where(i < nt - 1, dch, 0.0)
        gvec_ref[0:8, :] += _rowsum8(dcl * xhat)
        gvec_ref[8:16, :] += _rowsum8(dcl)
        gvec_ref[16:24, :] += _rowsum8(dc)

        for r0 in range(0, tm, CONV_ROWS):
            acc = jnp.zeros((CONV_ROWS, D), F32)
            dcr = dcbuf[r0 : r0 + CONV_ROWS, :]
            for k in range(CONV_K):
                o = r0 + (CONV_K - 1) - k
                acc = acc + w_ref[k : k + 1, :] * dcbuf[o : o + CONV_ROWS, :]
                ou = r0 + CONV_HALO - (CONV_K - 1) + k
                gw_ref[8 * k : 8 * k + 8, :] += _rowsum8(ubuf[ou : ou + CONV_ROWS, :] * dcr)
            dubuf[r0 : r0 + CONV_ROWS, :] = acc
        du = dubuf[...]
        dz_ref[:, 0:D] = (du * sg).astype(BF16)
        dz_ref[:, D:] = (du * val * sg * (1.0 - sg)).astype(BF16)

    def cur(cb):
        return pl.BlockSpec((None, tm, D), lambda s, i: (s, i, cb))

    def prev(cb):
        return pl.BlockSpec((None, CONV_HALO, D), lambda s, i: (s, jnp.maximum(i * r - 1, 0), cb))

    nxt = pl.BlockSpec((None, CONV_HALO, D), lambda s, i: (s, jnp.minimum((i + 1) * r, S // CONV_HALO - 1), 0))
    vec = pl.BlockSpec((1, D), lambda s, i: (0, 0))
    return pl.pallas_call(
        body, name="conv_bwd", grid=(NS, nt),
        out_shape=[jax.ShapeDtypeStruct((NS, S, 2 * D), BF16), jax.ShapeDtypeStruct((CONV_HALO * 8, D), F32),
                   jax.ShapeDtypeStruct((24, D), F32)],
        in_specs=[cur(CB_VAL), cur(CB_GLU), prev(CB_VAL), prev(CB_GLU), cur(0), cur(0), nxt, nxt,
                  pl.BlockSpec((CONV_HALO, D), lambda s, i: (0, 0)), vec, vec],
        out_specs=[pl.BlockSpec((None, tm, 2 * D), lambda s, i: (s, i, 0)),
                   pl.BlockSpec((CONV_HALO * 8, D), lambda s, i: (0, 0)), pl.BlockSpec((24, D), lambda s, i: (0, 0))],
        scratch_shapes=[pltpu.VMEM((tm + CONV_HALO, D), F32), pltpu.VMEM((tm + CONV_HALO, D), F32), pltpu.VMEM((tm, D), F32)],
        compiler_params=_cparams("arbitrary", "arbitrary"),
    )(z3, z3, z3, z3, c3, dcs3, c3, dcs3, w_dw, ln_g, ln_b)


def _attn_bwd(qr3, kr3, z3, do3, o3, lse3, sinks):
    NS, S, _ = qr3.shape

    def body(sink_ref, q_ref, kc_ref, kp_ref, vc_ref, vp_ref, do_ref, o_ref, lse_ref,
             dq_ref, dkc_ref, dkp_ref, dvc_ref, dvp_ref, dsk_ref):
        has_prev = pl.program_id(1) > 0
        kcat = jnp.concatenate([jnp.where(has_prev, kp_ref[...], jnp.zeros(kp_ref.shape, kp_ref.dtype)), kc_ref[...]], axis=0)
        vcat = jnp.concatenate([jnp.where(has_prev, vp_ref[...], jnp.zeros(vp_ref.shape, vp_ref.dtype)), vc_ref[...]], axis=0)
        mask = _attn_mask(has_prev)
        dqs, dks, dvs = [], [], []
        for g in range(N_KV):
            kg = kcat[:, g * HEAD_DIM : (g + 1) * HEAD_DIM]
            vg = vcat[:, g * HEAD_DIM : (g + 1) * HEAD_DIM]
            dk = jnp.zeros((2 * BLK, HEAD_DIM), F32)
            dv = jnp.zeros((2 * BLK, HEAD_DIM), F32)
            for h in range(g * GROUP, (g + 1) * GROUP):
                cols = slice(h * HEAD_DIM, (h + 1) * HEAD_DIM)
                qh = q_ref[:, cols]
                doh = do_ref[:, cols]
                lse = lse_ref[:, h : h + 1]
                p = jnp.exp(jnp.where(mask, _dot_nt(qh, kg), NEG_BIG) - lse)
                delta = jnp.sum(doh.astype(F32) * o_ref[:, cols].astype(F32), axis=-1, keepdims=True)
                ds = (p * (_dot_nt(doh, vg) - delta)).astype(BF16)
                dqs.append(_dot(ds, kg))
                dk = dk + _dot_tn(ds, qh)
                dv = dv + _dot_tn(p.astype(BF16), doh)
                dsk_ref[:, h : h + 1] = -jnp.exp(sink_ref[h] - lse) * delta
            dks.append(dk)
            dvs.append(dv)
        dq_ref[...] = jnp.concatenate(dqs, axis=-1)
        dk = jnp.concatenate(dks, axis=-1)
        dv = jnp.concatenate(dvs, axis=-1)
        dkp_ref[...] = dk[0:BLK]
        dkc_ref[...] = dk[BLK:]
        dvp_ref[...] = dv[0:BLK]
        dvc_ref[...] = dv[BLK:]

    qspec = pl.BlockSpec((None, BLK, D), lambda s, n: (s, n, 0))
    kvspec = pl.BlockSpec((None, BLK, KV_W), lambda s, n: (s, n, 0))
    hspec = pl.BlockSpec((None, BLK, N_HEADS), lambda s, n: (s, n, 0))
    kv = jax.ShapeDtypeStruct((NS, S, KV_W), F32)
    return pl.pallas_call(
        body, name="attn_bwd", grid=(NS, S // BLK),
        out_shape=[jax.ShapeDtypeStruct((NS, S, D), F32), kv, kv, kv, kv, jax.ShapeDtypeStruct((NS, S, N_HEADS), F32)],
        in_specs=list(_attn_specs()) + [qspec, qspec, hspec],
        out_specs=[qspec, kvspec, kvspec, kvspec, kvspec, hspec],
        compiler_params=_cparams("parallel", "parallel"),
    )(sinks, qr3, kr3, kr3, z3, z3, do3, o3, lse3)


def _attn_post(dqr3, dkc3, dkp3, dvc3, dvp3, cos3, sa3, sb3):
    NS, S, _ = dqr3.shape
    nb = S // BLK

    def body(dq_ref, dkc_ref, dkn_ref, dvc_ref, dvn_ref, cos_ref, sa_ref, sb_ref, dqo_ref, dkv_ref):
        has_next = pl.program_id(1) < nb - 1
        tabs = cos_ref[...], sa_ref[...], sb_ref[...]
        dqo_ref[...] = (_rope(dq_ref[...], *tabs, sign=-1.0) * (HEAD_DIM ** -0.5)).astype(BF16)
        dk = dkc_ref[...] + jnp.where(has_next, dkn_ref[...], 0.0)
        dkv_ref[:, 0:KV_W] = _rope(dk, *tabs, sign=-1.0).astype(BF16)
        dkv_ref[:, KV_W:] = (dvc_ref[...] + jnp.where(has_next, dvn_ref[...], 0.0)).astype(BF16)

    cur = pl.BlockSpec((None, BLK, KV_W), lambda s, j: (s, j, 0))
    nxt = pl.BlockSpec((None, BLK, KV_W), lambda s, j: (s, jnp.minimum(j + 1, nb - 1), 0))
    qspec = pl.BlockSpec((None, BLK, D), lambda s, j: (s, j, 0))
    return pl.pallas_call(
        body, name="attn_post", grid=(NS, nb),
        out_shape=[jax.ShapeDtypeStruct((NS, S, D), BF16), jax.ShapeDtypeStruct((NS, S, 2 * KV_W), BF16)],
        in_specs=[qspec, cur, nxt, cur, nxt, cur, cur, cur],
        out_specs=[qspec, pl.BlockSpec((None, BLK, 2 * KV_W), lambda s, j: (s, j, 0))],
        compiler_params=_cparams("parallel", "parallel"),
    )(dqr3, dkc3, dkp3, dvc3, dvp3, cos3, sa3, sb3)


def _in_bwd(segs, w_in_t, x2, dx1, ln_pre):
    T = x2.shape[0]
    tm = min(ROW_TILE, T)
    ns = len(segs)
    widths = [s.shape[1] for s in segs]

    def body(*refs):
        seg_refs = refs[:ns]
        wt_ref, x_ref, dx1_ref, g_ref, gx_ref, glp_ref = refs[ns:]

        @pl.when(pl.program_id(0) == 0)
        def _():
            glp_ref[...] = jnp.zeros_like(glp_ref)

        dh = None
        r0 = 0
        for sref, w in zip(seg_refs, widths):
            part = _dot(sref[...], wt_ref[r0 : r0 + w, :])
            dh = part if dh is None else dh + part
            r0 += w
        xv = x_ref[...]
        r1 = lax.rsqrt(jnp.mean(xv * xv, axis=-1, keepdims=True) + EPS)
        xhat = xv * r1
        glp_ref[...] += _rowsum8(dh * xhat)
        dhg = dh * g_ref[...]
        gx_ref[...] = dx1_ref[...] + r1 * (dhg - xhat * jnp.mean(dhg * xhat, axis=-1, keepdims=True))

    row = pl.BlockSpec((tm, D), lambda i: (i, 0))
    return pl.pallas_call(
        body, name="in_bwd", grid=(T // tm,),
        out_shape=[jax.ShapeDtypeStruct((T, D), F32), jax.ShapeDtypeStruct((8, D), F32)],
        in_specs=[pl.BlockSpec((tm, w), lambda i: (i, 0)) for w in widths] + [_weight_spec((NW, D)), row, row, _weight_spec((1, D))],
        out_specs=[row, pl.BlockSpec((8, D), lambda i: (0, 0))],
        compiler_params=_cparams("arbitrary"),
    )(*segs, w_in_t, x2, dx1, ln_pre)


def _grad_matmul(a, b, name):
    T, M = a.shape
    N = b.shape[1]
    tk = min(512, T)
    nk = T // tk

    def body(a_ref, b_ref, o_ref, acc):
        k = pl.program_id(0)

        @pl.when(k == 0)
        def _():
            acc[...] = jnp.zeros_like(acc)

        acc[...] += _dot_tn(a_ref[...].astype(BF16), b_ref[...])

        @pl.when(k == nk - 1)
        def _():
            o_ref[...] = acc[...].astype(BF16)

    return pl.pallas_call(
        body, name=name, grid=(nk,), out_shape=jax.ShapeDtypeStruct((M, N), BF16),
        in_specs=[pl.BlockSpec((tk, M), lambda k: (k, 0)), pl.BlockSpec((tk, N), lambda k: (k, 0))],
        out_specs=pl.BlockSpec((M, N), lambda k: (0, 0)),
        scratch_shapes=[pltpu.VMEM((M, N), F32)],
        compiler_params=_cparams("arbitrary"),
    )(a, b)


def _pack_small(gw, gvec, glp_pre, glp_post, dsk):
    T = dsk.shape[0]

    def body(gw_ref, gvec_ref, pre_ref, post_ref, dsk_ref, gdw_ref, gs_ref):
        gwf = gw_ref[...].reshape(CONV_HALO, 8, D).sum(axis=1)
        for d in range(N_DEV):
            gdw_ref[d] = gwf[:, 128 * d : 128 * (d + 1)]
        gs_ref[...] = jnp.zeros_like(gs_ref)
        gs_ref[0:1, :] = jnp.sum(pre_ref[...], axis=0, keepdims=True)
        gs_ref[1:2, :] = jnp.sum(post_ref[...], axis=0, keepdims=True)
        gs_ref[2:3, :] = jnp.sum(gvec_ref[16:24, :], axis=0, keepdims=True)
        gs_ref[3:4, :] = jnp.sum(gvec_ref[0:8, :], axis=0, keepdims=True)
        gs_ref[4:5, :] = jnp.sum(gvec_ref[8:16, :], axis=0, keepdims=True)
        gs_ref[5:6, 0:N_HEADS] = jnp.sum(dsk_ref[...], axis=0, keepdims=True)

    return pl.pallas_call(
        body, name="pack_small",
        out_shape=[jax.ShapeDtypeStruct((N_DEV, CONV_HALO, 128), F32), jax.ShapeDtypeStruct((8, D), F32)],
        compiler_params=_cparams(),
    )(gw, gvec, glp_pre, glp_post, dsk)


def _adamw(parts, w, m, v, name):
    R, C = w.shape
    tr = R if R <= 256 else 128

    def body(p_ref, w_ref, m_ref, v_ref, g_ref, d_ref, nm_ref, nv_ref):
        g = p_ref[0].astype(F32)
        for s in range(1, N_DEV):
            g = g + p_ref[s].astype(F32)
        nm = ADAM_B1 * m_ref[...] + (1.0 - ADAM_B1) * g
        nv = ADAM_B2 * v_ref[...] + (1.0 - ADAM_B2) * (g * g)
        m_hat = nm / (1.0 - ADAM_B1 ** ADAM_STEP)
        v_hat = nv / (1.0 - ADAM_B2 ** ADAM_STEP)
        g_ref[...] = g
        d_ref[...] = -ADAM_LR * (m_hat / (jnp.sqrt(v_hat) + ADAM_EPS) + ADAM_WD * w_ref[...])
        nm_ref[...] = nm
        nv_ref[...] = nv

    blk = pl.BlockSpec((tr, C), lambda i: (i, 0))
    return pl.pallas_call(
        body, name=name, grid=(R // tr,), out_shape=[jax.ShapeDtypeStruct((R, C), F32)] * 4,
        in_specs=[pl.BlockSpec((N_DEV, tr, C), lambda i: (0, i, 0)), blk, blk, blk], out_specs=[blk] * 4,
        compiler_params=_cparams("parallel"),
    )(parts, w, m, v)


def kernel(x, p, positions, w_in, ln_pre, ln_post, w_dw, b_dw, conv_ln_g, conv_ln_b, w_pw, sinks, w_br_conv, w_br_attn, w_out, w_ple_gate, w_ple_proj, loss_target, m_w_in, m_ln_pre, m_ln_post, m_w_dw, m_b_dw, m_conv_ln_g, m_conv_ln_b, m_w_pw, m_sinks, m_w_br_conv, m_w_br_attn, m_w_out, m_w_ple_gate, m_w_ple_proj, v_w_in, v_ln_pre, v_ln_post, v_w_dw, v_b_dw, v_conv_ln_g, v_conv_ln_b, v_w_pw, v_sinks, v_w_br_conv, v_w_br_attn, v_w_out, v_w_ple_gate, v_w_ple_proj):
    NS, S, _ = x.shape
    T = NS * S
    x2 = x.reshape(T, D)
    p2 = p.reshape(T, PLE)
    tgt = loss_target.reshape(T, D)
    pos = positions.reshape(T, 1)

    row_sharded = [w_pw[0], w_br_conv[0], w_br_attn[0], w_out[0], w_ple_gate[0]]
    sh_rows = D // N_DEV
    g_in, g_rows, g_pp, g_dw = _all_gather([
        w_in[0].astype(BF16),
        jnp.stack(row_sharded).astype(BF16),
        w_ple_proj[0].astype(BF16),
        jnp.pad(w_dw[0], ((0, CONV_HALO - CONV_K), (0, 0))),
    ])
    w_in_f = _to_perm(g_in.transpose(1, 0, 2).reshape(D, NW))
    w_in_t = w_in_f.T
    full = [g_rows[:, j].reshape(D, D) for j in range(5)]
    w_pw_f, w_brc_f, w_bra_f, w_out_f, w_pg_f = full
    w_pp_f = g_pp.transpose(1, 0, 2).reshape(PLE, D)
    w_dw_f = g_dw.transpose(1, 0, 2).reshape(CONV_HALO, D)

    cos, sa, sb = _rope_tables(pos)
    z2, h = _in_proj(x2, ln_pre, w_in_f)
    z3 = z2.reshape(NS, S, NW)
    c3, cs3 = _conv_fwd(z3, w_dw_f, b_dw, conv_ln_g, conv_ln_b)
    qr, kr = _rope_qk(z2, cos, sa, sb)
    qr3, kr3 = qr.reshape(NS, S, D), kr.reshape(NS, S, KV_W)
    sinks1 = sinks.reshape(N_HEADS)
    o3, lse3 = _attn_fwd(qr3, kr3, z3, sinks1)
    o = o3.reshape(T, D)
    ya0, ya = _branch_a_fwd(cs3.reshape(T, D), z2, w_pw_f, w_brc_f)
    yb, m, mo, x1, gate, pp, dy, loss_blk = _merge_fwd(o, z2, ya, x2, p2, tgt, ln_post, w_bra_f, w_out_f, w_pg_f, w_pp_f)
    loss = lax.psum(loss_blk[0, 0], MESH_AXES)

    dgl, dpp, dx1, dmo, dya, dyb, dgc, dga, glp_post = _merge_bwd(dy, gate, pp, mo, ya, yb, z2, ln_post, w_pg_f.T, w_out_f.T)
    dya0, dcg, ya1, dcs, do, dag, yb0 = _branches_bwd(dya, dyb, ya0, o, z2, w_brc_f.T, w_pw_f.T, w_bra_f.T)
    dzvu3, gw, gvec = _conv_bwd(z3, c3, dcs.reshape(NS, S, D), w_dw_f, conv_ln_g, conv_ln_b)
    dqr3, dkc3, dkp3, dvc3, dvp3, dsk3 = _attn_bwd(qr3, kr3, z3, do.reshape(NS, S, D), o3, lse3, sinks1)
    tab3 = [t.reshape(NS, S, 128) for t in (cos, sa, sb)]
    dq3, dkv3 = _attn_post(dqr3, dkc3, dkp3, dvc3, dvp3, *tab3)
    segs = [dzvu3.reshape(T, 2 * D), dcg, dq3.reshape(T, D), dag, dgc, dga, dkv3.reshape(T, 2 * KV_W)]
    grad_x2, glp_pre = _in_bwd(segs, w_in_t, x2, dx1, ln_pre)

    gp_in = _from_perm(jnp.concatenate([_grad_matmul(h, s, f"grad_w_in_{j}") for j, s in enumerate(segs)], axis=1))
    gp_in = gp_in.reshape(D, N_DEV, NW // N_DEV).transpose(1, 0, 2)
    gp_rows = [_grad_matmul(a, b, nm).reshape(N_DEV, sh_rows, D) for a, b, nm in (
        (cs3.reshape(T, D), dya0, "grad_w_pw"), (ya1, dya, "grad_w_br_conv"), (yb0, dyb, "grad_w_br_attn"),
        (m, dmo, "grad_w_out"), (x1, dgl, "grad_w_ple_gate"))]
    gp_pp = _grad_matmul(p2, dpp, "grad_w_ple_proj").reshape(PLE, N_DEV, D // N_DEV).transpose(1, 0, 2)
    gp_dw, gp_small = _pack_small(gw, gvec, glp_pre, glp_post, dsk3.reshape(T, N_HEADS))

    got = _grad_exchange([gp_in] + gp_rows + [gp_pp, gp_dw], [gp_small])
    r_in, r_rows, r_pp, r_dw, r_small = got[0], got[1:6], got[6], got[7], got[8]

    res = {}
    res["w_in"] = _adamw(r_in, w_in[0], m_w_in[0], v_w_in[0], "adamw_w_in")
    names_rows = ["w_pw", "w_br_conv", "w_br_attn", "w_out", "w_ple_gate"]
    wmv = {"w_pw": (w_pw, m_w_pw, v_w_pw), "w_br_conv": (w_br_conv, m_w_br_conv, v_w_br_conv),
           "w_br_attn": (w_br_attn, m_w_br_attn, v_w_br_attn), "w_out": (w_out, m_w_out, v_w_out),
           "w_ple_gate": (w_ple_gate, m_w_ple_gate, v_w_ple_gate)}
    for nm, parts in zip(names_rows, r_rows):
        w_, m_, v_ = wmv[nm]
        res[nm] = _adamw(parts, w_[0], m_[0], v_[0], "adamw_" + nm)
    res["w_ple_proj"] = _adamw(r_pp, w_ple_proj[0], m_w_ple_proj[0], v_w_ple_proj[0], "adamw_w_ple_proj")
    pad_dw = lambda a: jnp.pad(a[0], ((0, CONV_HALO - CONV_K), (0, 0)))
    res["w_dw"] = [a[:CONV_K] for a in _adamw(r_dw, pad_dw(w_dw), pad_dw(m_w_dw), pad_dw(v_w_dw), "adamw_w_dw")]

    def stack_small(a_pre, a_post, a_b, a_g, a_bb, a_s):
        sk = jnp.pad(a_s, ((0, 0), (0, D - N_HEADS)))
        return jnp.concatenate([a_pre, a_post, a_b, a_g, a_bb, sk, jnp.zeros((2, D), F32)], axis=0)

    small = _adamw(
        r_small, stack_small(ln_pre, ln_post, b_dw, conv_ln_g, conv_ln_b, sinks),
        stack_small(m_ln_pre, m_ln_post, m_b_dw, m_conv_ln_g, m_conv_ln_b, m_sinks),
        stack_small(v_ln_pre, v_ln_post, v_b_dw, v_conv_ln_g, v_conv_ln_b, v_sinks), "adamw_small")
    for j, nm in enumerate(["ln_pre", "ln_post", "b_dw", "conv_ln_g", "conv_ln_b"]):
        res[nm] = [a[j] for a in small]
    res["sinks"] = [a[5, :N_HEADS] for a in small]

    order = ["w_in", "ln_pre", "ln_post", "w_dw", "b_dw", "conv_ln_g", "conv_ln_b", "w_pw", "sinks", "w_br_conv",
             "w_br_attn", "w_out", "w_ple_gate", "w_ple_proj"]
    outs = [loss, grad_x2.reshape(NS, S, D)]
    for kind in range(4):
        outs += [res[nm][kind][None] for nm in order]
    return tuple(outs)
```

```python
import functools

import numpy as np

import jax
import jax.numpy as jnp
from jax import lax
from jax.experimental import pallas as pl
from jax.experimental.pallas import tpu as pltpu

F32 = jnp.float32
BF16 = jnp.bfloat16

D = 1024
N_HEADS = 16
N_KV = 2
HEAD_DIM = 64
GROUP = N_HEADS // N_KV
KV_W = N_KV * HEAD_DIM
CONV_K = 31
CONV_HALO = 32
BLK = 128
ROPE_DIM = 16
ROPE_THETA = 500000.0
EPS = 1e-6
PLE = 256
NW = 7 * D + 2 * KV_W
N_DEV = 8
MESH_AXES = ("x", "y", "c")

CB_VAL, CB_GLU, CB_CGATE, CB_Q, CB_AGATE, CB_GCONV, CB_GATTN = range(7)
CB_K = 7 * D // KV_W
CB_V = CB_K + 1

ADAM_LR, ADAM_B1, ADAM_B2, ADAM_EPS, ADAM_WD, ADAM_STEP = 0.001, 0.9, 0.999, 1e-08, 0.01, 10

VMEM_LIMIT = 56 * 1024 * 1024
ROW_TILE = 256


def _cparams(*sem):
    return pltpu.CompilerParams(dimension_semantics=sem if sem else None, vmem_limit_bytes=VMEM_LIMIT)


def _sig(v):
    return 1.0 / (1.0 + jnp.exp(-v))


def _rowsum8(a):
    return a.reshape(a.shape[0] // 8, 8, a.shape[1]).sum(axis=0)


def _dot(a, b):
    return jnp.dot(a, b, preferred_element_type=F32)


def _dot_nt(a, b):
    return lax.dot_general(a, b, (((1,), (1,)), ((), ())), preferred_element_type=F32)


def _dot_tn(a, b):
    return lax.dot_general(a, b, (((0,), (0,)), ((), ())), preferred_element_type=F32)


def _to_perm(w):
    return jnp.concatenate([w[..., : 4 * D], w[..., 4 * D + 2 * KV_W :], w[..., 4 * D : 4 * D + 2 * KV_W]], axis=-1)


def _from_perm(g):
    return jnp.concatenate([g[..., : 4 * D], g[..., 7 * D :], g[..., 4 * D : 7 * D]], axis=-1)


def _my_place():
    return lax.axis_index("x"), lax.axis_index("y"), lax.axis_index("c")


def _slot(px, py, pc):
    return 4 * px + 2 * py + pc


def _all_gather(shards):
    n = len(shards)

    def body(*refs):
        ins, outs = refs[:n], refs[n : 2 * n]
        send_sems, recv_sems, local_sems = refs[2 * n :]
        x, y, c = _my_place()
        me, sibling = (x, y, c), (x, y, 1 - c)
        chips = [(1 - x, y), (x, 1 - y), (1 - x, 1 - y)]

        def copy(a, k, block, to, src=None):
            rows = outs[a].at[_slot(*block)]
            return pltpu.make_async_remote_copy(
                src_ref=rows if src is None else src, dst_ref=rows, send_sem=send_sems.at[a, k],
                recv_sem=recv_sems.at[a, k], device_id=to, device_id_type=pl.DeviceIdType.MESH)

        mine, first, passed = [], [], []
        for a in range(n):
            cp = pltpu.make_async_copy(ins[a], outs[a].at[_slot(*me)], local_sems.at[a])
            cp.start()
            mine.append(cp)
            fa = [copy(a, 0, me, sibling, src=ins[a])]
            fa += [copy(a, 1 + j, me, (*chip, c), src=ins[a]) for j, chip in enumerate(chips)]
            for cp in fa:
                cp.start()
            first += fa
        for j, chip in enumerate(chips):
            for a in range(n):
                copy(a, 1 + j, (*chip, c), me).wait_recv()
                cp = copy(a, 4 + j, (*chip, c), sibling)
                cp.start()
                passed.append(cp)
        for a in range(n):
            copy(a, 0, sibling, me).wait_recv()
            for j, chip in enumerate(chips):
                copy(a, 4 + j, (*chip, 1 - c), me).wait_recv()
        for cp in first + passed:
            cp.wait_send()
        for cp in mine:
            cp.wait()

    any_spec = pl.BlockSpec(memory_space=pl.ANY)
    return pl.pallas_call(
        body, name="weight_all_gather",
        out_shape=[jax.ShapeDtypeStruct((N_DEV,) + s.shape, s.dtype) for s in shards],
        in_specs=[any_spec] * n, out_specs=[any_spec] * n,
        scratch_shapes=[pltpu.SemaphoreType.DMA((n, 7)), pltpu.SemaphoreType.DMA((n, 7)), pltpu.SemaphoreType.DMA((n,))],
    )(*shards)


class _Exchange:
    def __init__(self, scatter, bcast):
        self.arrs = list(scatter) + list(bcast)
        self.n, self.n_sc = len(self.arrs), len(scatter)
        self.out_shape = [jax.ShapeDtypeStruct(a.shape, a.dtype) for a in scatter]
        self.out_shape += [jax.ShapeDtypeStruct((N_DEV,) + a.shape, a.dtype) for a in bcast]
        self.specs = [pl.BlockSpec(memory_space=pl.ANY)] * self.n
        self.scratch = [pltpu.SemaphoreType.DMA((self.n, 7)), pltpu.SemaphoreType.DMA((self.n, 7)),
                        pltpu.SemaphoreType.DMA((self.n,))]

    def _copies(self, ins, outs, sems):
        send_sems, recv_sems, local_sems = sems
        x, y, c = _my_place()
        me = _slot(x, y, c)
        peers = [(1 - x if k & 4 else x, 1 - y if k & 2 else y, 1 - c if k & 1 else c) for k in range(1, N_DEV)]
        mine, sends, arrivals = [], [], []
        for a in range(self.n):
            src = ins[a].at[me] if a < self.n_sc else ins[a]
            mine.append(pltpu.make_async_copy(src, outs[a].at[me], local_sems.at[a]))
        for k, peer in enumerate(peers):
            for a in range(self.n):
                src = ins[a].at[_slot(*peer)] if a < self.n_sc else ins[a]
                sends.append(pltpu.make_async_remote_copy(
                    src_ref=src, dst_ref=outs[a].at[me], send_sem=send_sems.at[a, k], recv_sem=recv_sems.at[a, k],
                    device_id=peer, device_id_type=pl.DeviceIdType.MESH))
                rows = outs[a].at[_slot(*peer)]
                arrivals.append(pltpu.make_async_remote_copy(
                    src_ref=rows, dst_ref=rows, send_sem=send_sems.at[a, k], recv_sem=recv_sems.at[a, k],
                    device_id=peer, device_id_type=pl.DeviceIdType.MESH))
        return mine, sends, arrivals

    def start(self, ins, outs, sems):
        mine, sends, _ = self._copies(ins, outs, sems)
        for cp in mine + sends:
            cp.start()

    def finish(self, ins, outs, sems):
        mine, sends, arrivals = self._copies(ins, outs, sems)
        for cp in arrivals:
            cp.wait_recv()
        for cp in sends:
            cp.wait_send()
        for cp in mine:
            cp.wait()

    def carried(self, refs_in, refs_out, sems, first, last):
        @pl.when(first)
        def _():
            self.start(refs_in, refs_out, sems)

        @pl.when(last)
        def _():
            self.finish(refs_in, refs_out, sems)

    def alone(self, name):
        n = self.n

        def body(*refs):
            ins, outs, sems = refs[:n], refs[n : 2 * n], refs[2 * n :]
            self.start(ins, outs, sems)
            self.finish(ins, outs, sems)

        return pl.pallas_call(body, name=name, out_shape=self.out_shape, in_specs=self.specs, out_specs=self.specs,
                              scratch_shapes=self.scratch)(*self.arrs)


def _host_split(refs, n_in, n_out, n_scratch, ex):
    k = ex.n if ex is not None else 0
    a = n_in
    b = a + k
    c = b + n_out
    d = c + k
    e = d + n_scratch
    return refs[:a], refs[a:b], refs[b:c], refs[c:d], refs[d:e], refs[e:]


def _rope_tables(pos):
    T = pos.shape[0]
    tm = min(1024, T)
    lane = np.arange(128) % HEAD_DIM
    inv = np.power(np.float32(ROPE_THETA), -np.arange(0, ROPE_DIM, 2, dtype=np.float32) / np.float32(ROPE_DIM)).astype(np.float32)
    half = ROPE_DIM // 2
    invf = np.where(lane < ROPE_DIM, inv[lane % half], 0.0).astype(np.float32)[None, :]
    m_a = (lane < half).astype(np.float32)[None, :]
    m_b = ((lane >= half) & (lane < ROPE_DIM)).astype(np.float32)[None, :]

    def body(pos_ref, invf_ref, ma_ref, mb_ref, cos_ref, sa_ref, sb_ref):
        ang = pos_ref[...].astype(F32) * invf_ref[...]
        sn = jnp.sin(ang)
        cos_ref[...] = jnp.cos(ang)
        sa_ref[...] = -sn * ma_ref[...]
        sb_ref[...] = sn * mb_ref[...]

    row = pl.BlockSpec((tm, 128), lambda i: (i, 0))
    cst = pl.BlockSpec((1, 128), lambda i: (0, 0))
    return pl.pallas_call(
        body, name="rope_tables", grid=(T // tm,), out_shape=[jax.ShapeDtypeStruct((T, 128), F32)] * 3,
        in_specs=[pl.BlockSpec((tm, 1), lambda i: (i, 0)), cst, cst, cst], out_specs=[row] * 3,
        compiler_params=_cparams("parallel"),
    )(pos, jnp.asarray(invf), jnp.asarray(m_a), jnp.asarray(m_b))


def _rope(t, cos, sa, sb, sign=1.0):
    parts = []
    for i in range(t.shape[1] // 128):
        ti = t[:, 128 * i : 128 * (i + 1)]
        up = pltpu.roll(ti, 128 - ROPE_DIM // 2, 1)
        dn = pltpu.roll(ti, ROPE_DIM // 2, 1)
        parts.append(ti * cos + sign * (up * sa + dn * sb))
    return parts[0] if len(parts) == 1 else jnp.concatenate(parts, axis=-1)


def _in_proj(x2, ln_pre, w_in, ex):
    T = x2.shape[0]
    tm = min(512, T)
    tn = NW // 2
    chunk = 512
    ni, nj = T // tm, NW // tn

    def body(*refs):
        (x_ref, g_ref, w_ref), ex_in, (z_ref, h_ref), ex_out, (hs,), ex_sems = _host_split(refs, 3, 2, 1, ex)
        i, j = pl.program_id(0), pl.program_id(1)
        ex.carried(ex_in, ex_out, ex_sems, (i == 0) & (j == 0), (i == ni - 1) & (j == nj - 1))

        @pl.when(j == 0)
        def _():
            xv = x_ref[...]
            r = lax.rsqrt(jnp.mean(xv * xv, axis=-1, keepdims=True) + EPS)
            h = (xv * r * g_ref[...]).astype(BF16)
            hs[...] = h
            h_ref[...] = h

        for c0 in range(0, tn, chunk):
            cw = min(chunk, tn - c0)
            z_ref[:, c0 : c0 + cw] = _dot(hs[...], w_ref[:, c0 : c0 + cw]).astype(BF16)

    return pl.pallas_call(
        body, name="in_proj", grid=(ni, nj),
        out_shape=[jax.ShapeDtypeStruct((T, NW), BF16), jax.ShapeDtypeStruct((T, D), BF16)] + ex.out_shape,
        in_specs=[pl.BlockSpec((tm, D), lambda i, j: (i, 0)), pl.BlockSpec((1, D), lambda i, j: (0, 0)),
                  pl.BlockSpec((D, tn), lambda i, j: (0, j))] + ex.specs,
        out_specs=[pl.BlockSpec((tm, tn), lambda i, j: (i, j)), pl.BlockSpec((tm, D), lambda i, j: (i, 0))] + ex.specs,
        scratch_shapes=[pltpu.VMEM((tm, D), BF16)] + ex.scratch,
        compiler_params=_cparams("arbitrary", "arbitrary"),
    )(x2, ln_pre, w_in, *ex.arrs)


def _conv_tiles(S):
    tm = min(ROW_TILE, S)
    return tm, S // tm, tm // CONV_HALO


CONV_ROWS_FWD = 32
CONV_ROWS = 16


def _fill_shifted(sh, rows):
    for b in range(1, 8):
        sh[b, 0:rows, :] = sh[0, b : b + rows, :]


def _conv_fwd(z3, w_dw, b_dw, ln_g, ln_b):
    NS, S, _ = z3.shape
    tm, nt, r = _conv_tiles(S)

    def body(val_ref, glu_ref, hval_ref, hglu_ref, w_ref, b_ref, g_ref, bb_ref, c_ref, cs_ref, ush, cbuf):
        i = pl.program_id(1)
        ush[0, CONV_HALO:, :] = val_ref[...].astype(F32) * _sig(glu_ref[...].astype(F32))
        uh = hval_ref[...].astype(F32) * _sig(hglu_ref[...].astype(F32))
        ush[0, 0:CONV_HALO, :] = jnp.where(i > 0, uh, 0.0)
        _fill_shifted(ush, tm + CONV_HALO - 8)
        for r0 in range(0, tm, CONV_ROWS_FWD):
            acc = jnp.zeros((CONV_ROWS_FWD, D), F32)
            for k in range(CONV_K):
                a, b = divmod(CONV_HALO - (CONV_K - 1) + k, 8)
                acc = acc + w_ref[k : k + 1, :] * ush[b, r0 + 8 * a : r0 + 8 * a + CONV_ROWS_FWD, :]
            cbuf[r0 : r0 + CONV_ROWS_FWD, :] = acc + b_ref[...]
        cv = cbuf[...]
        mu = jnp.mean(cv, axis=-1, keepdims=True)
        xc = cv - mu
        var = jnp.mean(xc * xc, axis=-1, keepdims=True)
        cl = xc * lax.rsqrt(var + EPS) * g_ref[...] + bb_ref[...]
        c_ref[...] = cv.astype(BF16)
        cs_ref[...] = (cl * _sig(cl)).astype(BF16)

    def cur(cb):
        return pl.BlockSpec((None, tm, D), lambda s, i: (s, i, cb))

    def halo(cb):
        return pl.BlockSpec((None, CONV_HALO, D), lambda s, i: (s, jnp.maximum(i * r - 1, 0), cb))

    vec = pl.BlockSpec((1, D), lambda s, i: (0, 0))
    out = pl.BlockSpec((None, tm, D), lambda s, i: (s, i, 0))
    return pl.pallas_call(
        body, name="conv_fwd", grid=(NS, nt),
        out_shape=[jax.ShapeDtypeStruct((NS, S, D), BF16)] * 2,
        in_specs=[cur(CB_VAL), cur(CB_GLU), halo(CB_VAL), halo(CB_GLU),
                  pl.BlockSpec((CONV_HALO, D), lambda s, i: (0, 0)), vec, vec, vec],
        out_specs=[out, out],
        scratch_shapes=[pltpu.VMEM((8, tm + CONV_HALO, D), F32), pltpu.VMEM((tm, D), F32)],
        compiler_params=_cparams("parallel", "parallel"),
    )(z3, z3, z3, z3, w_dw, b_dw, ln_g, ln_b)


def _rope_qk(z2, cos, sa, sb):
    T = z2.shape[0]
    tm = min(ROW_TILE, T)

    def body(q_ref, k_ref, cos_ref, sa_ref, sb_ref, qr_ref, kr_ref):
        tabs = cos_ref[...], sa_ref[...], sb_ref[...]
        qr_ref[...] = (_rope(q_ref[...].astype(F32), *tabs) * (HEAD_DIM ** -0.5)).astype(BF16)
        kr_ref[...] = _rope(k_ref[...].astype(F32), *tabs).astype(BF16)

    tab = pl.BlockSpec((tm, 128), lambda i: (i, 0))
    return pl.pallas_call(
        body, name="rope_qk", grid=(T // tm,),
        out_shape=[jax.ShapeDtypeStruct((T, D), BF16), jax.ShapeDtypeStruct((T, KV_W), BF16)],
        in_specs=[pl.BlockSpec((tm, D), lambda i: (i, CB_Q)), pl.BlockSpec((tm, KV_W), lambda i: (i, CB_K)), tab, tab, tab],
        out_specs=[pl.BlockSpec((tm, D), lambda i: (i, 0)), pl.BlockSpec((tm, KV_W), lambda i: (i, 0))],
        compiler_params=_cparams("parallel"),
    )(z2, z2, cos, sa, sb)


def _attn_mask(has_prev):
    qi = lax.broadcasted_iota(jnp.int32, (BLK, 2 * BLK), 0)
    kj = lax.broadcasted_iota(jnp.int32, (BLK, 2 * BLK), 1)
    first_key = jnp.where(has_prev, 0, BLK)
    return (kj > qi) & (kj <= qi + BLK) & (kj >= first_key)


NEG_BIG = -1e30


def _attn_specs():
    q = pl.BlockSpec((None, BLK, D), lambda s, n: (s, n, 0))
    kc = pl.BlockSpec((None, BLK, KV_W), lambda s, n: (s, n, 0))
    kp = pl.BlockSpec((None, BLK, KV_W), lambda s, n: (s, jnp.maximum(n - 1, 0), 0))
    vc = pl.BlockSpec((None, BLK, KV_W), lambda s, n: (s, n, CB_V))
    vp = pl.BlockSpec((None, BLK, KV_W), lambda s, n: (s, jnp.maximum(n - 1, 0), CB_V))
    sink = pl.BlockSpec(memory_space=pltpu.SMEM)
    return sink, q, kc, kp, vc, vp


def _attn_fwd(qr3, kr3, z3, sinks):
    NS, S, _ = qr3.shape

    def body(sink_ref, q_ref, kc_ref, kp_ref, vc_ref, vp_ref, o_ref, lse_ref):
        has_prev = pl.program_id(1) > 0
        kcat = jnp.concatenate([jnp.where(has_prev, kp_ref[...], jnp.zeros(kp_ref.shape, kp_ref.dtype)), kc_ref[...]], axis=0)
        vcat = jnp.concatenate([jnp.where(has_prev, vp_ref[...], jnp.zeros(vp_ref.shape, vp_ref.dtype)), vc_ref[...]], axis=0)
        mask = _attn_mask(has_prev)
        outs = []
        for h in range(N_HEADS):
            g = h // GROUP
            qh = q_ref[:, h * HEAD_DIM : (h + 1) * HEAD_DIM]
            kg = kcat[:, g * HEAD_DIM : (g + 1) * HEAD_DIM]
            vg = vcat[:, g * HEAD_DIM : (g + 1) * HEAD_DIM]
            s = jnp.where(mask, _dot_nt(qh, kg), NEG_BIG)
            sk = sink_ref[h]
            mx = jnp.maximum(jnp.max(s, axis=-1, keepdims=True), sk)
            e = jnp.exp(s - mx)
            den = jnp.sum(e, axis=-1, keepdims=True) + jnp.exp(sk - mx)
            outs.append(_dot((e / den).astype(BF16), vg))
            lse_ref[:, h : h + 1] = mx + jnp.log(den)
        o_ref[...] = jnp.concatenate(outs, axis=-1).astype(BF16)

    return pl.pallas_call(
        body, name="attn_fwd", grid=(NS, S // BLK),
        out_shape=[jax.ShapeDtypeStruct((NS, S, D), BF16), jax.ShapeDtypeStruct((NS, S, N_HEADS), F32)],
        in_specs=list(_attn_specs()),
        out_specs=[pl.BlockSpec((None, BLK, D), lambda s, n: (s, n, 0)),
                   pl.BlockSpec((None, BLK, N_HEADS), lambda s, n: (s, n, 0))],
        compiler_params=_cparams("parallel", "parallel"),
    )(sinks, qr3, kr3, kr3, z3, z3)


def _weight_spec(shape):
    return pl.BlockSpec(shape, lambda i: (0,) * len(shape), pipeline_mode=pl.Buffered(1))


def _branch_a_fwd(cs, z2, w_pw, w_brc):
    T = cs.shape[0]
    tm = min(ROW_TILE, T)

    def body(cs_ref, cg_ref, wpw_ref, wbr_ref, ya0_ref, ya_ref):
        ya0 = _dot(cs_ref[...], wpw_ref[...])
        cg = cg_ref[...].astype(F32)
        ya0_ref[...] = ya0.astype(BF16)
        ya_ref[...] = _dot((ya0 * (cg * _sig(cg))).astype(BF16), wbr_ref[...]).astype(BF16)

    row = pl.BlockSpec((tm, D), lambda i: (i, 0))
    return pl.pallas_call(
        body, name="branch_a_fwd", grid=(T // tm,),
        out_shape=[jax.ShapeDtypeStruct((T, D), BF16)] * 2,
        in_specs=[row, pl.BlockSpec((tm, D), lambda i: (i, CB_CGATE)), _weight_spec((D, D)), _weight_spec((D, D))],
        out_specs=[row, row],
        compiler_params=_cparams("parallel"),
    )(cs, z2, w_pw, w_brc)


def _merge_fwd(o, z2, ya, x2, p2, tgt, ln_post, w_bra, w_out, w_pg, w_pp):
    T = o.shape[0]
    tm = min(ROW_TILE, T)
    nt = T // tm

    def body(o_ref, ag_ref, gc_ref, ga_ref, ya_ref, x_ref, p_ref, t_ref, g_ref, wbra_ref, wout_ref, wpg_ref, wpp_ref,
             yb_ref, m_ref, mo_ref, x1_ref, gate_ref, pp_ref, dy_ref, loss_ref, lacc):
        i = pl.program_id(0)

        @pl.when(i == 0)
        def _():
            lacc[...] = jnp.zeros_like(lacc)

        ag = ag_ref[...].astype(F32)
        yb = _dot((o_ref[...].astype(F32) * (ag * _sig(ag))).astype(BF16), wbra_ref[...])
        m = _sig(gc_ref[...].astype(F32)) * ya_ref[...].astype(F32) + _sig(ga_ref[...].astype(F32)) * yb
        mb = m.astype(BF16)
        mo = _dot(mb, wout_ref[...])
        r2 = lax.rsqrt(jnp.mean(mo * mo, axis=-1, keepdims=True) + EPS)
        x1 = x_ref[...] + mo * r2 * g_ref[...]
        x1b = x1.astype(BF16)
        gate = _sig(_dot(x1b, wpg_ref[...]))
        pp = _dot(p_ref[...].astype(BF16), wpp_ref[...])
        e = x1 + gate * pp - t_ref[...]
        yb_ref[...] = yb.astype(BF16)
        m_ref[...] = mb
        mo_ref[...] = mo
        x1_ref[...] = x1b
        gate_ref[...] = gate.astype(BF16)
        pp_ref[...] = pp.astype(BF16)
        dy_ref[...] = e * (1.0 / D)
        lacc[...] += _rowsum8(e * e)

        @pl.when(i == nt - 1)
        def _():
            loss_ref[...] = jnp.full(loss_ref.shape, jnp.sum(lacc[...]) * (0.5 / D), F32)

    row = pl.BlockSpec((tm, D), lambda i: (i, 0))

    def zcol(cb):
        return pl.BlockSpec((tm, D), lambda i: (i, cb))

    bf = jax.ShapeDtypeStruct((T, D), BF16)
    f32 = jax.ShapeDtypeStruct((T, D), F32)
    return pl.pallas_call(
        body, name="merge_fwd", grid=(nt,),
        out_shape=[bf, bf, f32, bf, bf, bf, f32, jax.ShapeDtypeStruct((8, 128), F32)],
        in_specs=[row, zcol(CB_AGATE), zcol(CB_GCONV), zcol(CB_GATTN), row, row, pl.BlockSpec((tm, PLE), lambda i: (i, 0)), row,
                  _weight_spec((1, D)), _weight_spec((D, D)), _weight_spec((D, D)), _weight_spec((D, D)), _weight_spec((PLE, D))],
        out_specs=[row] * 7 + [pl.BlockSpec((8, 128), lambda i: (0, 0))],
        scratch_shapes=[pltpu.VMEM((8, D), F32)],
        compiler_params=_cparams("arbitrary"),
    )(o, z2, z2, z2, ya, x2, p2, tgt, ln_post, w_bra, w_out, w_pg, w_pp)


def _merge_bwd(dy, gate, pp, mo, ya, yb, z2, ln_post, w_pg_t, w_out_t):
    T = dy.shape[0]
    tm = min(ROW_TILE, T)

    def body(dy_ref, gate_ref, pp_ref, mo_ref, ya_ref, yb_ref, gc_ref, ga_ref, g_ref, wpgt_ref, woutt_ref,
             dgl_ref, dpp_ref, dx1_ref, dmo_ref, dya_ref, dyb_ref, dgc_ref, dga_ref, glp_ref):
        @pl.when(pl.program_id(0) == 0)
        def _():
            glp_ref[...] = jnp.zeros_like(glp_ref)

        dyv = dy_ref[...]
        gate = gate_ref[...].astype(F32)
        dgl = (dyv * pp_ref[...].astype(F32) * gate * (1.0 - gate)).astype(BF16)
        dgl_ref[...] = dgl
        dpp_ref[...] = (dyv * gate).astype(BF16)
        dx1 = dyv + _dot(dgl, wpgt_ref[...])
        dx1_ref[...] = dx1
        mo = mo_ref[...]
        r2 = lax.rsqrt(jnp.mean(mo * mo, axis=-1, keepdims=True) + EPS)
        nrm = mo * r2
        glp_ref[...] += _rowsum8(dx1 * nrm)
        dn = dx1 * g_ref[...]
        dmo = (r2 * (dn - nrm * jnp.mean(dn * nrm, axis=-1, keepdims=True))).astype(BF16)
        dmo_ref[...] = dmo
        dm = _dot(dmo, woutt_ref[...])
        sg = _sig(gc_ref[...].astype(F32))
        sa = _sig(ga_ref[...].astype(F32))
        dya_ref[...] = (dm * sg).astype(BF16)
        dyb_ref[...] = (dm * sa).astype(BF16)
        dgc_ref[...] = (dm * ya_ref[...].astype(F32) * sg * (1.0 - sg)).astype(BF16)
        dga_ref[...] = (dm * yb_ref[...].astype(F32) * sa * (1.0 - sa)).astype(BF16)

    row = pl.BlockSpec((tm, D), lambda i: (i, 0))
    bf = jax.ShapeDtypeStruct((T, D), BF16)
    return pl.pallas_call(
        body, name="merge_bwd", grid=(T // tm,),
        out_shape=[bf, bf, jax.ShapeDtypeStruct((T, D), F32), bf, bf, bf, bf, bf, jax.ShapeDtypeStruct((8, D), F32)],
        in_specs=[row] * 6 + [pl.BlockSpec((tm, D), lambda i: (i, CB_GCONV)), pl.BlockSpec((tm, D), lambda i: (i, CB_GATTN)),
                              _weight_spec((1, D)), _weight_spec((D, D)), _weight_spec((D, D))],
        out_specs=[row] * 8 + [pl.BlockSpec((8, D), lambda i: (0, 0))],
        compiler_params=_cparams("arbitrary"),
    )(dy, gate, pp, mo, ya, yb, z2, z2, ln_post, w_pg_t, w_out_t)


def _dsilu(v, s):
    return s * (1.0 + v * (1.0 - s))


def _branches_bwd(dya, dyb, ya0, o, z2, w_brc_t, w_pw_t, w_bra_t):
    T = dya.shape[0]
    tm = min(ROW_TILE, T)

    def body(dya_ref, dyb_ref, ya0_ref, o_ref, cg_ref, ag_ref, wbrct_ref, wpwt_ref, wbrat_ref,
             dya0_ref, dcg_ref, ya1_ref, dcs_ref, do_ref, dag_ref, yb0_ref):
        cg = cg_ref[...].astype(F32)
        s = _sig(cg)
        silu = cg * s
        ya0 = ya0_ref[...].astype(F32)
        dya1 = _dot(dya_ref[...], wbrct_ref[...])
        dya0 = (dya1 * silu).astype(BF16)
        dya0_ref[...] = dya0
        dcg_ref[...] = (dya1 * ya0 * _dsilu(cg, s)).astype(BF16)
        ya1_ref[...] = (ya0 * silu).astype(BF16)
        dcs_ref[...] = _dot(dya0, wpwt_ref[...]).astype(BF16)
        ag = ag_ref[...].astype(F32)
        sa = _sig(ag)
        silua = ag * sa
        ov = o_ref[...].astype(F32)
        dyb0 = _dot(dyb_ref[...], wbrat_ref[...])
        do_ref[...] = (dyb0 * silua).astype(BF16)
        dag_ref[...] = (dyb0 * ov * _dsilu(ag, sa)).astype(BF16)
        yb0_ref[...] = (ov * silua).astype(BF16)

    row = pl.BlockSpec((tm, D), lambda i: (i, 0))
    return pl.pallas_call(
        body, name="branches_bwd", grid=(T // tm,),
        out_shape=[jax.ShapeDtypeStruct((T, D), BF16)] * 7,
        in_specs=[row] * 4 + [pl.BlockSpec((tm, D), lambda i: (i, CB_CGATE)), pl.BlockSpec((tm, D), lambda i: (i, CB_AGATE)),
                              _weight_spec((D, D)), _weight_spec((D, D)), _weight_spec((D, D))],
        out_specs=[row] * 7,
        compiler_params=_cparams("parallel"),
    )(dya, dyb, ya0, o, z2, z2, w_brc_t, w_pw_t, w_bra_t)


def _conv_bwd(z3, c3, dcs3, w_dw, ln_g, ln_b, ex):
    NS, S, _ = z3.shape
    tm, nt, r = _conv_tiles(S)

    def body(*refs):
        ins, ex_in, outs, ex_out, scratch, ex_sems = _host_split(refs, 9, 3, 3, ex)
        val_ref, glu_ref, c_ref, dcs_ref, hc_ref, hdcs_ref, w_ref, g_ref, bb_ref = ins
        dz_ref, gw_ref, gvec_ref = outs
        dsh, ubuf, dubuf = scratch
        i = pl.program_id(1)
        first = (pl.program_id(0) == 0) & (i == 0)
        ex.carried(ex_in, ex_out, ex_sems, first, (pl.program_id(0) == NS - 1) & (i == nt - 1))

        @pl.when(first)
        def _():
            gw_ref[...] = jnp.zeros_like(gw_ref)
            gvec_ref[...] = jnp.zeros_like(gvec_ref)

        val = val_ref[...].astype(F32)
        sg = _sig(glu_ref[...].astype(F32))
        ubuf[...] = val * sg

        def ln_bwd(cv, dcs):
            cv = cv.astype(F32)
            mu = jnp.mean(cv, axis=-1, keepdims=True)
            xc = cv - mu
            rstd = lax.rsqrt(jnp.mean(xc * xc, axis=-1, keepdims=True) + EPS)
            xhat = xc * rstd
            cl = xhat * g_ref[...] + bb_ref[...]
            s = _sig(cl)
            dcl = dcs.astype(F32) * _dsilu(cl, s)
            dxh = dcl * g_ref[...]
            dc = rstd * (dxh - jnp.mean(dxh, axis=-1, keepdims=True) - xhat * jnp.mean(dxh * xhat, axis=-1, keepdims=True))
            return dc, dcl, xhat

        dc, dcl, xhat = ln_bwd(c_ref[...], dcs_ref[...])
        dsh[0, 0:tm, :] = dc
        dch, _, _ = ln_bwd(hc_ref[...], hdcs_ref[...])
        dsh[0, tm:, :] = jnp.where(i < nt - 1, dch, 0.0)
        gvec_ref[0:8, :] += _rowsum8(dcl * xhat)
        gvec_ref[8:16, :] += _rowsum8(dcl)
        gvec_ref[16:24, :] += _rowsum8(dc)
        _fill_shifted(dsh, tm + CONV_HALO - 8)

        def dc_ahead(r0, k):
            a, b = divmod(CONV_K - 1 - k, 8)
            return dsh[b, r0 + 8 * a : r0 + 8 * a + CONV_ROWS, :]

        for r0 in range(0, tm, CONV_ROWS):
            acc = jnp.zeros((CONV_ROWS, D), F32)
            for k in range(CONV_K):
                acc = acc + w_ref[k : k + 1, :] * dc_ahead(r0, k)
            dubuf[r0 : r0 + CONV_ROWS, :] = acc
        for r0 in range(0, tm, CONV_ROWS):
            ur = ubuf[r0 : r0 + CONV_ROWS, :]
            for k in range(CONV_K):
                gw_ref[8 * k : 8 * k + 8, :] += _rowsum8(ur * dc_ahead(r0, k))
        du = dubuf[...]
        dz_ref[:, 0:D] = (du * sg).astype(BF16)
        dz_ref[:, D:] = (du * val * sg * (1.0 - sg)).astype(BF16)

    def cur(cb):
        return pl.BlockSpec((None, tm, D), lambda s, i: (s, i, cb))

    nxt = pl.BlockSpec((None, CONV_HALO, D), lambda s, i: (s, jnp.minimum((i + 1) * r, S // CONV_HALO - 1), 0))
    vec = pl.BlockSpec((1, D), lambda s, i: (0, 0))
    return pl.pallas_call(
        body, name="conv_bwd", grid=(NS, nt),
        out_shape=[jax.ShapeDtypeStruct((NS, S, 2 * D), BF16), jax.ShapeDtypeStruct((CONV_HALO * 8, D), F32),
                   jax.ShapeDtypeStruct((24, D), F32)] + ex.out_shape,
        in_specs=[cur(CB_VAL), cur(CB_GLU), cur(0), cur(0), nxt, nxt,
                  pl.BlockSpec((CONV_HALO, D), lambda s, i: (0, 0)), vec, vec] + ex.specs,
        out_specs=[pl.BlockSpec((None, tm, 2 * D), lambda s, i: (s, i, 0)),
                   pl.BlockSpec((CONV_HALO * 8, D), lambda s, i: (0, 0)), pl.BlockSpec((24, D), lambda s, i: (0, 0))] + ex.specs,
        scratch_shapes=[pltpu.VMEM((8, tm + CONV_HALO, D), F32), pltpu.VMEM((tm, D), F32), pltpu.VMEM((tm, D), F32)] + ex.scratch,
        compiler_params=_cparams("arbitrary", "arbitrary"),
    )(z3, z3, c3, dcs3, c3, dcs3, w_dw, ln_g, ln_b, *ex.arrs)


def _attn_bwd(qr3, kr3, z3, do3, o3, lse3, sinks):
    NS, S, _ = qr3.shape

    def body(sink_ref, q_ref, kc_ref, kp_ref, vc_ref, vp_ref, do_ref, o_ref, lse_ref,
             dq_ref, dkc_ref, dkp_ref, dvc_ref, dvp_ref, dsk_ref):
        has_prev = pl.program_id(1) > 0
        kcat = jnp.concatenate([jnp.where(has_prev, kp_ref[...], jnp.zeros(kp_ref.shape, kp_ref.dtype)), kc_ref[...]], axis=0)
        vcat = jnp.concatenate([jnp.where(has_prev, vp_ref[...], jnp.zeros(vp_ref.shape, vp_ref.dtype)), vc_ref[...]], axis=0)
        mask = _attn_mask(has_prev)
        dqs, dks, dvs = [], [], []
        for g in range(N_KV):
            kg = kcat[:, g * HEAD_DIM : (g + 1) * HEAD_DIM]
            vg = vcat[:, g * HEAD_DIM : (g + 1) * HEAD_DIM]
            dk = jnp.zeros((2 * BLK, HEAD_DIM), F32)
            dv = jnp.zeros((2 * BLK, HEAD_DIM), F32)
            for h in range(g * GROUP, (g + 1) * GROUP):
                cols = slice(h * HEAD_DIM, (h + 1) * HEAD_DIM)
                qh = q_ref[:, cols]
                doh = do_ref[:, cols]
                lse = lse_ref[:, h : h + 1]
                p = jnp.exp(jnp.where(mask, _dot_nt(qh, kg), NEG_BIG) - lse)
                delta = jnp.sum(doh.astype(F32) * o_ref[:, cols].astype(F32), axis=-1, keepdims=True)
                ds = (p * (_dot_nt(doh, vg) - delta)).astype(BF16)
                dqs.append(_dot(ds, kg))
                dk = dk + _dot_tn(ds, qh)
                dv = dv + _dot_tn(p.astype(BF16), doh)
                dsk_ref[:, h : h + 1] = -jnp.exp(sink_ref[h] - lse) * delta
            dks.append(dk)
            dvs.append(dv)
        dq_ref[...] = jnp.concatenate(dqs, axis=-1)
        dk = jnp.concatenate(dks, axis=-1)
        dv = jnp.concatenate(dvs, axis=-1)
        dkp_ref[...] = dk[0:BLK]
        dkc_ref[...] = dk[BLK:]
        dvp_ref[...] = dv[0:BLK]
        dvc_ref[...] = dv[BLK:]

    qspec = pl.BlockSpec((None, BLK, D), lambda s, n: (s, n, 0))
    kvspec = pl.BlockSpec((None, BLK, KV_W), lambda s, n: (s, n, 0))
    hspec = pl.BlockSpec((None, BLK, N_HEADS), lambda s, n: (s, n, 0))
    kv = jax.ShapeDtypeStruct((NS, S, KV_W), F32)
    return pl.pallas_call(
        body, name="attn_bwd", grid=(NS, S // BLK),
        out_shape=[jax.ShapeDtypeStruct((NS, S, D), F32), kv, kv, kv, kv, jax.ShapeDtypeStruct((NS, S, N_HEADS), F32)],
        in_specs=list(_attn_specs()) + [qspec, qspec, hspec],
        out_specs=[qspec, kvspec, kvspec, kvspec, kvspec, hspec],
        compiler_params=_cparams("parallel", "parallel"),
    )(sinks, qr3, kr3, kr3, z3, z3, do3, o3, lse3)


def _attn_post(dqr3, dkc3, dkp3, dvc3, dvp3, cos3, sa3, sb3):
    NS, S, _ = dqr3.shape
    nb = S // BLK

    def body(dq_ref, dkc_ref, dkn_ref, dvc_ref, dvn_ref, cos_ref, sa_ref, sb_ref, dqo_ref, dkv_ref):
        has_next = pl.program_id(1) < nb - 1
        tabs = cos_ref[...], sa_ref[...], sb_ref[...]
        dqo_ref[...] = (_rope(dq_ref[...], *tabs, sign=-1.0) * (HEAD_DIM ** -0.5)).astype(BF16)
        dk = dkc_ref[...] + jnp.where(has_next, dkn_ref[...], 0.0)
        dkv_ref[:, 0:KV_W] = _rope(dk, *tabs, sign=-1.0).astype(BF16)
        dkv_ref[:, KV_W:] = (dvc_ref[...] + jnp.where(has_next, dvn_ref[...], 0.0)).astype(BF16)

    cur = pl.BlockSpec((None, BLK, KV_W), lambda s, j: (s, j, 0))
    nxt = pl.BlockSpec((None, BLK, KV_W), lambda s, j: (s, jnp.minimum(j + 1, nb - 1), 0))
    qspec = pl.BlockSpec((None, BLK, D), lambda s, j: (s, j, 0))
    return pl.pallas_call(
        body, name="attn_post", grid=(NS, nb),
        out_shape=[jax.ShapeDtypeStruct((NS, S, D), BF16), jax.ShapeDtypeStruct((NS, S, 2 * KV_W), BF16)],
        in_specs=[qspec, cur, nxt, cur, nxt, cur, cur, cur],
        out_specs=[qspec, pl.BlockSpec((None, BLK, 2 * KV_W), lambda s, j: (s, j, 0))],
        compiler_params=_cparams("parallel", "parallel"),
    )(dqr3, dkc3, dkp3, dvc3, dvp3, cos3, sa3, sb3)


def _in_bwd(segs, w_in_t, x2, dx1, ln_pre, ex):
    T = x2.shape[0]
    tm = min(ROW_TILE, T)
    nt = T // tm
    ns = len(segs)
    widths = [s.shape[1] for s in segs]

    def body(*refs):
        ins, ex_in, (gx_ref, glp_ref), ex_out, _, ex_sems = _host_split(refs, ns + 4, 2, 0, ex)
        seg_refs = ins[:ns]
        wt_ref, x_ref, dx1_ref, g_ref = ins[ns:]
        i = pl.program_id(0)
        ex.carried(ex_in, ex_out, ex_sems, i == 0, i == nt - 1)

        @pl.when(i == 0)
        def _():
            glp_ref[...] = jnp.zeros_like(glp_ref)

        dh = None
        r0 = 0
        for sref, w in zip(seg_refs, widths):
            part = _dot(sref[...], wt_ref[r0 : r0 + w, :])
            dh = part if dh is None else dh + part
            r0 += w
        xv = x_ref[...]
        r1 = lax.rsqrt(jnp.mean(xv * xv, axis=-1, keepdims=True) + EPS)
        xhat = xv * r1
        glp_ref[...] += _rowsum8(dh * xhat)
        dhg = dh * g_ref[...]
        gx_ref[...] = dx1_ref[...] + r1 * (dhg - xhat * jnp.mean(dhg * xhat, axis=-1, keepdims=True))

    row = pl.BlockSpec((tm, D), lambda i: (i, 0))
    return pl.pallas_call(
        body, name="in_bwd", grid=(nt,),
        out_shape=[jax.ShapeDtypeStruct((T, D), F32), jax.ShapeDtypeStruct((8, D), F32)] + ex.out_shape,
        in_specs=[pl.BlockSpec((tm, w), lambda i: (i, 0)) for w in widths]
        + [_weight_spec((NW, D)), row, row, _weight_spec((1, D))] + ex.specs,
        out_specs=[row, pl.BlockSpec((8, D), lambda i: (0, 0))] + ex.specs,
        scratch_shapes=ex.scratch,
        compiler_params=_cparams("arbitrary"),
    )(*segs, w_in_t, x2, dx1, ln_pre, *ex.arrs)


def _grad_matmul(a, b, name):
    T, M = a.shape
    N = b.shape[1]
    tk = min(512, T)
    nk = T // tk

    def body(a_ref, b_ref, o_ref, acc):
        k = pl.program_id(0)

        @pl.when(k == 0)
        def _():
            acc[...] = jnp.zeros_like(acc)

        acc[...] += _dot_tn(a_ref[...].astype(BF16), b_ref[...])

        @pl.when(k == nk - 1)
        def _():
            o_ref[...] = acc[...].astype(BF16)

    return pl.pallas_call(
        body, name=name, grid=(nk,), out_shape=jax.ShapeDtypeStruct((M, N), BF16),
        in_specs=[pl.BlockSpec((tk, M), lambda k: (k, 0)), pl.BlockSpec((tk, N), lambda k: (k, 0))],
        out_specs=pl.BlockSpec((M, N), lambda k: (0, 0)),
        scratch_shapes=[pltpu.VMEM((M, N), F32)],
        compiler_params=_cparams("arbitrary"),
    )(a, b)


def _pack_small(gw, gvec, glp_pre, glp_post, dsk):
    T = dsk.shape[0]

    def body(gw_ref, gvec_ref, pre_ref, post_ref, dsk_ref, gdw_ref, gs_ref):
        gwf = gw_ref[...].reshape(CONV_HALO, 8, D).sum(axis=1)
        for d in range(N_DEV):
            gdw_ref[d] = gwf[:, 128 * d : 128 * (d + 1)]
        gs_ref[...] = jnp.zeros_like(gs_ref)
        gs_ref[0:1, :] = jnp.sum(pre_ref[...], axis=0, keepdims=True)
        gs_ref[1:2, :] = jnp.sum(post_ref[...], axis=0, keepdims=True)
        gs_ref[2:3, :] = jnp.sum(gvec_ref[16:24, :], axis=0, keepdims=True)
        gs_ref[3:4, :] = jnp.sum(gvec_ref[0:8, :], axis=0, keepdims=True)
        gs_ref[4:5, :] = jnp.sum(gvec_ref[8:16, :], axis=0, keepdims=True)
        gs_ref[5:6, 0:N_HEADS] = jnp.sum(dsk_ref[...], axis=0, keepdims=True)

    return pl.pallas_call(
        body, name="pack_small",
        out_shape=[jax.ShapeDtypeStruct((N_DEV, CONV_HALO, 128), F32), jax.ShapeDtypeStruct((8, D), F32)],
        compiler_params=_cparams(),
    )(gw, gvec, glp_pre, glp_post, dsk)


def _adamw(parts, w, m, v, name):
    R, C = w.shape
    tr = R if R <= 256 else 128

    def body(p_ref, w_ref, m_ref, v_ref, g_ref, d_ref, nm_ref, nv_ref):
        g = p_ref[0].astype(F32)
        for s in range(1, N_DEV):
            g = g + p_ref[s].astype(F32)
        nm = ADAM_B1 * m_ref[...] + (1.0 - ADAM_B1) * g
        nv = ADAM_B2 * v_ref[...] + (1.0 - ADAM_B2) * (g * g)
        m_hat = nm / (1.0 - ADAM_B1 ** ADAM_STEP)
        v_hat = nv / (1.0 - ADAM_B2 ** ADAM_STEP)
        g_ref[...] = g
        d_ref[...] = -ADAM_LR * (m_hat / (jnp.sqrt(v_hat) + ADAM_EPS) + ADAM_WD * w_ref[...])
        nm_ref[...] = nm
        nv_ref[...] = nv

    blk = pl.BlockSpec((tr, C), lambda i: (i, 0))
    return pl.pallas_call(
        body, name=name, grid=(R // tr,), out_shape=[jax.ShapeDtypeStruct((R, C), F32)] * 4,
        in_specs=[pl.BlockSpec((N_DEV, tr, C), lambda i: (0, i, 0)), blk, blk, blk], out_specs=[blk] * 4,
        compiler_params=_cparams("parallel"),
    )(parts, w, m, v)


def kernel(x, p, positions, w_in, ln_pre, ln_post, w_dw, b_dw, conv_ln_g, conv_ln_b, w_pw, sinks, w_br_conv, w_br_attn, w_out, w_ple_gate, w_ple_proj, loss_target, m_w_in, m_ln_pre, m_ln_post, m_w_dw, m_b_dw, m_conv_ln_g, m_conv_ln_b, m_w_pw, m_sinks, m_w_br_conv, m_w_br_attn, m_w_out, m_w_ple_gate, m_w_ple_proj, v_w_in, v_ln_pre, v_ln_post, v_w_dw, v_b_dw, v_conv_ln_g, v_conv_ln_b, v_w_pw, v_sinks, v_w_br_conv, v_w_br_attn, v_w_out, v_w_ple_gate, v_w_ple_proj):
    NS, S, _ = x.shape
    T = NS * S
    x2 = x.reshape(T, D)
    p2 = p.reshape(T, PLE)
    tgt = loss_target.reshape(T, D)
    pos = positions.reshape(T, 1)

    row_sharded = [w_pw[0], w_br_conv[0], w_br_attn[0], w_out[0], w_ple_gate[0]]
    sh_rows = D // N_DEV
    (g_in,) = _all_gather([w_in[0].astype(BF16)])
    w_in_f = _to_perm(g_in.transpose(1, 0, 2).reshape(D, NW))
    w_in_t = w_in_f.T
    gather_rest = _Exchange([], [jnp.stack(row_sharded).astype(BF16), w_ple_proj[0].astype(BF16),
                                 jnp.pad(w_dw[0], ((0, CONV_HALO - CONV_K), (0, 0)))])

    cos, sa, sb = _rope_tables(pos)
    z2, h, g_rows, g_pp, g_dw = _in_proj(x2, ln_pre, w_in_f, gather_rest)
    full = [g_rows[:, j].reshape(D, D) for j in range(5)]
    w_pw_f, w_brc_f, w_bra_f, w_out_f, w_pg_f = full
    w_pp_f = g_pp.transpose(1, 0, 2).reshape(PLE, D)
    w_dw_f = g_dw.transpose(1, 0, 2).reshape(CONV_HALO, D)
    z3 = z2.reshape(NS, S, NW)
    c3, cs3 = _conv_fwd(z3, w_dw_f, b_dw, conv_ln_g, conv_ln_b)
    qr, kr = _rope_qk(z2, cos, sa, sb)
    qr3, kr3 = qr.reshape(NS, S, D), kr.reshape(NS, S, KV_W)
    sinks1 = sinks.reshape(N_HEADS)
    o3, lse3 = _attn_fwd(qr3, kr3, z3, sinks1)
    o = o3.reshape(T, D)
    ya0, ya = _branch_a_fwd(cs3.reshape(T, D), z2, w_pw_f, w_brc_f)
    yb, m, mo, x1, gate, pp, dy, loss_blk = _merge_fwd(o, z2, ya, x2, p2, tgt, ln_post, w_bra_f, w_out_f, w_pg_f, w_pp_f)
    loss = lax.psum(loss_blk[0, 0], MESH_AXES)

    dgl, dpp, dx1, dmo, dya, dyb, dgc, dga, glp_post = _merge_bwd(dy, gate, pp, mo, ya, yb, z2, ln_post, w_pg_f.T, w_out_f.T)
    dya0, dcg, ya1, dcs, do, dag, yb0 = _branches_bwd(dya, dyb, ya0, o, z2, w_brc_f.T, w_pw_f.T, w_bra_f.T)
    gp_rows = [_grad_matmul(a, b, nm).reshape(N_DEV, sh_rows, D) for a, b, nm in (
        (cs3.reshape(T, D), dya0, "grad_w_pw"), (ya1, dya, "grad_w_br_conv"), (yb0, dyb, "grad_w_br_attn"),
        (m, dmo, "grad_w_out"), (x1, dgl, "grad_w_ple_gate"))]
    gp_pp = _grad_matmul(p2, dpp, "grad_w_ple_proj").reshape(PLE, N_DEV, D // N_DEV).transpose(1, 0, 2)
    conv_out = _conv_bwd(z3, c3, dcs.reshape(NS, S, D), w_dw_f, conv_ln_g, conv_ln_b, _Exchange(gp_rows + [gp_pp], []))
    dzvu3, gw, gvec, r_rows, r_pp = conv_out[0], conv_out[1], conv_out[2], conv_out[3:8], conv_out[8]
    dqr3, dkc3, dkp3, dvc3, dvp3, dsk3 = _attn_bwd(qr3, kr3, z3, do.reshape(NS, S, D), o3, lse3, sinks1)
    tab3 = [t.reshape(NS, S, 128) for t in (cos, sa, sb)]
    dq3, dkv3 = _attn_post(dqr3, dkc3, dkp3, dvc3, dvp3, *tab3)
    segs = [dzvu3.reshape(T, 2 * D), dcg, dq3.reshape(T, D), dag, dgc, dga, dkv3.reshape(T, 2 * KV_W)]
    gp_in = _from_perm(jnp.concatenate([_grad_matmul(h, s, f"grad_w_in_{j}") for j, s in enumerate(segs)], axis=1))
    gp_in = gp_in.reshape(D, N_DEV, NW // N_DEV).transpose(1, 0, 2)
    grad_x2, glp_pre, r_in = _in_bwd(segs, w_in_t, x2, dx1, ln_pre, _Exchange([gp_in], []))
    gp_dw, gp_small = _pack_small(gw, gvec, glp_pre, glp_post, dsk3.reshape(T, N_HEADS))
    r_dw, r_small = _Exchange([gp_dw], [gp_small]).alone("small_grad_exchange")

    res = {}
    res["w_in"] = _adamw(r_in, w_in[0], m_w_in[0], v_w_in[0], "adamw_w_in")
    names_rows = ["w_pw", "w_br_conv", "w_br_attn", "w_out", "w_ple_gate"]
    wmv = {"w_pw": (w_pw, m_w_pw, v_w_pw), "w_br_conv": (w_br_conv, m_w_br_conv, v_w_br_conv),
           "w_br_attn": (w_br_attn, m_w_br_attn, v_w_br_attn), "w_out": (w_out, m_w_out, v_w_out),
           "w_ple_gate": (w_ple_gate, m_w_ple_gate, v_w_ple_gate)}
    for nm, parts in zip(names_rows, r_rows):
        w_, m_, v_ = wmv[nm]
        res[nm] = _adamw(parts, w_[0], m_[0], v_[0], "adamw_" + nm)
    res["w_ple_proj"] = _adamw(r_pp, w_ple_proj[0], m_w_ple_proj[0], v_w_ple_proj[0], "adamw_w_ple_proj")
    pad_dw = lambda a: jnp.pad(a[0], ((0, CONV_HALO - CONV_K), (0, 0)))
    res["w_dw"] = [a[:CONV_K] for a in _adamw(r_dw, pad_dw(w_dw), pad_dw(m_w_dw), pad_dw(v_w_dw), "adamw_w_dw")]

    def stack_small(a_pre, a_post, a_b, a_g, a_bb, a_s):
        sk = jnp.pad(a_s, ((0, 0), (0, D - N_HEADS)))
        return jnp.concatenate([a_pre, a_post, a_b, a_g, a_bb, sk, jnp.zeros((2, D), F32)], axis=0)

    small = _adamw(
        r_small, stack_small(ln_pre, ln_post, b_dw, conv_ln_g, conv_ln_b, sinks),
        stack_small(m_ln_pre, m_ln_post, m_b_dw, m_conv_ln_g, m_conv_ln_b, m_sinks),
        stack_small(v_ln_pre, v_ln_post, v_b_dw, v_conv_ln_g, v_conv_ln_b, v_sinks), "adamw_small")
    for j, nm in enumerate(["ln_pre", "ln_post", "b_dw", "conv_ln_g", "conv_ln_b"]):
        res[nm] = [a[j] for a in small]
    res["sinks"] = [a[5, :N_HEADS] for a in small]

    order = ["w_in", "ln_pre", "ln_post", "w_dw", "b_dw", "conv_ln_g", "conv_ln_b", "w_pw", "sinks", "w_br_conv",
             "w_br_attn", "w_out", "w_ple_gate", "w_ple_proj"]
    outs = [loss, grad_x2.reshape(NS, S, D)]
    for kind in range(4):
        outs += [res[nm][kind][None] for nm in order]
    return tuple(outs)
```

```python
import functools

import numpy as np

import jax
import jax.numpy as jnp
from jax import lax
from jax.experimental import pallas as pl
from jax.experimental.pallas import tpu as pltpu

F32 = jnp.float32
BF16 = jnp.bfloat16

D = 1024
N_HEADS = 16
N_KV = 2
HEAD_DIM = 64
GROUP = N_HEADS // N_KV
KV_W = N_KV * HEAD_DIM
CONV_K = 31
CONV_HALO = 32
BLK = 128
ROPE_DIM = 16
ROPE_THETA = 500000.0
EPS = 1e-6
PLE = 256
NW = 7 * D + 2 * KV_W
N_DEV = 8
MESH_AXES = ("x", "y", "c")

CB_VAL, CB_GLU, CB_CGATE, CB_Q, CB_AGATE, CB_GCONV, CB_GATTN = range(7)
CB_K = 7 * D // KV_W
CB_V = CB_K + 1

ADAM_LR, ADAM_B1, ADAM_B2, ADAM_EPS, ADAM_WD, ADAM_STEP = 0.001, 0.9, 0.999, 1e-08, 0.01, 10

VMEM_LIMIT = 56 * 1024 * 1024
ROW_TILE = 256


def _cparams(*sem):
    return pltpu.CompilerParams(dimension_semantics=sem if sem else None, vmem_limit_bytes=VMEM_LIMIT)


def _sig(v):
    return 1.0 / (1.0 + jnp.exp(-v))


def _rowsum8(a):
    return a.reshape(a.shape[0] // 8, 8, a.shape[1]).sum(axis=0)


def _dot(a, b):
    return jnp.dot(a, b, preferred_element_type=F32)


def _dot_nt(a, b):
    return lax.dot_general(a, b, (((1,), (1,)), ((), ())), preferred_element_type=F32)


def _dot_tn(a, b):
    return lax.dot_general(a, b, (((0,), (0,)), ((), ())), preferred_element_type=F32)


def _to_perm(w):
    return jnp.concatenate([w[..., : 4 * D], w[..., 4 * D + 2 * KV_W :], w[..., 4 * D : 4 * D + 2 * KV_W]], axis=-1)


def _from_perm(g):
    return jnp.concatenate([g[..., : 4 * D], g[..., 7 * D :], g[..., 4 * D : 7 * D]], axis=-1)


def _my_place():
    return lax.axis_index("x"), lax.axis_index("y"), lax.axis_index("c")


def _slot(px, py, pc):
    return 4 * px + 2 * py + pc


def _all_gather(shards):
    n = len(shards)

    def body(*refs):
        ins, outs = refs[:n], refs[n : 2 * n]
        send_sems, recv_sems, local_sems = refs[2 * n :]
        x, y, c = _my_place()
        me, sibling = (x, y, c), (x, y, 1 - c)
        chips = [(1 - x, y), (x, 1 - y), (1 - x, 1 - y)]

        def copy(a, k, block, to, src=None):
            rows = outs[a].at[_slot(*block)]
            return pltpu.make_async_remote_copy(
                src_ref=rows if src is None else src, dst_ref=rows, send_sem=send_sems.at[a, k],
                recv_sem=recv_sems.at[a, k], device_id=to, device_id_type=pl.DeviceIdType.MESH)

        mine, first, passed = [], [], []
        for a in range(n):
            cp = pltpu.make_async_copy(ins[a], outs[a].at[_slot(*me)], local_sems.at[a])
            cp.start()
            mine.append(cp)
            fa = [copy(a, 0, me, sibling, src=ins[a])]
            fa += [copy(a, 1 + j, me, (*chip, c), src=ins[a]) for j, chip in enumerate(chips)]
            for cp in fa:
                cp.start()
            first += fa
        for j, chip in enumerate(chips):
            for a in range(n):
                copy(a, 1 + j, (*chip, c), me).wait_recv()
                cp = copy(a, 4 + j, (*chip, c), sibling)
                cp.start()
                passed.append(cp)
        for a in range(n):
            copy(a, 0, sibling, me).wait_recv()
            for j, chip in enumerate(chips):
                copy(a, 4 + j, (*chip, 1 - c), me).wait_recv()
        for cp in first + passed:
            cp.wait_send()
        for cp in mine:
            cp.wait()

    any_spec = pl.BlockSpec(memory_space=pl.ANY)
    return pl.pallas_call(
        body, name="weight_all_gather",
        out_shape=[jax.ShapeDtypeStruct((N_DEV,) + s.shape, s.dtype) for s in shards],
        in_specs=[any_spec] * n, out_specs=[any_spec] * n,
        scratch_shapes=[pltpu.SemaphoreType.DMA((n, 7)), pltpu.SemaphoreType.DMA((n, 7)), pltpu.SemaphoreType.DMA((n,))],
    )(*shards)


class _Exchange:
    def __init__(self, scatter, bcast):
        self.arrs = list(scatter) + list(bcast)
        self.n, self.n_sc = len(self.arrs), len(scatter)
        self.out_shape = [jax.ShapeDtypeStruct(a.shape, a.dtype) for a in scatter]
        self.out_shape += [jax.ShapeDtypeStruct((N_DEV,) + a.shape, a.dtype) for a in bcast]
        self.specs = [pl.BlockSpec(memory_space=pl.ANY)] * self.n
        self.scratch = [pltpu.SemaphoreType.DMA((self.n, 7)), pltpu.SemaphoreType.DMA((self.n, 7)),
                        pltpu.SemaphoreType.DMA((self.n,))]

    def _copies(self, ins, outs, sems):
        send_sems, recv_sems, local_sems = sems
        x, y, c = _my_place()
        me = _slot(x, y, c)
        peers = [(1 - x if k & 4 else x, 1 - y if k & 2 else y, 1 - c if k & 1 else c) for k in range(1, N_DEV)]
        mine, sends, arrivals = [], [], []
        for a in range(self.n):
            src = ins[a].at[me] if a < self.n_sc else ins[a]
            mine.append(pltpu.make_async_copy(src, outs[a].at[me], local_sems.at[a]))
        for k, peer in enumerate(peers):
            for a in range(self.n):
                src = ins[a].at[_slot(*peer)] if a < self.n_sc else ins[a]
                sends.append(pltpu.make_async_remote_copy(
                    src_ref=src, dst_ref=outs[a].at[me], send_sem=send_sems.at[a, k], recv_sem=recv_sems.at[a, k],
                    device_id=peer, device_id_type=pl.DeviceIdType.MESH))
                rows = outs[a].at[_slot(*peer)]
                arrivals.append(pltpu.make_async_remote_copy(
                    src_ref=rows, dst_ref=rows, send_sem=send_sems.at[a, k], recv_sem=recv_sems.at[a, k],
                    device_id=peer, device_id_type=pl.DeviceIdType.MESH))
        return mine, sends, arrivals

    def start(self, ins, outs, sems):
        mine, sends, _ = self._copies(ins, outs, sems)
        for cp in mine + sends:
            cp.start()

    def finish(self, ins, outs, sems):
        mine, sends, arrivals = self._copies(ins, outs, sems)
        for cp in arrivals:
            cp.wait_recv()
        for cp in sends:
            cp.wait_send()
        for cp in mine:
            cp.wait()

    def carried(self, refs_in, refs_out, sems, first, last):
        @pl.when(first)
        def _():
            self.start(refs_in, refs_out, sems)

        @pl.when(last)
        def _():
            self.finish(refs_in, refs_out, sems)

    def alone(self, name):
        n = self.n

        def body(*refs):
            ins, outs, sems = refs[:n], refs[n : 2 * n], refs[2 * n :]
            self.start(ins, outs, sems)
            self.finish(ins, outs, sems)

        return pl.pallas_call(body, name=name, out_shape=self.out_shape, in_specs=self.specs, out_specs=self.specs,
                              scratch_shapes=self.scratch)(*self.arrs)


def _host_split(refs, n_in, n_out, n_scratch, ex):
    k = ex.n if ex is not None else 0
    a = n_in
    b = a + k
    c = b + n_out
    d = c + k
    e = d + n_scratch
    return refs[:a], refs[a:b], refs[b:c], refs[c:d], refs[d:e], refs[e:]


def _rope_tables(pos):
    T = pos.shape[0]
    tm = min(1024, T)
    lane = np.arange(128) % HEAD_DIM
    inv = np.power(np.float32(ROPE_THETA), -np.arange(0, ROPE_DIM, 2, dtype=np.float32) / np.float32(ROPE_DIM)).astype(np.float32)
    half = ROPE_DIM // 2
    invf = np.where(lane < ROPE_DIM, inv[lane % half], 0.0).astype(np.float32)[None, :]
    m_a = (lane < half).astype(np.float32)[None, :]
    m_b = ((lane >= half) & (lane < ROPE_DIM)).astype(np.float32)[None, :]

    def body(pos_ref, invf_ref, ma_ref, mb_ref, cos_ref, sa_ref, sb_ref):
        ang = pos_ref[...].astype(F32) * invf_ref[...]
        sn = jnp.sin(ang)
        cos_ref[...] = jnp.cos(ang)
        sa_ref[...] = -sn * ma_ref[...]
        sb_ref[...] = sn * mb_ref[...]

    row = pl.BlockSpec((tm, 128), lambda i: (i, 0))
    cst = pl.BlockSpec((1, 128), lambda i: (0, 0))
    return pl.pallas_call(
        body, name="rope_tables", grid=(T // tm,), out_shape=[jax.ShapeDtypeStruct((T, 128), F32)] * 3,
        in_specs=[pl.BlockSpec((tm, 1), lambda i: (i, 0)), cst, cst, cst], out_specs=[row] * 3,
        compiler_params=_cparams("parallel"),
    )(pos, jnp.asarray(invf), jnp.asarray(m_a), jnp.asarray(m_b))


def _rope(t, cos, sa, sb, sign=1.0):
    parts = []
    for i in range(t.shape[1] // 128):
        ti = t[:, 128 * i : 128 * (i + 1)]
        up = pltpu.roll(ti, 128 - ROPE_DIM // 2, 1)
        dn = pltpu.roll(ti, ROPE_DIM // 2, 1)
        parts.append(ti * cos + sign * (up * sa + dn * sb))
    return parts[0] if len(parts) == 1 else jnp.concatenate(parts, axis=-1)


def _in_proj(x2, ln_pre, w_in, ex):
    T = x2.shape[0]
    tm = min(512, T)
    tn = NW // 2
    chunk = 512
    ni, nj = T // tm, NW // tn

    def body(*refs):
        (x_ref, g_ref, w_ref), ex_in, (z_ref, h_ref), ex_out, (hs,), ex_sems = _host_split(refs, 3, 2, 1, ex)
        i, j = pl.program_id(0), pl.program_id(1)
        ex.carried(ex_in, ex_out, ex_sems, (i == 0) & (j == 0), (i == ni - 1) & (j == nj - 1))

        @pl.when(j == 0)
        def _():
            xv = x_ref[...]
            r = lax.rsqrt(jnp.mean(xv * xv, axis=-1, keepdims=True) + EPS)
            h = (xv * r * g_ref[...]).astype(BF16)
            hs[...] = h
            h_ref[...] = h

        for c0 in range(0, tn, chunk):
            cw = min(chunk, tn - c0)
            z_ref[:, c0 : c0 + cw] = _dot(hs[...], w_ref[:, c0 : c0 + cw]).astype(BF16)

    return pl.pallas_call(
        body, name="in_proj", grid=(ni, nj),
        out_shape=[jax.ShapeDtypeStruct((T, NW), BF16), jax.ShapeDtypeStruct((T, D), BF16)] + ex.out_shape,
        in_specs=[pl.BlockSpec((tm, D), lambda i, j: (i, 0)), pl.BlockSpec((1, D), lambda i, j: (0, 0)),
                  pl.BlockSpec((D, tn), lambda i, j: (0, j))] + ex.specs,
        out_specs=[pl.BlockSpec((tm, tn), lambda i, j: (i, j)), pl.BlockSpec((tm, D), lambda i, j: (i, 0))] + ex.specs,
        scratch_shapes=[pltpu.VMEM((tm, D), BF16)] + ex.scratch,
        compiler_params=_cparams("arbitrary", "arbitrary"),
    )(x2, ln_pre, w_in, *ex.arrs)


def _conv_tiles(S):
    tm = min(ROW_TILE, S)
    return tm, S // tm, tm // CONV_HALO


CONV_ROWS_FWD = 32
CONV_ROWS = 16


def _fill_shifted(sh, rows):
    for b in range(1, 8):
        sh[b, 0:rows, :] = sh[0, b : b + rows, :]


def _conv_fwd(z3, w_dw, b_dw, ln_g, ln_b):
    NS, S, _ = z3.shape
    tm, nt, r = _conv_tiles(S)

    def body(val_ref, glu_ref, hval_ref, hglu_ref, w_ref, b_ref, g_ref, bb_ref, c_ref, cs_ref, ush, cbuf):
        i = pl.program_id(1)
        ush[0, CONV_HALO:, :] = val_ref[...].astype(F32) * _sig(glu_ref[...].astype(F32))
        uh = hval_ref[...].astype(F32) * _sig(hglu_ref[...].astype(F32))
        ush[0, 0:CONV_HALO, :] = jnp.where(i > 0, uh, 0.0)
        _fill_shifted(ush, tm + CONV_HALO - 8)
        for r0 in range(0, tm, CONV_ROWS_FWD):
            acc = jnp.zeros((CONV_ROWS_FWD, D), F32)
            for k in range(CONV_K):
                a, b = divmod(CONV_HALO - (CONV_K - 1) + k, 8)
                acc = acc + w_ref[k : k + 1, :] * ush[b, r0 + 8 * a : r0 + 8 * a + CONV_ROWS_FWD, :]
            cbuf[r0 : r0 + CONV_ROWS_FWD, :] = acc + b_ref[...]
        cv = cbuf[...]
        mu = jnp.mean(cv, axis=-1, keepdims=True)
        xc = cv - mu
        var = jnp.mean(xc * xc, axis=-1, keepdims=True)
        cl = xc * lax.rsqrt(var + EPS) * g_ref[...] + bb_ref[...]
        c_ref[...] = cv.astype(BF16)
        cs_ref[...] = (cl * _sig(cl)).astype(BF16)

    def cur(cb):
        return pl.BlockSpec((None, tm, D), lambda s, i: (s, i, cb))

    def halo(cb):
        return pl.BlockSpec((None, CONV_HALO, D), lambda s, i: (s, jnp.maximum(i * r - 1, 0), cb))

    vec = pl.BlockSpec((1, D), lambda s, i: (0, 0))
    out = pl.BlockSpec((None, tm, D), lambda s, i: (s, i, 0))
    return pl.pallas_call(
        body, name="conv_fwd", grid=(NS, nt),
        out_shape=[jax.ShapeDtypeStruct((NS, S, D), BF16)] * 2,
        in_specs=[cur(CB_VAL), cur(CB_GLU), halo(CB_VAL), halo(CB_GLU),
                  pl.BlockSpec((CONV_HALO, D), lambda s, i: (0, 0)), vec, vec, vec],
        out_specs=[out, out],
        scratch_shapes=[pltpu.VMEM((8, tm + CONV_HALO, D), F32), pltpu.VMEM((tm, D), F32)],
        compiler_params=_cparams("parallel", "parallel"),
    )(z3, z3, z3, z3, w_dw, b_dw, ln_g, ln_b)


PAIRS = GROUP // 2
QROWS = PAIRS * BLK


def _kv2_col(kind, g, par):
    return kind * 2 * N_KV + g * 2 + par


def _rope_qk(z2, cos, sa, sb):
    T = z2.shape[0]
    tm = min(ROW_TILE, T)

    def body(q_ref, k_ref, v_ref, cos_ref, sa_ref, sb_ref, qr_ref, kv_ref):
        tabs = cos_ref[...], sa_ref[...], sb_ref[...]
        qr_ref[...] = (_rope(q_ref[...].astype(F32), *tabs) * (HEAD_DIM ** -0.5)).astype(BF16)
        low = lax.broadcasted_iota(jnp.int32, (tm, KV_W), 1) < HEAD_DIM
        for kind, t in ((0, _rope(k_ref[...].astype(F32), *tabs)), (1, v_ref[...].astype(F32))):
            r = pltpu.roll(t, HEAD_DIM, 1)
            cols = {(0, 0): jnp.where(low, t, 0.0), (0, 1): jnp.where(low, 0.0, r),
                    (1, 0): jnp.where(low, r, 0.0), (1, 1): jnp.where(low, 0.0, t)}
            for (g, par), val in cols.items():
                c = _kv2_col(kind, g, par)
                kv_ref[:, c * KV_W : (c + 1) * KV_W] = val.astype(BF16)

    tab = pl.BlockSpec((tm, 128), lambda i: (i, 0))
    return pl.pallas_call(
        body, name="rope_qk", grid=(T // tm,),
        out_shape=[jax.ShapeDtypeStruct((T, D), BF16), jax.ShapeDtypeStruct((T, 4 * N_KV * KV_W), BF16)],
        in_specs=[pl.BlockSpec((tm, D), lambda i: (i, CB_Q)), pl.BlockSpec((tm, KV_W), lambda i: (i, CB_K)),
                  pl.BlockSpec((tm, KV_W), lambda i: (i, CB_V)), tab, tab, tab],
        out_specs=[pl.BlockSpec((tm, D), lambda i: (i, 0)), pl.BlockSpec((tm, 4 * N_KV * KV_W), lambda i: (i, 0))],
        compiler_params=_cparams("parallel"),
    )(z2, z2, z2, cos, sa, sb)


def _attn_mask(has_prev):
    qi = lax.broadcasted_iota(jnp.int32, (QROWS, 2 * BLK), 0) & (BLK - 1)
    kj = lax.broadcasted_iota(jnp.int32, (QROWS, 2 * BLK), 1)
    first_key = jnp.where(has_prev, 0, BLK)
    return (kj > qi) & (kj <= qi + BLK) & (kj >= first_key)


NEG_BIG = -1e30


def _attn_specs():
    q = pl.BlockSpec((None, BLK, D), lambda s, n: (s, n, 0))
    kv_cur = pl.BlockSpec((None, BLK, 4 * N_KV * KV_W), lambda s, n: (s, n, 0))
    kv_prev = pl.BlockSpec((None, BLK, 4 * N_KV * KV_W), lambda s, n: (s, jnp.maximum(n - 1, 0), 0))
    sink = pl.BlockSpec(memory_space=pltpu.SMEM)
    return sink, q, kv_cur, kv_prev


def _stack_pairs(ref, g):
    return jnp.concatenate([ref[:, (PAIRS * g + j) * 128 : (PAIRS * g + j + 1) * 128] for j in range(PAIRS)], axis=0)


def _unstack_pairs(ref, g, val):
    for j in range(PAIRS):
        ref[:, (PAIRS * g + j) * 128 : (PAIRS * g + j + 1) * 128] = val[j * BLK : (j + 1) * BLK].astype(ref.dtype)


def _pair_heads(g, par):
    return [GROUP * g + 2 * j + par for j in range(PAIRS)]


def _head_col_load(ref, g, par):
    return jnp.concatenate([ref[:, h : h + 1] for h in _pair_heads(g, par)], axis=0)


def _head_col_store(ref, g, par, col):
    for j, h in enumerate(_pair_heads(g, par)):
        ref[:, h : h + 1] = col[j * BLK : (j + 1) * BLK]


def _sink_col(sink_ref, g, par):
    return jnp.concatenate([jnp.full((BLK, 1), sink_ref[h], F32) for h in _pair_heads(g, par)], axis=0)


def _kv2_block(kvc_ref, kvp_ref, has_prev, c):
    col = slice(c * KV_W, (c + 1) * KV_W)
    prev = jnp.where(has_prev, kvp_ref[:, col], jnp.zeros((BLK, KV_W), BF16))
    return jnp.concatenate([prev, kvc_ref[:, col]], axis=0)


def _attn_fwd(qr3, kv3, sinks):
    NS, S, _ = qr3.shape

    def body(sink_ref, q_ref, kvc_ref, kvp_ref, o_ref, lse_ref):
        has_prev = pl.program_id(1) > 0
        mask = _attn_mask(has_prev)[0:BLK]
        for g in range(N_KV):
            kv = [[_kv2_block(kvc_ref, kvp_ref, has_prev, _kv2_col(kind, g, par)) for par in range(2)] for kind in range(2)]
            for j in range(PAIRS):
                cols = slice((PAIRS * g + j) * 128, (PAIRS * g + j + 1) * 128)
                q2 = q_ref[:, cols]
                o_pair = None
                for par in range(2):
                    h = GROUP * g + 2 * j + par
                    s = jnp.where(mask, _dot_nt(q2, kv[0][par]), NEG_BIG)
                    sk = sink_ref[h]
                    mx = jnp.maximum(jnp.max(s, axis=-1, keepdims=True), sk)
                    e = jnp.exp(s - mx)
                    den = jnp.sum(e, axis=-1, keepdims=True) + jnp.exp(sk - mx)
                    pv = _dot(e.astype(BF16), kv[1][par]) * (1.0 / den)
                    o_pair = pv if o_pair is None else o_pair + pv
                    lse_ref[:, h : h + 1] = mx + jnp.log(den)
                o_ref[:, cols] = o_pair.astype(BF16)

    return pl.pallas_call(
        body, name="attn_fwd", grid=(NS, S // BLK),
        out_shape=[jax.ShapeDtypeStruct((NS, S, D), BF16), jax.ShapeDtypeStruct((NS, S, N_HEADS), F32)],
        in_specs=list(_attn_specs()),
        out_specs=[pl.BlockSpec((None, BLK, D), lambda s, n: (s, n, 0)),
                   pl.BlockSpec((None, BLK, N_HEADS), lambda s, n: (s, n, 0))],
        compiler_params=_cparams("parallel", "parallel"),
    )(sinks, qr3, kv3, kv3)


def _weight_spec(shape):
    return pl.BlockSpec(shape, lambda i: (0,) * len(shape), pipeline_mode=pl.Buffered(1))


def _branch_a_fwd(cs, z2, w_pw, w_brc):
    T = cs.shape[0]
    tm = min(ROW_TILE, T)

    def body(cs_ref, cg_ref, wpw_ref, wbr_ref, ya0_ref, ya_ref):
        ya0 = _dot(cs_ref[...], wpw_ref[...])
        cg = cg_ref[...].astype(F32)
        ya0_ref[...] = ya0.astype(BF16)
        ya_ref[...] = _dot((ya0 * (cg * _sig(cg))).astype(BF16), wbr_ref[...]).astype(BF16)

    row = pl.BlockSpec((tm, D), lambda i: (i, 0))
    return pl.pallas_call(
        body, name="branch_a_fwd", grid=(T // tm,),
        out_shape=[jax.ShapeDtypeStruct((T, D), BF16)] * 2,
        in_specs=[row, pl.BlockSpec((tm, D), lambda i: (i, CB_CGATE)), _weight_spec((D, D)), _weight_spec((D, D))],
        out_specs=[row, row],
        compiler_params=_cparams("parallel"),
    )(cs, z2, w_pw, w_brc)


def _merge_fwd(o, z2, ya, x2, p2, tgt, ln_post, w_bra, w_out, w_pg, w_pp):
    T = o.shape[0]
    tm = min(ROW_TILE, T)
    nt = T // tm

    def body(o_ref, ag_ref, gc_ref, ga_ref, ya_ref, x_ref, p_ref, t_ref, g_ref, wbra_ref, wout_ref, wpg_ref, wpp_ref,
             yb_ref, m_ref, mo_ref, x1_ref, gate_ref, pp_ref, dy_ref, loss_ref, lacc):
        i = pl.program_id(0)

        @pl.when(i == 0)
        def _():
            lacc[...] = jnp.zeros_like(lacc)

        ag = ag_ref[...].astype(F32)
        yb = _dot((o_ref[...].astype(F32) * (ag * _sig(ag))).astype(BF16), wbra_ref[...])
        m = _sig(gc_ref[...].astype(F32)) * ya_ref[...].astype(F32) + _sig(ga_ref[...].astype(F32)) * yb
        mb = m.astype(BF16)
        mo = _dot(mb, wout_ref[...])
        r2 = lax.rsqrt(jnp.mean(mo * mo, axis=-1, keepdims=True) + EPS)
        x1 = x_ref[...] + mo * r2 * g_ref[...]
        x1b = x1.astype(BF16)
        gate = _sig(_dot(x1b, wpg_ref[...]))
        pp = _dot(p_ref[...].astype(BF16), wpp_ref[...])
        e = x1 + gate * pp - t_ref[...]
        yb_ref[...] = yb.astype(BF16)
        m_ref[...] = mb
        mo_ref[...] = mo
        x1_ref[...] = x1b
        gate_ref[...] = gate.astype(BF16)
        pp_ref[...] = pp.astype(BF16)
        dy_ref[...] = e * (1.0 / D)
        lacc[...] += _rowsum8(e * e)

        @pl.when(i == nt - 1)
        def _():
            loss_ref[...] = jnp.full(loss_ref.shape, jnp.sum(lacc[...]) * (0.5 / D), F32)

    row = pl.BlockSpec((tm, D), lambda i: (i, 0))

    def zcol(cb):
        return pl.BlockSpec((tm, D), lambda i: (i, cb))

    bf = jax.ShapeDtypeStruct((T, D), BF16)
    f32 = jax.ShapeDtypeStruct((T, D), F32)
    return pl.pallas_call(
        body, name="merge_fwd", grid=(nt,),
        out_shape=[bf, bf, f32, bf, bf, bf, f32, jax.ShapeDtypeStruct((8, 128), F32)],
        in_specs=[row, zcol(CB_AGATE), zcol(CB_GCONV), zcol(CB_GATTN), row, row, pl.BlockSpec((tm, PLE), lambda i: (i, 0)), row,
                  _weight_spec((1, D)), _weight_spec((D, D)), _weight_spec((D, D)), _weight_spec((D, D)), _weight_spec((PLE, D))],
        out_specs=[row] * 7 + [pl.BlockSpec((8, 128), lambda i: (0, 0))],
        scratch_shapes=[pltpu.VMEM((8, D), F32)],
        compiler_params=_cparams("arbitrary"),
    )(o, z2, z2, z2, ya, x2, p2, tgt, ln_post, w_bra, w_out, w_pg, w_pp)


def _merge_bwd(dy, gate, pp, mo, ya, yb, z2, ln_post, w_pg_t, w_out_t):
    T = dy.shape[0]
    tm = min(ROW_TILE, T)

    def body(dy_ref, gate_ref, pp_ref, mo_ref, ya_ref, yb_ref, gc_ref, ga_ref, g_ref, wpgt_ref, woutt_ref,
             dgl_ref, dpp_ref, dx1_ref, dmo_ref, dya_ref, dyb_ref, dgc_ref, dga_ref, glp_ref):
        @pl.when(pl.program_id(0) == 0)
        def _():
            glp_ref[...] = jnp.zeros_like(glp_ref)

        dyv = dy_ref[...]
        gate = gate_ref[...].astype(F32)
        dgl = (dyv * pp_ref[...].astype(F32) * gate * (1.0 - gate)).astype(BF16)
        dgl_ref[...] = dgl
        dpp_ref[...] = (dyv * gate).astype(BF16)
        dx1 = dyv + _dot(dgl, wpgt_ref[...])
        dx1_ref[...] = dx1
        mo = mo_ref[...]
        r2 = lax.rsqrt(jnp.mean(mo * mo, axis=-1, keepdims=True) + EPS)
        nrm = mo * r2
        glp_ref[...] += _rowsum8(dx1 * nrm)
        dn = dx1 * g_ref[...]
        dmo = (r2 * (dn - nrm * jnp.mean(dn * nrm, axis=-1, keepdims=True))).astype(BF16)
        dmo_ref[...] = dmo
        dm = _dot(dmo, woutt_ref[...])
        sg = _sig(gc_ref[...].astype(F32))
        sa = _sig(ga_ref[...].astype(F32))
        dya_ref[...] = (dm * sg).astype(BF16)
        dyb_ref[...] = (dm * sa).astype(BF16)
        dgc_ref[...] = (dm * ya_ref[...].astype(F32) * sg * (1.0 - sg)).astype(BF16)
        dga_ref[...] = (dm * yb_ref[...].astype(F32) * sa * (1.0 - sa)).astype(BF16)

    row = pl.BlockSpec((tm, D), lambda i: (i, 0))
    bf = jax.ShapeDtypeStruct((T, D), BF16)
    return pl.pallas_call(
        body, name="merge_bwd", grid=(T // tm,),
        out_shape=[bf, bf, jax.ShapeDtypeStruct((T, D), F32), bf, bf, bf, bf, bf, jax.ShapeDtypeStruct((8, D), F32)],
        in_specs=[row] * 6 + [pl.BlockSpec((tm, D), lambda i: (i, CB_GCONV)), pl.BlockSpec((tm, D), lambda i: (i, CB_GATTN)),
                              _weight_spec((1, D)), _weight_spec((D, D)), _weight_spec((D, D))],
        out_specs=[row] * 8 + [pl.BlockSpec((8, D), lambda i: (0, 0))],
        compiler_params=_cparams("arbitrary"),
    )(dy, gate, pp, mo, ya, yb, z2, z2, ln_post, w_pg_t, w_out_t)


def _dsilu(v, s):
    return s * (1.0 + v * (1.0 - s))


def _branches_bwd(dya, dyb, ya0, o, z2, w_brc_t, w_pw_t, w_bra_t):
    T = dya.shape[0]
    tm = min(ROW_TILE, T)

    def body(dya_ref, dyb_ref, ya0_ref, o_ref, cg_ref, ag_ref, wbrct_ref, wpwt_ref, wbrat_ref,
             dya0_ref, dcg_ref, ya1_ref, dcs_ref, do_ref, dag_ref, yb0_ref):
        cg = cg_ref[...].astype(F32)
        s = _sig(cg)
        silu = cg * s
        ya0 = ya0_ref[...].astype(F32)
        dya1 = _dot(dya_ref[...], wbrct_ref[...])
        dya0 = (dya1 * silu).astype(BF16)
        dya0_ref[...] = dya0
        dcg_ref[...] = (dya1 * ya0 * _dsilu(cg, s)).astype(BF16)
        ya1_ref[...] = (ya0 * silu).astype(BF16)
        dcs_ref[...] = _dot(dya0, wpwt_ref[...]).astype(BF16)
        ag = ag_ref[...].astype(F32)
        sa = _sig(ag)
        silua = ag * sa
        ov = o_ref[...].astype(F32)
        dyb0 = _dot(dyb_ref[...], wbrat_ref[...])
        do_ref[...] = (dyb0 * silua).astype(BF16)
        dag_ref[...] = (dyb0 * ov * _dsilu(ag, sa)).astype(BF16)
        yb0_ref[...] = (ov * silua).astype(BF16)

    row = pl.BlockSpec((tm, D), lambda i: (i, 0))
    return pl.pallas_call(
        body, name="branches_bwd", grid=(T // tm,),
        out_shape=[jax.ShapeDtypeStruct((T, D), BF16)] * 7,
        in_specs=[row] * 4 + [pl.BlockSpec((tm, D), lambda i: (i, CB_CGATE)), pl.BlockSpec((tm, D), lambda i: (i, CB_AGATE)),
                              _weight_spec((D, D)), _weight_spec((D, D)), _weight_spec((D, D))],
        out_specs=[row] * 7,
        compiler_params=_cparams("parallel"),
    )(dya, dyb, ya0, o, z2, z2, w_brc_t, w_pw_t, w_bra_t)


def _conv_bwd(z3, c3, dcs3, w_dw, ln_g, ln_b, ex):
    NS, S, _ = z3.shape
    tm, nt, r = _conv_tiles(S)

    def body(*refs):
        ins, ex_in, outs, ex_out, scratch, ex_sems = _host_split(refs, 9, 3, 3, ex)
        val_ref, glu_ref, c_ref, dcs_ref, hc_ref, hdcs_ref, w_ref, g_ref, bb_ref = ins
        dz_ref, gw_ref, gvec_ref = outs
        dsh, ubuf, dubuf = scratch
        i = pl.program_id(1)
        first = (pl.program_id(0) == 0) & (i == 0)
        ex.carried(ex_in, ex_out, ex_sems, first, (pl.program_id(0) == NS - 1) & (i == nt - 1))

        @pl.when(first)
        def _():
            gw_ref[...] = jnp.zeros_like(gw_ref)
            gvec_ref[...] = jnp.zeros_like(gvec_ref)

        val = val_ref[...].astype(F32)
        sg = _sig(glu_ref[...].astype(F32))
        ubuf[...] = val * sg

        def ln_bwd(cv, dcs):
            cv = cv.astype(F32)
            mu = jnp.mean(cv, axis=-1, keepdims=True)
            xc = cv - mu
            rstd = lax.rsqrt(jnp.mean(xc * xc, axis=-1, keepdims=True) + EPS)
            xhat = xc * rstd
            cl = xhat * g_ref[...] + bb_ref[...]
            s = _sig(cl)
            dcl = dcs.astype(F32) * _dsilu(cl, s)
            dxh = dcl * g_ref[...]
            dc = rstd * (dxh - jnp.mean(dxh, axis=-1, keepdims=True) - xhat * jnp.mean(dxh * xhat, axis=-1, keepdims=True))
            return dc, dcl, xhat

        dc, dcl, xhat = ln_bwd(c_ref[...], dcs_ref[...])
        dsh[0, 0:tm, :] = dc
        dch, _, _ = ln_bwd(hc_ref[...], hdcs_ref[...])
        dsh[0, tm:, :] = jnp.where(i < nt - 1, dch, 0.0)
        gvec_ref[0:8, :] += _rowsum8(dcl * xhat)
        gvec_ref[8:16, :] += _rowsum8(dcl)
        gvec_ref[16:24, :] += _rowsum8(dc)
        _fill_shifted(dsh, tm + CONV_HALO - 8)

        def dc_ahead(r0, k):
            a, b = divmod(CONV_K - 1 - k, 8)
            return dsh[b, r0 + 8 * a : r0 + 8 * a + CONV_ROWS, :]

        for r0 in range(0, tm, CONV_ROWS):
            acc = jnp.zeros((CONV_ROWS, D), F32)
            for k in range(CONV_K):
                acc = acc + w_ref[k : k + 1, :] * dc_ahead(r0, k)
            dubuf[r0 : r0 + CONV_ROWS, :] = acc
        for r0 in range(0, tm, CONV_ROWS):
            ur = ubuf[r0 : r0 + CONV_ROWS, :]
            for k in range(CONV_K):
                gw_ref[8 * k : 8 * k + 8, :] += _rowsum8(ur * dc_ahead(r0, k))
        du = dubuf[...]
        dz_ref[:, 0:D] = (du * sg).astype(BF16)
        dz_ref[:, D:] = (du * val * sg * (1.0 - sg)).astype(BF16)

    def cur(cb):
        return pl.BlockSpec((None, tm, D), lambda s, i: (s, i, cb))

    nxt = pl.BlockSpec((None, CONV_HALO, D), lambda s, i: (s, jnp.minimum((i + 1) * r, S // CONV_HALO - 1), 0))
    vec = pl.BlockSpec((1, D), lambda s, i: (0, 0))
    return pl.pallas_call(
        body, name="conv_bwd", grid=(NS, nt),
        out_shape=[jax.ShapeDtypeStruct((NS, S, 2 * D), BF16), jax.ShapeDtypeStruct((CONV_HALO * 8, D), F32),
                   jax.ShapeDtypeStruct((24, D), F32)] + ex.out_shape,
        in_specs=[cur(CB_VAL), cur(CB_GLU), cur(0), cur(0), nxt, nxt,
                  pl.BlockSpec((CONV_HALO, D), lambda s, i: (0, 0)), vec, vec] + ex.specs,
        out_specs=[pl.BlockSpec((None, tm, 2 * D), lambda s, i: (s, i, 0)),
                   pl.BlockSpec((CONV_HALO * 8, D), lambda s, i: (0, 0)), pl.BlockSpec((24, D), lambda s, i: (0, 0))] + ex.specs,
        scratch_shapes=[pltpu.VMEM((8, tm + CONV_HALO, D), F32), pltpu.VMEM((tm, D), F32), pltpu.VMEM((tm, D), F32)] + ex.scratch,
        compiler_params=_cparams("arbitrary", "arbitrary"),
    )(z3, z3, c3, dcs3, c3, dcs3, w_dw, ln_g, ln_b, *ex.arrs)


def _attn_bwd(qr3, kv3, do3, o3, lse3, sinks, cos3, sa3, sb3):
    NS, S, _ = qr3.shape

    def body(sink_ref, q_ref, kvc_ref, kvp_ref, do_ref, o_ref, lse_ref, cos_ref, sa_ref, sb_ref,
             dq_ref, dkc_ref, dkp_ref, dvc_ref, dvp_ref, dsk_ref):
        has_prev = pl.program_id(1) > 0
        mask = _attn_mask(has_prev)
        tabs = cos_ref[...], sa_ref[...], sb_ref[...]
        low_q = lax.broadcasted_iota(jnp.int32, (QROWS, 128), 1) < HEAD_DIM
        dk_g, dv_g = [], []
        for g in range(N_KV):
            qs = _stack_pairs(q_ref, g)
            dos = _stack_pairs(do_ref, g)
            prod = dos.astype(F32) * _stack_pairs(o_ref, g).astype(F32)
            deltas = [jnp.sum(jnp.where(low_q, prod, 0.0), axis=-1, keepdims=True),
                      jnp.sum(jnp.where(low_q, 0.0, prod), axis=-1, keepdims=True)]
            dq_acc, dk_par, dv_par = None, [], []
            for par in range(2):
                k2 = _kv2_block(kvc_ref, kvp_ref, has_prev, _kv2_col(0, g, par))
                v2 = _kv2_block(kvc_ref, kvp_ref, has_prev, _kv2_col(1, g, par))
                lse = _head_col_load(lse_ref, g, par)
                p = jnp.exp(jnp.where(mask, _dot_nt(qs, k2), NEG_BIG) - lse)
                ds = (p * (_dot_nt(dos, v2) - deltas[par])).astype(BF16)
                dq = _dot(ds, k2)
                dq_acc = dq if dq_acc is None else dq_acc + dq
                dk_par.append(_dot_tn(ds, qs))
                dv_par.append(_dot_tn(p.astype(BF16), dos))
                _head_col_store(dsk_ref, g, par, -jnp.exp(_sink_col(sink_ref, g, par) - lse) * deltas[par])
            for j in range(PAIRS):
                dq_pair = _rope(dq_acc[j * BLK : (j + 1) * BLK], *tabs, sign=-1.0) * (HEAD_DIM ** -0.5)
                dq_ref[:, (PAIRS * g + j) * 128 : (PAIRS * g + j + 1) * 128] = dq_pair.astype(BF16)
            dk_g.append(dk_par[0] + pltpu.roll(dk_par[1], HEAD_DIM, 1))
            dv_g.append(dv_par[0] + pltpu.roll(dv_par[1], HEAD_DIM, 1))
        low_k = lax.broadcasted_iota(jnp.int32, (2 * BLK, KV_W), 1) < HEAD_DIM
        dk = jnp.where(low_k, dk_g[0], pltpu.roll(dk_g[1], HEAD_DIM, 1))
        dv = jnp.where(low_k, dv_g[0], pltpu.roll(dv_g[1], HEAD_DIM, 1))
        dkp_ref[...] = dk[0:BLK]
        dkc_ref[...] = dk[BLK:]
        dvp_ref[...] = dv[0:BLK]
        dvc_ref[...] = dv[BLK:]

    qspec = pl.BlockSpec((None, BLK, D), lambda s, n: (s, n, 0))
    kvspec = pl.BlockSpec((None, BLK, KV_W), lambda s, n: (s, n, 0))
    hspec = pl.BlockSpec((None, BLK, N_HEADS), lambda s, n: (s, n, 0))
    kv = jax.ShapeDtypeStruct((NS, S, KV_W), F32)
    return pl.pallas_call(
        body, name="attn_bwd", grid=(NS, S // BLK),
        out_shape=[jax.ShapeDtypeStruct((NS, S, D), BF16), kv, kv, kv, kv, jax.ShapeDtypeStruct((NS, S, N_HEADS), F32)],
        in_specs=list(_attn_specs()) + [qspec, qspec, hspec, kvspec, kvspec, kvspec],
        out_specs=[qspec, kvspec, kvspec, kvspec, kvspec, hspec],
        compiler_params=_cparams("parallel", "parallel"),
    )(sinks, qr3, kv3, kv3, do3, o3, lse3, cos3, sa3, sb3)


def _attn_post(dkc3, dkp3, dvc3, dvp3, cos3, sa3, sb3):
    NS, S, _ = dkc3.shape
    tm = min(4 * BLK, S)
    nt = S // tm

    def body(dkc_ref, dkp_ref, dkn_ref, dvc_ref, dvp_ref, dvn_ref, cos_ref, sa_ref, sb_ref, dkv_ref):
        has_next = pl.program_id(1) < nt - 1

        def join(cur_ref, prev_ref, next_ref):
            ahead = jnp.where(has_next, next_ref[...], 0.0)
            shifted = ahead if tm == BLK else jnp.concatenate([prev_ref[BLK:, :], ahead], axis=0)
            return cur_ref[...] + shifted

        tabs = cos_ref[...], sa_ref[...], sb_ref[...]
        dkv_ref[:, 0:KV_W] = _rope(join(dkc_ref, dkp_ref, dkn_ref), *tabs, sign=-1.0).astype(BF16)
        dkv_ref[:, KV_W:] = join(dvc_ref, dvp_ref, dvn_ref).astype(BF16)

    cur = pl.BlockSpec((None, tm, KV_W), lambda s, j: (s, j, 0))
    nxt = pl.BlockSpec((None, BLK, KV_W), lambda s, j: (s, jnp.minimum((j + 1) * (tm // BLK), S // BLK - 1), 0))
    return pl.pallas_call(
        body, name="attn_post", grid=(NS, nt),
        out_shape=jax.ShapeDtypeStruct((NS, S, 2 * KV_W), BF16),
        in_specs=[cur, cur, nxt, cur, cur, nxt, cur, cur, cur],
        out_specs=pl.BlockSpec((None, tm, 2 * KV_W), lambda s, j: (s, j, 0)),
        compiler_params=_cparams("parallel", "parallel"),
    )(dkc3, dkp3, dkp3, dvc3, dvp3, dvp3, cos3, sa3, sb3)


def _in_bwd(segs, w_in_t, x2, dx1, ln_pre, ex):
    T = x2.shape[0]
    tm = min(ROW_TILE, T)
    nt = T // tm
    ns = len(segs)
    widths = [s.shape[1] for s in segs]

    def body(*refs):
        ins, ex_in, (gx_ref, glp_ref), ex_out, _, ex_sems = _host_split(refs, ns + 4, 2, 0, ex)
        seg_refs = ins[:ns]
        wt_ref, x_ref, dx1_ref, g_ref = ins[ns:]
        i = pl.program_id(0)
        ex.carried(ex_in, ex_out, ex_sems, i == 0, i == nt - 1)

        @pl.when(i == 0)
        def _():
            glp_ref[...] = jnp.zeros_like(glp_ref)

        dh = None
        r0 = 0
        for sref, w in zip(seg_refs, widths):
            part = _dot(sref[...], wt_ref[r0 : r0 + w, :])
            dh = part if dh is None else dh + part
            r0 += w
        xv = x_ref[...]
        r1 = lax.rsqrt(jnp.mean(xv * xv, axis=-1, keepdims=True) + EPS)
        xhat = xv * r1
        glp_ref[...] += _rowsum8(dh * xhat)
        dhg = dh * g_ref[...]
        gx_ref[...] = dx1_ref[...] + r1 * (dhg - xhat * jnp.mean(dhg * xhat, axis=-1, keepdims=True))

    row = pl.BlockSpec((tm, D), lambda i: (i, 0))
    return pl.pallas_call(
        body, name="in_bwd", grid=(nt,),
        out_shape=[jax.ShapeDtypeStruct((T, D), F32), jax.ShapeDtypeStruct((8, D), F32)] + ex.out_shape,
        in_specs=[pl.BlockSpec((tm, w), lambda i: (i, 0)) for w in widths]
        + [_weight_spec((NW, D)), row, row, _weight_spec((1, D))] + ex.specs,
        out_specs=[row, pl.BlockSpec((8, D), lambda i: (0, 0))] + ex.specs,
        scratch_shapes=ex.scratch,
        compiler_params=_cparams("arbitrary"),
    )(*segs, w_in_t, x2, dx1, ln_pre, *ex.arrs)


def _grad_matmul(a, b, name):
    T, M = a.shape
    N = b.shape[1]
    tk = min(1024, T)
    nk = T // tk

    def body(a_ref, b_ref, o_ref, acc):
        k = pl.program_id(0)

        @pl.when(k == 0)
        def _():
            acc[...] = jnp.zeros_like(acc)

        acc[...] += _dot_tn(a_ref[...].astype(BF16), b_ref[...])

        @pl.when(k == nk - 1)
        def _():
            o_ref[...] = acc[...].astype(BF16)

    return pl.pallas_call(
        body, name=name, grid=(nk,), out_shape=jax.ShapeDtypeStruct((M, N), BF16),
        in_specs=[pl.BlockSpec((tk, M), lambda k: (k, 0)), pl.BlockSpec((tk, N), lambda k: (k, 0))],
        out_specs=pl.BlockSpec((M, N), lambda k: (0, 0)),
        scratch_shapes=[pltpu.VMEM((M, N), F32)],
        compiler_params=_cparams("arbitrary"),
    )(a, b)


def _pack_small(gw, gvec, glp_pre, glp_post, dsk):
    T = dsk.shape[0]

    def body(gw_ref, gvec_ref, pre_ref, post_ref, dsk_ref, gdw_ref, gs_ref):
        gwf = gw_ref[...].reshape(CONV_HALO, 8, D).sum(axis=1)
        for d in range(N_DEV):
            gdw_ref[d] = gwf[:, 128 * d : 128 * (d + 1)]
        gs_ref[...] = jnp.zeros_like(gs_ref)
        gs_ref[0:1, :] = jnp.sum(pre_ref[...], axis=0, keepdims=True)
        gs_ref[1:2, :] = jnp.sum(post_ref[...], axis=0, keepdims=True)
        gs_ref[2:3, :] = jnp.sum(gvec_ref[16:24, :], axis=0, keepdims=True)
        gs_ref[3:4, :] = jnp.sum(gvec_ref[0:8, :], axis=0, keepdims=True)
        gs_ref[4:5, :] = jnp.sum(gvec_ref[8:16, :], axis=0, keepdims=True)
        gs_ref[5:6, 0:N_HEADS] = jnp.sum(dsk_ref[...], axis=0, keepdims=True)

    return pl.pallas_call(
        body, name="pack_small",
        out_shape=[jax.ShapeDtypeStruct((N_DEV, CONV_HALO, 128), F32), jax.ShapeDtypeStruct((8, D), F32)],
        compiler_params=_cparams(),
    )(gw, gvec, glp_pre, glp_post, dsk)


def _adamw(parts, w, m, v, name):
    R, C = w.shape
    tr = R if R <= 256 else 128

    def body(p_ref, w_ref, m_ref, v_ref, g_ref, d_ref, nm_ref, nv_ref):
        g = p_ref[0].astype(F32)
        for s in range(1, N_DEV):
            g = g + p_ref[s].astype(F32)
        nm = ADAM_B1 * m_ref[...] + (1.0 - ADAM_B1) * g
        nv = ADAM_B2 * v_ref[...] + (1.0 - ADAM_B2) * (g * g)
        m_hat = nm / (1.0 - ADAM_B1 ** ADAM_STEP)
        v_hat = nv / (1.0 - ADAM_B2 ** ADAM_STEP)
        g_ref[...] = g
        d_ref[...] = -ADAM_LR * (m_hat / (jnp.sqrt(v_hat) + ADAM_EPS) + ADAM_WD * w_ref[...])
        nm_ref[...] = nm
        nv_ref[...] = nv

    blk = pl.BlockSpec((tr, C), lambda i: (i, 0))
    return pl.pallas_call(
        body, name=name, grid=(R // tr,), out_shape=[jax.ShapeDtypeStruct((R, C), F32)] * 4,
        in_specs=[pl.BlockSpec((N_DEV, tr, C), lambda i: (0, i, 0)), blk, blk, blk], out_specs=[blk] * 4,
        compiler_params=_cparams("parallel"),
    )(parts, w, m, v)


def kernel(x, p, positions, w_in, ln_pre, ln_post, w_dw, b_dw, conv_ln_g, conv_ln_b, w_pw, sinks, w_br_conv, w_br_attn, w_out, w_ple_gate, w_ple_proj, loss_target, m_w_in, m_ln_pre, m_ln_post, m_w_dw, m_b_dw, m_conv_ln_g, m_conv_ln_b, m_w_pw, m_sinks, m_w_br_conv, m_w_br_attn, m_w_out, m_w_ple_gate, m_w_ple_proj, v_w_in, v_ln_pre, v_ln_post, v_w_dw, v_b_dw, v_conv_ln_g, v_conv_ln_b, v_w_pw, v_sinks, v_w_br_conv, v_w_br_attn, v_w_out, v_w_ple_gate, v_w_ple_proj):
    NS, S, _ = x.shape
    T = NS * S
    x2 = x.reshape(T, D)
    p2 = p.reshape(T, PLE)
    tgt = loss_target.reshape(T, D)
    pos = positions.reshape(T, 1)

    row_sharded = [w_pw[0], w_br_conv[0], w_br_attn[0], w_out[0], w_ple_gate[0]]
    sh_rows = D // N_DEV
    (g_in,) = _all_gather([w_in[0].astype(BF16)])
    w_in_f = _to_perm(g_in.transpose(1, 0, 2).reshape(D, NW))
    w_in_t = w_in_f.T
    gather_rest = _Exchange([], [jnp.stack(row_sharded).astype(BF16), w_ple_proj[0].astype(BF16),
                                 jnp.pad(w_dw[0], ((0, CONV_HALO - CONV_K), (0, 0)))])

    cos, sa, sb = _rope_tables(pos)
    z2, h, g_rows, g_pp, g_dw = _in_proj(x2, ln_pre, w_in_f, gather_rest)
    full = [g_rows[:, j].reshape(D, D) for j in range(5)]
    w_pw_f, w_brc_f, w_bra_f, w_out_f, w_pg_f = full
    w_pp_f = g_pp.transpose(1, 0, 2).reshape(PLE, D)
    w_dw_f = g_dw.transpose(1, 0, 2).reshape(CONV_HALO, D)
    z3 = z2.reshape(NS, S, NW)
    c3, cs3 = _conv_fwd(z3, w_dw_f, b_dw, conv_ln_g, conv_ln_b)
    qr, kv2 = _rope_qk(z2, cos, sa, sb)
    qr3, kv3 = qr.reshape(NS, S, D), kv2.reshape(NS, S, 4 * N_KV * KV_W)
    sinks1 = sinks.reshape(N_HEADS)
    o3, lse3 = _attn_fwd(qr3, kv3, sinks1)
    o = o3.reshape(T, D)
    ya0, ya = _branch_a_fwd(cs3.reshape(T, D), z2, w_pw_f, w_brc_f)
    yb, m, mo, x1, gate, pp, dy, loss_blk = _merge_fwd(o, z2, ya, x2, p2, tgt, ln_post, w_bra_f, w_out_f, w_pg_f, w_pp_f)
    loss = lax.psum(loss_blk[0, 0], MESH_AXES)

    dgl, dpp, dx1, dmo, dya, dyb, dgc, dga, glp_post = _merge_bwd(dy, gate, pp, mo, ya, yb, z2, ln_post, w_pg_f.T, w_out_f.T)
    dya0, dcg, ya1, dcs, do, dag, yb0 = _branches_bwd(dya, dyb, ya0, o, z2, w_brc_f.T, w_pw_f.T, w_bra_f.T)
    gp_rows = [_grad_matmul(a, b, nm).reshape(N_DEV, sh_rows, D) for a, b, nm in (
        (cs3.reshape(T, D), dya0, "grad_w_pw"), (ya1, dya, "grad_w_br_conv"), (yb0, dyb, "grad_w_br_attn"),
        (m, dmo, "grad_w_out"), (x1, dgl, "grad_w_ple_gate"))]
    gp_pp = _grad_matmul(p2, dpp, "grad_w_ple_proj").reshape(PLE, N_DEV, D // N_DEV).transpose(1, 0, 2)
    conv_out = _conv_bwd(z3, c3, dcs.reshape(NS, S, D), w_dw_f, conv_ln_g, conv_ln_b, _Exchange(gp_rows + [gp_pp], []))
    dzvu3, gw, gvec, r_rows, r_pp = conv_out[0], conv_out[1], conv_out[2], conv_out[3:8], conv_out[8]
    tab3 = [t.reshape(NS, S, 128) for t in (cos, sa, sb)]
    dq3, dkc3, dkp3, dvc3, dvp3, dsk3 = _attn_bwd(qr3, kv3, do.reshape(NS, S, D), o3, lse3, sinks1, *tab3)
    dkv3 = _attn_post(dkc3, dkp3, dvc3, dvp3, *tab3)
    segs = [dzvu3.reshape(T, 2 * D), dcg, dq3.reshape(T, D), dag, dgc, dga, dkv3.reshape(T, 2 * KV_W)]
    gp_in = _from_perm(jnp.concatenate([_grad_matmul(h, s, f"grad_w_in_{j}") for j, s in enumerate(segs)], axis=1))
    gp_in = gp_in.reshape(D, N_DEV, NW // N_DEV).transpose(1, 0, 2)
    grad_x2, glp_pre, r_in = _in_bwd(segs, w_in_t, x2, dx1, ln_pre, _Exchange([gp_in], []))
    gp_dw, gp_small = _pack_small(gw, gvec, glp_pre, glp_post, dsk3.reshape(T, N_HEADS))
    r_dw, r_small = _Exchange([gp_dw], [gp_small]).alone("small_grad_exchange")

    res = {}
    res["w_in"] = _adamw(r_in, w_in[0], m_w_in[0], v_w_in[0], "adamw_w_in")
    names_rows = ["w_pw", "w_br_conv", "w_br_attn", "w_out", "w_ple_gate"]
    wmv = {"w_pw": (w_pw, m_w_pw, v_w_pw), "w_br_conv": (w_br_conv, m_w_br_conv, v_w_br_conv),
           "w_br_attn": (w_br_attn, m_w_br_attn, v_w_br_attn), "w_out": (w_out, m_w_out, v_w_out),
           "w_ple_gate": (w_ple_gate, m_w_ple_gate, v_w_ple_gate)}
    for nm, parts in zip(names_rows, r_rows):
        w_, m_, v_ = wmv[nm]
        res[nm] = _adamw(parts, w_[0], m_[0], v_[0], "adamw_" + nm)
    res["w_ple_proj"] = _adamw(r_pp, w_ple_proj[0], m_w_ple_proj[0], v_w_ple_proj[0], "adamw_w_ple_proj")
    pad_dw = lambda a: jnp.pad(a[0], ((0, CONV_HALO - CONV_K), (0, 0)))
    res["w_dw"] = [a[:CONV_K] for a in _adamw(r_dw, pad_dw(w_dw), pad_dw(m_w_dw), pad_dw(v_w_dw), "adamw_w_dw")]

    def stack_small(a_pre, a_post, a_b, a_g, a_bb, a_s):
        sk = jnp.pad(a_s, ((0, 0), (0, D - N_HEADS)))
        return jnp.concatenate([a_pre, a_post, a_b, a_g, a_bb, sk, jnp.zeros((2, D), F32)], axis=0)

    small = _adamw(
        r_small, stack_small(ln_pre, ln_post, b_dw, conv_ln_g, conv_ln_b, sinks),
        stack_small(m_ln_pre, m_ln_post, m_b_dw, m_conv_ln_g, m_conv_ln_b, m_sinks),
        stack_small(v_ln_pre, v_ln_post, v_b_dw, v_conv_ln_g, v_conv_ln_b, v_sinks), "adamw_small")
    for j, nm in enumerate(["ln_pre", "ln_post", "b_dw", "conv_ln_g", "conv_ln_b"]):
        res[nm] = [a[j] for a in small]
    res["sinks"] = [a[5, :N_HEADS] for a in small]

    order = ["w_in", "ln_pre", "ln_post", "w_dw", "b_dw", "conv_ln_g", "conv_ln_b", "w_pw", "sinks", "w_br_conv",
             "w_br_attn", "w_out", "w_ple_gate", "w_ple_proj"]
    outs = [loss, grad_x2.reshape(NS, S, D)]
    for kind in range(4):
        outs += [res[nm][kind][None] for nm in order]
    return tuple(outs)
```

```python
import functools

import numpy as np

import jax
import jax.numpy as jnp
from jax import lax
from jax.experimental import pallas as pl
from jax.experimental.pallas import tpu as pltpu

F32 = jnp.float32
BF16 = jnp.bfloat16

D = 1024
N_HEADS = 16
N_KV = 2
HEAD_DIM = 64
GROUP = N_HEADS // N_KV
KV_W = N_KV * HEAD_DIM
CONV_K = 31
CONV_HALO = 32
BLK = 128
ROPE_DIM = 16
ROPE_THETA = 500000.0
EPS = 1e-6
PLE = 256
NW = 7 * D + 2 * KV_W
N_DEV = 8
MESH_AXES = ("x", "y", "c")

CB_VAL, CB_GLU, CB_CGATE, CB_Q, CB_AGATE, CB_GCONV, CB_GATTN = range(7)
CB_K = 7 * D // KV_W
CB_V = CB_K + 1

ADAM_LR, ADAM_B1, ADAM_B2, ADAM_EPS, ADAM_WD, ADAM_STEP = 0.001, 0.9, 0.999, 1e-08, 0.01, 10

VMEM_LIMIT = 56 * 1024 * 1024
ROW_TILE = 256


def _cparams(*sem):
    return pltpu.CompilerParams(dimension_semantics=sem if sem else None, vmem_limit_bytes=VMEM_LIMIT)


def _sig(v):
    return 1.0 / (1.0 + jnp.exp(-v))


def _rowsum8(a):
    return a.reshape(a.shape[0] // 8, 8, a.shape[1]).sum(axis=0)


def _dot(a, b):
    return jnp.dot(a, b, preferred_element_type=F32)


def _dot_nt(a, b):
    return lax.dot_general(a, b, (((1,), (1,)), ((), ())), preferred_element_type=F32)


def _dot_tn(a, b):
    return lax.dot_general(a, b, (((0,), (0,)), ((), ())), preferred_element_type=F32)


def _to_perm(w):
    return jnp.concatenate([w[..., : 4 * D], w[..., 4 * D + 2 * KV_W :], w[..., 4 * D : 4 * D + 2 * KV_W]], axis=-1)


def _from_perm(g):
    return jnp.concatenate([g[..., : 4 * D], g[..., 7 * D :], g[..., 4 * D : 7 * D]], axis=-1)


def _my_place():
    return lax.axis_index("x"), lax.axis_index("y"), lax.axis_index("c")


def _slot(px, py, pc):
    return 4 * px + 2 * py + pc


def _all_gather(shards):
    n = len(shards)

    def body(*refs):
        ins, outs = refs[:n], refs[n : 2 * n]
        send_sems, recv_sems, local_sems = refs[2 * n :]
        x, y, c = _my_place()
        me, sibling = (x, y, c), (x, y, 1 - c)
        chips = [(1 - x, y), (x, 1 - y), (1 - x, 1 - y)]

        def copy(a, k, block, to, src=None):
            rows = outs[a].at[_slot(*block)]
            return pltpu.make_async_remote_copy(
                src_ref=rows if src is None else src, dst_ref=rows, send_sem=send_sems.at[a, k],
                recv_sem=recv_sems.at[a, k], device_id=to, device_id_type=pl.DeviceIdType.MESH)

        mine, first, passed = [], [], []
        for a in range(n):
            cp = pltpu.make_async_copy(ins[a], outs[a].at[_slot(*me)], local_sems.at[a])
            cp.start()
            mine.append(cp)
            fa = [copy(a, 0, me, sibling, src=ins[a])]
            fa += [copy(a, 1 + j, me, (*chip, c), src=ins[a]) for j, chip in enumerate(chips)]
            for cp in fa:
                cp.start()
            first += fa
        for j, chip in enumerate(chips):
            for a in range(n):
                copy(a, 1 + j, (*chip, c), me).wait_recv()
                cp = copy(a, 4 + j, (*chip, c), sibling)
                cp.start()
                passed.append(cp)
        for a in range(n):
            copy(a, 0, sibling, me).wait_recv()
            for j, chip in enumerate(chips):
                copy(a, 4 + j, (*chip, 1 - c), me).wait_recv()
        for cp in first + passed:
            cp.wait_send()
        for cp in mine:
            cp.wait()

    any_spec = pl.BlockSpec(memory_space=pl.ANY)
    return pl.pallas_call(
        body, name="weight_all_gather",
        out_shape=[jax.ShapeDtypeStruct((N_DEV,) + s.shape, s.dtype) for s in shards],
        in_specs=[any_spec] * n, out_specs=[any_spec] * n,
        scratch_shapes=[pltpu.SemaphoreType.DMA((n, 7)), pltpu.SemaphoreType.DMA((n, 7)), pltpu.SemaphoreType.DMA((n,))],
    )(*shards)


class _Exchange:
    def __init__(self, scatter, bcast):
        self.arrs = list(scatter) + list(bcast)
        self.n, self.n_sc = len(self.arrs), len(scatter)
        self.out_shape = [jax.ShapeDtypeStruct(a.shape, a.dtype) for a in scatter]
        self.out_shape += [jax.ShapeDtypeStruct((N_DEV,) + a.shape, a.dtype) for a in bcast]
        self.specs = [pl.BlockSpec(memory_space=pl.ANY)] * self.n
        self.scratch = [pltpu.SemaphoreType.DMA((self.n, 7)), pltpu.SemaphoreType.DMA((self.n, 7)),
                        pltpu.SemaphoreType.DMA((self.n,))]

    def _copies(self, ins, outs, sems):
        send_sems, recv_sems, local_sems = sems
        x, y, c = _my_place()
        me = _slot(x, y, c)
        peers = [(1 - x if k & 4 else x, 1 - y if k & 2 else y, 1 - c if k & 1 else c) for k in range(1, N_DEV)]
        mine, sends, arrivals = [], [], []
        for a in range(self.n):
            src = ins[a].at[me] if a < self.n_sc else ins[a]
            mine.append(pltpu.make_async_copy(src, outs[a].at[me], local_sems.at[a]))
        for k, peer in enumerate(peers):
            for a in range(self.n):
                src = ins[a].at[_slot(*peer)] if a < self.n_sc else ins[a]
                sends.append(pltpu.make_async_remote_copy(
                    src_ref=src, dst_ref=outs[a].at[me], send_sem=send_sems.at[a, k], recv_sem=recv_sems.at[a, k],
                    device_id=peer, device_id_type=pl.DeviceIdType.MESH))
                rows = outs[a].at[_slot(*peer)]
                arrivals.append(pltpu.make_async_remote_copy(
                    src_ref=rows, dst_ref=rows, send_sem=send_sems.at[a, k], recv_sem=recv_sems.at[a, k],
                    device_id=peer, device_id_type=pl.DeviceIdType.MESH))
        return mine, sends, arrivals

    def start(self, ins, outs, sems):
        mine, sends, _ = self._copies(ins, outs, sems)
        for cp in mine + sends:
            cp.start()

    def finish(self, ins, outs, sems):
        mine, sends, arrivals = self._copies(ins, outs, sems)
        for cp in arrivals:
            cp.wait_recv()
        for cp in sends:
            cp.wait_send()
        for cp in mine:
            cp.wait()

    def carried(self, refs_in, refs_out, sems, first, last):
        @pl.when(first)
        def _():
            self.start(refs_in, refs_out, sems)

        @pl.when(last)
        def _():
            self.finish(refs_in, refs_out, sems)

    def alone(self, name):
        n = self.n

        def body(*refs):
            ins, outs, sems = refs[:n], refs[n : 2 * n], refs[2 * n :]
            self.start(ins, outs, sems)
            self.finish(ins, outs, sems)

        return pl.pallas_call(body, name=name, out_shape=self.out_shape, in_specs=self.specs, out_specs=self.specs,
                              scratch_shapes=self.scratch)(*self.arrs)


def _host_split(refs, n_in, n_out, n_scratch, ex):
    k = ex.n if ex is not None else 0
    a = n_in
    b = a + k
    c = b + n_out
    d = c + k
    e = d + n_scratch
    return refs[:a], refs[a:b], refs[b:c], refs[c:d], refs[d:e], refs[e:]


def _rope_tables(pos):
    T = pos.shape[0]
    tm = min(1024, T)
    lane = np.arange(128) % HEAD_DIM
    inv = np.power(np.float32(ROPE_THETA), -np.arange(0, ROPE_DIM, 2, dtype=np.float32) / np.float32(ROPE_DIM)).astype(np.float32)
    half = ROPE_DIM // 2
    invf = np.where(lane < ROPE_DIM, inv[lane % half], 0.0).astype(np.float32)[None, :]
    m_a = (lane < half).astype(np.float32)[None, :]
    m_b = ((lane >= half) & (lane < ROPE_DIM)).astype(np.float32)[None, :]

    def body(pos_ref, invf_ref, ma_ref, mb_ref, cos_ref, sa_ref, sb_ref):
        ang = pos_ref[...].astype(F32) * invf_ref[...]
        sn = jnp.sin(ang)
        cos_ref[...] = jnp.cos(ang)
        sa_ref[...] = -sn * ma_ref[...]
        sb_ref[...] = sn * mb_ref[...]

    row = pl.BlockSpec((tm, 128), lambda i: (i, 0))
    cst = pl.BlockSpec((1, 128), lambda i: (0, 0))
    return pl.pallas_call(
        body, name="rope_tables", grid=(T // tm,), out_shape=[jax.ShapeDtypeStruct((T, 128), F32)] * 3,
        in_specs=[pl.BlockSpec((tm, 1), lambda i: (i, 0)), cst, cst, cst], out_specs=[row] * 3,
        compiler_params=_cparams("parallel"),
    )(pos, jnp.asarray(invf), jnp.asarray(m_a), jnp.asarray(m_b))


def _rope(t, cos, sa, sb, sign=1.0):
    parts = []
    for i in range(t.shape[1] // 128):
        ti = t[:, 128 * i : 128 * (i + 1)]
        up = pltpu.roll(ti, 128 - ROPE_DIM // 2, 1)
        dn = pltpu.roll(ti, ROPE_DIM // 2, 1)
        parts.append(ti * cos + sign * (up * sa + dn * sb))
    return parts[0] if len(parts) == 1 else jnp.concatenate(parts, axis=-1)


def _in_proj(x2, ln_pre, w_in, ex):
    T = x2.shape[0]
    tm = min(512, T)
    tn = NW // 2
    chunk = 512
    ni, nj = T // tm, NW // tn

    def body(*refs):
        (x_ref, g_ref, w_ref), ex_in, (z_ref, h_ref), ex_out, (hs,), ex_sems = _host_split(refs, 3, 2, 1, ex)
        i, j = pl.program_id(0), pl.program_id(1)
        ex.carried(ex_in, ex_out, ex_sems, (i == 0) & (j == 0), (i == ni - 1) & (j == nj - 1))

        @pl.when(j == 0)
        def _():
            xv = x_ref[...]
            r = lax.rsqrt(jnp.mean(xv * xv, axis=-1, keepdims=True) + EPS)
            h = (xv * r * g_ref[...]).astype(BF16)
            hs[...] = h
            h_ref[...] = h

        for c0 in range(0, tn, chunk):
            cw = min(chunk, tn - c0)
            z_ref[:, c0 : c0 + cw] = _dot(hs[...], w_ref[:, c0 : c0 + cw]).astype(BF16)

    return pl.pallas_call(
        body, name="in_proj", grid=(ni, nj),
        out_shape=[jax.ShapeDtypeStruct((T, NW), BF16), jax.ShapeDtypeStruct((T, D), BF16)] + ex.out_shape,
        in_specs=[pl.BlockSpec((tm, D), lambda i, j: (i, 0)), pl.BlockSpec((1, D), lambda i, j: (0, 0)),
                  pl.BlockSpec((D, tn), lambda i, j: (0, j))] + ex.specs,
        out_specs=[pl.BlockSpec((tm, tn), lambda i, j: (i, j)), pl.BlockSpec((tm, D), lambda i, j: (i, 0))] + ex.specs,
        scratch_shapes=[pltpu.VMEM((tm, D), BF16)] + ex.scratch,
        compiler_params=_cparams("arbitrary", "arbitrary"),
    )(x2, ln_pre, w_in, *ex.arrs)


def _conv_tiles(S):
    tm = min(ROW_TILE, S)
    return tm, S // tm, tm // CONV_HALO


CONV_ROWS_FWD = 32
CONV_ROWS = 16


def _fill_shifted(sh, rows):
    for b in range(1, 8):
        sh[b, 0:rows, :] = sh[0, b : b + rows, :]


def _conv_fwd(z3, w_dw, b_dw, ln_g, ln_b):
    NS, S, _ = z3.shape
    tm, nt, r = _conv_tiles(S)

    def body(val_ref, glu_ref, hval_ref, hglu_ref, w_ref, b_ref, g_ref, bb_ref, c_ref, cs_ref, ush, cbuf):
        i = pl.program_id(1)
        ush[0, CONV_HALO:, :] = val_ref[...].astype(F32) * _sig(glu_ref[...].astype(F32))
        uh = hval_ref[...].astype(F32) * _sig(hglu_ref[...].astype(F32))
        ush[0, 0:CONV_HALO, :] = jnp.where(i > 0, uh, 0.0)
        _fill_shifted(ush, tm + CONV_HALO - 8)
        for r0 in range(0, tm, CONV_ROWS_FWD):
            acc = jnp.zeros((CONV_ROWS_FWD, D), F32)
            for k in range(CONV_K):
                a, b = divmod(CONV_HALO - (CONV_K - 1) + k, 8)
                acc = acc + w_ref[k : k + 1, :] * ush[b, r0 + 8 * a : r0 + 8 * a + CONV_ROWS_FWD, :]
            cbuf[r0 : r0 + CONV_ROWS_FWD, :] = acc + b_ref[...]
        cv = cbuf[...]
        mu = jnp.mean(cv, axis=-1, keepdims=True)
        xc = cv - mu
        var = jnp.mean(xc * xc, axis=-1, keepdims=True)
        cl = xc * lax.rsqrt(var + EPS) * g_ref[...] + bb_ref[...]
        c_ref[...] = cv.astype(BF16)
        cs_ref[...] = (cl * _sig(cl)).astype(BF16)

    def cur(cb):
        return pl.BlockSpec((None, tm, D), lambda s, i: (s, i, cb))

    def halo(cb):
        return pl.BlockSpec((None, CONV_HALO, D), lambda s, i: (s, jnp.maximum(i * r - 1, 0), cb))

    vec = pl.BlockSpec((1, D), lambda s, i: (0, 0))
    out = pl.BlockSpec((None, tm, D), lambda s, i: (s, i, 0))
    return pl.pallas_call(
        body, name="conv_fwd", grid=(NS, nt),
        out_shape=[jax.ShapeDtypeStruct((NS, S, D), BF16)] * 2,
        in_specs=[cur(CB_VAL), cur(CB_GLU), halo(CB_VAL), halo(CB_GLU),
                  pl.BlockSpec((CONV_HALO, D), lambda s, i: (0, 0)), vec, vec, vec],
        out_specs=[out, out],
        scratch_shapes=[pltpu.VMEM((8, tm + CONV_HALO, D), F32), pltpu.VMEM((tm, D), F32)],
        compiler_params=_cparams("parallel", "parallel"),
    )(z3, z3, z3, z3, w_dw, b_dw, ln_g, ln_b)


PAIRS = GROUP // 2
QROWS = PAIRS * BLK


def _kv2_col(kind, g, par):
    return kind * 2 * N_KV + g * 2 + par


def _rope_qk(z2, cos, sa, sb):
    T = z2.shape[0]
    tm = min(ROW_TILE, T)

    def body(q_ref, k_ref, v_ref, cos_ref, sa_ref, sb_ref, qr_ref, kv_ref):
        tabs = cos_ref[...], sa_ref[...], sb_ref[...]
        qr_ref[...] = (_rope(q_ref[...].astype(F32), *tabs) * (HEAD_DIM ** -0.5)).astype(BF16)
        low = lax.broadcasted_iota(jnp.int32, (tm, KV_W), 1) < HEAD_DIM
        for kind, t in ((0, _rope(k_ref[...].astype(F32), *tabs)), (1, v_ref[...].astype(F32))):
            r = pltpu.roll(t, HEAD_DIM, 1)
            cols = {(0, 0): jnp.where(low, t, 0.0), (0, 1): jnp.where(low, 0.0, r),
                    (1, 0): jnp.where(low, r, 0.0), (1, 1): jnp.where(low, 0.0, t)}
            for (g, par), val in cols.items():
                c = _kv2_col(kind, g, par)
                kv_ref[:, c * KV_W : (c + 1) * KV_W] = val.astype(BF16)

    tab = pl.BlockSpec((tm, 128), lambda i: (i, 0))
    return pl.pallas_call(
        body, name="rope_qk", grid=(T // tm,),
        out_shape=[jax.ShapeDtypeStruct((T, D), BF16), jax.ShapeDtypeStruct((T, 4 * N_KV * KV_W), BF16)],
        in_specs=[pl.BlockSpec((tm, D), lambda i: (i, CB_Q)), pl.BlockSpec((tm, KV_W), lambda i: (i, CB_K)),
                  pl.BlockSpec((tm, KV_W), lambda i: (i, CB_V)), tab, tab, tab],
        out_specs=[pl.BlockSpec((tm, D), lambda i: (i, 0)), pl.BlockSpec((tm, 4 * N_KV * KV_W), lambda i: (i, 0))],
        compiler_params=_cparams("parallel"),
    )(z2, z2, z2, cos, sa, sb)


def _attn_mask(has_prev):
    qi = lax.broadcasted_iota(jnp.int32, (QROWS, 2 * BLK), 0) & (BLK - 1)
    kj = lax.broadcasted_iota(jnp.int32, (QROWS, 2 * BLK), 1)
    first_key = jnp.where(has_prev, 0, BLK)
    return (kj > qi) & (kj <= qi + BLK) & (kj >= first_key)


NEG_BIG = -1e30


def _attn_specs():
    q = pl.BlockSpec((None, BLK, D), lambda s, n: (s, n, 0))
    kv_cur = pl.BlockSpec((None, BLK, 4 * N_KV * KV_W), lambda s, n: (s, n, 0))
    kv_prev = pl.BlockSpec((None, BLK, 4 * N_KV * KV_W), lambda s, n: (s, jnp.maximum(n - 1, 0), 0))
    sink = pl.BlockSpec(memory_space=pltpu.SMEM)
    return sink, q, kv_cur, kv_prev


def _stack_pairs(ref, g):
    return jnp.concatenate([ref[:, (PAIRS * g + j) * 128 : (PAIRS * g + j + 1) * 128] for j in range(PAIRS)], axis=0)


def _unstack_pairs(ref, g, val):
    for j in range(PAIRS):
        ref[:, (PAIRS * g + j) * 128 : (PAIRS * g + j + 1) * 128] = val[j * BLK : (j + 1) * BLK].astype(ref.dtype)


def _pair_heads(g, par):
    return [GROUP * g + 2 * j + par for j in range(PAIRS)]


def _head_col_load(ref, g, par):
    return jnp.concatenate([ref[:, h : h + 1] for h in _pair_heads(g, par)], axis=0)


def _head_col_store(ref, g, par, col):
    for j, h in enumerate(_pair_heads(g, par)):
        ref[:, h : h + 1] = col[j * BLK : (j + 1) * BLK]


def _sink_col(sink_ref, g, par):
    return jnp.concatenate([jnp.full((BLK, 1), sink_ref[h], F32) for h in _pair_heads(g, par)], axis=0)


def _kv2_block(kvc_ref, kvp_ref, has_prev, c):
    col = slice(c * KV_W, (c + 1) * KV_W)
    prev = jnp.where(has_prev, kvp_ref[:, col], jnp.zeros((BLK, KV_W), BF16))
    return jnp.concatenate([prev, kvc_ref[:, col]], axis=0)


def _attn_fwd(qr3, kv3, sinks):
    NS, S, _ = qr3.shape

    def body(sink_ref, q_ref, kvc_ref, kvp_ref, o_ref, lse_ref):
        has_prev = pl.program_id(1) > 0
        mask = _attn_mask(has_prev)[0:BLK]
        for g in range(N_KV):
            kv = [[_kv2_block(kvc_ref, kvp_ref, has_prev, _kv2_col(kind, g, par)) for par in range(2)] for kind in range(2)]
            for j in range(PAIRS):
                cols = slice((PAIRS * g + j) * 128, (PAIRS * g + j + 1) * 128)
                q2 = q_ref[:, cols]
                o_pair = None
                for par in range(2):
                    h = GROUP * g + 2 * j + par
                    s = jnp.where(mask, _dot_nt(q2, kv[0][par]), NEG_BIG)
                    sk = sink_ref[h]
                    mx = jnp.maximum(jnp.max(s, axis=-1, keepdims=True), sk)
                    e = jnp.exp(s - mx)
                    den = jnp.sum(e, axis=-1, keepdims=True) + jnp.exp(sk - mx)
                    pv = _dot(e.astype(BF16), kv[1][par]) * (1.0 / den)
                    o_pair = pv if o_pair is None else o_pair + pv
                    lse_ref[:, h : h + 1] = mx + jnp.log(den)
                o_ref[:, cols] = o_pair.astype(BF16)

    return pl.pallas_call(
        body, name="attn_fwd", grid=(NS, S // BLK),
        out_shape=[jax.ShapeDtypeStruct((NS, S, D), BF16), jax.ShapeDtypeStruct((NS, S, N_HEADS), F32)],
        in_specs=list(_attn_specs()),
        out_specs=[pl.BlockSpec((None, BLK, D), lambda s, n: (s, n, 0)),
                   pl.BlockSpec((None, BLK, N_HEADS), lambda s, n: (s, n, 0))],
        compiler_params=_cparams("parallel", "parallel"),
    )(sinks, qr3, kv3, kv3)


def _weight_spec(shape):
    return pl.BlockSpec(shape, lambda i: (0,) * len(shape), pipeline_mode=pl.Buffered(1))


def _dsilu(v, s):
    return s * (1.0 + v * (1.0 - s))


MID_TILE = 256


def _mid(cs, o, z2, x2, p2, tgt, ln_post, weights):
    T = cs.shape[0]
    tm = min(MID_TILE, T)
    nt = T // tm
    n_bf = 16

    def body(cs_ref, o_ref, cg_ref, ag_ref, gc_ref, ga_ref, x_ref, p_ref, t_ref, g_ref,
             wpw, wbrc, wbra, wout, wpg, wpp,
             dya0_ref, ya1_ref, dya_ref, yb0_ref, dyb_ref, m_ref, dmo_ref, x1_ref, dgl_ref, dpp_ref,
             dcs_ref, do_ref, dcg_ref, dag_ref, dgc_ref, dga_ref, dx1_ref, loss_ref, glp_ref, lacc):
        i = pl.program_id(0)

        @pl.when(i == 0)
        def _():
            lacc[...] = jnp.zeros_like(lacc)
            glp_ref[...] = jnp.zeros_like(glp_ref)

        cg = cg_ref[...].astype(F32)
        scg = _sig(cg)
        silu_c = cg * scg
        ya0 = _dot(cs_ref[...], wpw[...])
        ya1 = (ya0 * silu_c).astype(BF16)
        ya1_ref[...] = ya1
        ya = _dot(ya1, wbrc[...])
        ag = ag_ref[...].astype(F32)
        sag = _sig(ag)
        silu_a = ag * sag
        ov = o_ref[...].astype(F32)
        yb0 = (ov * silu_a).astype(BF16)
        yb0_ref[...] = yb0
        yb = _dot(yb0, wbra[...])
        sgc = _sig(gc_ref[...].astype(F32))
        sga = _sig(ga_ref[...].astype(F32))
        mb = (sgc * ya + sga * yb).astype(BF16)
        m_ref[...] = mb
        mo = _dot(mb, wout[...])
        r2 = lax.rsqrt(jnp.mean(mo * mo, axis=-1, keepdims=True) + EPS)
        nrm = mo * r2
        x1 = x_ref[...] + nrm * g_ref[...]
        x1b = x1.astype(BF16)
        x1_ref[...] = x1b
        gate = _sig(_dot(x1b, wpg[...]))
        pp = _dot(p_ref[...].astype(BF16), wpp[...])
        e = x1 + gate * pp - t_ref[...]
        lacc[...] += _rowsum8(e * e)
        dy = e * (1.0 / D)

        dgl = (dy * pp * gate * (1.0 - gate)).astype(BF16)
        dgl_ref[...] = dgl
        dpp_ref[...] = (dy * gate).astype(BF16)
        dx1 = dy + _dot_nt(dgl, wpg[...])
        dx1_ref[...] = dx1
        glp_ref[...] += _rowsum8(dx1 * nrm)
        dn = dx1 * g_ref[...]
        dmo = (r2 * (dn - nrm * jnp.mean(dn * nrm, axis=-1, keepdims=True))).astype(BF16)
        dmo_ref[...] = dmo
        dm = _dot_nt(dmo, wout[...])
        dya = (dm * sgc).astype(BF16)
        dyb = (dm * sga).astype(BF16)
        dya_ref[...] = dya
        dyb_ref[...] = dyb
        dgc_ref[...] = (dm * ya * sgc * (1.0 - sgc)).astype(BF16)
        dga_ref[...] = (dm * yb * sga * (1.0 - sga)).astype(BF16)
        dya1 = _dot_nt(dya, wbrc[...])
        dya0 = (dya1 * silu_c).astype(BF16)
        dya0_ref[...] = dya0
        dcg_ref[...] = (dya1 * ya0 * _dsilu(cg, scg)).astype(BF16)
        dcs_ref[...] = _dot_nt(dya0, wpw[...]).astype(BF16)
        dyb0 = _dot_nt(dyb, wbra[...])
        do_ref[...] = (dyb0 * silu_a).astype(BF16)
        dag_ref[...] = (dyb0 * ov * _dsilu(ag, sag)).astype(BF16)

        @pl.when(i == nt - 1)
        def _():
            loss_ref[...] = jnp.full(loss_ref.shape, jnp.sum(lacc[...]) * (0.5 / D), F32)

    row = pl.BlockSpec((tm, D), lambda i: (i, 0))

    def zcol(cb):
        return pl.BlockSpec((tm, D), lambda i: (i, cb))

    bf = jax.ShapeDtypeStruct((T, D), BF16)
    return pl.pallas_call(
        body, name="mid_fwd_bwd", grid=(nt,),
        out_shape=[bf] * n_bf + [jax.ShapeDtypeStruct((T, D), F32), jax.ShapeDtypeStruct((8, 128), F32),
                                 jax.ShapeDtypeStruct((8, D), F32)],
        in_specs=[row, row, zcol(CB_CGATE), zcol(CB_AGATE), zcol(CB_GCONV), zcol(CB_GATTN), row,
                  pl.BlockSpec((tm, PLE), lambda i: (i, 0)), row, _weight_spec((1, D))]
        + [_weight_spec((D, D))] * 5 + [_weight_spec((PLE, D))],
        out_specs=[row] * (n_bf + 1) + [pl.BlockSpec((8, 128), lambda i: (0, 0)), pl.BlockSpec((8, D), lambda i: (0, 0))],
        scratch_shapes=[pltpu.VMEM((8, D), F32)],
        compiler_params=_cparams("arbitrary"),
    )(cs, o, z2, z2, z2, z2, x2, p2, tgt, ln_post, *weights)


def _conv_bwd(z3, c3, dcs3, w_dw, ln_g, ln_b, ex):
    NS, S, _ = z3.shape
    tm, nt, r = _conv_tiles(S)

    def body(*refs):
        ins, ex_in, outs, ex_out, scratch, ex_sems = _host_split(refs, 9, 3, 3, ex)
        val_ref, glu_ref, c_ref, dcs_ref, hc_ref, hdcs_ref, w_ref, g_ref, bb_ref = ins
        dz_ref, gw_ref, gvec_ref = outs
        dsh, ubuf, dubuf = scratch
        i = pl.program_id(1)
        first = (pl.program_id(0) == 0) & (i == 0)
        ex.carried(ex_in, ex_out, ex_sems, first, (pl.program_id(0) == NS - 1) & (i == nt - 1))

        @pl.when(first)
        def _():
            gw_ref[...] = jnp.zeros_like(gw_ref)
            gvec_ref[...] = jnp.zeros_like(gvec_ref)

        val = val_ref[...].astype(F32)
        sg = _sig(glu_ref[...].astype(F32))
        ubuf[...] = val * sg

        def ln_bwd(cv, dcs):
            cv = cv.astype(F32)
            mu = jnp.mean(cv, axis=-1, keepdims=True)
            xc = cv - mu
            rstd = lax.rsqrt(jnp.mean(xc * xc, axis=-1, keepdims=True) + EPS)
            xhat = xc * rstd
            cl = xhat * g_ref[...] + bb_ref[...]
            s = _sig(cl)
            dcl = dcs.astype(F32) * _dsilu(cl, s)
            dxh = dcl * g_ref[...]
            dc = rstd * (dxh - jnp.mean(dxh, axis=-1, keepdims=True) - xhat * jnp.mean(dxh * xhat, axis=-1, keepdims=True))
            return dc, dcl, xhat

        dc, dcl, xhat = ln_bwd(c_ref[...], dcs_ref[...])
        dsh[0, 0:tm, :] = dc
        dch, _, _ = ln_bwd(hc_ref[...], hdcs_ref[...])
        dsh[0, tm:, :] = jnp.where(i < nt - 1, dch, 0.0)
        gvec_ref[0:8, :] += _rowsum8(dcl * xhat)
        gvec_ref[8:16, :] += _rowsum8(dcl)
        gvec_ref[16:24, :] += _rowsum8(dc)
        _fill_shifted(dsh, tm + CONV_HALO - 8)

        def dc_ahead(r0, k):
            a, b = divmod(CONV_K - 1 - k, 8)
            return dsh[b, r0 + 8 * a : r0 + 8 * a + CONV_ROWS, :]

        for r0 in range(0, tm, CONV_ROWS):
            acc = jnp.zeros((CONV_ROWS, D), F32)
            for k in range(CONV_K):
                acc = acc + w_ref[k : k + 1, :] * dc_ahead(r0, k)
            dubuf[r0 : r0 + CONV_ROWS, :] = acc
        for r0 in range(0, tm, CONV_ROWS):
            ur = ubuf[r0 : r0 + CONV_ROWS, :]
            for k in range(CONV_K):
                gw_ref[8 * k : 8 * k + 8, :] += _rowsum8(ur * dc_ahead(r0, k))
        du = dubuf[...]
        dz_ref[:, 0:D] = (du * sg).astype(BF16)
        dz_ref[:, D:] = (du * val * sg * (1.0 - sg)).astype(BF16)

    def cur(cb):
        return pl.BlockSpec((None, tm, D), lambda s, i: (s, i, cb))

    nxt = pl.BlockSpec((None, CONV_HALO, D), lambda s, i: (s, jnp.minimum((i + 1) * r, S // CONV_HALO - 1), 0))
    vec = pl.BlockSpec((1, D), lambda s, i: (0, 0))
    return pl.pallas_call(
        body, name="conv_bwd", grid=(NS, nt),
        out_shape=[jax.ShapeDtypeStruct((NS, S, 2 * D), BF16), jax.ShapeDtypeStruct((CONV_HALO * 8, D), F32),
                   jax.ShapeDtypeStruct((24, D), F32)] + ex.out_shape,
        in_specs=[cur(CB_VAL), cur(CB_GLU), cur(0), cur(0), nxt, nxt,
                  pl.BlockSpec((CONV_HALO, D), lambda s, i: (0, 0)), vec, vec] + ex.specs,
        out_specs=[pl.BlockSpec((None, tm, 2 * D), lambda s, i: (s, i, 0)),
                   pl.BlockSpec((CONV_HALO * 8, D), lambda s, i: (0, 0)), pl.BlockSpec((24, D), lambda s, i: (0, 0))] + ex.specs,
        scratch_shapes=[pltpu.VMEM((8, tm + CONV_HALO, D), F32), pltpu.VMEM((tm, D), F32), pltpu.VMEM((tm, D), F32)] + ex.scratch,
        compiler_params=_cparams("arbitrary", "arbitrary"),
    )(z3, z3, c3, dcs3, c3, dcs3, w_dw, ln_g, ln_b, *ex.arrs)


def _attn_bwd(qr3, kv3, do3, o3, lse3, sinks, cos3, sa3, sb3):
    NS, S, _ = qr3.shape

    def body(sink_ref, q_ref, kvc_ref, kvp_ref, do_ref, o_ref, lse_ref, cos_ref, sa_ref, sb_ref,
             dq_ref, dkc_ref, dkp_ref, dvc_ref, dvp_ref, dsk_ref):
        has_prev = pl.program_id(1) > 0
        mask = _attn_mask(has_prev)
        tabs = cos_ref[...], sa_ref[...], sb_ref[...]
        low_q = lax.broadcasted_iota(jnp.int32, (QROWS, 128), 1) < HEAD_DIM
        dk_g, dv_g = [], []
        for g in range(N_KV):
            qs = _stack_pairs(q_ref, g)
            dos = _stack_pairs(do_ref, g)
            prod = dos.astype(F32) * _stack_pairs(o_ref, g).astype(F32)
            deltas = [jnp.sum(jnp.where(low_q, prod, 0.0), axis=-1, keepdims=True),
                      jnp.sum(jnp.where(low_q, 0.0, prod), axis=-1, keepdims=True)]
            dq_acc, dk_par, dv_par = None, [], []
            for par in range(2):
                k2 = _kv2_block(kvc_ref, kvp_ref, has_prev, _kv2_col(0, g, par))
                v2 = _kv2_block(kvc_ref, kvp_ref, has_prev, _kv2_col(1, g, par))
                lse = _head_col_load(lse_ref, g, par)
                p = jnp.exp(jnp.where(mask, _dot_nt(qs, k2), NEG_BIG) - lse)
                ds = (p * (_dot_nt(dos, v2) - deltas[par])).astype(BF16)
                dq = _dot(ds, k2)
                dq_acc = dq if dq_acc is None else dq_acc + dq
                dk_par.append(_dot_tn(ds, qs))
                dv_par.append(_dot_tn(p.astype(BF16), dos))
                _head_col_store(dsk_ref, g, par, -jnp.exp(_sink_col(sink_ref, g, par) - lse) * deltas[par])
            for j in range(PAIRS):
                dq_pair = _rope(dq_acc[j * BLK : (j + 1) * BLK], *tabs, sign=-1.0) * (HEAD_DIM ** -0.5)
                dq_ref[:, (PAIRS * g + j) * 128 : (PAIRS * g + j + 1) * 128] = dq_pair.astype(BF16)
            dk_g.append(dk_par[0] + pltpu.roll(dk_par[1], HEAD_DIM, 1))
            dv_g.append(dv_par[0] + pltpu.roll(dv_par[1], HEAD_DIM, 1))
        low_k = lax.broadcasted_iota(jnp.int32, (2 * BLK, KV_W), 1) < HEAD_DIM
        dk = jnp.where(low_k, dk_g[0], pltpu.roll(dk_g[1], HEAD_DIM, 1))
        dv = jnp.where(low_k, dv_g[0], pltpu.roll(dv_g[1], HEAD_DIM, 1))
        dkp_ref[...] = dk[0:BLK]
        dkc_ref[...] = dk[BLK:]
        dvp_ref[...] = dv[0:BLK]
        dvc_ref[...] = dv[BLK:]

    qspec = pl.BlockSpec((None, BLK, D), lambda s, n: (s, n, 0))
    kvspec = pl.BlockSpec((None, BLK, KV_W), lambda s, n: (s, n, 0))
    hspec = pl.BlockSpec((None, BLK, N_HEADS), lambda s, n: (s, n, 0))
    kv = jax.ShapeDtypeStruct((NS, S, KV_W), F32)
    return pl.pallas_call(
        body, name="attn_bwd", grid=(NS, S // BLK),
        out_shape=[jax.ShapeDtypeStruct((NS, S, D), BF16), kv, kv, kv, kv, jax.ShapeDtypeStruct((NS, S, N_HEADS), F32)],
        in_specs=list(_attn_specs()) + [qspec, qspec, hspec, kvspec, kvspec, kvspec],
        out_specs=[qspec, kvspec, kvspec, kvspec, kvspec, hspec],
        compiler_params=_cparams("parallel", "parallel"),
    )(sinks, qr3, kv3, kv3, do3, o3, lse3, cos3, sa3, sb3)


def _attn_post(dkc3, dkp3, dvc3, dvp3, cos3, sa3, sb3):
    NS, S, _ = dkc3.shape
    tm = min(4 * BLK, S)
    nt = S // tm

    def body(dkc_ref, dkp_ref, dkn_ref, dvc_ref, dvp_ref, dvn_ref, cos_ref, sa_ref, sb_ref, dkv_ref):
        has_next = pl.program_id(1) < nt - 1

        def join(cur_ref, prev_ref, next_ref):
            ahead = jnp.where(has_next, next_ref[...], 0.0)
            shifted = ahead if tm == BLK else jnp.concatenate([prev_ref[BLK:, :], ahead], axis=0)
            return cur_ref[...] + shifted

        tabs = cos_ref[...], sa_ref[...], sb_ref[...]
        dkv_ref[:, 0:KV_W] = _rope(join(dkc_ref, dkp_ref, dkn_ref), *tabs, sign=-1.0).astype(BF16)
        dkv_ref[:, KV_W:] = join(dvc_ref, dvp_ref, dvn_ref).astype(BF16)

    cur = pl.BlockSpec((None, tm, KV_W), lambda s, j: (s, j, 0))
    nxt = pl.BlockSpec((None, BLK, KV_W), lambda s, j: (s, jnp.minimum((j + 1) * (tm // BLK), S // BLK - 1), 0))
    return pl.pallas_call(
        body, name="attn_post", grid=(NS, nt),
        out_shape=jax.ShapeDtypeStruct((NS, S, 2 * KV_W), BF16),
        in_specs=[cur, cur, nxt, cur, cur, nxt, cur, cur, cur],
        out_specs=pl.BlockSpec((None, tm, 2 * KV_W), lambda s, j: (s, j, 0)),
        compiler_params=_cparams("parallel", "parallel"),
    )(dkc3, dkp3, dkp3, dvc3, dvp3, dvp3, cos3, sa3, sb3)


def _in_bwd(segs, w_in_t, x2, dx1, ln_pre, ex):
    T = x2.shape[0]
    tm = min(ROW_TILE, T)
    nt = T // tm
    ns = len(segs)
    widths = [s.shape[1] for s in segs]

    def body(*refs):
        ins, ex_in, (gx_ref, glp_ref), ex_out, _, ex_sems = _host_split(refs, ns + 4, 2, 0, ex)
        seg_refs = ins[:ns]
        wt_ref, x_ref, dx1_ref, g_ref = ins[ns:]
        i = pl.program_id(0)
        ex.carried(ex_in, ex_out, ex_sems, i == 0, i == nt - 1)

        @pl.when(i == 0)
        def _():
            glp_ref[...] = jnp.zeros_like(glp_ref)

        dh = None
        r0 = 0
        for sref, w in zip(seg_refs, widths):
            part = _dot(sref[...], wt_ref[r0 : r0 + w, :])
            dh = part if dh is None else dh + part
            r0 += w
        xv = x_ref[...]
        r1 = lax.rsqrt(jnp.mean(xv * xv, axis=-1, keepdims=True) + EPS)
        xhat = xv * r1
        glp_ref[...] += _rowsum8(dh * xhat)
        dhg = dh * g_ref[...]
        gx_ref[...] = dx1_ref[...] + r1 * (dhg - xhat * jnp.mean(dhg * xhat, axis=-1, keepdims=True))

    row = pl.BlockSpec((tm, D), lambda i: (i, 0))
    return pl.pallas_call(
        body, name="in_bwd", grid=(nt,),
        out_shape=[jax.ShapeDtypeStruct((T, D), F32), jax.ShapeDtypeStruct((8, D), F32)] + ex.out_shape,
        in_specs=[pl.BlockSpec((tm, w), lambda i: (i, 0)) for w in widths]
        + [_weight_spec((NW, D)), row, row, _weight_spec((1, D))] + ex.specs,
        out_specs=[row, pl.BlockSpec((8, D), lambda i: (0, 0))] + ex.specs,
        scratch_shapes=ex.scratch,
        compiler_params=_cparams("arbitrary"),
    )(*segs, w_in_t, x2, dx1, ln_pre, *ex.arrs)


def _grad_matmul(a, b, name):
    T, M = a.shape
    N = b.shape[1]
    tk = min(1024, T)
    nk = T // tk

    def body(a_ref, b_ref, o_ref, acc):
        k = pl.program_id(0)

        @pl.when(k == 0)
        def _():
            acc[...] = jnp.zeros_like(acc)

        acc[...] += _dot_tn(a_ref[...].astype(BF16), b_ref[...])

        @pl.when(k == nk - 1)
        def _():
            o_ref[...] = acc[...].astype(BF16)

    return pl.pallas_call(
        body, name=name, grid=(nk,), out_shape=jax.ShapeDtypeStruct((M, N), BF16),
        in_specs=[pl.BlockSpec((tk, M), lambda k: (k, 0)), pl.BlockSpec((tk, N), lambda k: (k, 0))],
        out_specs=pl.BlockSpec((M, N), lambda k: (0, 0)),
        scratch_shapes=[pltpu.VMEM((M, N), F32)],
        compiler_params=_cparams("arbitrary"),
    )(a, b)


def _pack_small(gw, gvec, glp_pre, glp_post, dsk):
    T = dsk.shape[0]

    def body(gw_ref, gvec_ref, pre_ref, post_ref, dsk_ref, gdw_ref, gs_ref):
        gwf = gw_ref[...].reshape(CONV_HALO, 8, D).sum(axis=1)
        for d in range(N_DEV):
            gdw_ref[d] = gwf[:, 128 * d : 128 * (d + 1)]
        gs_ref[...] = jnp.zeros_like(gs_ref)
        gs_ref[0:1, :] = jnp.sum(pre_ref[...], axis=0, keepdims=True)
        gs_ref[1:2, :] = jnp.sum(post_ref[...], axis=0, keepdims=True)
        gs_ref[2:3, :] = jnp.sum(gvec_ref[16:24, :], axis=0, keepdims=True)
        gs_ref[3:4, :] = jnp.sum(gvec_ref[0:8, :], axis=0, keepdims=True)
        gs_ref[4:5, :] = jnp.sum(gvec_ref[8:16, :], axis=0, keepdims=True)
        gs_ref[5:6, 0:N_HEADS] = jnp.sum(dsk_ref[...], axis=0, keepdims=True)

    return pl.pallas_call(
        body, name="pack_small",
        out_shape=[jax.ShapeDtypeStruct((N_DEV, CONV_HALO, 128), F32), jax.ShapeDtypeStruct((8, D), F32)],
        compiler_params=_cparams(),
    )(gw, gvec, glp_pre, glp_post, dsk)


def _adamw(parts, w, m, v, name):
    R, C = w.shape
    tr = R if R <= 256 else 128

    def body(p_ref, w_ref, m_ref, v_ref, g_ref, d_ref, nm_ref, nv_ref):
        g = p_ref[0].astype(F32)
        for s in range(1, N_DEV):
            g = g + p_ref[s].astype(F32)
        nm = ADAM_B1 * m_ref[...] + (1.0 - ADAM_B1) * g
        nv = ADAM_B2 * v_ref[...] + (1.0 - ADAM_B2) * (g * g)
        m_hat = nm / (1.0 - ADAM_B1 ** ADAM_STEP)
        v_hat = nv / (1.0 - ADAM_B2 ** ADAM_STEP)
        g_ref[...] = g
        d_ref[...] = -ADAM_LR * (m_hat / (jnp.sqrt(v_hat) + ADAM_EPS) + ADAM_WD * w_ref[...])
        nm_ref[...] = nm
        nv_ref[...] = nv

    blk = pl.BlockSpec((tr, C), lambda i: (i, 0))
    return pl.pallas_call(
        body, name=name, grid=(R // tr,), out_shape=[jax.ShapeDtypeStruct((R, C), F32)] * 4,
        in_specs=[pl.BlockSpec((N_DEV, tr, C), lambda i: (0, i, 0)), blk, blk, blk], out_specs=[blk] * 4,
        compiler_params=_cparams("parallel"),
    )(parts, w, m, v)


def kernel(x, p, positions, w_in, ln_pre, ln_post, w_dw, b_dw, conv_ln_g, conv_ln_b, w_pw, sinks, w_br_conv, w_br_attn, w_out, w_ple_gate, w_ple_proj, loss_target, m_w_in, m_ln_pre, m_ln_post, m_w_dw, m_b_dw, m_conv_ln_g, m_conv_ln_b, m_w_pw, m_sinks, m_w_br_conv, m_w_br_attn, m_w_out, m_w_ple_gate, m_w_ple_proj, v_w_in, v_ln_pre, v_ln_post, v_w_dw, v_b_dw, v_conv_ln_g, v_conv_ln_b, v_w_pw, v_sinks, v_w_br_conv, v_w_br_attn, v_w_out, v_w_ple_gate, v_w_ple_proj):
    NS, S, _ = x.shape
    T = NS * S
    x2 = x.reshape(T, D)
    p2 = p.reshape(T, PLE)
    tgt = loss_target.reshape(T, D)
    pos = positions.reshape(T, 1)

    row_sharded = [w_pw[0], w_br_conv[0], w_br_attn[0], w_out[0], w_ple_gate[0]]
    sh_rows = D // N_DEV
    (g_in,) = _all_gather([w_in[0].astype(BF16)])
    w_in_f = _to_perm(g_in.transpose(1, 0, 2).reshape(D, NW))
    w_in_t = w_in_f.T
    gather_rest = _Exchange([], [jnp.stack(row_sharded).astype(BF16), w_ple_proj[0].astype(BF16),
                                 jnp.pad(w_dw[0], ((0, CONV_HALO - CONV_K), (0, 0)))])

    cos, sa, sb = _rope_tables(pos)
    z2, h, g_rows, g_pp, g_dw = _in_proj(x2, ln_pre, w_in_f, gather_rest)
    full = [g_rows[:, j].reshape(D, D) for j in range(5)]
    w_pw_f, w_brc_f, w_bra_f, w_out_f, w_pg_f = full
    w_pp_f = g_pp.transpose(1, 0, 2).reshape(PLE, D)
    w_dw_f = g_dw.transpose(1, 0, 2).reshape(CONV_HALO, D)
    z3 = z2.reshape(NS, S, NW)
    c3, cs3 = _conv_fwd(z3, w_dw_f, b_dw, conv_ln_g, conv_ln_b)
    qr, kv2 = _rope_qk(z2, cos, sa, sb)
    qr3, kv3 = qr.reshape(NS, S, D), kv2.reshape(NS, S, 4 * N_KV * KV_W)
    sinks1 = sinks.reshape(N_HEADS)
    o3, lse3 = _attn_fwd(qr3, kv3, sinks1)
    o = o3.reshape(T, D)
    cs = cs3.reshape(T, D)
    (dya0, ya1, dya, yb0, dyb, m, dmo, x1, dgl, dpp, dcs, do, dcg, dag, dgc, dga, dx1, loss_blk, glp_post) = _mid(
        cs, o, z2, x2, p2, tgt, ln_post, full + [w_pp_f])
    loss = lax.psum(loss_blk[0, 0], MESH_AXES)

    gp_rows = [_grad_matmul(a, b, nm).reshape(N_DEV, sh_rows, D) for a, b, nm in (
        (cs, dya0, "grad_w_pw"), (ya1, dya, "grad_w_br_conv"), (yb0, dyb, "grad_w_br_attn"),
        (m, dmo, "grad_w_out"), (x1, dgl, "grad_w_ple_gate"))]
    gp_pp = _grad_matmul(p2, dpp, "grad_w_ple_proj").reshape(PLE, N_DEV, D // N_DEV).transpose(1, 0, 2)
    conv_out = _conv_bwd(z3, c3, dcs.reshape(NS, S, D), w_dw_f, conv_ln_g, conv_ln_b, _Exchange(gp_rows + [gp_pp], []))
    dzvu3, gw, gvec, r_rows, r_pp = conv_out[0], conv_out[1], conv_out[2], conv_out[3:8], conv_out[8]
    tab3 = [t.reshape(NS, S, 128) for t in (cos, sa, sb)]
    dq3, dkc3, dkp3, dvc3, dvp3, dsk3 = _attn_bwd(qr3, kv3, do.reshape(NS, S, D), o3, lse3, sinks1, *tab3)
    dkv3 = _attn_post(dkc3, dkp3, dvc3, dvp3, *tab3)
    segs = [dzvu3.reshape(T, 2 * D), dcg, dq3.reshape(T, D), dag, dgc, dga, dkv3.reshape(T, 2 * KV_W)]
    gp_in = _from_perm(jnp.concatenate([_grad_matmul(h, s, f"grad_w_in_{j}") for j, s in enumerate(segs)], axis=1))
    gp_in = gp_in.reshape(D, N_DEV, NW // N_DEV).transpose(1, 0, 2)
    grad_x2, glp_pre, r_in = _in_bwd(segs, w_in_t, x2, dx1, ln_pre, _Exchange([gp_in], []))
    gp_dw, gp_small = _pack_small(gw, gvec, glp_pre, glp_post, dsk3.reshape(T, N_HEADS))
    r_dw, r_small = _Exchange([gp_dw], [gp_small]).alone("small_grad_exchange")

    res = {}
    res["w_in"] = _adamw(r_in, w_in[0], m_w_in[0], v_w_in[0], "adamw_w_in")
    names_rows = ["w_pw", "w_br_conv", "w_br_attn", "w_out", "w_ple_gate"]
    wmv = {"w_pw": (w_pw, m_w_pw, v_w_pw), "w_br_conv": (w_br_conv, m_w_br_conv, v_w_br_conv),
           "w_br_attn": (w_br_attn, m_w_br_attn, v_w_br_attn), "w_out": (w_out, m_w_out, v_w_out),
           "w_ple_gate": (w_ple_gate, m_w_ple_gate, v_w_ple_gate)}
    for nm, parts in zip(names_rows, r_rows):
        w_, m_, v_ = wmv[nm]
        res[nm] = _adamw(parts, w_[0], m_[0], v_[0], "adamw_" + nm)
    res["w_ple_proj"] = _adamw(r_pp, w_ple_proj[0], m_w_ple_proj[0], v_w_ple_proj[0], "adamw_w_ple_proj")
    pad_dw = lambda a: jnp.pad(a[0], ((0, CONV_HALO - CONV_K), (0, 0)))
    res["w_dw"] = [a[:CONV_K] for a in _adamw(r_dw, pad_dw(w_dw), pad_dw(m_w_dw), pad_dw(v_w_dw), "adamw_w_dw")]

    def stack_small(a_pre, a_post, a_b, a_g, a_bb, a_s):
        sk = jnp.pad(a_s, ((0, 0), (0, D - N_HEADS)))
        return jnp.concatenate([a_pre, a_post, a_b, a_g, a_bb, sk, jnp.zeros((2, D), F32)], axis=0)

    small = _adamw(
        r_small, stack_small(ln_pre, ln_post, b_dw, conv_ln_g, conv_ln_b, sinks),
        stack_small(m_ln_pre, m_ln_post, m_b_dw, m_conv_ln_g, m_conv_ln_b, m_sinks),
        stack_small(v_ln_pre, v_ln_post, v_b_dw, v_conv_ln_g, v_conv_ln_b, v_sinks), "adamw_small")
    for j, nm in enumerate(["ln_pre", "ln_post", "b_dw", "conv_ln_g", "conv_ln_b"]):
        res[nm] = [a[j] for a in small]
    res["sinks"] = [a[5, :N_HEADS] for a in small]

    order = ["w_in", "ln_pre", "ln_post", "w_dw", "b_dw", "conv_ln_g", "conv_ln_b", "w_pw", "sinks", "w_br_conv",
             "w_br_attn", "w_out", "w_ple_gate", "w_ple_proj"]
    outs = [loss, grad_x2.reshape(NS, S, D)]
    for kind in range(4):
        outs += [res[nm][kind][None] for nm in order]
    return tuple(outs)
```

```python
import functools

import numpy as np

import jax
import jax.numpy as jnp
from jax import lax
from jax.experimental import pallas as pl
from jax.experimental.pallas import tpu as pltpu

F32 = jnp.float32
BF16 = jnp.bfloat16

D = 1024
N_HEADS = 16
N_KV = 2
HEAD_DIM = 64
GROUP = N_HEADS // N_KV
KV_W = N_KV * HEAD_DIM
CONV_K = 31
CONV_HALO = 32
BLK = 128
ROPE_DIM = 16
ROPE_THETA = 500000.0
EPS = 1e-6
PLE = 256
NW = 7 * D + 2 * KV_W
N_DEV = 8

CB_VAL, CB_GLU, CB_CGATE, CB_Q, CB_AGATE, CB_GCONV, CB_GATTN = range(7)
CB_K = 7 * D // KV_W
CB_V = CB_K + 1

ADAM_LR, ADAM_B1, ADAM_B2, ADAM_EPS, ADAM_WD, ADAM_STEP = 0.001, 0.9, 0.999, 1e-08, 0.01, 10

VMEM_LIMIT = 56 * 1024 * 1024
ROW_TILE = 256


def _cparams(*sem):
    return pltpu.CompilerParams(dimension_semantics=sem if sem else None, vmem_limit_bytes=VMEM_LIMIT)


def _sig(v):
    return 1.0 / (1.0 + jnp.exp(-v))


def _rowsum8(a):
    return a.reshape(a.shape[0] // 8, 8, a.shape[1]).sum(axis=0)


def _dot(a, b):
    return jnp.dot(a, b, preferred_element_type=F32)


def _dot_nt(a, b):
    return lax.dot_general(a, b, (((1,), (1,)), ((), ())), preferred_element_type=F32)


def _dot_tn(a, b):
    return lax.dot_general(a, b, (((0,), (0,)), ((), ())), preferred_element_type=F32)


def _to_perm(w):
    return jnp.concatenate([w[..., : 4 * D], w[..., 4 * D + 2 * KV_W :], w[..., 4 * D : 4 * D + 2 * KV_W]], axis=-1)


def _from_perm(g):
    return jnp.concatenate([g[..., : 4 * D], g[..., 7 * D :], g[..., 4 * D : 7 * D]], axis=-1)


def _my_place():
    return lax.axis_index("x"), lax.axis_index("y"), lax.axis_index("c")


def _slot(px, py, pc):
    return 4 * px + 2 * py + pc


def _all_gather(shards):
    n = len(shards)

    def body(*refs):
        ins, outs = refs[:n], refs[n : 2 * n]
        send_sems, recv_sems, local_sems = refs[2 * n :]
        x, y, c = _my_place()
        me, sibling = (x, y, c), (x, y, 1 - c)
        chips = [(1 - x, y), (x, 1 - y), (1 - x, 1 - y)]

        def copy(a, k, block, to, src=None):
            rows = outs[a].at[_slot(*block)]
            return pltpu.make_async_remote_copy(
                src_ref=rows if src is None else src, dst_ref=rows, send_sem=send_sems.at[a, k],
                recv_sem=recv_sems.at[a, k], device_id=to, device_id_type=pl.DeviceIdType.MESH)

        mine, first, passed = [], [], []
        for a in range(n):
            cp = pltpu.make_async_copy(ins[a], outs[a].at[_slot(*me)], local_sems.at[a])
            cp.start()
            mine.append(cp)
            fa = [copy(a, 0, me, sibling, src=ins[a])]
            fa += [copy(a, 1 + j, me, (*chip, c), src=ins[a]) for j, chip in enumerate(chips)]
            for cp in fa:
                cp.start()
            first += fa
        for j, chip in enumerate(chips):
            for a in range(n):
                copy(a, 1 + j, (*chip, c), me).wait_recv()
                cp = copy(a, 4 + j, (*chip, c), sibling)
                cp.start()
                passed.append(cp)
        for a in range(n):
            copy(a, 0, sibling, me).wait_recv()
            for j, chip in enumerate(chips):
                copy(a, 4 + j, (*chip, 1 - c), me).wait_recv()
        for cp in first + passed:
            cp.wait_send()
        for cp in mine:
            cp.wait()

    any_spec = pl.BlockSpec(memory_space=pl.ANY)
    return pl.pallas_call(
        body, name="weight_all_gather",
        out_shape=[jax.ShapeDtypeStruct((N_DEV,) + s.shape, s.dtype) for s in shards],
        in_specs=[any_spec] * n, out_specs=[any_spec] * n,
        scratch_shapes=[pltpu.SemaphoreType.DMA((n, 7)), pltpu.SemaphoreType.DMA((n, 7)), pltpu.SemaphoreType.DMA((n,))],
    )(*shards)


class _Exchange:
    def __init__(self, scatter, bcast):
        self.arrs = list(scatter) + list(bcast)
        self.n, self.n_sc = len(self.arrs), len(scatter)
        self.out_shape = [jax.ShapeDtypeStruct(a.shape, a.dtype) for a in scatter]
        self.out_shape += [jax.ShapeDtypeStruct((N_DEV,) + a.shape, a.dtype) for a in bcast]
        self.specs = [pl.BlockSpec(memory_space=pl.ANY)] * self.n
        self.scratch = [pltpu.SemaphoreType.DMA((self.n, 7)), pltpu.SemaphoreType.DMA((self.n, 7)),
                        pltpu.SemaphoreType.DMA((self.n,))]

    def _copies(self, ins, outs, sems):
        send_sems, recv_sems, local_sems = sems
        x, y, c = _my_place()
        me = _slot(x, y, c)
        peers = [(1 - x if k & 4 else x, 1 - y if k & 2 else y, 1 - c if k & 1 else c) for k in range(1, N_DEV)]
        mine, sends, arrivals = [], [], []
        for a in range(self.n):
            src = ins[a].at[me] if a < self.n_sc else ins[a]
            mine.append(pltpu.make_async_copy(src, outs[a].at[me], local_sems.at[a]))
        for k, peer in enumerate(peers):
            for a in range(self.n):
                src = ins[a].at[_slot(*peer)] if a < self.n_sc else ins[a]
                sends.append(pltpu.make_async_remote_copy(
                    src_ref=src, dst_ref=outs[a].at[me], send_sem=send_sems.at[a, k], recv_sem=recv_sems.at[a, k],
                    device_id=peer, device_id_type=pl.DeviceIdType.MESH))
                rows = outs[a].at[_slot(*peer)]
                arrivals.append(pltpu.make_async_remote_copy(
                    src_ref=rows, dst_ref=rows, send_sem=send_sems.at[a, k], recv_sem=recv_sems.at[a, k],
                    device_id=peer, device_id_type=pl.DeviceIdType.MESH))
        return mine, sends, arrivals

    def start(self, ins, outs, sems):
        mine, sends, _ = self._copies(ins, outs, sems)
        for cp in mine + sends:
            cp.start()

    def finish(self, ins, outs, sems):
        mine, sends, arrivals = self._copies(ins, outs, sems)
        for cp in arrivals:
            cp.wait_recv()
        for cp in sends:
            cp.wait_send()
        for cp in mine:
            cp.wait()

    def carried(self, refs_in, refs_out, sems, first, last):
        @pl.when(first)
        def _():
            self.start(refs_in, refs_out, sems)

        @pl.when(last)
        def _():
            self.finish(refs_in, refs_out, sems)

    def alone(self, name):
        n = self.n

        def body(*refs):
            ins, outs, sems = refs[:n], refs[n : 2 * n], refs[2 * n :]
            self.start(ins, outs, sems)
            self.finish(ins, outs, sems)

        return pl.pallas_call(body, name=name, out_shape=self.out_shape, in_specs=self.specs, out_specs=self.specs,
                              scratch_shapes=self.scratch)(*self.arrs)


def _host_split(refs, n_in, n_out, n_scratch, ex):
    k = ex.n if ex is not None else 0
    a = n_in
    b = a + k
    c = b + n_out
    d = c + k
    e = d + n_scratch
    return refs[:a], refs[a:b], refs[b:c], refs[c:d], refs[d:e], refs[e:]


def _rope_tables(pos):
    T = pos.shape[0]
    tm = min(1024, T)
    lane = np.arange(128) % HEAD_DIM
    inv = np.power(np.float32(ROPE_THETA), -np.arange(0, ROPE_DIM, 2, dtype=np.float32) / np.float32(ROPE_DIM)).astype(np.float32)
    half = ROPE_DIM // 2
    invf = np.where(lane < ROPE_DIM, inv[lane % half], 0.0).astype(np.float32)[None, :]
    m_a = (lane < half).astype(np.float32)[None, :]
    m_b = ((lane >= half) & (lane < ROPE_DIM)).astype(np.float32)[None, :]

    def body(pos_ref, invf_ref, ma_ref, mb_ref, cos_ref, sa_ref, sb_ref):
        ang = pos_ref[...].astype(F32) * invf_ref[...]
        sn = jnp.sin(ang)
        cos_ref[...] = jnp.cos(ang)
        sa_ref[...] = -sn * ma_ref[...]
        sb_ref[...] = sn * mb_ref[...]

    row = pl.BlockSpec((tm, 128), lambda i: (i, 0))
    cst = pl.BlockSpec((1, 128), lambda i: (0, 0))
    return pl.pallas_call(
        body, name="rope_tables", grid=(T // tm,), out_shape=[jax.ShapeDtypeStruct((T, 128), F32)] * 3,
        in_specs=[pl.BlockSpec((tm, 1), lambda i: (i, 0)), cst, cst, cst], out_specs=[row] * 3,
        compiler_params=_cparams("parallel"),
    )(pos, jnp.asarray(invf), jnp.asarray(m_a), jnp.asarray(m_b))


def _rope(t, cos, sa, sb, sign=1.0):
    parts = []
    for i in range(t.shape[1] // 128):
        ti = t[:, 128 * i : 128 * (i + 1)]
        up = pltpu.roll(ti, 128 - ROPE_DIM // 2, 1)
        dn = pltpu.roll(ti, ROPE_DIM // 2, 1)
        parts.append(ti * cos + sign * (up * sa + dn * sb))
    return parts[0] if len(parts) == 1 else jnp.concatenate(parts, axis=-1)


def _in_proj(x2, ln_pre, w_in, ex):
    T = x2.shape[0]
    tm = min(512, T)
    tn = NW // 2
    chunk = 512
    ni, nj = T // tm, NW // tn

    def body(*refs):
        (x_ref, g_ref, w_ref), ex_in, (z_ref, h_ref), ex_out, (hs,), ex_sems = _host_split(refs, 3, 2, 1, ex)
        i, j = pl.program_id(0), pl.program_id(1)
        ex.carried(ex_in, ex_out, ex_sems, (i == 0) & (j == 0), (i == ni - 1) & (j == nj - 1))

        @pl.when(j == 0)
        def _():
            xv = x_ref[...]
            r = lax.rsqrt(jnp.mean(xv * xv, axis=-1, keepdims=True) + EPS)
            h = (xv * r * g_ref[...]).astype(BF16)
            hs[...] = h
            h_ref[...] = h

        for c0 in range(0, tn, chunk):
            cw = min(chunk, tn - c0)
            z_ref[:, c0 : c0 + cw] = _dot(hs[...], w_ref[:, c0 : c0 + cw]).astype(BF16)

    return pl.pallas_call(
        body, name="in_proj", grid=(ni, nj),
        out_shape=[jax.ShapeDtypeStruct((T, NW), BF16), jax.ShapeDtypeStruct((T, D), BF16)] + ex.out_shape,
        in_specs=[pl.BlockSpec((tm, D), lambda i, j: (i, 0)), pl.BlockSpec((1, D), lambda i, j: (0, 0)),
                  pl.BlockSpec((D, tn), lambda i, j: (0, j))] + ex.specs,
        out_specs=[pl.BlockSpec((tm, tn), lambda i, j: (i, j)), pl.BlockSpec((tm, D), lambda i, j: (i, 0))] + ex.specs,
        scratch_shapes=[pltpu.VMEM((tm, D), BF16)] + ex.scratch,
        compiler_params=_cparams("arbitrary", "arbitrary"),
    )(x2, ln_pre, w_in, *ex.arrs)


def _conv_tiles(S):
    tm = min(ROW_TILE, S)
    return tm, S // tm, tm // CONV_HALO


CONV_ROWS_FWD = 32
CONV_ROWS = 16


def _fill_shifted(sh, rows):
    for b in range(1, 8):
        sh[b, 0:rows, :] = sh[0, b : b + rows, :]


def _conv_fwd(z3, w_dw, b_dw, ln_g, ln_b):
    NS, S, _ = z3.shape
    tm, nt, r = _conv_tiles(S)

    def body(val_ref, glu_ref, hval_ref, hglu_ref, w_ref, b_ref, g_ref, bb_ref, c_ref, cs_ref, ush, cbuf):
        i = pl.program_id(1)
        ush[0, CONV_HALO:, :] = val_ref[...].astype(F32) * _sig(glu_ref[...].astype(F32))
        uh = hval_ref[...].astype(F32) * _sig(hglu_ref[...].astype(F32))
        ush[0, 0:CONV_HALO, :] = jnp.where(i > 0, uh, 0.0)
        _fill_shifted(ush, tm + CONV_HALO - 8)
        for r0 in range(0, tm, CONV_ROWS_FWD):
            acc = jnp.zeros((CONV_ROWS_FWD, D), F32)
            for k in range(CONV_K):
                a, b = divmod(CONV_HALO - (CONV_K - 1) + k, 8)
                acc = acc + w_ref[k : k + 1, :] * ush[b, r0 + 8 * a : r0 + 8 * a + CONV_ROWS_FWD, :]
            cbuf[r0 : r0 + CONV_ROWS_FWD, :] = acc + b_ref[...]
        cv = cbuf[...]
        mu = jnp.mean(cv, axis=-1, keepdims=True)
        xc = cv - mu
        var = jnp.mean(xc * xc, axis=-1, keepdims=True)
        cl = xc * lax.rsqrt(var + EPS) * g_ref[...] + bb_ref[...]
        c_ref[...] = cv.astype(BF16)
        cs_ref[...] = (cl * _sig(cl)).astype(BF16)

    def cur(cb):
        return pl.BlockSpec((None, tm, D), lambda s, i: (s, i, cb))

    def halo(cb):
        return pl.BlockSpec((None, CONV_HALO, D), lambda s, i: (s, jnp.maximum(i * r - 1, 0), cb))

    vec = pl.BlockSpec((1, D), lambda s, i: (0, 0))
    out = pl.BlockSpec((None, tm, D), lambda s, i: (s, i, 0))
    return pl.pallas_call(
        body, name="conv_fwd", grid=(NS, nt),
        out_shape=[jax.ShapeDtypeStruct((NS, S, D), BF16)] * 2,
        in_specs=[cur(CB_VAL), cur(CB_GLU), halo(CB_VAL), halo(CB_GLU),
                  pl.BlockSpec((CONV_HALO, D), lambda s, i: (0, 0)), vec, vec, vec],
        out_specs=[out, out],
        scratch_shapes=[pltpu.VMEM((8, tm + CONV_HALO, D), F32), pltpu.VMEM((tm, D), F32)],
        compiler_params=_cparams("parallel", "parallel"),
    )(z3, z3, z3, z3, w_dw, b_dw, ln_g, ln_b)


PAIRS = GROUP // 2
QROWS = PAIRS * BLK


def _kv2_col(kind, g, par):
    return kind * 2 * N_KV + g * 2 + par


def _rope_qk(z2, cos, sa, sb):
    T = z2.shape[0]
    tm = min(ROW_TILE, T)

    def body(q_ref, k_ref, v_ref, cos_ref, sa_ref, sb_ref, qr_ref, kv_ref):
        tabs = cos_ref[...], sa_ref[...], sb_ref[...]
        qr_ref[...] = (_rope(q_ref[...].astype(F32), *tabs) * (HEAD_DIM ** -0.5)).astype(BF16)
        low = lax.broadcasted_iota(jnp.int32, (tm, KV_W), 1) < HEAD_DIM
        for kind, t in ((0, _rope(k_ref[...].astype(F32), *tabs)), (1, v_ref[...].astype(F32))):
            r = pltpu.roll(t, HEAD_DIM, 1)
            cols = {(0, 0): jnp.where(low, t, 0.0), (0, 1): jnp.where(low, 0.0, r),
                    (1, 0): jnp.where(low, r, 0.0), (1, 1): jnp.where(low, 0.0, t)}
            for (g, par), val in cols.items():
                c = _kv2_col(kind, g, par)
                kv_ref[:, c * KV_W : (c + 1) * KV_W] = val.astype(BF16)

    tab = pl.BlockSpec((tm, 128), lambda i: (i, 0))
    return pl.pallas_call(
        body, name="rope_qk", grid=(T // tm,),
        out_shape=[jax.ShapeDtypeStruct((T, D), BF16), jax.ShapeDtypeStruct((T, 4 * N_KV * KV_W), BF16)],
        in_specs=[pl.BlockSpec((tm, D), lambda i: (i, CB_Q)), pl.BlockSpec((tm, KV_W), lambda i: (i, CB_K)),
                  pl.BlockSpec((tm, KV_W), lambda i: (i, CB_V)), tab, tab, tab],
        out_specs=[pl.BlockSpec((tm, D), lambda i: (i, 0)), pl.BlockSpec((tm, 4 * N_KV * KV_W), lambda i: (i, 0))],
        compiler_params=_cparams("parallel"),
    )(z2, z2, z2, cos, sa, sb)


def _attn_mask(has_prev):
    qi = lax.broadcasted_iota(jnp.int32, (QROWS, 2 * BLK), 0) & (BLK - 1)
    kj = lax.broadcasted_iota(jnp.int32, (QROWS, 2 * BLK), 1)
    first_key = jnp.where(has_prev, 0, BLK)
    return (kj > qi) & (kj <= qi + BLK) & (kj >= first_key)


NEG_BIG = -1e30


def _attn_specs():
    q = pl.BlockSpec((None, BLK, D), lambda s, n: (s, n, 0))
    kv_cur = pl.BlockSpec((None, BLK, 4 * N_KV * KV_W), lambda s, n: (s, n, 0))
    kv_prev = pl.BlockSpec((None, BLK, 4 * N_KV * KV_W), lambda s, n: (s, jnp.maximum(n - 1, 0), 0))
    sink = pl.BlockSpec(memory_space=pltpu.SMEM)
    return sink, q, kv_cur, kv_prev


def _stack_pairs(ref, g):
    return jnp.concatenate([ref[:, (PAIRS * g + j) * 128 : (PAIRS * g + j + 1) * 128] for j in range(PAIRS)], axis=0)


def _unstack_pairs(ref, g, val):
    for j in range(PAIRS):
        ref[:, (PAIRS * g + j) * 128 : (PAIRS * g + j + 1) * 128] = val[j * BLK : (j + 1) * BLK].astype(ref.dtype)


def _pair_heads(g, par):
    return [GROUP * g + 2 * j + par for j in range(PAIRS)]


def _head_col_load(ref, g, par):
    return jnp.concatenate([ref[:, h : h + 1] for h in _pair_heads(g, par)], axis=0)


def _head_col_store(ref, g, par, col):
    for j, h in enumerate(_pair_heads(g, par)):
        ref[:, h : h + 1] = col[j * BLK : (j + 1) * BLK]


def _sink_col(sink_ref, g, par):
    return jnp.concatenate([jnp.full((BLK, 1), sink_ref[h], F32) for h in _pair_heads(g, par)], axis=0)


def _kv2_block(kvc_ref, kvp_ref, has_prev, c):
    col = slice(c * KV_W, (c + 1) * KV_W)
    prev = jnp.where(has_prev, kvp_ref[:, col], jnp.zeros((BLK, KV_W), BF16))
    return jnp.concatenate([prev, kvc_ref[:, col]], axis=0)


def _attn_fwd(qr3, kv3, sinks):
    NS, S, _ = qr3.shape

    def body(sink_ref, q_ref, kvc_ref, kvp_ref, o_ref, lse_ref):
        has_prev = pl.program_id(1) > 0
        mask = _attn_mask(has_prev)[0:BLK]
        for g in range(N_KV):
            kv = [[_kv2_block(kvc_ref, kvp_ref, has_prev, _kv2_col(kind, g, par)) for par in range(2)] for kind in range(2)]
            for j in range(PAIRS):
                cols = slice((PAIRS * g + j) * 128, (PAIRS * g + j + 1) * 128)
                q2 = q_ref[:, cols]
                o_pair = None
                for par in range(2):
                    h = GROUP * g + 2 * j + par
                    s = jnp.where(mask, _dot_nt(q2, kv[0][par]), NEG_BIG)
                    sk = sink_ref[h]
                    mx = jnp.maximum(jnp.max(s, axis=-1, keepdims=True), sk)
                    e = jnp.exp(s - mx)
                    den = jnp.sum(e, axis=-1, keepdims=True) + jnp.exp(sk - mx)
                    pv = _dot(e.astype(BF16), kv[1][par]) * (1.0 / den)
                    o_pair = pv if o_pair is None else o_pair + pv
                    lse_ref[:, h : h + 1] = mx + jnp.log(den)
                o_ref[:, cols] = o_pair.astype(BF16)

    return pl.pallas_call(
        body, name="attn_fwd", grid=(NS, S // BLK),
        out_shape=[jax.ShapeDtypeStruct((NS, S, D), BF16), jax.ShapeDtypeStruct((NS, S, N_HEADS), F32)],
        in_specs=list(_attn_specs()),
        out_specs=[pl.BlockSpec((None, BLK, D), lambda s, n: (s, n, 0)),
                   pl.BlockSpec((None, BLK, N_HEADS), lambda s, n: (s, n, 0))],
        compiler_params=_cparams("parallel", "parallel"),
    )(sinks, qr3, kv3, kv3)


def _weight_spec(shape):
    return pl.BlockSpec(shape, lambda i: (0,) * len(shape), pipeline_mode=pl.Buffered(1))


def _dsilu(v, s):
    return s * (1.0 + v * (1.0 - s))


MID_TILE = 256


def _mid(cs, o, z2, x2, p2, tgt, ln_post, weights):
    T = cs.shape[0]
    tm = min(MID_TILE, T)
    nt = T // tm
    n_bf = 16

    def body(cs_ref, o_ref, cg_ref, ag_ref, gc_ref, ga_ref, x_ref, p_ref, t_ref, g_ref,
             wpw, wbrc, wbra, wout, wpg, wpp,
             dya0_ref, ya1_ref, dya_ref, yb0_ref, dyb_ref, m_ref, dmo_ref, x1_ref, dgl_ref, dpp_ref,
             dcs_ref, do_ref, dcg_ref, dag_ref, dgc_ref, dga_ref, dx1_ref, loss_ref, glp_ref, lacc):
        i = pl.program_id(0)

        @pl.when(i == 0)
        def _():
            lacc[...] = jnp.zeros_like(lacc)
            glp_ref[...] = jnp.zeros_like(glp_ref)

        cg = cg_ref[...].astype(F32)
        scg = _sig(cg)
        silu_c = cg * scg
        ya0 = _dot(cs_ref[...], wpw[...])
        ya1 = (ya0 * silu_c).astype(BF16)
        ya1_ref[...] = ya1
        ya = _dot(ya1, wbrc[...])
        ag = ag_ref[...].astype(F32)
        sag = _sig(ag)
        silu_a = ag * sag
        ov = o_ref[...].astype(F32)
        yb0 = (ov * silu_a).astype(BF16)
        yb0_ref[...] = yb0
        yb = _dot(yb0, wbra[...])
        sgc = _sig(gc_ref[...].astype(F32))
        sga = _sig(ga_ref[...].astype(F32))
        mb = (sgc * ya + sga * yb).astype(BF16)
        m_ref[...] = mb
        mo = _dot(mb, wout[...])
        r2 = lax.rsqrt(jnp.mean(mo * mo, axis=-1, keepdims=True) + EPS)
        nrm = mo * r2
        x1 = x_ref[...] + nrm * g_ref[...]
        x1b = x1.astype(BF16)
        x1_ref[...] = x1b
        gate = _sig(_dot(x1b, wpg[...]))
        pp = _dot(p_ref[...].astype(BF16), wpp[...])
        e = x1 + gate * pp - t_ref[...]
        lacc[...] += _rowsum8(e * e)
        dy = e * (1.0 / D)

        dgl = (dy * pp * gate * (1.0 - gate)).astype(BF16)
        dgl_ref[...] = dgl
        dpp_ref[...] = (dy * gate).astype(BF16)
        dx1 = dy + _dot_nt(dgl, wpg[...])
        dx1_ref[...] = dx1
        glp_ref[...] += _rowsum8(dx1 * nrm)
        dn = dx1 * g_ref[...]
        dmo = (r2 * (dn - nrm * jnp.mean(dn * nrm, axis=-1, keepdims=True))).astype(BF16)
        dmo_ref[...] = dmo
        dm = _dot_nt(dmo, wout[...])
        dya = (dm * sgc).astype(BF16)
        dyb = (dm * sga).astype(BF16)
        dya_ref[...] = dya
        dyb_ref[...] = dyb
        dgc_ref[...] = (dm * ya * sgc * (1.0 - sgc)).astype(BF16)
        dga_ref[...] = (dm * yb * sga * (1.0 - sga)).astype(BF16)
        dya1 = _dot_nt(dya, wbrc[...])
        dya0 = (dya1 * silu_c).astype(BF16)
        dya0_ref[...] = dya0
        dcg_ref[...] = (dya1 * ya0 * _dsilu(cg, scg)).astype(BF16)
        dcs_ref[...] = _dot_nt(dya0, wpw[...]).astype(BF16)
        dyb0 = _dot_nt(dyb, wbra[...])
        do_ref[...] = (dyb0 * silu_a).astype(BF16)
        dag_ref[...] = (dyb0 * ov * _dsilu(ag, sag)).astype(BF16)

        @pl.when(i == nt - 1)
        def _():
            loss_ref[...] = jnp.full(loss_ref.shape, jnp.sum(lacc[...]) * (0.5 / D), F32)

    row = pl.BlockSpec((tm, D), lambda i: (i, 0))

    def zcol(cb):
        return pl.BlockSpec((tm, D), lambda i: (i, cb))

    bf = jax.ShapeDtypeStruct((T, D), BF16)
    return pl.pallas_call(
        body, name="mid_fwd_bwd", grid=(nt,),
        out_shape=[bf] * n_bf + [jax.ShapeDtypeStruct((T, D), F32), jax.ShapeDtypeStruct((8, 128), F32),
                                 jax.ShapeDtypeStruct((8, D), F32)],
        in_specs=[row, row, zcol(CB_CGATE), zcol(CB_AGATE), zcol(CB_GCONV), zcol(CB_GATTN), row,
                  pl.BlockSpec((tm, PLE), lambda i: (i, 0)), row, _weight_spec((1, D))]
        + [_weight_spec((D, D))] * 5 + [_weight_spec((PLE, D))],
        out_specs=[row] * (n_bf + 1) + [pl.BlockSpec((8, 128), lambda i: (0, 0)), pl.BlockSpec((8, D), lambda i: (0, 0))],
        scratch_shapes=[pltpu.VMEM((8, D), F32)],
        compiler_params=_cparams("arbitrary"),
    )(cs, o, z2, z2, z2, z2, x2, p2, tgt, ln_post, *weights)


def _conv_bwd(z3, c3, dcs3, w_dw, ln_g, ln_b, ex):
    NS, S, _ = z3.shape
    tm, nt, r = _conv_tiles(S)

    def body(*refs):
        ins, ex_in, outs, ex_out, scratch, ex_sems = _host_split(refs, 9, 3, 3, ex)
        val_ref, glu_ref, c_ref, dcs_ref, hc_ref, hdcs_ref, w_ref, g_ref, bb_ref = ins
        dz_ref, gw_ref, gvec_ref = outs
        dsh, ubuf, dubuf = scratch
        i = pl.program_id(1)
        first = (pl.program_id(0) == 0) & (i == 0)
        ex.carried(ex_in, ex_out, ex_sems, first, (pl.program_id(0) == NS - 1) & (i == nt - 1))

        @pl.when(first)
        def _():
            gw_ref[...] = jnp.zeros_like(gw_ref)
            gvec_ref[...] = jnp.zeros_like(gvec_ref)

        val = val_ref[...].astype(F32)
        sg = _sig(glu_ref[...].astype(F32))
        ubuf[...] = val * sg

        def ln_bwd(cv, dcs):
            cv = cv.astype(F32)
            mu = jnp.mean(cv, axis=-1, keepdims=True)
            xc = cv - mu
            rstd = lax.rsqrt(jnp.mean(xc * xc, axis=-1, keepdims=True) + EPS)
            xhat = xc * rstd
            cl = xhat * g_ref[...] + bb_ref[...]
            s = _sig(cl)
            dcl = dcs.astype(F32) * _dsilu(cl, s)
            dxh = dcl * g_ref[...]
            dc = rstd * (dxh - jnp.mean(dxh, axis=-1, keepdims=True) - xhat * jnp.mean(dxh * xhat, axis=-1, keepdims=True))
            return dc, dcl, xhat

        dc, dcl, xhat = ln_bwd(c_ref[...], dcs_ref[...])
        dsh[0, 0:tm, :] = dc
        dch, _, _ = ln_bwd(hc_ref[...], hdcs_ref[...])
        dsh[0, tm:, :] = jnp.where(i < nt - 1, dch, 0.0)
        gvec_ref[0:8, :] += _rowsum8(dcl * xhat)
        gvec_ref[8:16, :] += _rowsum8(dcl)
        gvec_ref[16:24, :] += _rowsum8(dc)
        _fill_shifted(dsh, tm + CONV_HALO - 8)

        def dc_ahead(r0, k):
            a, b = divmod(CONV_K - 1 - k, 8)
            return dsh[b, r0 + 8 * a : r0 + 8 * a + CONV_ROWS, :]

        for r0 in range(0, tm, CONV_ROWS):
            acc = jnp.zeros((CONV_ROWS, D), F32)
            for k in range(CONV_K):
                acc = acc + w_ref[k : k + 1, :] * dc_ahead(r0, k)
            dubuf[r0 : r0 + CONV_ROWS, :] = acc
        for r0 in range(0, tm, CONV_ROWS):
            ur = ubuf[r0 : r0 + CONV_ROWS, :]
            for k in range(CONV_K):
                gw_ref[8 * k : 8 * k + 8, :] += _rowsum8(ur * dc_ahead(r0, k))
        du = dubuf[...]
        dz_ref[:, 0:D] = (du * sg).astype(BF16)
        dz_ref[:, D:] = (du * val * sg * (1.0 - sg)).astype(BF16)

    def cur(cb):
        return pl.BlockSpec((None, tm, D), lambda s, i: (s, i, cb))

    nxt = pl.BlockSpec((None, CONV_HALO, D), lambda s, i: (s, jnp.minimum((i + 1) * r, S // CONV_HALO - 1), 0))
    vec = pl.BlockSpec((1, D), lambda s, i: (0, 0))
    return pl.pallas_call(
        body, name="conv_bwd", grid=(NS, nt),
        out_shape=[jax.ShapeDtypeStruct((NS, S, 2 * D), BF16), jax.ShapeDtypeStruct((CONV_HALO * 8, D), F32),
                   jax.ShapeDtypeStruct((24, D), F32)] + ex.out_shape,
        in_specs=[cur(CB_VAL), cur(CB_GLU), cur(0), cur(0), nxt, nxt,
                  pl.BlockSpec((CONV_HALO, D), lambda s, i: (0, 0)), vec, vec] + ex.specs,
        out_specs=[pl.BlockSpec((None, tm, 2 * D), lambda s, i: (s, i, 0)),
                   pl.BlockSpec((CONV_HALO * 8, D), lambda s, i: (0, 0)), pl.BlockSpec((24, D), lambda s, i: (0, 0))] + ex.specs,
        scratch_shapes=[pltpu.VMEM((8, tm + CONV_HALO, D), F32), pltpu.VMEM((tm, D), F32), pltpu.VMEM((tm, D), F32)] + ex.scratch,
        compiler_params=_cparams("arbitrary", "arbitrary"),
    )(z3, z3, c3, dcs3, c3, dcs3, w_dw, ln_g, ln_b, *ex.arrs)


def _attn_bwd(qr3, kv3, do3, o3, lse3, sinks, cos3, sa3, sb3):
    NS, S, _ = qr3.shape

    def body(sink_ref, q_ref, kvc_ref, kvp_ref, do_ref, o_ref, lse_ref, cos_ref, sa_ref, sb_ref,
             dq_ref, dkc_ref, dkp_ref, dvc_ref, dvp_ref, dsk_ref):
        has_prev = pl.program_id(1) > 0
        mask = _attn_mask(has_prev)
        tabs = cos_ref[...], sa_ref[...], sb_ref[...]
        low_q = lax.broadcasted_iota(jnp.int32, (QROWS, 128), 1) < HEAD_DIM
        dk_g, dv_g = [], []
        for g in range(N_KV):
            qs = _stack_pairs(q_ref, g)
            dos = _stack_pairs(do_ref, g)
            prod = dos.astype(F32) * _stack_pairs(o_ref, g).astype(F32)
            deltas = [jnp.sum(jnp.where(low_q, prod, 0.0), axis=-1, keepdims=True),
                      jnp.sum(jnp.where(low_q, 0.0, prod), axis=-1, keepdims=True)]
            dq_acc, dk_par, dv_par = None, [], []
            for par in range(2):
                k2 = _kv2_block(kvc_ref, kvp_ref, has_prev, _kv2_col(0, g, par))
                v2 = _kv2_block(kvc_ref, kvp_ref, has_prev, _kv2_col(1, g, par))
                lse = _head_col_load(lse_ref, g, par)
                p = jnp.exp(jnp.where(mask, _dot_nt(qs, k2), NEG_BIG) - lse)
                ds = (p * (_dot_nt(dos, v2) - deltas[par])).astype(BF16)
                dq = _dot(ds, k2)
                dq_acc = dq if dq_acc is None else dq_acc + dq
                dk_par.append(_dot_tn(ds, qs))
                dv_par.append(_dot_tn(p.astype(BF16), dos))
                _head_col_store(dsk_ref, g, par, -jnp.exp(_sink_col(sink_ref, g, par) - lse) * deltas[par])
            for j in range(PAIRS):
                dq_pair = _rope(dq_acc[j * BLK : (j + 1) * BLK], *tabs, sign=-1.0) * (HEAD_DIM ** -0.5)
                dq_ref[:, (PAIRS * g + j) * 128 : (PAIRS * g + j + 1) * 128] = dq_pair.astype(BF16)
            dk_g.append(dk_par[0] + pltpu.roll(dk_par[1], HEAD_DIM, 1))
            dv_g.append(dv_par[0] + pltpu.roll(dv_par[1], HEAD_DIM, 1))
        low_k = lax.broadcasted_iota(jnp.int32, (2 * BLK, KV_W), 1) < HEAD_DIM
        dk = jnp.where(low_k, dk_g[0], pltpu.roll(dk_g[1], HEAD_DIM, 1))
        dv = jnp.where(low_k, dv_g[0], pltpu.roll(dv_g[1], HEAD_DIM, 1))
        dkp_ref[...] = dk[0:BLK]
        dkc_ref[...] = dk[BLK:]
        dvp_ref[...] = dv[0:BLK]
        dvc_ref[...] = dv[BLK:]

    qspec = pl.BlockSpec((None, BLK, D), lambda s, n: (s, n, 0))
    kvspec = pl.BlockSpec((None, BLK, KV_W), lambda s, n: (s, n, 0))
    hspec = pl.BlockSpec((None, BLK, N_HEADS), lambda s, n: (s, n, 0))
    kv = jax.ShapeDtypeStruct((NS, S, KV_W), F32)
    return pl.pallas_call(
        body, name="attn_bwd", grid=(NS, S // BLK),
        out_shape=[jax.ShapeDtypeStruct((NS, S, D), BF16), kv, kv, kv, kv, jax.ShapeDtypeStruct((NS, S, N_HEADS), F32)],
        in_specs=list(_attn_specs()) + [qspec, qspec, hspec, kvspec, kvspec, kvspec],
        out_specs=[qspec, kvspec, kvspec, kvspec, kvspec, hspec],
        compiler_params=_cparams("parallel", "parallel"),
    )(sinks, qr3, kv3, kv3, do3, o3, lse3, cos3, sa3, sb3)


def _attn_post(dkc3, dkp3, dvc3, dvp3, cos3, sa3, sb3):
    NS, S, _ = dkc3.shape
    tm = min(4 * BLK, S)
    nt = S // tm

    def body(dkc_ref, dkp_ref, dkn_ref, dvc_ref, dvp_ref, dvn_ref, cos_ref, sa_ref, sb_ref, dkv_ref):
        has_next = pl.program_id(1) < nt - 1

        def join(cur_ref, prev_ref, next_ref):
            ahead = jnp.where(has_next, next_ref[...], 0.0)
            shifted = ahead if tm == BLK else jnp.concatenate([prev_ref[BLK:, :], ahead], axis=0)
            return cur_ref[...] + shifted

        tabs = cos_ref[...], sa_ref[...], sb_ref[...]
        dkv_ref[:, 0:KV_W] = _rope(join(dkc_ref, dkp_ref, dkn_ref), *tabs, sign=-1.0).astype(BF16)
        dkv_ref[:, KV_W:] = join(dvc_ref, dvp_ref, dvn_ref).astype(BF16)

    cur = pl.BlockSpec((None, tm, KV_W), lambda s, j: (s, j, 0))
    nxt = pl.BlockSpec((None, BLK, KV_W), lambda s, j: (s, jnp.minimum((j + 1) * (tm // BLK), S // BLK - 1), 0))
    return pl.pallas_call(
        body, name="attn_post", grid=(NS, nt),
        out_shape=jax.ShapeDtypeStruct((NS, S, 2 * KV_W), BF16),
        in_specs=[cur, cur, nxt, cur, cur, nxt, cur, cur, cur],
        out_specs=pl.BlockSpec((None, tm, 2 * KV_W), lambda s, j: (s, j, 0)),
        compiler_params=_cparams("parallel", "parallel"),
    )(dkc3, dkp3, dkp3, dvc3, dvp3, dvp3, cos3, sa3, sb3)


def _in_bwd(segs, w_in, x2, dx1, ln_pre, ex):
    T = x2.shape[0]
    tm = min(ROW_TILE, T)
    nt = T // tm
    ns = len(segs)
    widths = [s.shape[1] for s in segs]

    def body(*refs):
        ins, ex_in, (gx_ref, glp_ref), ex_out, _, ex_sems = _host_split(refs, ns + 4, 2, 0, ex)
        seg_refs = ins[:ns]
        w_ref, x_ref, dx1_ref, g_ref = ins[ns:]
        i = pl.program_id(0)
        ex.carried(ex_in, ex_out, ex_sems, i == 0, i == nt - 1)

        @pl.when(i == 0)
        def _():
            glp_ref[...] = jnp.zeros_like(glp_ref)

        dh = None
        r0 = 0
        for sref, w in zip(seg_refs, widths):
            part = _dot_nt(sref[...], w_ref[:, r0 : r0 + w])
            dh = part if dh is None else dh + part
            r0 += w
        xv = x_ref[...]
        r1 = lax.rsqrt(jnp.mean(xv * xv, axis=-1, keepdims=True) + EPS)
        xhat = xv * r1
        glp_ref[...] += _rowsum8(dh * xhat)
        dhg = dh * g_ref[...]
        gx_ref[...] = dx1_ref[...] + r1 * (dhg - xhat * jnp.mean(dhg * xhat, axis=-1, keepdims=True))

    row = pl.BlockSpec((tm, D), lambda i: (i, 0))
    return pl.pallas_call(
        body, name="in_bwd", grid=(nt,),
        out_shape=[jax.ShapeDtypeStruct((T, D), F32), jax.ShapeDtypeStruct((8, D), F32)] + ex.out_shape,
        in_specs=[pl.BlockSpec((tm, w), lambda i: (i, 0)) for w in widths]
        + [_weight_spec((D, NW)), row, row, _weight_spec((1, D))] + ex.specs,
        out_specs=[row, pl.BlockSpec((8, D), lambda i: (0, 0))] + ex.specs,
        scratch_shapes=ex.scratch,
        compiler_params=_cparams("arbitrary"),
    )(*segs, w_in, x2, dx1, ln_pre, *ex.arrs)


def _grad_matmul(a, b, name):
    T, M = a.shape
    N = b.shape[1]
    tk = min(1024, T)
    nk = T // tk

    def body(a_ref, b_ref, o_ref, acc):
        k = pl.program_id(0)

        @pl.when(k == 0)
        def _():
            acc[...] = jnp.zeros_like(acc)

        acc[...] += _dot_tn(a_ref[...].astype(BF16), b_ref[...])

        @pl.when(k == nk - 1)
        def _():
            o_ref[...] = acc[...].astype(BF16)

    return pl.pallas_call(
        body, name=name, grid=(nk,), out_shape=jax.ShapeDtypeStruct((M, N), BF16),
        in_specs=[pl.BlockSpec((tk, M), lambda k: (k, 0)), pl.BlockSpec((tk, N), lambda k: (k, 0))],
        out_specs=pl.BlockSpec((M, N), lambda k: (0, 0)),
        scratch_shapes=[pltpu.VMEM((M, N), F32)],
        compiler_params=_cparams("arbitrary"),
    )(a, b)


def _pack_small(gw, gvec, glp_pre, glp_post, dsk):
    T = dsk.shape[0]

    def body(gw_ref, gvec_ref, pre_ref, post_ref, dsk_ref, gdw_ref, gs_ref):
        gwf = gw_ref[...].reshape(CONV_HALO, 8, D).sum(axis=1)
        for d in range(N_DEV):
            gdw_ref[d] = gwf[:, 128 * d : 128 * (d + 1)]
        gs_ref[...] = jnp.zeros_like(gs_ref)
        gs_ref[0:1, :] = jnp.sum(pre_ref[...], axis=0, keepdims=True)
        gs_ref[1:2, :] = jnp.sum(post_ref[...], axis=0, keepdims=True)
        gs_ref[2:3, :] = jnp.sum(gvec_ref[16:24, :], axis=0, keepdims=True)
        gs_ref[3:4, :] = jnp.sum(gvec_ref[0:8, :], axis=0, keepdims=True)
        gs_ref[4:5, :] = jnp.sum(gvec_ref[8:16, :], axis=0, keepdims=True)
        gs_ref[5:6, 0:N_HEADS] = jnp.sum(dsk_ref[...], axis=0, keepdims=True)

    return pl.pallas_call(
        body, name="pack_small",
        out_shape=[jax.ShapeDtypeStruct((N_DEV, CONV_HALO, 128), F32), jax.ShapeDtypeStruct((8, D), F32)],
        compiler_params=_cparams(),
    )(gw, gvec, glp_pre, glp_post, dsk)


def _adamw(parts, w, m, v, name):
    R, C = w.shape
    tr = R if R <= 256 else 128

    def body(p_ref, w_ref, m_ref, v_ref, g_ref, d_ref, nm_ref, nv_ref):
        g = p_ref[0].astype(F32)
        for s in range(1, N_DEV):
            g = g + p_ref[s].astype(F32)
        nm = ADAM_B1 * m_ref[...] + (1.0 - ADAM_B1) * g
        nv = ADAM_B2 * v_ref[...] + (1.0 - ADAM_B2) * (g * g)
        m_hat = nm / (1.0 - ADAM_B1 ** ADAM_STEP)
        v_hat = nv / (1.0 - ADAM_B2 ** ADAM_STEP)
        g_ref[...] = g
        d_ref[...] = -ADAM_LR * (m_hat / (jnp.sqrt(v_hat) + ADAM_EPS) + ADAM_WD * w_ref[...])
        nm_ref[...] = nm
        nv_ref[...] = nv

    blk = pl.BlockSpec((tr, C), lambda i: (i, 0))
    return pl.pallas_call(
        body, name=name, grid=(R // tr,), out_shape=[jax.ShapeDtypeStruct((R, C), F32)] * 4,
        in_specs=[pl.BlockSpec((N_DEV, tr, C), lambda i: (0, i, 0)), blk, blk, blk], out_specs=[blk] * 4,
        compiler_params=_cparams("parallel"),
    )(parts, w, m, v)


def kernel(x, p, positions, w_in, ln_pre, ln_post, w_dw, b_dw, conv_ln_g, conv_ln_b, w_pw, sinks, w_br_conv, w_br_attn, w_out, w_ple_gate, w_ple_proj, loss_target, m_w_in, m_ln_pre, m_ln_post, m_w_dw, m_b_dw, m_conv_ln_g, m_conv_ln_b, m_w_pw, m_sinks, m_w_br_conv, m_w_br_attn, m_w_out, m_w_ple_gate, m_w_ple_proj, v_w_in, v_ln_pre, v_ln_post, v_w_dw, v_b_dw, v_conv_ln_g, v_conv_ln_b, v_w_pw, v_sinks, v_w_br_conv, v_w_br_attn, v_w_out, v_w_ple_gate, v_w_ple_proj):
    NS, S, _ = x.shape
    T = NS * S
    x2 = x.reshape(T, D)
    p2 = p.reshape(T, PLE)
    tgt = loss_target.reshape(T, D)
    pos = positions.reshape(T, 1)

    row_sharded = [w_pw[0], w_br_conv[0], w_br_attn[0], w_out[0], w_ple_gate[0]]
    sh_rows = D // N_DEV
    (g_in,) = _all_gather([w_in[0].astype(BF16)])
    w_in_f = _to_perm(g_in.transpose(1, 0, 2).reshape(D, NW))
    gather_rest = _Exchange([], [w.astype(BF16) for w in row_sharded] + [
        w_ple_proj[0].astype(BF16), jnp.pad(w_dw[0], ((0, CONV_HALO - CONV_K), (0, 0)))])

    cos, sa, sb = _rope_tables(pos)
    in_out = _in_proj(x2, ln_pre, w_in_f, gather_rest)
    z2, h, g_rows, g_pp, g_dw = in_out[0], in_out[1], in_out[2:7], in_out[7], in_out[8]
    full = [g.reshape(D, D) for g in g_rows]
    w_pp_f = g_pp.transpose(1, 0, 2).reshape(PLE, D)
    w_dw_f = g_dw.transpose(1, 0, 2).reshape(CONV_HALO, D)
    z3 = z2.reshape(NS, S, NW)
    c3, cs3 = _conv_fwd(z3, w_dw_f, b_dw, conv_ln_g, conv_ln_b)
    qr, kv2 = _rope_qk(z2, cos, sa, sb)
    qr3, kv3 = qr.reshape(NS, S, D), kv2.reshape(NS, S, 4 * N_KV * KV_W)
    sinks1 = sinks.reshape(N_HEADS)
    o3, lse3 = _attn_fwd(qr3, kv3, sinks1)
    o = o3.reshape(T, D)
    cs = cs3.reshape(T, D)
    (dya0, ya1, dya, yb0, dyb, m, dmo, x1, dgl, dpp, dcs, do, dcg, dag, dgc, dga, dx1, loss_blk, glp_post) = _mid(
        cs, o, z2, x2, p2, tgt, ln_post, full + [w_pp_f])

    gp_rows = [_grad_matmul(a, b, nm).reshape(N_DEV, sh_rows, D) for a, b, nm in (
        (cs, dya0, "grad_w_pw"), (ya1, dya, "grad_w_br_conv"), (yb0, dyb, "grad_w_br_attn"),
        (m, dmo, "grad_w_out"), (x1, dgl, "grad_w_ple_gate"))]
    gp_pp = _grad_matmul(p2, dpp, "grad_w_ple_proj").reshape(PLE, N_DEV, D // N_DEV).transpose(1, 0, 2)
    conv_out = _conv_bwd(z3, c3, dcs.reshape(NS, S, D), w_dw_f, conv_ln_g, conv_ln_b,
                         _Exchange(gp_rows + [gp_pp], [loss_blk]))
    dzvu3, gw, gvec, r_rows, r_pp, r_loss = conv_out[0], conv_out[1], conv_out[2], conv_out[3:8], conv_out[8], conv_out[9]
    loss = jnp.sum(r_loss[:, 0, 0])
    tab3 = [t.reshape(NS, S, 128) for t in (cos, sa, sb)]
    dq3, dkc3, dkp3, dvc3, dvp3, dsk3 = _attn_bwd(qr3, kv3, do.reshape(NS, S, D), o3, lse3, sinks1, *tab3)
    dkv3 = _attn_post(dkc3, dkp3, dvc3, dvp3, *tab3)
    segs = [dzvu3.reshape(T, 2 * D), dcg, dq3.reshape(T, D), dag, dgc, dga, dkv3.reshape(T, 2 * KV_W)]
    gp_in = _from_perm(jnp.concatenate([_grad_matmul(h, s, f"grad_w_in_{j}") for j, s in enumerate(segs)], axis=1))
    gp_in = gp_in.reshape(D, N_DEV, NW // N_DEV).transpose(1, 0, 2)
    grad_x2, glp_pre, r_in = _in_bwd(segs, w_in_f, x2, dx1, ln_pre, _Exchange([gp_in], []))
    gp_dw, gp_small = _pack_small(gw, gvec, glp_pre, glp_post, dsk3.reshape(T, N_HEADS))
    r_dw, r_small = _Exchange([gp_dw], [gp_small]).alone("small_grad_exchange")

    res = {}
    res["w_in"] = _adamw(r_in, w_in[0], m_w_in[0], v_w_in[0], "adamw_w_in")
    names_rows = ["w_pw", "w_br_conv", "w_br_attn", "w_out", "w_ple_gate"]
    wmv = {"w_pw": (w_pw, m_w_pw, v_w_pw), "w_br_conv": (w_br_conv, m_w_br_conv, v_w_br_conv),
           "w_br_attn": (w_br_attn, m_w_br_attn, v_w_br_attn), "w_out": (w_out, m_w_out, v_w_out),
           "w_ple_gate": (w_ple_gate, m_w_ple_gate, v_w_ple_gate)}
    for nm, parts in zip(names_rows, r_rows):
        w_, m_, v_ = wmv[nm]
        res[nm] = _adamw(parts, w_[0], m_[0], v_[0], "adamw_" + nm)
    res["w_ple_proj"] = _adamw(r_pp, w_ple_proj[0], m_w_ple_proj[0], v_w_ple_proj[0], "adamw_w_ple_proj")
    pad_dw = lambda a: jnp.pad(a[0], ((0, CONV_HALO - CONV_K), (0, 0)))
    res["w_dw"] = [a[:CONV_K] for a in _adamw(r_dw, pad_dw(w_dw), pad_dw(m_w_dw), pad_dw(v_w_dw), "adamw_w_dw")]

    def stack_small(a_pre, a_post, a_b, a_g, a_bb, a_s):
        sk = jnp.pad(a_s, ((0, 0), (0, D - N_HEADS)))
        return jnp.concatenate([a_pre, a_post, a_b, a_g, a_bb, sk, jnp.zeros((2, D), F32)], axis=0)

    small = _adamw(
        r_small, stack_small(ln_pre, ln_post, b_dw, conv_ln_g, conv_ln_b, sinks),
        stack_small(m_ln_pre, m_ln_post, m_b_dw, m_conv_ln_g, m_conv_ln_b, m_sinks),
        stack_small(v_ln_pre, v_ln_post, v_b_dw, v_conv_ln_g, v_conv_ln_b, v_sinks), "adamw_small")
    for j, nm in enumerate(["ln_pre", "ln_post", "b_dw", "conv_ln_g", "conv_ln_b"]):
        res[nm] = [a[j] for a in small]
    res["sinks"] = [a[5, :N_HEADS] for a in small]

    order = ["w_in", "ln_pre", "ln_post", "w_dw", "b_dw", "conv_ln_g", "conv_ln_b", "w_pw", "sinks", "w_br_conv",
             "w_br_attn", "w_out", "w_ple_gate", "w_ple_proj"]
    outs = [loss, grad_x2.reshape(NS, S, D)]
    for kind in range(4):
        outs += [res[nm][kind][None] for nm in order]
    return tuple(outs)
```

```python
import functools

import numpy as np

import jax
import jax.numpy as jnp
from jax import lax
from jax.experimental import pallas as pl
from jax.experimental.pallas import tpu as pltpu

F32 = jnp.float32
BF16 = jnp.bfloat16

D = 1024
N_HEADS = 16
N_KV = 2
HEAD_DIM = 64
GROUP = N_HEADS // N_KV
KV_W = N_KV * HEAD_DIM
CONV_K = 31
CONV_HALO = 32
BLK = 128
ROPE_DIM = 16
ROPE_THETA = 500000.0
EPS = 1e-6
PLE = 256
NW = 7 * D + 2 * KV_W
N_DEV = 8

CB_VAL, CB_GLU, CB_CGATE, CB_Q, CB_AGATE, CB_GCONV, CB_GATTN = range(7)
CB_K = 7 * D // KV_W
CB_V = CB_K + 1

ADAM_LR, ADAM_B1, ADAM_B2, ADAM_EPS, ADAM_WD, ADAM_STEP = 0.001, 0.9, 0.999, 1e-08, 0.01, 10

VMEM_LIMIT = 56 * 1024 * 1024
ROW_TILE = 256


def _cparams(*sem):
    return pltpu.CompilerParams(dimension_semantics=sem if sem else None, vmem_limit_bytes=VMEM_LIMIT)


def _sig(v):
    return 1.0 / (1.0 + jnp.exp(-v))


def _rowsum8(a):
    return a.reshape(a.shape[0] // 8, 8, a.shape[1]).sum(axis=0)


def _dot(a, b):
    return jnp.dot(a, b, preferred_element_type=F32)


def _dot_nt(a, b):
    return lax.dot_general(a, b, (((1,), (1,)), ((), ())), preferred_element_type=F32)


def _dot_tn(a, b):
    return lax.dot_general(a, b, (((0,), (0,)), ((), ())), preferred_element_type=F32)


def _to_perm(w):
    return jnp.concatenate([w[..., : 4 * D], w[..., 4 * D + 2 * KV_W :], w[..., 4 * D : 4 * D + 2 * KV_W]], axis=-1)


def _from_perm(g):
    return jnp.concatenate([g[..., : 4 * D], g[..., 7 * D :], g[..., 4 * D : 7 * D]], axis=-1)


def _my_place():
    return lax.axis_index("x"), lax.axis_index("y"), lax.axis_index("c")


def _slot(px, py, pc):
    return 4 * px + 2 * py + pc


def _all_gather(shards):
    n = len(shards)

    def body(*refs):
        ins, outs = refs[:n], refs[n : 2 * n]
        send_sems, recv_sems, local_sems = refs[2 * n :]
        x, y, c = _my_place()
        me, sibling = (x, y, c), (x, y, 1 - c)
        chips = [(1 - x, y), (x, 1 - y), (1 - x, 1 - y)]

        def copy(a, k, block, to, src=None):
            rows = outs[a].at[_slot(*block)]
            return pltpu.make_async_remote_copy(
                src_ref=rows if src is None else src, dst_ref=rows, send_sem=send_sems.at[a, k],
                recv_sem=recv_sems.at[a, k], device_id=to, device_id_type=pl.DeviceIdType.MESH)

        mine, first, passed = [], [], []
        for a in range(n):
            cp = pltpu.make_async_copy(ins[a], outs[a].at[_slot(*me)], local_sems.at[a])
            cp.start()
            mine.append(cp)
            fa = [copy(a, 0, me, sibling, src=ins[a])]
            fa += [copy(a, 1 + j, me, (*chip, c), src=ins[a]) for j, chip in enumerate(chips)]
            for cp in fa:
                cp.start()
            first += fa
        for j, chip in enumerate(chips):
            for a in range(n):
                copy(a, 1 + j, (*chip, c), me).wait_recv()
                cp = copy(a, 4 + j, (*chip, c), sibling)
                cp.start()
                passed.append(cp)
        for a in range(n):
            copy(a, 0, sibling, me).wait_recv()
            for j, chip in enumerate(chips):
                copy(a, 4 + j, (*chip, 1 - c), me).wait_recv()
        for cp in first + passed:
            cp.wait_send()
        for cp in mine:
            cp.wait()

    any_spec = pl.BlockSpec(memory_space=pl.ANY)
    return pl.pallas_call(
        body, name="weight_all_gather",
        out_shape=[jax.ShapeDtypeStruct((N_DEV,) + s.shape, s.dtype) for s in shards],
        in_specs=[any_spec] * n, out_specs=[any_spec] * n,
        scratch_shapes=[pltpu.SemaphoreType.DMA((n, 7)), pltpu.SemaphoreType.DMA((n, 7)), pltpu.SemaphoreType.DMA((n,))],
    )(*shards)


class _Exchange:
    def __init__(self, scatter, bcast):
        self.arrs = list(scatter) + list(bcast)
        self.n, self.n_sc = len(self.arrs), len(scatter)
        self.out_shape = [jax.ShapeDtypeStruct(a.shape, a.dtype) for a in scatter]
        self.out_shape += [jax.ShapeDtypeStruct((N_DEV,) + a.shape, a.dtype) for a in bcast]
        self.specs = [pl.BlockSpec(memory_space=pl.ANY)] * self.n
        self.scratch = [pltpu.SemaphoreType.DMA((self.n, 7)), pltpu.SemaphoreType.DMA((self.n, 7)),
                        pltpu.SemaphoreType.DMA((self.n,))]

    def _copies(self, ins, outs, sems):
        send_sems, recv_sems, local_sems = sems
        x, y, c = _my_place()
        me = _slot(x, y, c)
        peers = _peers(x, y, c)
        mine, sends, arrivals = [], [], []
        for a in range(self.n):
            src = ins[a].at[me] if a < self.n_sc else ins[a]
            mine.append(pltpu.make_async_copy(src, outs[a].at[me], local_sems.at[a]))
        for k, peer in enumerate(peers):
            for a in range(self.n):
                src = ins[a].at[_slot(*peer)] if a < self.n_sc else ins[a]
                sends.append(pltpu.make_async_remote_copy(
                    src_ref=src, dst_ref=outs[a].at[me], send_sem=send_sems.at[a, k], recv_sem=recv_sems.at[a, k],
                    device_id=peer, device_id_type=pl.DeviceIdType.MESH))
                rows = outs[a].at[_slot(*peer)]
                arrivals.append(pltpu.make_async_remote_copy(
                    src_ref=rows, dst_ref=rows, send_sem=send_sems.at[a, k], recv_sem=recv_sems.at[a, k],
                    device_id=peer, device_id_type=pl.DeviceIdType.MESH))
        return mine, sends, arrivals

    def start(self, ins, outs, sems):
        mine, sends, _ = self._copies(ins, outs, sems)
        for cp in mine + sends:
            cp.start()

    def finish(self, ins, outs, sems):
        mine, sends, arrivals = self._copies(ins, outs, sems)
        for cp in arrivals:
            cp.wait_recv()
        for cp in sends:
            cp.wait_send()
        for cp in mine:
            cp.wait()

    def carried(self, refs_in, refs_out, sems, first, last):
        @pl.when(first)
        def _():
            self.start(refs_in, refs_out, sems)

        @pl.when(last)
        def _():
            self.finish(refs_in, refs_out, sems)

    def alone(self, name):
        n = self.n

        def body(*refs):
            ins, outs, sems = refs[:n], refs[n : 2 * n], refs[2 * n :]
            self.start(ins, outs, sems)
            self.finish(ins, outs, sems)

        return pl.pallas_call(body, name=name, out_shape=self.out_shape, in_specs=self.specs, out_specs=self.specs,
                              scratch_shapes=self.scratch)(*self.arrs)


def _peers(x, y, c):
    return [(1 - x if k & 4 else x, 1 - y if k & 2 else y, 1 - c if k & 1 else c) for k in range(1, N_DEV)]


def _scatter_send(g, name):
    hbm = pl.BlockSpec(memory_space=pltpu.HBM)
    sem = pl.BlockSpec(memory_space=pltpu.SEMAPHORE)

    def body(g_ref, land_ref, send_sems, recv_sems, g_thru, land_thru, token):
        x, y, c = _my_place()
        me = _slot(x, y, c)
        for k, peer in enumerate(_peers(x, y, c)):
            pltpu.make_async_remote_copy(
                src_ref=g_ref.at[_slot(*peer)], dst_ref=land_ref.at[me], send_sem=send_sems.at[k], recv_sem=recv_sems.at[k],
                device_id=peer, device_id_type=pl.DeviceIdType.MESH).start()
        token[...] = jnp.zeros_like(token)

    return pl.pallas_call(
        body, name=name,
        out_shape=(pltpu.SemaphoreType.DMA((N_DEV - 1,)), pltpu.SemaphoreType.DMA((N_DEV - 1,)),
                   pltpu.HBM(g.shape, g.dtype), pltpu.HBM(g.shape, g.dtype), jax.ShapeDtypeStruct((8, 128), F32)),
        in_specs=(hbm, hbm), out_specs=(sem, sem, hbm, hbm, pl.BlockSpec(memory_space=pltpu.VMEM)),
        input_output_aliases={0: 2, 1: 3},
        compiler_params=pltpu.CompilerParams(has_side_effects=pltpu.SideEffectType.DATAFLOW_SIDE_EFFECTING),
    )(pltpu.with_memory_space_constraint(g, pltpu.HBM),
      pltpu.with_memory_space_constraint(lax.empty(g.shape, g.dtype), pltpu.HBM))


def _scatter_wait(send_sems, recv_sems, g_thru, land_thru, after, name):
    hbm = pl.BlockSpec(memory_space=pltpu.HBM)
    sem = pl.BlockSpec(memory_space=pltpu.SEMAPHORE)

    def body(g_ref, land_ref, send_sems, recv_sems, after_ref, g_out, land_out):
        x, y, c = _my_place()
        for k, peer in enumerate(_peers(x, y, c)):
            cp = pltpu.make_async_remote_copy(
                src_ref=g_ref.at[_slot(*peer)], dst_ref=land_ref.at[_slot(*peer)], send_sem=send_sems.at[k],
                recv_sem=recv_sems.at[k], device_id=peer, device_id_type=pl.DeviceIdType.MESH)
            cp.wait_send()
            cp.wait_recv()

    return pl.pallas_call(
        body, name=name,
        out_shape=(pltpu.HBM(g_thru.shape, g_thru.dtype), pltpu.HBM(land_thru.shape, land_thru.dtype)),
        in_specs=(hbm, hbm, sem, sem, pl.BlockSpec(memory_space=pl.ANY)), out_specs=(hbm, hbm),
        input_output_aliases={0: 0, 1: 1},
        compiler_params=pltpu.CompilerParams(has_side_effects=pltpu.SideEffectType.DATAFLOW_SIDE_EFFECTING),
    )(g_thru, land_thru, send_sems, recv_sems, after)


def _host_split(refs, n_in, n_out, n_scratch, ex):
    k = ex.n if ex is not None else 0
    a = n_in
    b = a + k
    c = b + n_out
    d = c + k
    e = d + n_scratch
    return refs[:a], refs[a:b], refs[b:c], refs[c:d], refs[d:e], refs[e:]


def _rope_tables(pos):
    T = pos.shape[0]
    tm = min(1024, T)
    lane = np.arange(128) % HEAD_DIM
    inv = np.power(np.float32(ROPE_THETA), -np.arange(0, ROPE_DIM, 2, dtype=np.float32) / np.float32(ROPE_DIM)).astype(np.float32)
    half = ROPE_DIM // 2
    invf = np.where(lane < ROPE_DIM, inv[lane % half], 0.0).astype(np.float32)[None, :]
    m_a = (lane < half).astype(np.float32)[None, :]
    m_b = ((lane >= half) & (lane < ROPE_DIM)).astype(np.float32)[None, :]

    def body(pos_ref, invf_ref, ma_ref, mb_ref, cos_ref, sa_ref, sb_ref):
        ang = pos_ref[...].astype(F32) * invf_ref[...]
        sn = jnp.sin(ang)
        cos_ref[...] = jnp.cos(ang)
        sa_ref[...] = -sn * ma_ref[...]
        sb_ref[...] = sn * mb_ref[...]

    row = pl.BlockSpec((tm, 128), lambda i: (i, 0))
    cst = pl.BlockSpec((1, 128), lambda i: (0, 0))
    return pl.pallas_call(
        body, name="rope_tables", grid=(T // tm,), out_shape=[jax.ShapeDtypeStruct((T, 128), F32)] * 3,
        in_specs=[pl.BlockSpec((tm, 1), lambda i: (i, 0)), cst, cst, cst], out_specs=[row] * 3,
        compiler_params=_cparams("parallel"),
    )(pos, jnp.asarray(invf), jnp.asarray(m_a), jnp.asarray(m_b))


def _rope(t, cos, sa, sb, sign=1.0):
    parts = []
    for i in range(t.shape[1] // 128):
        ti = t[:, 128 * i : 128 * (i + 1)]
        up = pltpu.roll(ti, 128 - ROPE_DIM // 2, 1)
        dn = pltpu.roll(ti, ROPE_DIM // 2, 1)
        parts.append(ti * cos + sign * (up * sa + dn * sb))
    return parts[0] if len(parts) == 1 else jnp.concatenate(parts, axis=-1)


def _in_proj(x2, ln_pre, w_in, ex):
    T = x2.shape[0]
    tm = min(512, T)
    nt = T // tm
    chunk = 1024

    def body(*refs):
        (x_ref, g_ref, w_ref), ex_in, (z_ref, h_ref), ex_out, _, ex_sems = _host_split(refs, 3, 2, 0, ex)
        i = pl.program_id(0)
        ex.carried(ex_in, ex_out, ex_sems, i == 0, i == nt - 1)
        xv = x_ref[...]
        r = lax.rsqrt(jnp.mean(xv * xv, axis=-1, keepdims=True) + EPS)
        h = (xv * r * g_ref[...]).astype(BF16)
        h_ref[...] = h
        for c0 in range(0, NW, chunk):
            cw = min(chunk, NW - c0)
            z_ref[:, c0 : c0 + cw] = _dot(h, w_ref[:, c0 : c0 + cw]).astype(BF16)

    return pl.pallas_call(
        body, name="in_proj", grid=(nt,),
        out_shape=[jax.ShapeDtypeStruct((T, NW), BF16), jax.ShapeDtypeStruct((T, D), BF16)] + ex.out_shape,
        in_specs=[pl.BlockSpec((tm, D), lambda i: (i, 0)), _weight_spec((1, D)), _weight_spec((D, NW))] + ex.specs,
        out_specs=[pl.BlockSpec((tm, NW), lambda i: (i, 0)), pl.BlockSpec((tm, D), lambda i: (i, 0))] + ex.specs,
        scratch_shapes=ex.scratch,
        compiler_params=_cparams("arbitrary"),
    )(x2, ln_pre, w_in, *ex.arrs)


def _conv_tiles(S):
    tm = min(ROW_TILE, S)
    return tm, S // tm, tm // CONV_HALO


CONV_ROWS_FWD = 32
CONV_ROWS = 16


def _fill_shifted(sh, rows):
    for b in range(1, 8):
        sh[b, 0:rows, :] = sh[0, b : b + rows, :]


def _conv_fwd(z3, w_dw, b_dw, ln_g, ln_b):
    NS, S, _ = z3.shape
    tm, nt, r = _conv_tiles(S)

    def body(val_ref, glu_ref, hval_ref, hglu_ref, w_ref, b_ref, g_ref, bb_ref, c_ref, cs_ref, ush, cbuf):
        i = pl.program_id(1)
        ush[0, CONV_HALO:, :] = val_ref[...].astype(F32) * _sig(glu_ref[...].astype(F32))
        uh = hval_ref[...].astype(F32) * _sig(hglu_ref[...].astype(F32))
        ush[0, 0:CONV_HALO, :] = jnp.where(i > 0, uh, 0.0)
        _fill_shifted(ush, tm + CONV_HALO - 8)
        for r0 in range(0, tm, CONV_ROWS_FWD):
            acc = jnp.zeros((CONV_ROWS_FWD, D), F32)
            for k in range(CONV_K):
                a, b = divmod(CONV_HALO - (CONV_K - 1) + k, 8)
                acc = acc + w_ref[k : k + 1, :] * ush[b, r0 + 8 * a : r0 + 8 * a + CONV_ROWS_FWD, :]
            cbuf[r0 : r0 + CONV_ROWS_FWD, :] = acc + b_ref[...]
        cv = cbuf[...]
        mu = jnp.mean(cv, axis=-1, keepdims=True)
        xc = cv - mu
        var = jnp.mean(xc * xc, axis=-1, keepdims=True)
        cl = xc * lax.rsqrt(var + EPS) * g_ref[...] + bb_ref[...]
        c_ref[...] = cv.astype(BF16)
        cs_ref[...] = (cl * _sig(cl)).astype(BF16)

    def cur(cb):
        return pl.BlockSpec((None, tm, D), lambda s, i: (s, i, cb))

    def halo(cb):
        return pl.BlockSpec((None, CONV_HALO, D), lambda s, i: (s, jnp.maximum(i * r - 1, 0), cb))

    vec = pl.BlockSpec((1, D), lambda s, i: (0, 0))
    out = pl.BlockSpec((None, tm, D), lambda s, i: (s, i, 0))
    return pl.pallas_call(
        body, name="conv_fwd", grid=(NS, nt),
        out_shape=[jax.ShapeDtypeStruct((NS, S, D), BF16)] * 2,
        in_specs=[cur(CB_VAL), cur(CB_GLU), halo(CB_VAL), halo(CB_GLU),
                  pl.BlockSpec((CONV_HALO, D), lambda s, i: (0, 0)), vec, vec, vec],
        out_specs=[out, out],
        scratch_shapes=[pltpu.VMEM((8, tm + CONV_HALO, D), F32), pltpu.VMEM((tm, D), F32)],
        compiler_params=_cparams("parallel", "parallel"),
    )(z3, z3, z3, z3, w_dw, b_dw, ln_g, ln_b)


PAIRS = GROUP // 2
QROWS = PAIRS * BLK


def _kv2_col(kind, g, par):
    return kind * 2 * N_KV + g * 2 + par


def _rope_qk(z2, cos, sa, sb):
    T = z2.shape[0]
    tm = min(ROW_TILE, T)

    def body(q_ref, k_ref, v_ref, cos_ref, sa_ref, sb_ref, qr_ref, kv_ref):
        tabs = cos_ref[...], sa_ref[...], sb_ref[...]
        qr_ref[...] = (_rope(q_ref[...].astype(F32), *tabs) * (HEAD_DIM ** -0.5)).astype(BF16)
        low = lax.broadcasted_iota(jnp.int32, (tm, KV_W), 1) < HEAD_DIM
        for kind, t in ((0, _rope(k_ref[...].astype(F32), *tabs)), (1, v_ref[...].astype(F32))):
            r = pltpu.roll(t, HEAD_DIM, 1)
            cols = {(0, 0): jnp.where(low, t, 0.0), (0, 1): jnp.where(low, 0.0, r),
                    (1, 0): jnp.where(low, r, 0.0), (1, 1): jnp.where(low, 0.0, t)}
            for (g, par), val in cols.items():
                c = _kv2_col(kind, g, par)
                kv_ref[:, c * KV_W : (c + 1) * KV_W] = val.astype(BF16)

    tab = pl.BlockSpec((tm, 128), lambda i: (i, 0))
    return pl.pallas_call(
        body, name="rope_qk", grid=(T // tm,),
        out_shape=[jax.ShapeDtypeStruct((T, D), BF16), jax.ShapeDtypeStruct((T, 4 * N_KV * KV_W), BF16)],
        in_specs=[pl.BlockSpec((tm, D), lambda i: (i, CB_Q)), pl.BlockSpec((tm, KV_W), lambda i: (i, CB_K)),
                  pl.BlockSpec((tm, KV_W), lambda i: (i, CB_V)), tab, tab, tab],
        out_specs=[pl.BlockSpec((tm, D), lambda i: (i, 0)), pl.BlockSpec((tm, 4 * N_KV * KV_W), lambda i: (i, 0))],
        compiler_params=_cparams("parallel"),
    )(z2, z2, z2, cos, sa, sb)


def _attn_mask(has_prev):
    qi = lax.broadcasted_iota(jnp.int32, (QROWS, 2 * BLK), 0) & (BLK - 1)
    kj = lax.broadcasted_iota(jnp.int32, (QROWS, 2 * BLK), 1)
    first_key = jnp.where(has_prev, 0, BLK)
    return (kj > qi) & (kj <= qi + BLK) & (kj >= first_key)


NEG_BIG = -1e30


def _attn_specs():
    q = pl.BlockSpec((None, BLK, D), lambda s, n: (s, n, 0))
    kv_cur = pl.BlockSpec((None, BLK, 4 * N_KV * KV_W), lambda s, n: (s, n, 0))
    kv_prev = pl.BlockSpec((None, BLK, 4 * N_KV * KV_W), lambda s, n: (s, jnp.maximum(n - 1, 0), 0))
    sink = pl.BlockSpec(memory_space=pltpu.SMEM)
    return sink, q, kv_cur, kv_prev


def _stack_pairs(ref, g):
    return jnp.concatenate([ref[:, (PAIRS * g + j) * 128 : (PAIRS * g + j + 1) * 128] for j in range(PAIRS)], axis=0)


def _unstack_pairs(ref, g, val):
    for j in range(PAIRS):
        ref[:, (PAIRS * g + j) * 128 : (PAIRS * g + j + 1) * 128] = val[j * BLK : (j + 1) * BLK].astype(ref.dtype)


def _pair_heads(g, par):
    return [GROUP * g + 2 * j + par for j in range(PAIRS)]


def _head_col_load(ref, g, par):
    return jnp.concatenate([ref[:, h : h + 1] for h in _pair_heads(g, par)], axis=0)


def _head_col_store(ref, g, par, col):
    for j, h in enumerate(_pair_heads(g, par)):
        ref[:, h : h + 1] = col[j * BLK : (j + 1) * BLK]


def _sink_col(sink_ref, g, par):
    return jnp.concatenate([jnp.full((BLK, 1), sink_ref[h], F32) for h in _pair_heads(g, par)], axis=0)


def _kv2_block(kvc_ref, kvp_ref, has_prev, c):
    col = slice(c * KV_W, (c + 1) * KV_W)
    prev = jnp.where(has_prev, kvp_ref[:, col], jnp.zeros((BLK, KV_W), BF16))
    return jnp.concatenate([prev, kvc_ref[:, col]], axis=0)


def _attn_fwd(qr3, kv3, sinks):
    NS, S, _ = qr3.shape

    def body(sink_ref, q_ref, kvc_ref, kvp_ref, o_ref, lse_ref):
        has_prev = pl.program_id(1) > 0
        mask = _attn_mask(has_prev)[0:BLK]
        for g in range(N_KV):
            kv = [[_kv2_block(kvc_ref, kvp_ref, has_prev, _kv2_col(kind, g, par)) for par in range(2)] for kind in range(2)]
            for j in range(PAIRS):
                cols = slice((PAIRS * g + j) * 128, (PAIRS * g + j + 1) * 128)
                q2 = q_ref[:, cols]
                o_pair = None
                for par in range(2):
                    h = GROUP * g + 2 * j + par
                    s = jnp.where(mask, _dot_nt(q2, kv[0][par]), NEG_BIG)
                    sk = sink_ref[h]
                    mx = jnp.maximum(jnp.max(s, axis=-1, keepdims=True), sk)
                    e = jnp.exp(s - mx)
                    den = jnp.sum(e, axis=-1, keepdims=True) + jnp.exp(sk - mx)
                    pv = _dot(e.astype(BF16), kv[1][par]) * (1.0 / den)
                    o_pair = pv if o_pair is None else o_pair + pv
                    lse_ref[:, h : h + 1] = mx + jnp.log(den)
                o_ref[:, cols] = o_pair.astype(BF16)

    return pl.pallas_call(
        body, name="attn_fwd", grid=(NS, S // BLK),
        out_shape=[jax.ShapeDtypeStruct((NS, S, D), BF16), jax.ShapeDtypeStruct((NS, S, N_HEADS), F32)],
        in_specs=list(_attn_specs()),
        out_specs=[pl.BlockSpec((None, BLK, D), lambda s, n: (s, n, 0)),
                   pl.BlockSpec((None, BLK, N_HEADS), lambda s, n: (s, n, 0))],
        compiler_params=_cparams("parallel", "parallel"),
    )(sinks, qr3, kv3, kv3)


def _weight_spec(shape):
    return pl.BlockSpec(shape, lambda i: (0,) * len(shape), pipeline_mode=pl.Buffered(1))


def _dsilu(v, s):
    return s * (1.0 + v * (1.0 - s))


MID_TILE = 256


def _mid(cs, o, z2, x2, p2, tgt, ln_post, weights):
    T = cs.shape[0]
    tm = min(MID_TILE, T)
    nt = T // tm
    n_bf = 16

    def body(cs_ref, o_ref, cg_ref, ag_ref, gc_ref, ga_ref, x_ref, p_ref, t_ref, g_ref,
             wpw, wbrc, wbra, wout, wpg, wpp,
             dya0_ref, ya1_ref, dya_ref, yb0_ref, dyb_ref, m_ref, dmo_ref, x1_ref, dgl_ref, dpp_ref,
             dcs_ref, do_ref, dcg_ref, dag_ref, dgc_ref, dga_ref, dx1_ref, loss_ref, glp_ref, lacc):
        i = pl.program_id(0)

        @pl.when(i == 0)
        def _():
            lacc[...] = jnp.zeros_like(lacc)
            glp_ref[...] = jnp.zeros_like(glp_ref)

        cg = cg_ref[...].astype(F32)
        scg = _sig(cg)
        silu_c = cg * scg
        ya0 = _dot(cs_ref[...], wpw[...])
        ya1 = (ya0 * silu_c).astype(BF16)
        ya1_ref[...] = ya1
        ya = _dot(ya1, wbrc[...])
        ag = ag_ref[...].astype(F32)
        sag = _sig(ag)
        silu_a = ag * sag
        ov = o_ref[...].astype(F32)
        yb0 = (ov * silu_a).astype(BF16)
        yb0_ref[...] = yb0
        yb = _dot(yb0, wbra[...])
        sgc = _sig(gc_ref[...].astype(F32))
        sga = _sig(ga_ref[...].astype(F32))
        mb = (sgc * ya + sga * yb).astype(BF16)
        m_ref[...] = mb
        mo = _dot(mb, wout[...])
        r2 = lax.rsqrt(jnp.mean(mo * mo, axis=-1, keepdims=True) + EPS)
        nrm = mo * r2
        x1 = x_ref[...] + nrm * g_ref[...]
        x1b = x1.astype(BF16)
        x1_ref[...] = x1b
        gate = _sig(_dot(x1b, wpg[...]))
        pp = _dot(p_ref[...].astype(BF16), wpp[...])
        e = x1 + gate * pp - t_ref[...]
        lacc[...] += _rowsum8(e * e)
        dy = e * (1.0 / D)

        dgl = (dy * pp * gate * (1.0 - gate)).astype(BF16)
        dgl_ref[...] = dgl
        dpp_ref[...] = (dy * gate).astype(BF16)
        dx1 = dy + _dot_nt(dgl, wpg[...])
        dx1_ref[...] = dx1
        glp_ref[...] += _rowsum8(dx1 * nrm)
        dn = dx1 * g_ref[...]
        dmo = (r2 * (dn - nrm * jnp.mean(dn * nrm, axis=-1, keepdims=True))).astype(BF16)
        dmo_ref[...] = dmo
        dm = _dot_nt(dmo, wout[...])
        dya = (dm * sgc).astype(BF16)
        dyb = (dm * sga).astype(BF16)
        dya_ref[...] = dya
        dyb_ref[...] = dyb
        dgc_ref[...] = (dm * ya * sgc * (1.0 - sgc)).astype(BF16)
        dga_ref[...] = (dm * yb * sga * (1.0 - sga)).astype(BF16)
        dya1 = _dot_nt(dya, wbrc[...])
        dya0 = (dya1 * silu_c).astype(BF16)
        dya0_ref[...] = dya0
        dcg_ref[...] = (dya1 * ya0 * _dsilu(cg, scg)).astype(BF16)
        dcs_ref[...] = _dot_nt(dya0, wpw[...]).astype(BF16)
        dyb0 = _dot_nt(dyb, wbra[...])
        do_ref[...] = (dyb0 * silu_a).astype(BF16)
        dag_ref[...] = (dyb0 * ov * _dsilu(ag, sag)).astype(BF16)

        @pl.when(i == nt - 1)
        def _():
            loss_ref[...] = jnp.full(loss_ref.shape, jnp.sum(lacc[...]) * (0.5 / D), F32)

    row = pl.BlockSpec((tm, D), lambda i: (i, 0))

    def zcol(cb):
        return pl.BlockSpec((tm, D), lambda i: (i, cb))

    bf = jax.ShapeDtypeStruct((T, D), BF16)
    return pl.pallas_call(
        body, name="mid_fwd_bwd", grid=(nt,),
        out_shape=[bf] * n_bf + [jax.ShapeDtypeStruct((T, D), F32), jax.ShapeDtypeStruct((8, 128), F32),
                                 jax.ShapeDtypeStruct((8, D), F32)],
        in_specs=[row, row, zcol(CB_CGATE), zcol(CB_AGATE), zcol(CB_GCONV), zcol(CB_GATTN), row,
                  pl.BlockSpec((tm, PLE), lambda i: (i, 0)), row, _weight_spec((1, D))]
        + [_weight_spec((D, D))] * 5 + [_weight_spec((PLE, D))],
        out_specs=[row] * (n_bf + 1) + [pl.BlockSpec((8, 128), lambda i: (0, 0)), pl.BlockSpec((8, D), lambda i: (0, 0))],
        scratch_shapes=[pltpu.VMEM((8, D), F32)],
        compiler_params=_cparams("arbitrary"),
    )(cs, o, z2, z2, z2, z2, x2, p2, tgt, ln_post, *weights)


def _conv_bwd(z3, c3, dcs3, w_dw, ln_g, ln_b, ex):
    NS, S, _ = z3.shape
    tm, nt, r = _conv_tiles(S)

    def body(*refs):
        ins, ex_in, outs, ex_out, scratch, ex_sems = _host_split(refs, 9, 3, 3, ex)
        val_ref, glu_ref, c_ref, dcs_ref, hc_ref, hdcs_ref, w_ref, g_ref, bb_ref = ins
        dz_ref, gw_ref, gvec_ref = outs
        dsh, ubuf, dubuf = scratch
        i = pl.program_id(1)
        first = (pl.program_id(0) == 0) & (i == 0)
        ex.carried(ex_in, ex_out, ex_sems, first, (pl.program_id(0) == NS - 1) & (i == nt - 1))

        @pl.when(first)
        def _():
            gw_ref[...] = jnp.zeros_like(gw_ref)
            gvec_ref[...] = jnp.zeros_like(gvec_ref)

        val = val_ref[...].astype(F32)
        sg = _sig(glu_ref[...].astype(F32))
        ubuf[...] = val * sg

        def ln_bwd(cv, dcs):
            cv = cv.astype(F32)
            mu = jnp.mean(cv, axis=-1, keepdims=True)
            xc = cv - mu
            rstd = lax.rsqrt(jnp.mean(xc * xc, axis=-1, keepdims=True) + EPS)
            xhat = xc * rstd
            cl = xhat * g_ref[...] + bb_ref[...]
            s = _sig(cl)
            dcl = dcs.astype(F32) * _dsilu(cl, s)
            dxh = dcl * g_ref[...]
            dc = rstd * (dxh - jnp.mean(dxh, axis=-1, keepdims=True) - xhat * jnp.mean(dxh * xhat, axis=-1, keepdims=True))
            return dc, dcl, xhat

        dc, dcl, xhat = ln_bwd(c_ref[...], dcs_ref[...])
        dsh[0, 0:tm, :] = dc
        dch, _, _ = ln_bwd(hc_ref[...], hdcs_ref[...])
        dsh[0, tm:, :] = jnp.where(i < nt - 1, dch, 0.0)
        gvec_ref[0:8, :] += _rowsum8(dcl * xhat)
        gvec_ref[8:16, :] += _rowsum8(dcl)
        gvec_ref[16:24, :] += _rowsum8(dc)
        _fill_shifted(dsh, tm + CONV_HALO - 8)

        def dc_ahead(r0, k):
            a, b = divmod(CONV_K - 1 - k, 8)
            return dsh[b, r0 + 8 * a : r0 + 8 * a + CONV_ROWS, :]

        for r0 in range(0, tm, CONV_ROWS):
            acc = jnp.zeros((CONV_ROWS, D), F32)
            for k in range(CONV_K):
                acc = acc + w_ref[k : k + 1, :] * dc_ahead(r0, k)
            dubuf[r0 : r0 + CONV_ROWS, :] = acc
        for r0 in range(0, tm, CONV_ROWS):
            ur = ubuf[r0 : r0 + CONV_ROWS, :]
            for k in range(CONV_K):
                gw_ref[8 * k : 8 * k + 8, :] += _rowsum8(ur * dc_ahead(r0, k))
        du = dubuf[...]
        dz_ref[:, 0:D] = (du * sg).astype(BF16)
        dz_ref[:, D:] = (du * val * sg * (1.0 - sg)).astype(BF16)

    def cur(cb):
        return pl.BlockSpec((None, tm, D), lambda s, i: (s, i, cb))

    nxt = pl.BlockSpec((None, CONV_HALO, D), lambda s, i: (s, jnp.minimum((i + 1) * r, S // CONV_HALO - 1), 0))
    vec = pl.BlockSpec((1, D), lambda s, i: (0, 0))
    return pl.pallas_call(
        body, name="conv_bwd", grid=(NS, nt),
        out_shape=[jax.ShapeDtypeStruct((NS, S, 2 * D), BF16), jax.ShapeDtypeStruct((CONV_HALO * 8, D), F32),
                   jax.ShapeDtypeStruct((24, D), F32)] + ex.out_shape,
        in_specs=[cur(CB_VAL), cur(CB_GLU), cur(0), cur(0), nxt, nxt,
                  pl.BlockSpec((CONV_HALO, D), lambda s, i: (0, 0)), vec, vec] + ex.specs,
        out_specs=[pl.BlockSpec((None, tm, 2 * D), lambda s, i: (s, i, 0)),
                   pl.BlockSpec((CONV_HALO * 8, D), lambda s, i: (0, 0)), pl.BlockSpec((24, D), lambda s, i: (0, 0))] + ex.specs,
        scratch_shapes=[pltpu.VMEM((8, tm + CONV_HALO, D), F32), pltpu.VMEM((tm, D), F32), pltpu.VMEM((tm, D), F32)] + ex.scratch,
        compiler_params=_cparams("arbitrary", "arbitrary"),
    )(z3, z3, c3, dcs3, c3, dcs3, w_dw, ln_g, ln_b, *ex.arrs)


def _attn_bwd(qr3, kv3, do3, o3, lse3, sinks, cos3, sa3, sb3):
    NS, S, _ = qr3.shape

    def body(sink_ref, q_ref, kvc_ref, kvp_ref, do_ref, o_ref, lse_ref, cos_ref, sa_ref, sb_ref,
             dq_ref, dkc_ref, dkp_ref, dvc_ref, dvp_ref, dsk_ref):
        has_prev = pl.program_id(1) > 0
        mask = _attn_mask(has_prev)
        tabs = cos_ref[...], sa_ref[...], sb_ref[...]
        low_q = lax.broadcasted_iota(jnp.int32, (QROWS, 128), 1) < HEAD_DIM
        dk_g, dv_g = [], []
        for g in range(N_KV):
            qs = _stack_pairs(q_ref, g)
            dos = _stack_pairs(do_ref, g)
            prod = dos.astype(F32) * _stack_pairs(o_ref, g).astype(F32)
            deltas = [jnp.sum(jnp.where(low_q, prod, 0.0), axis=-1, keepdims=True),
                      jnp.sum(jnp.where(low_q, 0.0, prod), axis=-1, keepdims=True)]
            dq_acc, dk_par, dv_par = None, [], []
            for par in range(2):
                k2 = _kv2_block(kvc_ref, kvp_ref, has_prev, _kv2_col(0, g, par))
                v2 = _kv2_block(kvc_ref, kvp_ref, has_prev, _kv2_col(1, g, par))
                lse = _head_col_load(lse_ref, g, par)
                p = jnp.exp(jnp.where(mask, _dot_nt(qs, k2), NEG_BIG) - lse)
                ds = (p * (_dot_nt(dos, v2) - deltas[par])).astype(BF16)
                dq = _dot(ds, k2)
                dq_acc = dq if dq_acc is None else dq_acc + dq
                dk_par.append(_dot_tn(ds, qs))
                dv_par.append(_dot_tn(p.astype(BF16), dos))
                _head_col_store(dsk_ref, g, par, -jnp.exp(_sink_col(sink_ref, g, par) - lse) * deltas[par])
            for j in range(PAIRS):
                dq_pair = _rope(dq_acc[j * BLK : (j + 1) * BLK], *tabs, sign=-1.0) * (HEAD_DIM ** -0.5)
                dq_ref[:, (PAIRS * g + j) * 128 : (PAIRS * g + j + 1) * 128] = dq_pair.astype(BF16)
            dk_g.append(dk_par[0] + pltpu.roll(dk_par[1], HEAD_DIM, 1))
            dv_g.append(dv_par[0] + pltpu.roll(dv_par[1], HEAD_DIM, 1))
        low_k = lax.broadcasted_iota(jnp.int32, (2 * BLK, KV_W), 1) < HEAD_DIM
        dk = jnp.where(low_k, dk_g[0], pltpu.roll(dk_g[1], HEAD_DIM, 1))
        dv = jnp.where(low_k, dv_g[0], pltpu.roll(dv_g[1], HEAD_DIM, 1))
        dkp_ref[...] = dk[0:BLK]
        dkc_ref[...] = dk[BLK:]
        dvp_ref[...] = dv[0:BLK]
        dvc_ref[...] = dv[BLK:]

    qspec = pl.BlockSpec((None, BLK, D), lambda s, n: (s, n, 0))
    kvspec = pl.BlockSpec((None, BLK, KV_W), lambda s, n: (s, n, 0))
    hspec = pl.BlockSpec((None, BLK, N_HEADS), lambda s, n: (s, n, 0))
    kv = jax.ShapeDtypeStruct((NS, S, KV_W), F32)
    return pl.pallas_call(
        body, name="attn_bwd", grid=(NS, S // BLK),
        out_shape=[jax.ShapeDtypeStruct((NS, S, D), BF16), kv, kv, kv, kv, jax.ShapeDtypeStruct((NS, S, N_HEADS), F32)],
        in_specs=list(_attn_specs()) + [qspec, qspec, hspec, kvspec, kvspec, kvspec],
        out_specs=[qspec, kvspec, kvspec, kvspec, kvspec, hspec],
        compiler_params=_cparams("parallel", "parallel"),
    )(sinks, qr3, kv3, kv3, do3, o3, lse3, cos3, sa3, sb3)


def _attn_post(dkc3, dkp3, dvc3, dvp3, cos3, sa3, sb3):
    NS, S, _ = dkc3.shape
    tm = min(4 * BLK, S)
    nt = S // tm

    def body(dkc_ref, dkp_ref, dkn_ref, dvc_ref, dvp_ref, dvn_ref, cos_ref, sa_ref, sb_ref, dkv_ref):
        has_next = pl.program_id(1) < nt - 1

        def join(cur_ref, prev_ref, next_ref):
            ahead = jnp.where(has_next, next_ref[...], 0.0)
            shifted = ahead if tm == BLK else jnp.concatenate([prev_ref[BLK:, :], ahead], axis=0)
            return cur_ref[...] + shifted

        tabs = cos_ref[...], sa_ref[...], sb_ref[...]
        dkv_ref[:, 0:KV_W] = _rope(join(dkc_ref, dkp_ref, dkn_ref), *tabs, sign=-1.0).astype(BF16)
        dkv_ref[:, KV_W:] = join(dvc_ref, dvp_ref, dvn_ref).astype(BF16)

    cur = pl.BlockSpec((None, tm, KV_W), lambda s, j: (s, j, 0))
    nxt = pl.BlockSpec((None, BLK, KV_W), lambda s, j: (s, jnp.minimum((j + 1) * (tm // BLK), S // BLK - 1), 0))
    return pl.pallas_call(
        body, name="attn_post", grid=(NS, nt),
        out_shape=jax.ShapeDtypeStruct((NS, S, 2 * KV_W), BF16),
        in_specs=[cur, cur, nxt, cur, cur, nxt, cur, cur, cur],
        out_specs=pl.BlockSpec((None, tm, 2 * KV_W), lambda s, j: (s, j, 0)),
        compiler_params=_cparams("parallel", "parallel"),
    )(dkc3, dkp3, dkp3, dvc3, dvp3, dvp3, cos3, sa3, sb3)


def _in_bwd(segs, w_in, x2, dx1, ln_pre, after):
    T = x2.shape[0]
    tm = min(ROW_TILE, T)
    nt = T // tm
    ns = len(segs)
    widths = [s.shape[1] for s in segs]

    def body(*refs):
        seg_refs = refs[:ns]
        w_ref, x_ref, dx1_ref, g_ref, _, gx_ref, glp_ref = refs[ns:]
        i = pl.program_id(0)

        @pl.when(i == 0)
        def _():
            glp_ref[...] = jnp.zeros_like(glp_ref)

        dh = None
        r0 = 0
        for sref, w in zip(seg_refs, widths):
            part = _dot_nt(sref[...], w_ref[:, r0 : r0 + w])
            dh = part if dh is None else dh + part
            r0 += w
        xv = x_ref[...]
        r1 = lax.rsqrt(jnp.mean(xv * xv, axis=-1, keepdims=True) + EPS)
        xhat = xv * r1
        glp_ref[...] += _rowsum8(dh * xhat)
        dhg = dh * g_ref[...]
        gx_ref[...] = dx1_ref[...] + r1 * (dhg - xhat * jnp.mean(dhg * xhat, axis=-1, keepdims=True))

    row = pl.BlockSpec((tm, D), lambda i: (i, 0))
    return pl.pallas_call(
        body, name="in_bwd", grid=(nt,),
        out_shape=[jax.ShapeDtypeStruct((T, D), F32), jax.ShapeDtypeStruct((8, D), F32)],
        in_specs=[pl.BlockSpec((tm, w), lambda i: (i, 0)) for w in widths]
        + [_weight_spec((D, NW)), row, row, _weight_spec((1, D)), pl.BlockSpec(memory_space=pl.ANY)],
        out_specs=[row, pl.BlockSpec((8, D), lambda i: (0, 0))],
        compiler_params=_cparams("arbitrary"),
    )(*segs, w_in, x2, dx1, ln_pre, after)


def _grad_matmul(a, b, name):
    T, M = a.shape
    N = b.shape[1]
    tk = min(1024, T)
    nk = T // tk

    def body(a_ref, b_ref, o_ref, acc):
        k = pl.program_id(0)

        @pl.when(k == 0)
        def _():
            acc[...] = jnp.zeros_like(acc)

        acc[...] += _dot_tn(a_ref[...].astype(BF16), b_ref[...])

        @pl.when(k == nk - 1)
        def _():
            o_ref[...] = acc[...].astype(BF16)

    return pl.pallas_call(
        body, name=name, grid=(nk,), out_shape=jax.ShapeDtypeStruct((M, N), BF16),
        in_specs=[pl.BlockSpec((tk, M), lambda k: (k, 0)), pl.BlockSpec((tk, N), lambda k: (k, 0))],
        out_specs=pl.BlockSpec((M, N), lambda k: (0, 0)),
        scratch_shapes=[pltpu.VMEM((M, N), F32)],
        compiler_params=_cparams("arbitrary"),
    )(a, b)


def _pack_small(gw, gvec, glp_pre, glp_post, dsk):
    T = dsk.shape[0]

    def body(gw_ref, gvec_ref, pre_ref, post_ref, dsk_ref, gdw_ref, gs_ref):
        gwf = gw_ref[...].reshape(CONV_HALO, 8, D).sum(axis=1)
        for d in range(N_DEV):
            gdw_ref[d] = gwf[:, 128 * d : 128 * (d + 1)]
        gs_ref[...] = jnp.zeros_like(gs_ref)
        gs_ref[0:1, :] = jnp.sum(pre_ref[...], axis=0, keepdims=True)
        gs_ref[1:2, :] = jnp.sum(post_ref[...], axis=0, keepdims=True)
        gs_ref[2:3, :] = jnp.sum(gvec_ref[16:24, :], axis=0, keepdims=True)
        gs_ref[3:4, :] = jnp.sum(gvec_ref[0:8, :], axis=0, keepdims=True)
        gs_ref[4:5, :] = jnp.sum(gvec_ref[8:16, :], axis=0, keepdims=True)
        gs_ref[5:6, 0:N_HEADS] = jnp.sum(dsk_ref[...], axis=0, keepdims=True)

    return pl.pallas_call(
        body, name="pack_small",
        out_shape=[jax.ShapeDtypeStruct((N_DEV, CONV_HALO, 128), F32), jax.ShapeDtypeStruct((8, D), F32)],
        compiler_params=_cparams(),
    )(gw, gvec, glp_pre, glp_post, dsk)


def _adam_update(g, w_ref, m_ref, v_ref, g_ref, d_ref, nm_ref, nv_ref):
    nm = ADAM_B1 * m_ref[...] + (1.0 - ADAM_B1) * g
    nv = ADAM_B2 * v_ref[...] + (1.0 - ADAM_B2) * (g * g)
    m_hat = nm / (1.0 - ADAM_B1 ** ADAM_STEP)
    v_hat = nv / (1.0 - ADAM_B2 ** ADAM_STEP)
    g_ref[...] = g
    d_ref[...] = -ADAM_LR * (m_hat / (jnp.sqrt(v_hat) + ADAM_EPS) + ADAM_WD * w_ref[...])
    nm_ref[...] = nm
    nv_ref[...] = nv


def _adamw(parts, w, m, v, name):
    R, C = w.shape
    tr = R if R <= 256 else 128

    def body(p_ref, *rest):
        g = p_ref[0].astype(F32)
        for s in range(1, N_DEV):
            g = g + p_ref[s].astype(F32)
        _adam_update(g, *rest)

    blk = pl.BlockSpec((tr, C), lambda i: (i, 0))
    return pl.pallas_call(
        body, name=name, grid=(R // tr,), out_shape=[jax.ShapeDtypeStruct((R, C), F32)] * 4,
        in_specs=[pl.BlockSpec((N_DEV, tr, C), lambda i: (0, i, 0)), blk, blk, blk], out_specs=[blk] * 4,
        compiler_params=_cparams("parallel"),
    )(parts, w, m, v)


def _adamw_own(parts, own, me, w, m, v, name):
    R, C = w.shape
    tr = 128

    def body(me_ref, p_ref, own_ref, *rest):
        g = None
        for s in range(N_DEV):
            part = jnp.where(me_ref[0] == s, own_ref[...], p_ref[s]).astype(F32)
            g = part if g is None else g + part
        _adam_update(g, *rest)

    blk = pl.BlockSpec((tr, C), lambda i, me: (i, 0))
    return pl.pallas_call(
        body, name=name, out_shape=[jax.ShapeDtypeStruct((R, C), F32)] * 4,
        grid_spec=pltpu.PrefetchScalarGridSpec(
            num_scalar_prefetch=1, grid=(R // tr,),
            in_specs=[pl.BlockSpec((N_DEV, tr, C), lambda i, me: (0, i, 0)),
                      pl.BlockSpec((None, tr, C), lambda i, me: (me[0], i, 0)), blk, blk, blk],
            out_specs=[blk] * 4),
        compiler_params=_cparams("parallel"),
    )(me, parts, own, w, m, v)


def kernel(x, p, positions, w_in, ln_pre, ln_post, w_dw, b_dw, conv_ln_g, conv_ln_b, w_pw, sinks, w_br_conv, w_br_attn, w_out, w_ple_gate, w_ple_proj, loss_target, m_w_in, m_ln_pre, m_ln_post, m_w_dw, m_b_dw, m_conv_ln_g, m_conv_ln_b, m_w_pw, m_sinks, m_w_br_conv, m_w_br_attn, m_w_out, m_w_ple_gate, m_w_ple_proj, v_w_in, v_ln_pre, v_ln_post, v_w_dw, v_b_dw, v_conv_ln_g, v_conv_ln_b, v_w_pw, v_sinks, v_w_br_conv, v_w_br_attn, v_w_out, v_w_ple_gate, v_w_ple_proj):
    NS, S, _ = x.shape
    T = NS * S
    x2 = x.reshape(T, D)
    p2 = p.reshape(T, PLE)
    tgt = loss_target.reshape(T, D)
    pos = positions.reshape(T, 1)

    row_sharded = [w_pw[0], w_br_conv[0], w_br_attn[0], w_out[0], w_ple_gate[0]]
    sh_rows = D // N_DEV
    (g_in,) = _all_gather([w_in[0].astype(BF16)])
    w_in_f = _to_perm(g_in.transpose(1, 0, 2).reshape(D, NW))
    gather_rest = _Exchange([], [w.astype(BF16) for w in row_sharded] + [
        w_ple_proj[0].astype(BF16), jnp.pad(w_dw[0], ((0, CONV_HALO - CONV_K), (0, 0)))])

    cos, sa, sb = _rope_tables(pos)
    in_out = _in_proj(x2, ln_pre, w_in_f, gather_rest)
    z2, h, g_rows, g_pp, g_dw = in_out[0], in_out[1], in_out[2:7], in_out[7], in_out[8]
    full = [g.reshape(D, D) for g in g_rows]
    w_pp_f = g_pp.transpose(1, 0, 2).reshape(PLE, D)
    w_dw_f = g_dw.transpose(1, 0, 2).reshape(CONV_HALO, D)
    z3 = z2.reshape(NS, S, NW)
    c3, cs3 = _conv_fwd(z3, w_dw_f, b_dw, conv_ln_g, conv_ln_b)
    qr, kv2 = _rope_qk(z2, cos, sa, sb)
    qr3, kv3 = qr.reshape(NS, S, D), kv2.reshape(NS, S, 4 * N_KV * KV_W)
    sinks1 = sinks.reshape(N_HEADS)
    o3, lse3 = _attn_fwd(qr3, kv3, sinks1)
    o = o3.reshape(T, D)
    cs = cs3.reshape(T, D)
    (dya0, ya1, dya, yb0, dyb, m, dmo, x1, dgl, dpp, dcs, do, dcg, dag, dgc, dga, dx1, loss_blk, glp_post) = _mid(
        cs, o, z2, x2, p2, tgt, ln_post, full + [w_pp_f])

    gp_rows = [_grad_matmul(a, b, nm).reshape(N_DEV, sh_rows, D) for a, b, nm in (
        (cs, dya0, "grad_w_pw"), (ya1, dya, "grad_w_br_conv"), (yb0, dyb, "grad_w_br_attn"),
        (m, dmo, "grad_w_out"), (x1, dgl, "grad_w_ple_gate"))]
    gp_pp = _grad_matmul(p2, dpp, "grad_w_ple_proj").reshape(PLE, N_DEV, D // N_DEV).transpose(1, 0, 2)
    conv_out = _conv_bwd(z3, c3, dcs.reshape(NS, S, D), w_dw_f, conv_ln_g, conv_ln_b,
                         _Exchange(gp_rows + [gp_pp], [loss_blk]))
    dzvu3, gw, gvec, r_rows, r_pp, r_loss = conv_out[0], conv_out[1], conv_out[2], conv_out[3:8], conv_out[8], conv_out[9]
    loss = jnp.sum(r_loss[:, 0, 0])
    tab3 = [t.reshape(NS, S, 128) for t in (cos, sa, sb)]
    dq3, dkc3, dkp3, dvc3, dvp3, dsk3 = _attn_bwd(qr3, kv3, do.reshape(NS, S, D), o3, lse3, sinks1, *tab3)
    dkv3 = _attn_post(dkc3, dkp3, dvc3, dvp3, *tab3)
    segs = [dzvu3.reshape(T, 2 * D), dcg, dq3.reshape(T, D), dag, dgc, dga, dkv3.reshape(T, 2 * KV_W)]
    gp_in = _from_perm(jnp.concatenate([_grad_matmul(h, s, f"grad_w_in_{j}") for j, s in enumerate(segs)], axis=1))
    gp_in = gp_in.reshape(D, N_DEV, NW // N_DEV).transpose(1, 0, 2)
    in_send, in_recv, in_own, in_land, sent = _scatter_send(gp_in, "w_in_grad_send")
    grad_x2, glp_pre = _in_bwd(segs, w_in_f, x2, dx1, ln_pre, sent)
    gp_dw, gp_small = _pack_small(gw, gvec, glp_pre, glp_post, dsk3.reshape(T, N_HEADS))
    r_dw, r_small = _Exchange([gp_dw], [gp_small]).alone("small_grad_exchange")

    res = {}
    names_rows = ["w_pw", "w_br_conv", "w_br_attn", "w_out", "w_ple_gate"]
    wmv = {"w_pw": (w_pw, m_w_pw, v_w_pw), "w_br_conv": (w_br_conv, m_w_br_conv, v_w_br_conv),
           "w_br_attn": (w_br_attn, m_w_br_attn, v_w_br_attn), "w_out": (w_out, m_w_out, v_w_out),
           "w_ple_gate": (w_ple_gate, m_w_ple_gate, v_w_ple_gate)}
    for nm, parts in zip(names_rows, r_rows):
        w_, m_, v_ = wmv[nm]
        res[nm] = _adamw(parts, w_[0], m_[0], v_[0], "adamw_" + nm)
    res["w_ple_proj"] = _adamw(r_pp, w_ple_proj[0], m_w_ple_proj[0], v_w_ple_proj[0], "adamw_w_ple_proj")
    pad_dw = lambda a: jnp.pad(a[0], ((0, CONV_HALO - CONV_K), (0, 0)))
    res["w_dw"] = [a[:CONV_K] for a in _adamw(r_dw, pad_dw(w_dw), pad_dw(m_w_dw), pad_dw(v_w_dw), "adamw_w_dw")]

    def stack_small(a_pre, a_post, a_b, a_g, a_bb, a_s):
        sk = jnp.pad(a_s, ((0, 0), (0, D - N_HEADS)))
        return jnp.concatenate([a_pre, a_post, a_b, a_g, a_bb, sk, jnp.zeros((2, D), F32)], axis=0)

    small = _adamw(
        r_small, stack_small(ln_pre, ln_post, b_dw, conv_ln_g, conv_ln_b, sinks),
        stack_small(m_ln_pre, m_ln_post, m_b_dw, m_conv_ln_g, m_conv_ln_b, m_sinks),
        stack_small(v_ln_pre, v_ln_post, v_b_dw, v_conv_ln_g, v_conv_ln_b, v_sinks), "adamw_small")
    for j, nm in enumerate(["ln_pre", "ln_post", "b_dw", "conv_ln_g", "conv_ln_b"]):
        res[nm] = [a[j] for a in small]
    res["sinks"] = [a[5, :N_HEADS] for a in small]
    in_own, in_land = _scatter_wait(in_send, in_recv, in_own, in_land, small[0], "w_in_grad_wait")
    me = _slot(*_my_place()).astype(jnp.int32).reshape(1)
    res["w_in"] = _adamw_own(in_land, in_own, me, w_in[0], m_w_in[0], v_w_in[0], "adamw_w_in")

    order = ["w_in", "ln_pre", "ln_post", "w_dw", "b_dw", "conv_ln_g", "conv_ln_b", "w_pw", "sinks", "w_br_conv",
             "w_br_attn", "w_out", "w_ple_gate", "w_ple_proj"]
    outs = [loss, grad_x2.reshape(NS, S, D)]
    for kind in range(4):
        outs += [res[nm][kind][None] for nm in order]
    return tuple(outs)
```

```python
import functools

import numpy as np

import jax
import jax.numpy as jnp
from jax import lax
from jax.experimental import pallas as pl
from jax.experimental.pallas import tpu as pltpu

F32 = jnp.float32
BF16 = jnp.bfloat16

D = 1024
N_HEADS = 16
N_KV = 2
HEAD_DIM = 64
GROUP = N_HEADS // N_KV
KV_W = N_KV * HEAD_DIM
CONV_K = 31
CONV_HALO = 32
BLK = 128
ROPE_DIM = 16
ROPE_THETA = 500000.0
EPS = 1e-6
PLE = 256
NW = 7 * D + 2 * KV_W
N_DEV = 8

CB_VAL, CB_GLU, CB_CGATE, CB_Q, CB_AGATE, CB_GCONV, CB_GATTN = range(7)
CB_K = 7 * D // KV_W

ADAM_LR, ADAM_B1, ADAM_B2, ADAM_EPS, ADAM_WD, ADAM_STEP = 0.001, 0.9, 0.999, 1e-08, 0.01, 10

VMEM_LIMIT = 56 * 1024 * 1024
ROW_TILE = 256


def _cparams(*sem):
    return pltpu.CompilerParams(dimension_semantics=sem if sem else None, vmem_limit_bytes=VMEM_LIMIT)


def _sig(v):
    return 1.0 / (1.0 + jnp.exp(-v))


def _rowsum8(a):
    return a.reshape(a.shape[0] // 8, 8, a.shape[1]).sum(axis=0)


def _dot(a, b):
    return jnp.dot(a, b, preferred_element_type=F32)


def _dot_nt(a, b):
    return lax.dot_general(a, b, (((1,), (1,)), ((), ())), preferred_element_type=F32)


def _dot_tn(a, b):
    return lax.dot_general(a, b, (((0,), (0,)), ((), ())), preferred_element_type=F32)


def _to_perm(w):
    return jnp.concatenate([w[..., : 4 * D], w[..., 4 * D + 2 * KV_W :], w[..., 4 * D : 4 * D + 2 * KV_W]], axis=-1)


def _from_perm(g):
    return jnp.concatenate([g[..., : 4 * D], g[..., 7 * D :], g[..., 4 * D : 7 * D]], axis=-1)


def _my_place():
    return lax.axis_index("x"), lax.axis_index("y"), lax.axis_index("c")


def _slot(px, py, pc):
    return 4 * px + 2 * py + pc


class _TwoLevelGather:
    def __init__(self, shards):
        self.arrs = list(shards)
        self.n = len(self.arrs)
        self.out_shape = [jax.ShapeDtypeStruct((N_DEV,) + s.shape, s.dtype) for s in shards]
        self.specs = [pl.BlockSpec(memory_space=pl.ANY)] * self.n
        self.scratch = [pltpu.SemaphoreType.DMA((self.n, 7)), pltpu.SemaphoreType.DMA((self.n, 7)),
                        pltpu.SemaphoreType.DMA((self.n,))]

    def _plan(self, ins, outs, sems):
        send_sems, recv_sems, local_sems = sems
        x, y, c = _my_place()
        me, sibling = (x, y, c), (x, y, 1 - c)
        chips = [(1 - x, y), (x, 1 - y), (1 - x, 1 - y)]

        def copy(a, k, block, to, src=None):
            rows = outs[a].at[_slot(*block)]
            return pltpu.make_async_remote_copy(
                src_ref=rows if src is None else src, dst_ref=rows, send_sem=send_sems.at[a, k],
                recv_sem=recv_sems.at[a, k], device_id=to, device_id_type=pl.DeviceIdType.MESH)

        mine = [pltpu.make_async_copy(ins[a], outs[a].at[_slot(*me)], local_sems.at[a]) for a in range(self.n)]
        first = []
        for a in range(self.n):
            first.append(copy(a, 0, me, sibling, src=ins[a]))
            first += [copy(a, 1 + j, me, (*chip, c), src=ins[a]) for j, chip in enumerate(chips)]
        return copy, mine, first, me, sibling, chips, c

    def start(self, ins, outs, sems):
        _, mine, first, *_ = self._plan(ins, outs, sems)
        for cp in mine + first:
            cp.start()

    def finish(self, ins, outs, sems):
        copy, mine, first, me, sibling, chips, c = self._plan(ins, outs, sems)
        passed = []
        for j, chip in enumerate(chips):
            for a in range(self.n):
                copy(a, 1 + j, (*chip, c), me).wait_recv()
                cp = copy(a, 4 + j, (*chip, c), sibling)
                cp.start()
                passed.append(cp)
        for a in range(self.n):
            copy(a, 0, sibling, me).wait_recv()
            for j, chip in enumerate(chips):
                copy(a, 4 + j, (*chip, 1 - c), me).wait_recv()
        for cp in first + passed:
            cp.wait_send()
        for cp in mine:
            cp.wait()

    def carried(self, refs_in, refs_out, sems, first, last):
        @pl.when(first)
        def _():
            self.start(refs_in, refs_out, sems)

        @pl.when(last)
        def _():
            self.finish(refs_in, refs_out, sems)


class _Exchange:
    def __init__(self, scatter, bcast):
        self.arrs = list(scatter) + list(bcast)
        self.n, self.n_sc = len(self.arrs), len(scatter)
        self.out_shape = [jax.ShapeDtypeStruct(a.shape, a.dtype) for a in scatter]
        self.out_shape += [jax.ShapeDtypeStruct((N_DEV,) + a.shape, a.dtype) for a in bcast]
        self.specs = [pl.BlockSpec(memory_space=pl.ANY)] * self.n
        self.scratch = [pltpu.SemaphoreType.DMA((self.n, 7)), pltpu.SemaphoreType.DMA((self.n, 7)),
                        pltpu.SemaphoreType.DMA((self.n,))]

    def _copies(self, ins, outs, sems):
        send_sems, recv_sems, local_sems = sems
        x, y, c = _my_place()
        me = _slot(x, y, c)
        peers = _peers(x, y, c)
        mine, sends, arrivals = [], [], []
        for a in range(self.n):
            src = ins[a].at[me] if a < self.n_sc else ins[a]
            mine.append(pltpu.make_async_copy(src, outs[a].at[me], local_sems.at[a]))
        for k, peer in enumerate(peers):
            for a in range(self.n):
                src = ins[a].at[_slot(*peer)] if a < self.n_sc else ins[a]
                sends.append(pltpu.make_async_remote_copy(
                    src_ref=src, dst_ref=outs[a].at[me], send_sem=send_sems.at[a, k], recv_sem=recv_sems.at[a, k],
                    device_id=peer, device_id_type=pl.DeviceIdType.MESH))
                rows = outs[a].at[_slot(*peer)]
                arrivals.append(pltpu.make_async_remote_copy(
                    src_ref=rows, dst_ref=rows, send_sem=send_sems.at[a, k], recv_sem=recv_sems.at[a, k],
                    device_id=peer, device_id_type=pl.DeviceIdType.MESH))
        return mine, sends, arrivals

    def start(self, ins, outs, sems):
        mine, sends, _ = self._copies(ins, outs, sems)
        for cp in mine + sends:
            cp.start()

    def finish(self, ins, outs, sems):
        mine, sends, arrivals = self._copies(ins, outs, sems)
        for cp in arrivals:
            cp.wait_recv()
        for cp in sends:
            cp.wait_send()
        for cp in mine:
            cp.wait()

    def carried(self, refs_in, refs_out, sems, first, last):
        @pl.when(first)
        def _():
            self.start(refs_in, refs_out, sems)

        @pl.when(last)
        def _():
            self.finish(refs_in, refs_out, sems)

    def alone(self, name):
        n = self.n

        def body(*refs):
            ins, outs, sems = refs[:n], refs[n : 2 * n], refs[2 * n :]
            self.start(ins, outs, sems)
            self.finish(ins, outs, sems)

        return pl.pallas_call(body, name=name, out_shape=self.out_shape, in_specs=self.specs, out_specs=self.specs,
                              scratch_shapes=self.scratch)(*self.arrs)


def _peers(x, y, c):
    return [(1 - x if k & 4 else x, 1 - y if k & 2 else y, 1 - c if k & 1 else c) for k in range(1, N_DEV)]


def _scatter_send(g, name):
    hbm = pl.BlockSpec(memory_space=pltpu.HBM)
    sem = pl.BlockSpec(memory_space=pltpu.SEMAPHORE)

    def body(g_ref, land_ref, send_sems, recv_sems, g_thru, land_thru, token):
        x, y, c = _my_place()
        me = _slot(x, y, c)
        for k, peer in enumerate(_peers(x, y, c)):
            pltpu.make_async_remote_copy(
                src_ref=g_ref.at[_slot(*peer)], dst_ref=land_ref.at[me], send_sem=send_sems.at[k], recv_sem=recv_sems.at[k],
                device_id=peer, device_id_type=pl.DeviceIdType.MESH).start()
        token[...] = jnp.zeros_like(token)

    return pl.pallas_call(
        body, name=name,
        out_shape=(pltpu.SemaphoreType.DMA((N_DEV - 1,)), pltpu.SemaphoreType.DMA((N_DEV - 1,)),
                   pltpu.HBM(g.shape, g.dtype), pltpu.HBM(g.shape, g.dtype), jax.ShapeDtypeStruct((8, 128), F32)),
        in_specs=(hbm, hbm), out_specs=(sem, sem, hbm, hbm, pl.BlockSpec(memory_space=pltpu.VMEM)),
        input_output_aliases={0: 2, 1: 3},
        compiler_params=pltpu.CompilerParams(has_side_effects=pltpu.SideEffectType.DATAFLOW_SIDE_EFFECTING),
    )(pltpu.with_memory_space_constraint(g, pltpu.HBM),
      pltpu.with_memory_space_constraint(lax.empty(g.shape, g.dtype), pltpu.HBM))


def _scatter_wait(send_sems, recv_sems, g_thru, land_thru, after, name):
    hbm = pl.BlockSpec(memory_space=pltpu.HBM)
    sem = pl.BlockSpec(memory_space=pltpu.SEMAPHORE)

    def body(g_ref, land_ref, send_sems, recv_sems, after_ref, g_out, land_out):
        x, y, c = _my_place()
        for k, peer in enumerate(_peers(x, y, c)):
            cp = pltpu.make_async_remote_copy(
                src_ref=g_ref.at[_slot(*peer)], dst_ref=land_ref.at[_slot(*peer)], send_sem=send_sems.at[k],
                recv_sem=recv_sems.at[k], device_id=peer, device_id_type=pl.DeviceIdType.MESH)
            cp.wait_send()
            cp.wait_recv()

    return pl.pallas_call(
        body, name=name,
        out_shape=(pltpu.HBM(g_thru.shape, g_thru.dtype), pltpu.HBM(land_thru.shape, land_thru.dtype)),
        in_specs=(hbm, hbm, sem, sem, pl.BlockSpec(memory_space=pl.ANY)), out_specs=(hbm, hbm),
        input_output_aliases={0: 0, 1: 1},
        compiler_params=pltpu.CompilerParams(has_side_effects=pltpu.SideEffectType.DATAFLOW_SIDE_EFFECTING),
    )(g_thru, land_thru, send_sems, recv_sems, after)


def _host_split(refs, n_in, n_out, n_scratch, ex):
    k = ex.n if ex is not None else 0
    a = n_in
    b = a + k
    c = b + n_out
    d = c + k
    e = d + n_scratch
    return refs[:a], refs[a:b], refs[b:c], refs[c:d], refs[d:e], refs[e:]


def _rope_tables(pos, ex):
    T = pos.shape[0]
    tm = min(1024, T)
    nt = T // tm
    lane = np.arange(128) % HEAD_DIM
    inv = np.power(np.float32(ROPE_THETA), -np.arange(0, ROPE_DIM, 2, dtype=np.float32) / np.float32(ROPE_DIM)).astype(np.float32)
    half = ROPE_DIM // 2
    invf = np.where(lane < ROPE_DIM, inv[lane % half], 0.0).astype(np.float32)[None, :]
    m_a = (lane < half).astype(np.float32)[None, :]
    m_b = ((lane >= half) & (lane < ROPE_DIM)).astype(np.float32)[None, :]

    def body(*refs):
        ins, ex_in, (cos_ref, sa_ref, sb_ref), ex_out, _, ex_sems = _host_split(refs, 4, 3, 0, ex)
        pos_ref, invf_ref, ma_ref, mb_ref = ins
        i = pl.program_id(0)
        ex.carried(ex_in, ex_out, ex_sems, i == 0, i == nt - 1)
        ang = pos_ref[...].astype(F32) * invf_ref[...]
        sn = jnp.sin(ang)
        cos_ref[...] = jnp.cos(ang)
        sa_ref[...] = -sn * ma_ref[...]
        sb_ref[...] = sn * mb_ref[...]

    row = pl.BlockSpec((tm, 128), lambda i: (i, 0))
    cst = pl.BlockSpec((1, 128), lambda i: (0, 0))
    return pl.pallas_call(
        body, name="rope_tables", grid=(nt,), out_shape=[jax.ShapeDtypeStruct((T, 128), F32)] * 3 + ex.out_shape,
        in_specs=[pl.BlockSpec((tm, 1), lambda i: (i, 0)), cst, cst, cst] + ex.specs, out_specs=[row] * 3 + ex.specs,
        scratch_shapes=ex.scratch,
        compiler_params=_cparams("arbitrary"),
    )(pos, jnp.asarray(invf), jnp.asarray(m_a), jnp.asarray(m_b), *ex.arrs)


def _rope(t, cos, sa, sb, sign=1.0):
    parts = []
    for i in range(t.shape[1] // 128):
        ti = t[:, 128 * i : 128 * (i + 1)]
        up = pltpu.roll(ti, 128 - ROPE_DIM // 2, 1)
        dn = pltpu.roll(ti, ROPE_DIM // 2, 1)
        parts.append(ti * cos + sign * (up * sa + dn * sb))
    return parts[0] if len(parts) == 1 else jnp.concatenate(parts, axis=-1)


def _in_proj(x2, ln_pre, w_in, cos, sa, sb, ex):
    T = x2.shape[0]
    tm = min(512, T)
    nt = T // tm
    chunk = D
    kv_cols = 4 * N_KV * KV_W

    def body(*refs):
        ins, ex_in, (z_ref, h_ref, qr_ref, kv_ref), ex_out, _, ex_sems = _host_split(refs, 6, 4, 0, ex)
        x_ref, g_ref, w_ref, cos_ref, sa_ref, sb_ref = ins
        i = pl.program_id(0)
        ex.carried(ex_in, ex_out, ex_sems, i == 0, i == nt - 1)
        xv = x_ref[...]
        r = lax.rsqrt(jnp.mean(xv * xv, axis=-1, keepdims=True) + EPS)
        h = (xv * r * g_ref[...]).astype(BF16)
        h_ref[...] = h
        tabs = cos_ref[...], sa_ref[...], sb_ref[...]
        for c0 in range(0, NW, chunk):
            cw = min(chunk, NW - c0)
            zc = _dot(h, w_ref[:, c0 : c0 + cw])
            z_ref[:, c0 : c0 + cw] = zc.astype(BF16)
            if c0 == CB_Q * D:
                qr_ref[...] = (_rope(zc, *tabs) * (HEAD_DIM ** -0.5)).astype(BF16)
            if c0 == CB_K * KV_W:
                low = lax.broadcasted_iota(jnp.int32, (tm, KV_W), 1) < HEAD_DIM
                for kind, t in ((0, _rope(zc[:, 0:KV_W], *tabs)), (1, zc[:, KV_W:])):
                    swapped = pltpu.roll(t, HEAD_DIM, 1)
                    cols = {(0, 0): jnp.where(low, t, 0.0), (0, 1): jnp.where(low, 0.0, swapped),
                            (1, 0): jnp.where(low, swapped, 0.0), (1, 1): jnp.where(low, 0.0, t)}
                    for (g, par), val in cols.items():
                        c = _kv2_col(kind, g, par)
                        kv_ref[:, c * KV_W : (c + 1) * KV_W] = val.astype(BF16)

    def row(w):
        return pl.BlockSpec((tm, w), lambda i: (i, 0))

    bf = lambda w: jax.ShapeDtypeStruct((T, w), BF16)
    return pl.pallas_call(
        body, name="in_proj", grid=(nt,),
        out_shape=[bf(NW), bf(D), bf(D), bf(kv_cols)] + ex.out_shape,
        in_specs=[row(D), _weight_spec((1, D)), _weight_spec((D, NW)), row(128), row(128), row(128)] + ex.specs,
        out_specs=[row(NW), row(D), row(D), row(kv_cols)] + ex.specs,
        scratch_shapes=ex.scratch,
        compiler_params=_cparams("arbitrary"),
    )(x2, ln_pre, w_in, cos, sa, sb, *ex.arrs)


def _conv_tiles(S):
    tm = min(ROW_TILE, S)
    return tm, S // tm, tm // CONV_HALO


CONV_ROWS_FWD = 32
CONV_ROWS = 16


def _fill_shifted(sh, rows):
    for b in range(1, 8):
        sh[b, 0:rows, :] = sh[0, b : b + rows, :]


def _conv_fwd(z3, w_dw, b_dw, ln_g, ln_b):
    NS, S, _ = z3.shape
    tm, nt, r = _conv_tiles(S)

    def body(val_ref, glu_ref, hval_ref, hglu_ref, w_ref, b_ref, g_ref, bb_ref, c_ref, cs_ref, ush, cbuf):
        i = pl.program_id(1)
        ush[0, CONV_HALO:, :] = val_ref[...].astype(F32) * _sig(glu_ref[...].astype(F32))
        uh = hval_ref[...].astype(F32) * _sig(hglu_ref[...].astype(F32))
        ush[0, 0:CONV_HALO, :] = jnp.where(i > 0, uh, 0.0)
        _fill_shifted(ush, tm + CONV_HALO - 8)
        for r0 in range(0, tm, CONV_ROWS_FWD):
            acc = jnp.zeros((CONV_ROWS_FWD, D), F32)
            for k in range(CONV_K):
                a, b = divmod(CONV_HALO - (CONV_K - 1) + k, 8)
                acc = acc + w_ref[k : k + 1, :] * ush[b, r0 + 8 * a : r0 + 8 * a + CONV_ROWS_FWD, :]
            cbuf[r0 : r0 + CONV_ROWS_FWD, :] = acc + b_ref[...]
        cv = cbuf[...]
        mu = jnp.mean(cv, axis=-1, keepdims=True)
        xc = cv - mu
        var = jnp.mean(xc * xc, axis=-1, keepdims=True)
        cl = xc * lax.rsqrt(var + EPS) * g_ref[...] + bb_ref[...]
        c_ref[...] = cv.astype(BF16)
        cs_ref[...] = (cl * _sig(cl)).astype(BF16)

    def cur(cb):
        return pl.BlockSpec((None, tm, D), lambda s, i: (s, i, cb))

    def halo(cb):
        return pl.BlockSpec((None, CONV_HALO, D), lambda s, i: (s, jnp.maximum(i * r - 1, 0), cb))

    vec = pl.BlockSpec((1, D), lambda s, i: (0, 0))
    out = pl.BlockSpec((None, tm, D), lambda s, i: (s, i, 0))
    return pl.pallas_call(
        body, name="conv_fwd", grid=(NS, nt),
        out_shape=[jax.ShapeDtypeStruct((NS, S, D), BF16)] * 2,
        in_specs=[cur(CB_VAL), cur(CB_GLU), halo(CB_VAL), halo(CB_GLU),
                  pl.BlockSpec((CONV_HALO, D), lambda s, i: (0, 0)), vec, vec, vec],
        out_specs=[out, out],
        scratch_shapes=[pltpu.VMEM((8, tm + CONV_HALO, D), F32), pltpu.VMEM((tm, D), F32)],
        compiler_params=_cparams("parallel", "parallel"),
    )(z3, z3, z3, z3, w_dw, b_dw, ln_g, ln_b)


PAIRS = GROUP // 2
QROWS = PAIRS * BLK


def _kv2_col(kind, g, par):
    return kind * 2 * N_KV + g * 2 + par


def _attn_mask(has_prev):
    qi = lax.broadcasted_iota(jnp.int32, (QROWS, 2 * BLK), 0) & (BLK - 1)
    kj = lax.broadcasted_iota(jnp.int32, (QROWS, 2 * BLK), 1)
    first_key = jnp.where(has_prev, 0, BLK)
    return (kj > qi) & (kj <= qi + BLK) & (kj >= first_key)


NEG_BIG = -1e30


def _attn_specs():
    q = pl.BlockSpec((None, BLK, D), lambda s, n: (s, n, 0))
    kv_cur = pl.BlockSpec((None, BLK, 4 * N_KV * KV_W), lambda s, n: (s, n, 0))
    kv_prev = pl.BlockSpec((None, BLK, 4 * N_KV * KV_W), lambda s, n: (s, jnp.maximum(n - 1, 0), 0))
    sink = pl.BlockSpec(memory_space=pltpu.SMEM)
    return sink, q, kv_cur, kv_prev


def _stack_pairs(ref, g):
    return jnp.concatenate([ref[:, (PAIRS * g + j) * 128 : (PAIRS * g + j + 1) * 128] for j in range(PAIRS)], axis=0)


def _unstack_pairs(ref, g, val):
    for j in range(PAIRS):
        ref[:, (PAIRS * g + j) * 128 : (PAIRS * g + j + 1) * 128] = val[j * BLK : (j + 1) * BLK].astype(ref.dtype)


def _pair_heads(g, par):
    return [GROUP * g + 2 * j + par for j in range(PAIRS)]


def _head_col_load(ref, g, par):
    return jnp.concatenate([ref[:, h : h + 1] for h in _pair_heads(g, par)], axis=0)


def _head_col_store(ref, g, par, col):
    for j, h in enumerate(_pair_heads(g, par)):
        ref[:, h : h + 1] = col[j * BLK : (j + 1) * BLK]


def _sink_col(sink_ref, g, par):
    return jnp.concatenate([jnp.full((BLK, 1), sink_ref[h], F32) for h in _pair_heads(g, par)], axis=0)


def _kv2_block(kvc_ref, kvp_ref, has_prev, c):
    col = slice(c * KV_W, (c + 1) * KV_W)
    prev = jnp.where(has_prev, kvp_ref[:, col], jnp.zeros((BLK, KV_W), BF16))
    return jnp.concatenate([prev, kvc_ref[:, col]], axis=0)


def _attn_fwd(qr3, kv3, sinks):
    NS, S, _ = qr3.shape

    def body(sink_ref, q_ref, kvc_ref, kvp_ref, o_ref, lse_ref):
        has_prev = pl.program_id(1) > 0
        mask = _attn_mask(has_prev)[0:BLK]
        for g in range(N_KV):
            kv = [[_kv2_block(kvc_ref, kvp_ref, has_prev, _kv2_col(kind, g, par)) for par in range(2)] for kind in range(2)]
            for j in range(PAIRS):
                cols = slice((PAIRS * g + j) * 128, (PAIRS * g + j + 1) * 128)
                q2 = q_ref[:, cols]
                o_pair = None
                for par in range(2):
                    h = GROUP * g + 2 * j + par
                    s = jnp.where(mask, _dot_nt(q2, kv[0][par]), NEG_BIG)
                    sk = sink_ref[h]
                    mx = jnp.maximum(jnp.max(s, axis=-1, keepdims=True), sk)
                    e = jnp.exp(s - mx)
                    den = jnp.sum(e, axis=-1, keepdims=True) + jnp.exp(sk - mx)
                    pv = _dot(e.astype(BF16), kv[1][par]) * (1.0 / den)
                    o_pair = pv if o_pair is None else o_pair + pv
                    lse_ref[:, h : h + 1] = mx + jnp.log(den)
                o_ref[:, cols] = o_pair.astype(BF16)

    return pl.pallas_call(
        body, name="attn_fwd", grid=(NS, S // BLK),
        out_shape=[jax.ShapeDtypeStruct((NS, S, D), BF16), jax.ShapeDtypeStruct((NS, S, N_HEADS), F32)],
        in_specs=list(_attn_specs()),
        out_specs=[pl.BlockSpec((None, BLK, D), lambda s, n: (s, n, 0)),
                   pl.BlockSpec((None, BLK, N_HEADS), lambda s, n: (s, n, 0))],
        compiler_params=_cparams("parallel", "parallel"),
    )(sinks, qr3, kv3, kv3)


def _weight_spec(shape):
    return pl.BlockSpec(shape, lambda i: (0,) * len(shape), pipeline_mode=pl.Buffered(1))


def _dsilu(v, s):
    return s * (1.0 + v * (1.0 - s))


MID_TILE = 256


def _mid(cs, o, z2, x2, p2, tgt, ln_post, weights):
    T = cs.shape[0]
    tm = min(MID_TILE, T)
    nt = T // tm
    n_bf = 16

    def body(cs_ref, o_ref, cg_ref, ag_ref, gc_ref, ga_ref, x_ref, p_ref, t_ref, g_ref,
             wpw, wbrc, wbra, wout, wpg, wpp,
             dya0_ref, ya1_ref, dya_ref, yb0_ref, dyb_ref, m_ref, dmo_ref, x1_ref, dgl_ref, dpp_ref,
             dcs_ref, do_ref, dcg_ref, dag_ref, dgc_ref, dga_ref, dx1_ref, loss_ref, glp_ref, lacc):
        i = pl.program_id(0)

        @pl.when(i == 0)
        def _():
            lacc[...] = jnp.zeros_like(lacc)
            glp_ref[...] = jnp.zeros_like(glp_ref)

        cg = cg_ref[...].astype(F32)
        scg = _sig(cg)
        silu_c = cg * scg
        ya0 = _dot(cs_ref[...], wpw[...])
        ya1 = (ya0 * silu_c).astype(BF16)
        ya1_ref[...] = ya1
        ya = _dot(ya1, wbrc[...])
        ag = ag_ref[...].astype(F32)
        sag = _sig(ag)
        silu_a = ag * sag
        ov = o_ref[...].astype(F32)
        yb0 = (ov * silu_a).astype(BF16)
        yb0_ref[...] = yb0
        yb = _dot(yb0, wbra[...])
        sgc = _sig(gc_ref[...].astype(F32))
        sga = _sig(ga_ref[...].astype(F32))
        mb = (sgc * ya + sga * yb).astype(BF16)
        m_ref[...] = mb
        mo = _dot(mb, wout[...])
        r2 = lax.rsqrt(jnp.mean(mo * mo, axis=-1, keepdims=True) + EPS)
        nrm = mo * r2
        x1 = x_ref[...] + nrm * g_ref[...]
        x1b = x1.astype(BF16)
        x1_ref[...] = x1b
        gate = _sig(_dot(x1b, wpg[...]))
        pp = _dot(p_ref[...].astype(BF16), wpp[...])
        e = x1 + gate * pp - t_ref[...]
        lacc[...] += _rowsum8(e * e)
        dy = e * (1.0 / D)

        dgl = (dy * pp * gate * (1.0 - gate)).astype(BF16)
        dgl_ref[...] = dgl
        dpp_ref[...] = (dy * gate).astype(BF16)
        dx1 = dy + _dot_nt(dgl, wpg[...])
        dx1_ref[...] = dx1
        glp_ref[...] += _rowsum8(dx1 * nrm)
        dn = dx1 * g_ref[...]
        dmo = (r2 * (dn - nrm * jnp.mean(dn * nrm, axis=-1, keepdims=True))).astype(BF16)
        dmo_ref[...] = dmo
        dm = _dot_nt(dmo, wout[...])
        dya = (dm * sgc).astype(BF16)
        dyb = (dm * sga).astype(BF16)
        dya_ref[...] = dya
        dyb_ref[...] = dyb
        dgc_ref[...] = (dm * ya * sgc * (1.0 - sgc)).astype(BF16)
        dga_ref[...] = (dm * yb * sga * (1.0 - sga)).astype(BF16)
        dya1 = _dot_nt(dya, wbrc[...])
        dya0 = (dya1 * silu_c).astype(BF16)
        dya0_ref[...] = dya0
        dcg_ref[...] = (dya1 * ya0 * _dsilu(cg, scg)).astype(BF16)
        dcs_ref[...] = _dot_nt(dya0, wpw[...]).astype(BF16)
        dyb0 = _dot_nt(dyb, wbra[...])
        do_ref[...] = (dyb0 * silu_a).astype(BF16)
        dag_ref[...] = (dyb0 * ov * _dsilu(ag, sag)).astype(BF16)

        @pl.when(i == nt - 1)
        def _():
            loss_ref[...] = jnp.full(loss_ref.shape, jnp.sum(lacc[...]) * (0.5 / D), F32)

    row = pl.BlockSpec((tm, D), lambda i: (i, 0))

    def zcol(cb):
        return pl.BlockSpec((tm, D), lambda i: (i, cb))

    bf = jax.ShapeDtypeStruct((T, D), BF16)
    return pl.pallas_call(
        body, name="mid_fwd_bwd", grid=(nt,),
        out_shape=[bf] * n_bf + [jax.ShapeDtypeStruct((T, D), F32), jax.ShapeDtypeStruct((8, 128), F32),
                                 jax.ShapeDtypeStruct((8, D), F32)],
        in_specs=[row, row, zcol(CB_CGATE), zcol(CB_AGATE), zcol(CB_GCONV), zcol(CB_GATTN), row,
                  pl.BlockSpec((tm, PLE), lambda i: (i, 0)), row, _weight_spec((1, D))]
        + [_weight_spec((D, D))] * 5 + [_weight_spec((PLE, D))],
        out_specs=[row] * (n_bf + 1) + [pl.BlockSpec((8, 128), lambda i: (0, 0)), pl.BlockSpec((8, D), lambda i: (0, 0))],
        scratch_shapes=[pltpu.VMEM((8, D), F32)],
        compiler_params=_cparams("arbitrary"),
    )(cs, o, z2, z2, z2, z2, x2, p2, tgt, ln_post, *weights)


def _conv_bwd(z3, c3, dcs3, w_dw, ln_g, ln_b, ex):
    NS, S, _ = z3.shape
    tm, nt, r = _conv_tiles(S)

    def body(*refs):
        ins, ex_in, outs, ex_out, scratch, ex_sems = _host_split(refs, 9, 3, 3, ex)
        val_ref, glu_ref, c_ref, dcs_ref, hc_ref, hdcs_ref, w_ref, g_ref, bb_ref = ins
        dz_ref, gw_ref, gvec_ref = outs
        dsh, ubuf, dubuf = scratch
        i = pl.program_id(1)
        first = (pl.program_id(0) == 0) & (i == 0)
        ex.carried(ex_in, ex_out, ex_sems, first, (pl.program_id(0) == NS - 1) & (i == nt - 1))

        @pl.when(first)
        def _():
            gw_ref[...] = jnp.zeros_like(gw_ref)
            gvec_ref[...] = jnp.zeros_like(gvec_ref)

        val = val_ref[...].astype(F32)
        sg = _sig(glu_ref[...].astype(F32))
        ubuf[...] = val * sg

        def ln_bwd(cv, dcs):
            cv = cv.astype(F32)
            mu = jnp.mean(cv, axis=-1, keepdims=True)
            xc = cv - mu
            rstd = lax.rsqrt(jnp.mean(xc * xc, axis=-1, keepdims=True) + EPS)
            xhat = xc * rstd
            cl = xhat * g_ref[...] + bb_ref[...]
            s = _sig(cl)
            dcl = dcs.astype(F32) * _dsilu(cl, s)
            dxh = dcl * g_ref[...]
            dc = rstd * (dxh - jnp.mean(dxh, axis=-1, keepdims=True) - xhat * jnp.mean(dxh * xhat, axis=-1, keepdims=True))
            return dc, dcl, xhat

        dc, dcl, xhat = ln_bwd(c_ref[...], dcs_ref[...])
        dsh[0, 0:tm, :] = dc
        dch, _, _ = ln_bwd(hc_ref[...], hdcs_ref[...])
        dsh[0, tm:, :] = jnp.where(i < nt - 1, dch, 0.0)
        gvec_ref[0:8, :] += _rowsum8(dcl * xhat)
        gvec_ref[8:16, :] += _rowsum8(dcl)
        gvec_ref[16:24, :] += _rowsum8(dc)
        _fill_shifted(dsh, tm + CONV_HALO - 8)

        def dc_ahead(r0, k):
            a, b = divmod(CONV_K - 1 - k, 8)
            return dsh[b, r0 + 8 * a : r0 + 8 * a + CONV_ROWS, :]

        for r0 in range(0, tm, CONV_ROWS):
            acc = jnp.zeros((CONV_ROWS, D), F32)
            for k in range(CONV_K):
                acc = acc + w_ref[k : k + 1, :] * dc_ahead(r0, k)
            dubuf[r0 : r0 + CONV_ROWS, :] = acc
        for r0 in range(0, tm, CONV_ROWS):
            ur = ubuf[r0 : r0 + CONV_ROWS, :]
            for k in range(CONV_K):
                gw_ref[8 * k : 8 * k + 8, :] += _rowsum8(ur * dc_ahead(r0, k))
        du = dubuf[...]
        dz_ref[:, 0:D] = (du * sg).astype(BF16)
        dz_ref[:, D:] = (du * val * sg * (1.0 - sg)).astype(BF16)

    def cur(cb):
        return pl.BlockSpec((None, tm, D), lambda s, i: (s, i, cb))

    nxt = pl.BlockSpec((None, CONV_HALO, D), lambda s, i: (s, jnp.minimum((i + 1) * r, S // CONV_HALO - 1), 0))
    vec = pl.BlockSpec((1, D), lambda s, i: (0, 0))
    return pl.pallas_call(
        body, name="conv_bwd", grid=(NS, nt),
        out_shape=[jax.ShapeDtypeStruct((NS, S, 2 * D), BF16), jax.ShapeDtypeStruct((CONV_HALO * 8, D), F32),
                   jax.ShapeDtypeStruct((24, D), F32)] + ex.out_shape,
        in_specs=[cur(CB_VAL), cur(CB_GLU), cur(0), cur(0), nxt, nxt,
                  pl.BlockSpec((CONV_HALO, D), lambda s, i: (0, 0)), vec, vec] + ex.specs,
        out_specs=[pl.BlockSpec((None, tm, 2 * D), lambda s, i: (s, i, 0)),
                   pl.BlockSpec((CONV_HALO * 8, D), lambda s, i: (0, 0)), pl.BlockSpec((24, D), lambda s, i: (0, 0))] + ex.specs,
        scratch_shapes=[pltpu.VMEM((8, tm + CONV_HALO, D), F32), pltpu.VMEM((tm, D), F32), pltpu.VMEM((tm, D), F32)] + ex.scratch,
        compiler_params=_cparams("arbitrary", "arbitrary"),
    )(z3, z3, c3, dcs3, c3, dcs3, w_dw, ln_g, ln_b, *ex.arrs)


def _attn_bwd(qr3, kv3, do3, o3, lse3, sinks, cos3, sa3, sb3):
    NS, S, _ = qr3.shape

    def body(sink_ref, q_ref, kvc_ref, kvp_ref, do_ref, o_ref, lse_ref, cos_ref, sa_ref, sb_ref,
             dq_ref, dkc_ref, dkp_ref, dvc_ref, dvp_ref, dsk_ref):
        has_prev = pl.program_id(1) > 0
        mask = _attn_mask(has_prev)
        tabs = cos_ref[...], sa_ref[...], sb_ref[...]
        low_q = lax.broadcasted_iota(jnp.int32, (QROWS, 128), 1) < HEAD_DIM
        dk_g, dv_g = [], []
        for g in range(N_KV):
            qs = _stack_pairs(q_ref, g)
            dos = _stack_pairs(do_ref, g)
            prod = dos.astype(F32) * _stack_pairs(o_ref, g).astype(F32)
            deltas = [jnp.sum(jnp.where(low_q, prod, 0.0), axis=-1, keepdims=True),
                      jnp.sum(jnp.where(low_q, 0.0, prod), axis=-1, keepdims=True)]
            dq_acc, dk_par, dv_par = None, [], []
            for par in range(2):
                k2 = _kv2_block(kvc_ref, kvp_ref, has_prev, _kv2_col(0, g, par))
                v2 = _kv2_block(kvc_ref, kvp_ref, has_prev, _kv2_col(1, g, par))
                lse = _head_col_load(lse_ref, g, par)
                p = jnp.exp(jnp.where(mask, _dot_nt(qs, k2), NEG_BIG) - lse)
                ds = (p * (_dot_nt(dos, v2) - deltas[par])).astype(BF16)
                dq = _dot(ds, k2)
                dq_acc = dq if dq_acc is None else dq_acc + dq
                dk_par.append(_dot_tn(ds, qs))
                dv_par.append(_dot_tn(p.astype(BF16), dos))
                _head_col_store(dsk_ref, g, par, -jnp.exp(_sink_col(sink_ref, g, par) - lse) * deltas[par])
            for j in range(PAIRS):
                dq_pair = _rope(dq_acc[j * BLK : (j + 1) * BLK], *tabs, sign=-1.0) * (HEAD_DIM ** -0.5)
                dq_ref[:, (PAIRS * g + j) * 128 : (PAIRS * g + j + 1) * 128] = dq_pair.astype(BF16)
            dk_g.append(dk_par[0] + pltpu.roll(dk_par[1], HEAD_DIM, 1))
            dv_g.append(dv_par[0] + pltpu.roll(dv_par[1], HEAD_DIM, 1))
        low_k = lax.broadcasted_iota(jnp.int32, (2 * BLK, KV_W), 1) < HEAD_DIM
        dk = jnp.where(low_k, dk_g[0], pltpu.roll(dk_g[1], HEAD_DIM, 1))
        dv = jnp.where(low_k, dv_g[0], pltpu.roll(dv_g[1], HEAD_DIM, 1))
        dkp_ref[...] = dk[0:BLK]
        dkc_ref[...] = dk[BLK:]
        dvp_ref[...] = dv[0:BLK]
        dvc_ref[...] = dv[BLK:]

    qspec = pl.BlockSpec((None, BLK, D), lambda s, n: (s, n, 0))
    kvspec = pl.BlockSpec((None, BLK, KV_W), lambda s, n: (s, n, 0))
    hspec = pl.BlockSpec((None, BLK, N_HEADS), lambda s, n: (s, n, 0))
    kv = jax.ShapeDtypeStruct((NS, S, KV_W), F32)
    return pl.pallas_call(
        body, name="attn_bwd", grid=(NS, S // BLK),
        out_shape=[jax.ShapeDtypeStruct((NS, S, D), BF16), kv, kv, kv, kv, jax.ShapeDtypeStruct((NS, S, N_HEADS), F32)],
        in_specs=list(_attn_specs()) + [qspec, qspec, hspec, kvspec, kvspec, kvspec],
        out_specs=[qspec, kvspec, kvspec, kvspec, kvspec, hspec],
        compiler_params=_cparams("parallel", "parallel"),
    )(sinks, qr3, kv3, kv3, do3, o3, lse3, cos3, sa3, sb3)


def _attn_post(dkc3, dkp3, dvc3, dvp3, cos3, sa3, sb3):
    NS, S, _ = dkc3.shape
    tm = min(4 * BLK, S)
    nt = S // tm

    def body(dkc_ref, dkp_ref, dkn_ref, dvc_ref, dvp_ref, dvn_ref, cos_ref, sa_ref, sb_ref, dkv_ref):
        has_next = pl.program_id(1) < nt - 1

        def join(cur_ref, prev_ref, next_ref):
            ahead = jnp.where(has_next, next_ref[...], 0.0)
            shifted = ahead if tm == BLK else jnp.concatenate([prev_ref[BLK:, :], ahead], axis=0)
            return cur_ref[...] + shifted

        tabs = cos_ref[...], sa_ref[...], sb_ref[...]
        dkv_ref[:, 0:KV_W] = _rope(join(dkc_ref, dkp_ref, dkn_ref), *tabs, sign=-1.0).astype(BF16)
        dkv_ref[:, KV_W:] = join(dvc_ref, dvp_ref, dvn_ref).astype(BF16)

    cur = pl.BlockSpec((None, tm, KV_W), lambda s, j: (s, j, 0))
    nxt = pl.BlockSpec((None, BLK, KV_W), lambda s, j: (s, jnp.minimum((j + 1) * (tm // BLK), S // BLK - 1), 0))
    return pl.pallas_call(
        body, name="attn_post", grid=(NS, nt),
        out_shape=jax.ShapeDtypeStruct((NS, S, 2 * KV_W), BF16),
        in_specs=[cur, cur, nxt, cur, cur, nxt, cur, cur, cur],
        out_specs=pl.BlockSpec((None, tm, 2 * KV_W), lambda s, j: (s, j, 0)),
        compiler_params=_cparams("parallel", "parallel"),
    )(dkc3, dkp3, dkp3, dvc3, dvp3, dvp3, cos3, sa3, sb3)


def _in_bwd(segs, w_in, x2, dx1, ln_pre, after):
    T = x2.shape[0]
    tm = min(ROW_TILE, T)
    nt = T // tm
    ns = len(segs)
    widths = [s.shape[1] for s in segs]

    def body(*refs):
        seg_refs = refs[:ns]
        w_ref, x_ref, dx1_ref, g_ref, _, gx_ref, glp_ref = refs[ns:]
        i = pl.program_id(0)

        @pl.when(i == 0)
        def _():
            glp_ref[...] = jnp.zeros_like(glp_ref)

        dh = None
        r0 = 0
        for sref, w in zip(seg_refs, widths):
            part = _dot_nt(sref[...], w_ref[:, r0 : r0 + w])
            dh = part if dh is None else dh + part
            r0 += w
        xv = x_ref[...]
        r1 = lax.rsqrt(jnp.mean(xv * xv, axis=-1, keepdims=True) + EPS)
        xhat = xv * r1
        glp_ref[...] += _rowsum8(dh * xhat)
        dhg = dh * g_ref[...]
        gx_ref[...] = dx1_ref[...] + r1 * (dhg - xhat * jnp.mean(dhg * xhat, axis=-1, keepdims=True))

    row = pl.BlockSpec((tm, D), lambda i: (i, 0))
    return pl.pallas_call(
        body, name="in_bwd", grid=(nt,),
        out_shape=[jax.ShapeDtypeStruct((T, D), F32), jax.ShapeDtypeStruct((8, D), F32)],
        in_specs=[pl.BlockSpec((tm, w), lambda i: (i, 0)) for w in widths]
        + [_weight_spec((D, NW)), row, row, _weight_spec((1, D)), pl.BlockSpec(memory_space=pl.ANY)],
        out_specs=[row, pl.BlockSpec((8, D), lambda i: (0, 0))],
        compiler_params=_cparams("arbitrary"),
    )(*segs, w_in, x2, dx1, ln_pre, after)


def _grad_matmul(a, b, name):
    T, M = a.shape
    N = b.shape[1]
    tk = min(1024, T)
    nk = T // tk

    def body(a_ref, b_ref, o_ref, acc):
        k = pl.program_id(0)

        @pl.when(k == 0)
        def _():
            acc[...] = jnp.zeros_like(acc)

        acc[...] += _dot_tn(a_ref[...].astype(BF16), b_ref[...])

        @pl.when(k == nk - 1)
        def _():
            o_ref[...] = acc[...].astype(BF16)

    return pl.pallas_call(
        body, name=name, grid=(nk,), out_shape=jax.ShapeDtypeStruct((M, N), BF16),
        in_specs=[pl.BlockSpec((tk, M), lambda k: (k, 0)), pl.BlockSpec((tk, N), lambda k: (k, 0))],
        out_specs=pl.BlockSpec((M, N), lambda k: (0, 0)),
        scratch_shapes=[pltpu.VMEM((M, N), F32)],
        compiler_params=_cparams("arbitrary"),
    )(a, b)


def _pack_small(gw, gvec, glp_pre, glp_post, dsk):
    T = dsk.shape[0]

    def body(gw_ref, gvec_ref, pre_ref, post_ref, dsk_ref, gdw_ref, gs_ref):
        gwf = gw_ref[...].reshape(CONV_HALO, 8, D).sum(axis=1)
        for d in range(N_DEV):
            gdw_ref[d] = gwf[:, 128 * d : 128 * (d + 1)]
        gs_ref[...] = jnp.zeros_like(gs_ref)
        gs_ref[0:1, :] = jnp.sum(pre_ref[...], axis=0, keepdims=True)
        gs_ref[1:2, :] = jnp.sum(post_ref[...], axis=0, keepdims=True)
        gs_ref[2:3, :] = jnp.sum(gvec_ref[16:24, :], axis=0, keepdims=True)
        gs_ref[3:4, :] = jnp.sum(gvec_ref[0:8, :], axis=0, keepdims=True)
        gs_ref[4:5, :] = jnp.sum(gvec_ref[8:16, :], axis=0, keepdims=True)
        gs_ref[5:6, 0:N_HEADS] = jnp.sum(dsk_ref[...], axis=0, keepdims=True)

    return pl.pallas_call(
        body, name="pack_small",
        out_shape=[jax.ShapeDtypeStruct((N_DEV, CONV_HALO, 128), F32), jax.ShapeDtypeStruct((8, D), F32)],
        compiler_params=_cparams(),
    )(gw, gvec, glp_pre, glp_post, dsk)


def _adam_update(g, w_ref, m_ref, v_ref, g_ref, d_ref, nm_ref, nv_ref):
    nm = ADAM_B1 * m_ref[...] + (1.0 - ADAM_B1) * g
    nv = ADAM_B2 * v_ref[...] + (1.0 - ADAM_B2) * (g * g)
    m_hat = nm / (1.0 - ADAM_B1 ** ADAM_STEP)
    v_hat = nv / (1.0 - ADAM_B2 ** ADAM_STEP)
    g_ref[...] = g
    d_ref[...] = -ADAM_LR * (m_hat / (jnp.sqrt(v_hat) + ADAM_EPS) + ADAM_WD * w_ref[...])
    nm_ref[...] = nm
    nv_ref[...] = nv


def _adamw(parts, w, m, v, name):
    R, C = w.shape
    tr = R if R <= 256 else 128

    def body(p_ref, *rest):
        g = p_ref[0].astype(F32)
        for s in range(1, N_DEV):
            g = g + p_ref[s].astype(F32)
        _adam_update(g, *rest)

    blk = pl.BlockSpec((tr, C), lambda i: (i, 0))
    return pl.pallas_call(
        body, name=name, grid=(R // tr,), out_shape=[jax.ShapeDtypeStruct((R, C), F32)] * 4,
        in_specs=[pl.BlockSpec((N_DEV, tr, C), lambda i: (0, i, 0)), blk, blk, blk], out_specs=[blk] * 4,
        compiler_params=_cparams("parallel"),
    )(parts, w, m, v)


def _adamw_own(parts, own, me, w, m, v, name):
    R, C = w.shape
    tr = 128

    def body(me_ref, p_ref, own_ref, *rest):
        g = None
        for s in range(N_DEV):
            part = jnp.where(me_ref[0] == s, own_ref[...], p_ref[s]).astype(F32)
            g = part if g is None else g + part
        _adam_update(g, *rest)

    blk = pl.BlockSpec((tr, C), lambda i, me: (i, 0))
    return pl.pallas_call(
        body, name=name, out_shape=[jax.ShapeDtypeStruct((R, C), F32)] * 4,
        grid_spec=pltpu.PrefetchScalarGridSpec(
            num_scalar_prefetch=1, grid=(R // tr,),
            in_specs=[pl.BlockSpec((N_DEV, tr, C), lambda i, me: (0, i, 0)),
                      pl.BlockSpec((None, tr, C), lambda i, me: (me[0], i, 0)), blk, blk, blk],
            out_specs=[blk] * 4),
        compiler_params=_cparams("parallel"),
    )(me, parts, own, w, m, v)


def kernel(x, p, positions, w_in, ln_pre, ln_post, w_dw, b_dw, conv_ln_g, conv_ln_b, w_pw, sinks, w_br_conv, w_br_attn, w_out, w_ple_gate, w_ple_proj, loss_target, m_w_in, m_ln_pre, m_ln_post, m_w_dw, m_b_dw, m_conv_ln_g, m_conv_ln_b, m_w_pw, m_sinks, m_w_br_conv, m_w_br_attn, m_w_out, m_w_ple_gate, m_w_ple_proj, v_w_in, v_ln_pre, v_ln_post, v_w_dw, v_b_dw, v_conv_ln_g, v_conv_ln_b, v_w_pw, v_sinks, v_w_br_conv, v_w_br_attn, v_w_out, v_w_ple_gate, v_w_ple_proj):
    NS, S, _ = x.shape
    T = NS * S
    x2 = x.reshape(T, D)
    p2 = p.reshape(T, PLE)
    tgt = loss_target.reshape(T, D)
    pos = positions.reshape(T, 1)

    row_sharded = [w_pw[0], w_br_conv[0], w_br_attn[0], w_out[0], w_ple_gate[0]]
    sh_rows = D // N_DEV
    cos, sa, sb, g_in = _rope_tables(pos, _TwoLevelGather([w_in[0].astype(BF16)]))
    w_in_f = _to_perm(g_in.transpose(1, 0, 2).reshape(D, NW))
    gather_rest = _Exchange([], [w.astype(BF16) for w in row_sharded] + [
        w_ple_proj[0].astype(BF16), jnp.pad(w_dw[0], ((0, CONV_HALO - CONV_K), (0, 0)))])

    in_out = _in_proj(x2, ln_pre, w_in_f, cos, sa, sb, gather_rest)
    z2, h, qr, kv2 = in_out[:4]
    g_rows, g_pp, g_dw = in_out[4:9], in_out[9], in_out[10]
    full = [g.reshape(D, D) for g in g_rows]
    w_pp_f = g_pp.transpose(1, 0, 2).reshape(PLE, D)
    w_dw_f = g_dw.transpose(1, 0, 2).reshape(CONV_HALO, D)
    z3 = z2.reshape(NS, S, NW)
    c3, cs3 = _conv_fwd(z3, w_dw_f, b_dw, conv_ln_g, conv_ln_b)
    qr3, kv3 = qr.reshape(NS, S, D), kv2.reshape(NS, S, 4 * N_KV * KV_W)
    sinks1 = sinks.reshape(N_HEADS)
    o3, lse3 = _attn_fwd(qr3, kv3, sinks1)
    o = o3.reshape(T, D)
    cs = cs3.reshape(T, D)
    (dya0, ya1, dya, yb0, dyb, m, dmo, x1, dgl, dpp, dcs, do, dcg, dag, dgc, dga, dx1, loss_blk, glp_post) = _mid(
        cs, o, z2, x2, p2, tgt, ln_post, full + [w_pp_f])

    gp_rows = [_grad_matmul(a, b, nm).reshape(N_DEV, sh_rows, D) for a, b, nm in (
        (cs, dya0, "grad_w_pw"), (ya1, dya, "grad_w_br_conv"), (yb0, dyb, "grad_w_br_attn"),
        (m, dmo, "grad_w_out"), (x1, dgl, "grad_w_ple_gate"))]
    gp_pp = _grad_matmul(p2, dpp, "grad_w_ple_proj").reshape(PLE, N_DEV, D // N_DEV).transpose(1, 0, 2)
    conv_out = _conv_bwd(z3, c3, dcs.reshape(NS, S, D), w_dw_f, conv_ln_g, conv_ln_b,
                         _Exchange(gp_rows + [gp_pp], [loss_blk]))
    dzvu3, gw, gvec, r_rows, r_pp, r_loss = conv_out[0], conv_out[1], conv_out[2], conv_out[3:8], conv_out[8], conv_out[9]
    loss = jnp.sum(r_loss[:, 0, 0])
    tab3 = [t.reshape(NS, S, 128) for t in (cos, sa, sb)]
    dq3, dkc3, dkp3, dvc3, dvp3, dsk3 = _attn_bwd(qr3, kv3, do.reshape(NS, S, D), o3, lse3, sinks1, *tab3)
    dkv3 = _attn_post(dkc3, dkp3, dvc3, dvp3, *tab3)
    segs = [dzvu3.reshape(T, 2 * D), dcg, dq3.reshape(T, D), dag, dgc, dga, dkv3.reshape(T, 2 * KV_W)]
    gp_in = _from_perm(jnp.concatenate([_grad_matmul(h, s, f"grad_w_in_{j}") for j, s in enumerate(segs)], axis=1))
    gp_in = gp_in.reshape(D, N_DEV, NW // N_DEV).transpose(1, 0, 2)
    in_send, in_recv, in_own, in_land, sent = _scatter_send(gp_in, "w_in_grad_send")
    grad_x2, glp_pre = _in_bwd(segs, w_in_f, x2, dx1, ln_pre, sent)
    gp_dw, gp_small = _pack_small(gw, gvec, glp_pre, glp_post, dsk3.reshape(T, N_HEADS))
    r_dw, r_small = _Exchange([gp_dw], [gp_small]).alone("small_grad_exchange")

    res = {}
    names_rows = ["w_pw", "w_br_conv", "w_br_attn", "w_out", "w_ple_gate"]
    wmv = {"w_pw": (w_pw, m_w_pw, v_w_pw), "w_br_conv": (w_br_conv, m_w_br_conv, v_w_br_conv),
           "w_br_attn": (w_br_attn, m_w_br_attn, v_w_br_attn), "w_out": (w_out, m_w_out, v_w_out),
           "w_ple_gate": (w_ple_gate, m_w_ple_gate, v_w_ple_gate)}
    for nm, parts in zip(names_rows, r_rows):
        w_, m_, v_ = wmv[nm]
        res[nm] = _adamw(parts, w_[0], m_[0], v_[0], "adamw_" + nm)
    res["w_ple_proj"] = _adamw(r_pp, w_ple_proj[0], m_w_ple_proj[0], v_w_ple_proj[0], "adamw_w_ple_proj")
    pad_dw = lambda a: jnp.pad(a[0], ((0, CONV_HALO - CONV_K), (0, 0)))
    res["w_dw"] = [a[:CONV_K] for a in _adamw(r_dw, pad_dw(w_dw), pad_dw(m_w_dw), pad_dw(v_w_dw), "adamw_w_dw")]

    def stack_small(a_pre, a_post, a_b, a_g, a_bb, a_s):
        sk = jnp.pad(a_s, ((0, 0), (0, D - N_HEADS)))
        return jnp.concatenate([a_pre, a_post, a_b, a_g, a_bb, sk, jnp.zeros((2, D), F32)], axis=0)

    small = _adamw(
        r_small, stack_small(ln_pre, ln_post, b_dw, conv_ln_g, conv_ln_b, sinks),
        stack_small(m_ln_pre, m_ln_post, m_b_dw, m_conv_ln_g, m_conv_ln_b, m_sinks),
        stack_small(v_ln_pre, v_ln_post, v_b_dw, v_conv_ln_g, v_conv_ln_b, v_sinks), "adamw_small")
    for j, nm in enumerate(["ln_pre", "ln_post", "b_dw", "conv_ln_g", "conv_ln_b"]):
        res[nm] = [a[j] for a in small]
    res["sinks"] = [a[5, :N_HEADS] for a in small]
    in_own, in_land = _scatter_wait(in_send, in_recv, in_own, in_land, small[0], "w_in_grad_wait")
    me = _slot(*_my_place()).astype(jnp.int32).reshape(1)
    res["w_in"] = _adamw_own(in_land, in_own, me, w_in[0], m_w_in[0], v_w_in[0], "adamw_w_in")

    order = ["w_in", "ln_pre", "ln_post", "w_dw", "b_dw", "conv_ln_g", "conv_ln_b", "w_pw", "sinks", "w_br_conv",
             "w_br_attn", "w_out", "w_ple_gate", "w_ple_proj"]
    outs = [loss, grad_x2.reshape(NS, S, D)]
    for kind in range(4):
        outs += [res[nm][kind][None] for nm in order]
    return tuple(outs)
```

```python
import functools

import numpy as np

import jax
import jax.numpy as jnp
from jax import lax
from jax.experimental import pallas as pl
from jax.experimental.pallas import tpu as pltpu

F32 = jnp.float32
BF16 = jnp.bfloat16

D = 1024
N_HEADS = 16
N_KV = 2
HEAD_DIM = 64
GROUP = N_HEADS // N_KV
KV_W = N_KV * HEAD_DIM
CONV_K = 31
CONV_HALO = 32
BLK = 128
ROPE_DIM = 16
ROPE_THETA = 500000.0
EPS = 1e-6
PLE = 256
NW = 7 * D + 2 * KV_W
N_DEV = 8

CB_VAL, CB_GLU, CB_CGATE, CB_Q, CB_AGATE, CB_GCONV, CB_GATTN = range(7)
CB_K = 7 * D // KV_W

ADAM_LR, ADAM_B1, ADAM_B2, ADAM_EPS, ADAM_WD, ADAM_STEP = 0.001, 0.9, 0.999, 1e-08, 0.01, 10

VMEM_LIMIT = 56 * 1024 * 1024
ROW_TILE = 256


def _cparams(*sem):
    return pltpu.CompilerParams(dimension_semantics=sem if sem else None, vmem_limit_bytes=VMEM_LIMIT)


def _sig(v):
    return 1.0 / (1.0 + jnp.exp(-v))


def _rowsum8(a):
    return a.reshape(a.shape[0] // 8, 8, a.shape[1]).sum(axis=0)


def _dot(a, b):
    return jnp.dot(a, b, preferred_element_type=F32)


def _dot_nt(a, b):
    return lax.dot_general(a, b, (((1,), (1,)), ((), ())), preferred_element_type=F32)


def _dot_tn(a, b):
    return lax.dot_general(a, b, (((0,), (0,)), ((), ())), preferred_element_type=F32)


def _orig_col(zc):
    if zc < 4 * D:
        return zc
    return zc - 7 * D + 4 * D if zc >= 7 * D else zc + 2 * KV_W


def _my_place():
    return lax.axis_index("x"), lax.axis_index("y"), lax.axis_index("c")


def _slot(px, py, pc):
    return 4 * px + 2 * py + pc


class _TwoLevelGather:
    def __init__(self, shards):
        self.arrs = list(shards)
        self.n = len(self.arrs)
        self.out_shape = [jax.ShapeDtypeStruct((N_DEV,) + s.shape, s.dtype) for s in shards]
        self.specs = [pl.BlockSpec(memory_space=pl.ANY)] * self.n
        self.scratch = [pltpu.SemaphoreType.DMA((self.n, 7)), pltpu.SemaphoreType.DMA((self.n, 7)),
                        pltpu.SemaphoreType.DMA((self.n,))]

    def _plan(self, ins, outs, sems):
        send_sems, recv_sems, local_sems = sems
        x, y, c = _my_place()
        me, sibling = (x, y, c), (x, y, 1 - c)
        chips = [(1 - x, y), (x, 1 - y), (1 - x, 1 - y)]

        def copy(a, k, block, to, src=None):
            rows = outs[a].at[_slot(*block)]
            return pltpu.make_async_remote_copy(
                src_ref=rows if src is None else src, dst_ref=rows, send_sem=send_sems.at[a, k],
                recv_sem=recv_sems.at[a, k], device_id=to, device_id_type=pl.DeviceIdType.MESH)

        mine = [pltpu.make_async_copy(ins[a], outs[a].at[_slot(*me)], local_sems.at[a]) for a in range(self.n)]
        first = []
        for a in range(self.n):
            first.append(copy(a, 0, me, sibling, src=ins[a]))
            first += [copy(a, 1 + j, me, (*chip, c), src=ins[a]) for j, chip in enumerate(chips)]
        return copy, mine, first, me, sibling, chips, c

    def start(self, ins, outs, sems):
        _, mine, first, *_ = self._plan(ins, outs, sems)
        for cp in mine + first:
            cp.start()

    def finish(self, ins, outs, sems):
        copy, mine, first, me, sibling, chips, c = self._plan(ins, outs, sems)
        passed = []
        for j, chip in enumerate(chips):
            for a in range(self.n):
                copy(a, 1 + j, (*chip, c), me).wait_recv()
                cp = copy(a, 4 + j, (*chip, c), sibling)
                cp.start()
                passed.append(cp)
        for a in range(self.n):
            copy(a, 0, sibling, me).wait_recv()
            for j, chip in enumerate(chips):
                copy(a, 4 + j, (*chip, 1 - c), me).wait_recv()
        for cp in first + passed:
            cp.wait_send()
        for cp in mine:
            cp.wait()

    def carried(self, refs_in, refs_out, sems, first, last):
        @pl.when(first)
        def _():
            self.start(refs_in, refs_out, sems)

        @pl.when(last)
        def _():
            self.finish(refs_in, refs_out, sems)


class _Exchange:
    def __init__(self, scatter, bcast):
        self.arrs = list(scatter) + list(bcast)
        self.n, self.n_sc = len(self.arrs), len(scatter)
        self.out_shape = [jax.ShapeDtypeStruct(a.shape, a.dtype) for a in scatter]
        self.out_shape += [jax.ShapeDtypeStruct((N_DEV,) + a.shape, a.dtype) for a in bcast]
        self.specs = [pl.BlockSpec(memory_space=pl.ANY)] * self.n
        self.scratch = [pltpu.SemaphoreType.DMA((self.n, 7)), pltpu.SemaphoreType.DMA((self.n, 7)),
                        pltpu.SemaphoreType.DMA((self.n,))]

    def _copies(self, ins, outs, sems):
        send_sems, recv_sems, local_sems = sems
        x, y, c = _my_place()
        me = _slot(x, y, c)
        peers = _peers(x, y, c)
        mine, sends, arrivals = [], [], []
        for a in range(self.n):
            src = ins[a].at[me] if a < self.n_sc else ins[a]
            mine.append(pltpu.make_async_copy(src, outs[a].at[me], local_sems.at[a]))
        for k, peer in enumerate(peers):
            for a in range(self.n):
                src = ins[a].at[_slot(*peer)] if a < self.n_sc else ins[a]
                sends.append(pltpu.make_async_remote_copy(
                    src_ref=src, dst_ref=outs[a].at[me], send_sem=send_sems.at[a, k], recv_sem=recv_sems.at[a, k],
                    device_id=peer, device_id_type=pl.DeviceIdType.MESH))
                rows = outs[a].at[_slot(*peer)]
                arrivals.append(pltpu.make_async_remote_copy(
                    src_ref=rows, dst_ref=rows, send_sem=send_sems.at[a, k], recv_sem=recv_sems.at[a, k],
                    device_id=peer, device_id_type=pl.DeviceIdType.MESH))
        return mine, sends, arrivals

    def start(self, ins, outs, sems):
        mine, sends, _ = self._copies(ins, outs, sems)
        for cp in mine + sends:
            cp.start()

    def finish(self, ins, outs, sems):
        mine, sends, arrivals = self._copies(ins, outs, sems)
        for cp in arrivals:
            cp.wait_recv()
        for cp in sends:
            cp.wait_send()
        for cp in mine:
            cp.wait()

    def carried(self, refs_in, refs_out, sems, first, last):
        @pl.when(first)
        def _():
            self.start(refs_in, refs_out, sems)

        @pl.when(last)
        def _():
            self.finish(refs_in, refs_out, sems)

    def alone(self, name):
        n = self.n

        def body(*refs):
            ins, outs, sems = refs[:n], refs[n : 2 * n], refs[2 * n :]
            self.start(ins, outs, sems)
            self.finish(ins, outs, sems)

        return pl.pallas_call(body, name=name, out_shape=self.out_shape, in_specs=self.specs, out_specs=self.specs,
                              scratch_shapes=self.scratch)(*self.arrs)


def _peers(x, y, c):
    return [(1 - x if k & 4 else x, 1 - y if k & 2 else y, 1 - c if k & 1 else c) for k in range(1, N_DEV)]


def _scatter_send(g, name):
    hbm = pl.BlockSpec(memory_space=pltpu.HBM)
    sem = pl.BlockSpec(memory_space=pltpu.SEMAPHORE)

    def body(g_ref, land_ref, send_sems, recv_sems, g_thru, land_thru, token):
        x, y, c = _my_place()
        me = _slot(x, y, c)
        for k, peer in enumerate(_peers(x, y, c)):
            pltpu.make_async_remote_copy(
                src_ref=g_ref.at[_slot(*peer)], dst_ref=land_ref.at[me], send_sem=send_sems.at[k], recv_sem=recv_sems.at[k],
                device_id=peer, device_id_type=pl.DeviceIdType.MESH).start()
        token[...] = jnp.zeros_like(token)

    return pl.pallas_call(
        body, name=name,
        out_shape=(pltpu.SemaphoreType.DMA((N_DEV - 1,)), pltpu.SemaphoreType.DMA((N_DEV - 1,)),
                   pltpu.HBM(g.shape, g.dtype), pltpu.HBM(g.shape, g.dtype), jax.ShapeDtypeStruct((8, 128), F32)),
        in_specs=(hbm, hbm), out_specs=(sem, sem, hbm, hbm, pl.BlockSpec(memory_space=pltpu.VMEM)),
        input_output_aliases={0: 2, 1: 3},
        compiler_params=pltpu.CompilerParams(has_side_effects=pltpu.SideEffectType.DATAFLOW_SIDE_EFFECTING),
    )(pltpu.with_memory_space_constraint(g, pltpu.HBM),
      pltpu.with_memory_space_constraint(lax.empty(g.shape, g.dtype), pltpu.HBM))


def _scatter_wait(send_sems, recv_sems, g_thru, land_thru, after, name):
    hbm = pl.BlockSpec(memory_space=pltpu.HBM)
    sem = pl.BlockSpec(memory_space=pltpu.SEMAPHORE)

    def body(g_ref, land_ref, send_sems, recv_sems, after_ref, g_out, land_out):
        x, y, c = _my_place()
        for k, peer in enumerate(_peers(x, y, c)):
            cp = pltpu.make_async_remote_copy(
                src_ref=g_ref.at[_slot(*peer)], dst_ref=land_ref.at[_slot(*peer)], send_sem=send_sems.at[k],
                recv_sem=recv_sems.at[k], device_id=peer, device_id_type=pl.DeviceIdType.MESH)
            cp.wait_send()
            cp.wait_recv()

    return pl.pallas_call(
        body, name=name,
        out_shape=(pltpu.HBM(g_thru.shape, g_thru.dtype), pltpu.HBM(land_thru.shape, land_thru.dtype)),
        in_specs=(hbm, hbm, sem, sem, pl.BlockSpec(memory_space=pl.ANY)), out_specs=(hbm, hbm),
        input_output_aliases={0: 0, 1: 1},
        compiler_params=pltpu.CompilerParams(has_side_effects=pltpu.SideEffectType.DATAFLOW_SIDE_EFFECTING),
    )(g_thru, land_thru, send_sems, recv_sems, after)


def _host_split(refs, n_in, n_out, n_scratch, ex):
    k = ex.n if ex is not None else 0
    a = n_in
    b = a + k
    c = b + n_out
    d = c + k
    e = d + n_scratch
    return refs[:a], refs[a:b], refs[b:c], refs[c:d], refs[d:e], refs[e:]


def _rope_tables(pos, ex):
    T = pos.shape[0]
    tm = min(1024, T)
    nt = T // tm
    lane = np.arange(128) % HEAD_DIM
    inv = np.power(np.float32(ROPE_THETA), -np.arange(0, ROPE_DIM, 2, dtype=np.float32) / np.float32(ROPE_DIM)).astype(np.float32)
    half = ROPE_DIM // 2
    invf = np.where(lane < ROPE_DIM, inv[lane % half], 0.0).astype(np.float32)[None, :]
    m_a = (lane < half).astype(np.float32)[None, :]
    m_b = ((lane >= half) & (lane < ROPE_DIM)).astype(np.float32)[None, :]

    def body(*refs):
        ins, ex_in, (cos_ref, sa_ref, sb_ref), ex_out, _, ex_sems = _host_split(refs, 4, 3, 0, ex)
        pos_ref, invf_ref, ma_ref, mb_ref = ins
        i = pl.program_id(0)
        ex.carried(ex_in, ex_out, ex_sems, i == 0, i == nt - 1)
        ang = pos_ref[...].astype(F32) * invf_ref[...]
        sn = jnp.sin(ang)
        cos_ref[...] = jnp.cos(ang)
        sa_ref[...] = -sn * ma_ref[...]
        sb_ref[...] = sn * mb_ref[...]

    row = pl.BlockSpec((tm, 128), lambda i: (i, 0))
    cst = pl.BlockSpec((1, 128), lambda i: (0, 0))
    return pl.pallas_call(
        body, name="rope_tables", grid=(nt,), out_shape=[jax.ShapeDtypeStruct((T, 128), F32)] * 3 + ex.out_shape,
        in_specs=[pl.BlockSpec((tm, 1), lambda i: (i, 0)), cst, cst, cst] + ex.specs, out_specs=[row] * 3 + ex.specs,
        scratch_shapes=ex.scratch,
        compiler_params=_cparams("arbitrary"),
    )(pos, jnp.asarray(invf), jnp.asarray(m_a), jnp.asarray(m_b), *ex.arrs)


def _rope(t, cos, sa, sb, sign=1.0):
    parts = []
    for i in range(t.shape[1] // 128):
        ti = t[:, 128 * i : 128 * (i + 1)]
        up = pltpu.roll(ti, 128 - ROPE_DIM // 2, 1)
        dn = pltpu.roll(ti, ROPE_DIM // 2, 1)
        parts.append(ti * cos + sign * (up * sa + dn * sb))
    return parts[0] if len(parts) == 1 else jnp.concatenate(parts, axis=-1)


def _in_proj(x2, ln_pre, w_in, cos, sa, sb, ex):
    T = x2.shape[0]
    tm = min(512, T)
    nt = T // tm
    chunk = D
    kv_cols = 4 * N_KV * KV_W

    def body(*refs):
        ins, ex_in, (z_ref, h_ref, qr_ref, kv_ref), ex_out, _, ex_sems = _host_split(refs, 6, 4, 0, ex)
        x_ref, g_ref, w_ref, cos_ref, sa_ref, sb_ref = ins
        i = pl.program_id(0)
        ex.carried(ex_in, ex_out, ex_sems, i == 0, i == nt - 1)
        xv = x_ref[...]
        r = lax.rsqrt(jnp.mean(xv * xv, axis=-1, keepdims=True) + EPS)
        h = (xv * r * g_ref[...]).astype(BF16)
        h_ref[...] = h
        tabs = cos_ref[...], sa_ref[...], sb_ref[...]
        for c0 in range(0, NW, chunk):
            cw = min(chunk, NW - c0)
            zc = _dot_nt(h, w_ref[_orig_col(c0) : _orig_col(c0) + cw, :])
            z_ref[:, c0 : c0 + cw] = zc.astype(BF16)
            if c0 == CB_Q * D:
                qr_ref[...] = (_rope(zc, *tabs) * (HEAD_DIM ** -0.5)).astype(BF16)
            if c0 == CB_K * KV_W:
                low = lax.broadcasted_iota(jnp.int32, (tm, KV_W), 1) < HEAD_DIM
                for kind, t in ((0, _rope(zc[:, 0:KV_W], *tabs)), (1, zc[:, KV_W:])):
                    swapped = pltpu.roll(t, HEAD_DIM, 1)
                    cols = {(0, 0): jnp.where(low, t, 0.0), (0, 1): jnp.where(low, 0.0, swapped),
                            (1, 0): jnp.where(low, swapped, 0.0), (1, 1): jnp.where(low, 0.0, t)}
                    for (g, par), val in cols.items():
                        c = _kv2_col(kind, g, par)
                        kv_ref[:, c * KV_W : (c + 1) * KV_W] = val.astype(BF16)

    def row(w):
        return pl.BlockSpec((tm, w), lambda i: (i, 0))

    bf = lambda w: jax.ShapeDtypeStruct((T, w), BF16)
    return pl.pallas_call(
        body, name="in_proj", grid=(nt,),
        out_shape=[bf(NW), bf(D), bf(D), bf(kv_cols)] + ex.out_shape,
        in_specs=[row(D), _weight_spec((1, D)), _weight_spec((NW, D)), row(128), row(128), row(128)] + ex.specs,
        out_specs=[row(NW), row(D), row(D), row(kv_cols)] + ex.specs,
        scratch_shapes=ex.scratch,
        compiler_params=_cparams("arbitrary"),
    )(x2, ln_pre, w_in, cos, sa, sb, *ex.arrs)


def _conv_tiles(S):
    tm = min(ROW_TILE, S)
    return tm, S // tm, tm // CONV_HALO


CONV_ROWS_FWD = 32
CONV_ROWS = 16


def _fill_shifted(sh, rows):
    for b in range(1, 8):
        sh[b, 0:rows, :] = sh[0, b : b + rows, :]


def _conv_fwd(z3, w_dw, b_dw, ln_g, ln_b):
    NS, S, _ = z3.shape
    tm, nt, r = _conv_tiles(S)

    def body(val_ref, glu_ref, hval_ref, hglu_ref, w_ref, b_ref, g_ref, bb_ref, c_ref, cs_ref, ush, cbuf):
        i = pl.program_id(1)
        ush[0, CONV_HALO:, :] = val_ref[...].astype(F32) * _sig(glu_ref[...].astype(F32))
        uh = hval_ref[...].astype(F32) * _sig(hglu_ref[...].astype(F32))
        ush[0, 0:CONV_HALO, :] = jnp.where(i > 0, uh, 0.0)
        _fill_shifted(ush, tm + CONV_HALO - 8)
        for r0 in range(0, tm, CONV_ROWS_FWD):
            acc = jnp.zeros((CONV_ROWS_FWD, D), F32)
            for k in range(CONV_K):
                a, b = divmod(CONV_HALO - (CONV_K - 1) + k, 8)
                acc = acc + w_ref[k : k + 1, :] * ush[b, r0 + 8 * a : r0 + 8 * a + CONV_ROWS_FWD, :]
            cbuf[r0 : r0 + CONV_ROWS_FWD, :] = acc + b_ref[...]
        cv = cbuf[...]
        mu = jnp.mean(cv, axis=-1, keepdims=True)
        xc = cv - mu
        var = jnp.mean(xc * xc, axis=-1, keepdims=True)
        cl = xc * lax.rsqrt(var + EPS) * g_ref[...] + bb_ref[...]
        c_ref[...] = cv.astype(BF16)
        cs_ref[...] = (cl * _sig(cl)).astype(BF16)

    def cur(cb):
        return pl.BlockSpec((None, tm, D), lambda s, i: (s, i, cb))

    def halo(cb):
        return pl.BlockSpec((None, CONV_HALO, D), lambda s, i: (s, jnp.maximum(i * r - 1, 0), cb))

    vec = pl.BlockSpec((1, D), lambda s, i: (0, 0))
    out = pl.BlockSpec((None, tm, D), lambda s, i: (s, i, 0))
    return pl.pallas_call(
        body, name="conv_fwd", grid=(NS, nt),
        out_shape=[jax.ShapeDtypeStruct((NS, S, D), BF16)] * 2,
        in_specs=[cur(CB_VAL), cur(CB_GLU), halo(CB_VAL), halo(CB_GLU),
                  pl.BlockSpec((CONV_HALO, D), lambda s, i: (0, 0)), vec, vec, vec],
        out_specs=[out, out],
        scratch_shapes=[pltpu.VMEM((8, tm + CONV_HALO, D), F32), pltpu.VMEM((tm, D), F32)],
        compiler_params=_cparams("parallel", "parallel"),
    )(z3, z3, z3, z3, w_dw, b_dw, ln_g, ln_b)


PAIRS = GROUP // 2
QROWS = PAIRS * BLK


def _kv2_col(kind, g, par):
    return kind * 2 * N_KV + g * 2 + par


def _attn_mask(has_prev):
    qi = lax.broadcasted_iota(jnp.int32, (QROWS, 2 * BLK), 0) & (BLK - 1)
    kj = lax.broadcasted_iota(jnp.int32, (QROWS, 2 * BLK), 1)
    first_key = jnp.where(has_prev, 0, BLK)
    return (kj > qi) & (kj <= qi + BLK) & (kj >= first_key)


NEG_BIG = -1e30


def _attn_specs():
    q = pl.BlockSpec((None, BLK, D), lambda s, n: (s, n, 0))
    kv_cur = pl.BlockSpec((None, BLK, 4 * N_KV * KV_W), lambda s, n: (s, n, 0))
    kv_prev = pl.BlockSpec((None, BLK, 4 * N_KV * KV_W), lambda s, n: (s, jnp.maximum(n - 1, 0), 0))
    sink = pl.BlockSpec(memory_space=pltpu.SMEM)
    return sink, q, kv_cur, kv_prev


def _stack_pairs(ref, g):
    return jnp.concatenate([ref[:, (PAIRS * g + j) * 128 : (PAIRS * g + j + 1) * 128] for j in range(PAIRS)], axis=0)


def _unstack_pairs(ref, g, val):
    for j in range(PAIRS):
        ref[:, (PAIRS * g + j) * 128 : (PAIRS * g + j + 1) * 128] = val[j * BLK : (j + 1) * BLK].astype(ref.dtype)


def _pair_heads(g, par):
    return [GROUP * g + 2 * j + par for j in range(PAIRS)]


def _head_col_load(ref, g, par):
    return jnp.concatenate([ref[:, h : h + 1] for h in _pair_heads(g, par)], axis=0)


def _head_col_store(ref, g, par, col):
    for j, h in enumerate(_pair_heads(g, par)):
        ref[:, h : h + 1] = col[j * BLK : (j + 1) * BLK]


def _sink_col(sink_ref, g, par):
    return jnp.concatenate([jnp.full((BLK, 1), sink_ref[h], F32) for h in _pair_heads(g, par)], axis=0)


def _kv2_block(kvc_ref, kvp_ref, has_prev, c):
    col = slice(c * KV_W, (c + 1) * KV_W)
    prev = jnp.where(has_prev, kvp_ref[:, col], jnp.zeros((BLK, KV_W), BF16))
    return jnp.concatenate([prev, kvc_ref[:, col]], axis=0)


def _attn_fwd(qr3, kv3, sinks):
    NS, S, _ = qr3.shape

    def body(sink_ref, q_ref, kvc_ref, kvp_ref, o_ref, lse_ref):
        has_prev = pl.program_id(1) > 0
        mask = _attn_mask(has_prev)[0:BLK]
        for g in range(N_KV):
            kv = [[_kv2_block(kvc_ref, kvp_ref, has_prev, _kv2_col(kind, g, par)) for par in range(2)] for kind in range(2)]
            for j in range(PAIRS):
                cols = slice((PAIRS * g + j) * 128, (PAIRS * g + j + 1) * 128)
                q2 = q_ref[:, cols]
                o_pair = None
                for par in range(2):
                    h = GROUP * g + 2 * j + par
                    s = jnp.where(mask, _dot_nt(q2, kv[0][par]), NEG_BIG)
                    sk = sink_ref[h]
                    mx = jnp.maximum(jnp.max(s, axis=-1, keepdims=True), sk)
                    e = jnp.exp(s - mx)
                    den = jnp.sum(e, axis=-1, keepdims=True) + jnp.exp(sk - mx)
                    pv = _dot(e.astype(BF16), kv[1][par]) * (1.0 / den)
                    o_pair = pv if o_pair is None else o_pair + pv
                    lse_ref[:, h : h + 1] = mx + jnp.log(den)
                o_ref[:, cols] = o_pair.astype(BF16)

    return pl.pallas_call(
        body, name="attn_fwd", grid=(NS, S // BLK),
        out_shape=[jax.ShapeDtypeStruct((NS, S, D), BF16), jax.ShapeDtypeStruct((NS, S, N_HEADS), F32)],
        in_specs=list(_attn_specs()),
        out_specs=[pl.BlockSpec((None, BLK, D), lambda s, n: (s, n, 0)),
                   pl.BlockSpec((None, BLK, N_HEADS), lambda s, n: (s, n, 0))],
        compiler_params=_cparams("parallel", "parallel"),
    )(sinks, qr3, kv3, kv3)


def _weight_spec(shape):
    return pl.BlockSpec(shape, lambda i: (0,) * len(shape), pipeline_mode=pl.Buffered(1))


def _dsilu(v, s):
    return s * (1.0 + v * (1.0 - s))


MID_TILE = 256


def _mid(cs, o, z2, x2, p2, tgt, ln_post, weights):
    T = cs.shape[0]
    tm = min(MID_TILE, T)
    nt = T // tm
    n_bf = 16

    def body(cs_ref, o_ref, cg_ref, ag_ref, gc_ref, ga_ref, x_ref, p_ref, t_ref, g_ref,
             wpw, wbrc, wbra, wout, wpg, wpp,
             dya0_ref, ya1_ref, dya_ref, yb0_ref, dyb_ref, m_ref, dmo_ref, x1_ref, dgl_ref, dpp_ref,
             dcs_ref, do_ref, dcg_ref, dag_ref, dgc_ref, dga_ref, dx1_ref, loss_ref, glp_ref, lacc):
        i = pl.program_id(0)

        @pl.when(i == 0)
        def _():
            lacc[...] = jnp.zeros_like(lacc)
            glp_ref[...] = jnp.zeros_like(glp_ref)

        cg = cg_ref[...].astype(F32)
        scg = _sig(cg)
        silu_c = cg * scg
        ya0 = _dot(cs_ref[...], wpw[...])
        ya1 = (ya0 * silu_c).astype(BF16)
        ya1_ref[...] = ya1
        ya = _dot(ya1, wbrc[...])
        ag = ag_ref[...].astype(F32)
        sag = _sig(ag)
        silu_a = ag * sag
        ov = o_ref[...].astype(F32)
        yb0 = (ov * silu_a).astype(BF16)
        yb0_ref[...] = yb0
        yb = _dot(yb0, wbra[...])
        sgc = _sig(gc_ref[...].astype(F32))
        sga = _sig(ga_ref[...].astype(F32))
        mb = (sgc * ya + sga * yb).astype(BF16)
        m_ref[...] = mb
        mo = _dot(mb, wout[...])
        r2 = lax.rsqrt(jnp.mean(mo * mo, axis=-1, keepdims=True) + EPS)
        nrm = mo * r2
        x1 = x_ref[...] + nrm * g_ref[...]
        x1b = x1.astype(BF16)
        x1_ref[...] = x1b
        gate = _sig(_dot(x1b, wpg[...]))
        pp = _dot(p_ref[...].astype(BF16), wpp[...])
        e = x1 + gate * pp - t_ref[...]
        lacc[...] += _rowsum8(e * e)
        dy = e * (1.0 / D)

        dgl = (dy * pp * gate * (1.0 - gate)).astype(BF16)
        dgl_ref[...] = dgl
        dpp_ref[...] = (dy * gate).astype(BF16)
        dx1 = dy + _dot_nt(dgl, wpg[...])
        dx1_ref[...] = dx1
        glp_ref[...] += _rowsum8(dx1 * nrm)
        dn = dx1 * g_ref[...]
        dmo = (r2 * (dn - nrm * jnp.mean(dn * nrm, axis=-1, keepdims=True))).astype(BF16)
        dmo_ref[...] = dmo
        dm = _dot_nt(dmo, wout[...])
        dya = (dm * sgc).astype(BF16)
        dyb = (dm * sga).astype(BF16)
        dya_ref[...] = dya
        dyb_ref[...] = dyb
        dgc_ref[...] = (dm * ya * sgc * (1.0 - sgc)).astype(BF16)
        dga_ref[...] = (dm * yb * sga * (1.0 - sga)).astype(BF16)
        dya1 = _dot_nt(dya, wbrc[...])
        dya0 = (dya1 * silu_c).astype(BF16)
        dya0_ref[...] = dya0
        dcg_ref[...] = (dya1 * ya0 * _dsilu(cg, scg)).astype(BF16)
        dcs_ref[...] = _dot_nt(dya0, wpw[...]).astype(BF16)
        dyb0 = _dot_nt(dyb, wbra[...])
        do_ref[...] = (dyb0 * silu_a).astype(BF16)
        dag_ref[...] = (dyb0 * ov * _dsilu(ag, sag)).astype(BF16)

        @pl.when(i == nt - 1)
        def _():
            loss_ref[...] = jnp.full(loss_ref.shape, jnp.sum(lacc[...]) * (0.5 / D), F32)

    row = pl.BlockSpec((tm, D), lambda i: (i, 0))

    def zcol(cb):
        return pl.BlockSpec((tm, D), lambda i: (i, cb))

    bf = jax.ShapeDtypeStruct((T, D), BF16)
    return pl.pallas_call(
        body, name="mid_fwd_bwd", grid=(nt,),
        out_shape=[bf] * n_bf + [jax.ShapeDtypeStruct((T, D), F32), jax.ShapeDtypeStruct((8, 128), F32),
                                 jax.ShapeDtypeStruct((8, D), F32)],
        in_specs=[row, row, zcol(CB_CGATE), zcol(CB_AGATE), zcol(CB_GCONV), zcol(CB_GATTN), row,
                  pl.BlockSpec((tm, PLE), lambda i: (i, 0)), row, _weight_spec((1, D))]
        + [_weight_spec((D, D))] * 5 + [_weight_spec((PLE, D))],
        out_specs=[row] * (n_bf + 1) + [pl.BlockSpec((8, 128), lambda i: (0, 0)), pl.BlockSpec((8, D), lambda i: (0, 0))],
        scratch_shapes=[pltpu.VMEM((8, D), F32)],
        compiler_params=_cparams("arbitrary"),
    )(cs, o, z2, z2, z2, z2, x2, p2, tgt, ln_post, *weights)


def _conv_bwd(z3, c3, dcs3, w_dw, ln_g, ln_b, ex):
    NS, S, _ = z3.shape
    tm, nt, r = _conv_tiles(S)

    def body(*refs):
        ins, ex_in, outs, ex_out, scratch, ex_sems = _host_split(refs, 9, 3, 3, ex)
        val_ref, glu_ref, c_ref, dcs_ref, hc_ref, hdcs_ref, w_ref, g_ref, bb_ref = ins
        dz_ref, gw_ref, gvec_ref = outs
        dsh, ubuf, dubuf = scratch
        i = pl.program_id(1)
        first = (pl.program_id(0) == 0) & (i == 0)
        ex.carried(ex_in, ex_out, ex_sems, first, (pl.program_id(0) == NS - 1) & (i == nt - 1))

        @pl.when(first)
        def _():
            gw_ref[...] = jnp.zeros_like(gw_ref)
            gvec_ref[...] = jnp.zeros_like(gvec_ref)

        val = val_ref[...].astype(F32)
        sg = _sig(glu_ref[...].astype(F32))
        ubuf[...] = val * sg

        def ln_bwd(cv, dcs):
            cv = cv.astype(F32)
            mu = jnp.mean(cv, axis=-1, keepdims=True)
            xc = cv - mu
            rstd = lax.rsqrt(jnp.mean(xc * xc, axis=-1, keepdims=True) + EPS)
            xhat = xc * rstd
            cl = xhat * g_ref[...] + bb_ref[...]
            s = _sig(cl)
            dcl = dcs.astype(F32) * _dsilu(cl, s)
            dxh = dcl * g_ref[...]
            dc = rstd * (dxh - jnp.mean(dxh, axis=-1, keepdims=True) - xhat * jnp.mean(dxh * xhat, axis=-1, keepdims=True))
            return dc, dcl, xhat

        dc, dcl, xhat = ln_bwd(c_ref[...], dcs_ref[...])
        dsh[0, 0:tm, :] = dc
        dch, _, _ = ln_bwd(hc_ref[...], hdcs_ref[...])
        dsh[0, tm:, :] = jnp.where(i < nt - 1, dch, 0.0)
        gvec_ref[0:8, :] += _rowsum8(dcl * xhat)
        gvec_ref[8:16, :] += _rowsum8(dcl)
        gvec_ref[16:24, :] += _rowsum8(dc)
        _fill_shifted(dsh, tm + CONV_HALO - 8)

        def dc_ahead(r0, k):
            a, b = divmod(CONV_K - 1 - k, 8)
            return dsh[b, r0 + 8 * a : r0 + 8 * a + CONV_ROWS, :]

        for r0 in range(0, tm, CONV_ROWS):
            acc = jnp.zeros((CONV_ROWS, D), F32)
            for k in range(CONV_K):
                acc = acc + w_ref[k : k + 1, :] * dc_ahead(r0, k)
            dubuf[r0 : r0 + CONV_ROWS, :] = acc
        for r0 in range(0, tm, CONV_ROWS):
            ur = ubuf[r0 : r0 + CONV_ROWS, :]
            for k in range(CONV_K):
                gw_ref[8 * k : 8 * k + 8, :] += _rowsum8(ur * dc_ahead(r0, k))
        du = dubuf[...]
        dz_ref[:, 0:D] = (du * sg).astype(BF16)
        dz_ref[:, D:] = (du * val * sg * (1.0 - sg)).astype(BF16)

    def cur(cb):
        return pl.BlockSpec((None, tm, D), lambda s, i: (s, i, cb))

    nxt = pl.BlockSpec((None, CONV_HALO, D), lambda s, i: (s, jnp.minimum((i + 1) * r, S // CONV_HALO - 1), 0))
    vec = pl.BlockSpec((1, D), lambda s, i: (0, 0))
    return pl.pallas_call(
        body, name="conv_bwd", grid=(NS, nt),
        out_shape=[jax.ShapeDtypeStruct((NS, S, 2 * D), BF16), jax.ShapeDtypeStruct((CONV_HALO * 8, D), F32),
                   jax.ShapeDtypeStruct((24, D), F32)] + ex.out_shape,
        in_specs=[cur(CB_VAL), cur(CB_GLU), cur(0), cur(0), nxt, nxt,
                  pl.BlockSpec((CONV_HALO, D), lambda s, i: (0, 0)), vec, vec] + ex.specs,
        out_specs=[pl.BlockSpec((None, tm, 2 * D), lambda s, i: (s, i, 0)),
                   pl.BlockSpec((CONV_HALO * 8, D), lambda s, i: (0, 0)), pl.BlockSpec((24, D), lambda s, i: (0, 0))] + ex.specs,
        scratch_shapes=[pltpu.VMEM((8, tm + CONV_HALO, D), F32), pltpu.VMEM((tm, D), F32), pltpu.VMEM((tm, D), F32)] + ex.scratch,
        compiler_params=_cparams("arbitrary", "arbitrary"),
    )(z3, z3, c3, dcs3, c3, dcs3, w_dw, ln_g, ln_b, *ex.arrs)


def _attn_bwd(qr3, kv3, do3, o3, lse3, sinks, cos3, sa3, sb3):
    NS, S, _ = qr3.shape

    def body(sink_ref, q_ref, kvc_ref, kvp_ref, do_ref, o_ref, lse_ref, cos_ref, sa_ref, sb_ref,
             dq_ref, dkc_ref, dkp_ref, dvc_ref, dvp_ref, dsk_ref):
        has_prev = pl.program_id(1) > 0
        mask = _attn_mask(has_prev)
        tabs = cos_ref[...], sa_ref[...], sb_ref[...]
        low_q = lax.broadcasted_iota(jnp.int32, (QROWS, 128), 1) < HEAD_DIM
        dk_g, dv_g = [], []
        for g in range(N_KV):
            qs = _stack_pairs(q_ref, g)
            dos = _stack_pairs(do_ref, g)
            prod = dos.astype(F32) * _stack_pairs(o_ref, g).astype(F32)
            deltas = [jnp.sum(jnp.where(low_q, prod, 0.0), axis=-1, keepdims=True),
                      jnp.sum(jnp.where(low_q, 0.0, prod), axis=-1, keepdims=True)]
            dq_acc, dk_par, dv_par = None, [], []
            for par in range(2):
                k2 = _kv2_block(kvc_ref, kvp_ref, has_prev, _kv2_col(0, g, par))
                v2 = _kv2_block(kvc_ref, kvp_ref, has_prev, _kv2_col(1, g, par))
                lse = _head_col_load(lse_ref, g, par)
                p = jnp.exp(jnp.where(mask, _dot_nt(qs, k2), NEG_BIG) - lse)
                ds = (p * (_dot_nt(dos, v2) - deltas[par])).astype(BF16)
                dq = _dot(ds, k2)
                dq_acc = dq if dq_acc is None else dq_acc + dq
                dk_par.append(_dot_tn(ds, qs))
                dv_par.append(_dot_tn(p.astype(BF16), dos))
                _head_col_store(dsk_ref, g, par, -jnp.exp(_sink_col(sink_ref, g, par) - lse) * deltas[par])
            for j in range(PAIRS):
                dq_pair = _rope(dq_acc[j * BLK : (j + 1) * BLK], *tabs, sign=-1.0) * (HEAD_DIM ** -0.5)
                dq_ref[:, (PAIRS * g + j) * 128 : (PAIRS * g + j + 1) * 128] = dq_pair.astype(BF16)
            dk_g.append(dk_par[0] + pltpu.roll(dk_par[1], HEAD_DIM, 1))
            dv_g.append(dv_par[0] + pltpu.roll(dv_par[1], HEAD_DIM, 1))
        low_k = lax.broadcasted_iota(jnp.int32, (2 * BLK, KV_W), 1) < HEAD_DIM
        dk = jnp.where(low_k, dk_g[0], pltpu.roll(dk_g[1], HEAD_DIM, 1))
        dv = jnp.where(low_k, dv_g[0], pltpu.roll(dv_g[1], HEAD_DIM, 1))
        dkp_ref[...] = dk[0:BLK]
        dkc_ref[...] = dk[BLK:]
        dvp_ref[...] = dv[0:BLK]
        dvc_ref[...] = dv[BLK:]

    qspec = pl.BlockSpec((None, BLK, D), lambda s, n: (s, n, 0))
    kvspec = pl.BlockSpec((None, BLK, KV_W), lambda s, n: (s, n, 0))
    hspec = pl.BlockSpec((None, BLK, N_HEADS), lambda s, n: (s, n, 0))
    kv = jax.ShapeDtypeStruct((NS, S, KV_W), F32)
    return pl.pallas_call(
        body, name="attn_bwd", grid=(NS, S // BLK),
        out_shape=[jax.ShapeDtypeStruct((NS, S, D), BF16), kv, kv, kv, kv, jax.ShapeDtypeStruct((NS, S, N_HEADS), F32)],
        in_specs=list(_attn_specs()) + [qspec, qspec, hspec, kvspec, kvspec, kvspec],
        out_specs=[qspec, kvspec, kvspec, kvspec, kvspec, hspec],
        compiler_params=_cparams("parallel", "parallel"),
    )(sinks, qr3, kv3, kv3, do3, o3, lse3, cos3, sa3, sb3)


def _attn_post(dkc3, dkp3, dvc3, dvp3, cos3, sa3, sb3):
    NS, S, _ = dkc3.shape
    tm = min(4 * BLK, S)
    nt = S // tm

    def body(dkc_ref, dkp_ref, dkn_ref, dvc_ref, dvp_ref, dvn_ref, cos_ref, sa_ref, sb_ref, dkv_ref):
        has_next = pl.program_id(1) < nt - 1

        def join(cur_ref, prev_ref, next_ref):
            ahead = jnp.where(has_next, next_ref[...], 0.0)
            shifted = ahead if tm == BLK else jnp.concatenate([prev_ref[BLK:, :], ahead], axis=0)
            return cur_ref[...] + shifted

        tabs = cos_ref[...], sa_ref[...], sb_ref[...]
        dkv_ref[:, 0:KV_W] = _rope(join(dkc_ref, dkp_ref, dkn_ref), *tabs, sign=-1.0).astype(BF16)
        dkv_ref[:, KV_W:] = join(dvc_ref, dvp_ref, dvn_ref).astype(BF16)

    cur = pl.BlockSpec((None, tm, KV_W), lambda s, j: (s, j, 0))
    nxt = pl.BlockSpec((None, BLK, KV_W), lambda s, j: (s, jnp.minimum((j + 1) * (tm // BLK), S // BLK - 1), 0))
    return pl.pallas_call(
        body, name="attn_post", grid=(NS, nt),
        out_shape=jax.ShapeDtypeStruct((NS, S, 2 * KV_W), BF16),
        in_specs=[cur, cur, nxt, cur, cur, nxt, cur, cur, cur],
        out_specs=pl.BlockSpec((None, tm, 2 * KV_W), lambda s, j: (s, j, 0)),
        compiler_params=_cparams("parallel", "parallel"),
    )(dkc3, dkp3, dkp3, dvc3, dvp3, dvp3, cos3, sa3, sb3)


def _in_bwd(segs, w_in_t, x2, dx1, ln_pre, after):
    T = x2.shape[0]
    tm = min(ROW_TILE, T)
    nt = T // tm
    ns = len(segs)
    widths = [s.shape[1] for s in segs]

    def body(*refs):
        seg_refs = refs[:ns]
        w_ref, x_ref, dx1_ref, g_ref, _, gx_ref, glp_ref = refs[ns:]
        i = pl.program_id(0)

        @pl.when(i == 0)
        def _():
            glp_ref[...] = jnp.zeros_like(glp_ref)

        dh = None
        zc = 0
        for sref, w in zip(seg_refs, widths):
            part = _dot(sref[...], w_ref[_orig_col(zc) : _orig_col(zc) + w, :])
            dh = part if dh is None else dh + part
            zc += w
        xv = x_ref[...]
        r1 = lax.rsqrt(jnp.mean(xv * xv, axis=-1, keepdims=True) + EPS)
        xhat = xv * r1
        glp_ref[...] += _rowsum8(dh * xhat)
        dhg = dh * g_ref[...]
        gx_ref[...] = dx1_ref[...] + r1 * (dhg - xhat * jnp.mean(dhg * xhat, axis=-1, keepdims=True))

    row = pl.BlockSpec((tm, D), lambda i: (i, 0))
    return pl.pallas_call(
        body, name="in_bwd", grid=(nt,),
        out_shape=[jax.ShapeDtypeStruct((T, D), F32), jax.ShapeDtypeStruct((8, D), F32)],
        in_specs=[pl.BlockSpec((tm, w), lambda i: (i, 0)) for w in widths]
        + [_weight_spec((NW, D)), row, row, _weight_spec((1, D)), pl.BlockSpec(memory_space=pl.ANY)],
        out_specs=[row, pl.BlockSpec((8, D), lambda i: (0, 0))],
        compiler_params=_cparams("arbitrary"),
    )(*segs, w_in_t, x2, dx1, ln_pre, after)


def _grad_matmul(a, b, name):
    T, M = a.shape
    N = b.shape[1]
    tk = min(1024, T)
    nk = T // tk

    def body(a_ref, b_ref, o_ref, acc):
        k = pl.program_id(0)

        @pl.when(k == 0)
        def _():
            acc[...] = jnp.zeros_like(acc)

        acc[...] += _dot_tn(a_ref[...].astype(BF16), b_ref[...])

        @pl.when(k == nk - 1)
        def _():
            o_ref[...] = acc[...].astype(BF16)

    return pl.pallas_call(
        body, name=name, grid=(nk,), out_shape=jax.ShapeDtypeStruct((M, N), BF16),
        in_specs=[pl.BlockSpec((tk, M), lambda k: (k, 0)), pl.BlockSpec((tk, N), lambda k: (k, 0))],
        out_specs=pl.BlockSpec((M, N), lambda k: (0, 0)),
        scratch_shapes=[pltpu.VMEM((M, N), F32)],
        compiler_params=_cparams("arbitrary"),
    )(a, b)


def _pack_small(gw, gvec, glp_pre, glp_post, dsk):
    T = dsk.shape[0]

    def body(gw_ref, gvec_ref, pre_ref, post_ref, dsk_ref, gdw_ref, gs_ref):
        gwf = gw_ref[...].reshape(CONV_HALO, 8, D).sum(axis=1)
        for d in range(N_DEV):
            gdw_ref[d] = gwf[:, 128 * d : 128 * (d + 1)]
        gs_ref[...] = jnp.zeros_like(gs_ref)
        gs_ref[0:1, :] = jnp.sum(pre_ref[...], axis=0, keepdims=True)
        gs_ref[1:2, :] = jnp.sum(post_ref[...], axis=0, keepdims=True)
        gs_ref[2:3, :] = jnp.sum(gvec_ref[16:24, :], axis=0, keepdims=True)
        gs_ref[3:4, :] = jnp.sum(gvec_ref[0:8, :], axis=0, keepdims=True)
        gs_ref[4:5, :] = jnp.sum(gvec_ref[8:16, :], axis=0, keepdims=True)
        gs_ref[5:6, 0:N_HEADS] = jnp.sum(dsk_ref[...], axis=0, keepdims=True)

    return pl.pallas_call(
        body, name="pack_small",
        out_shape=[jax.ShapeDtypeStruct((N_DEV, CONV_HALO, 128), F32), jax.ShapeDtypeStruct((8, D), F32)],
        compiler_params=_cparams(),
    )(gw, gvec, glp_pre, glp_post, dsk)


def _adam_update(g, w_ref, m_ref, v_ref, g_ref, d_ref, nm_ref, nv_ref):
    nm = ADAM_B1 * m_ref[...] + (1.0 - ADAM_B1) * g
    nv = ADAM_B2 * v_ref[...] + (1.0 - ADAM_B2) * (g * g)
    m_hat = nm / (1.0 - ADAM_B1 ** ADAM_STEP)
    v_hat = nv / (1.0 - ADAM_B2 ** ADAM_STEP)
    g_ref[...] = g
    d_ref[...] = -ADAM_LR * (m_hat / (jnp.sqrt(v_hat) + ADAM_EPS) + ADAM_WD * w_ref[...])
    nm_ref[...] = nm
    nv_ref[...] = nv


def _adamw(parts, w, m, v, name):
    R, C = w.shape
    tr = R if R <= 256 else 128

    def body(p_ref, *rest):
        g = p_ref[0].astype(F32)
        for s in range(1, N_DEV):
            g = g + p_ref[s].astype(F32)
        _adam_update(g, *rest)

    blk = pl.BlockSpec((tr, C), lambda i: (i, 0))
    return pl.pallas_call(
        body, name=name, grid=(R // tr,), out_shape=[jax.ShapeDtypeStruct((R, C), F32)] * 4,
        in_specs=[pl.BlockSpec((N_DEV, tr, C), lambda i: (0, i, 0)), blk, blk, blk], out_specs=[blk] * 4,
        compiler_params=_cparams("parallel"),
    )(parts, w, m, v)


def _adamw_own(parts, own, me, w, m, v, name):
    R, C = w.shape
    tr = max(t for t in range(16, 513, 16) if R % t == 0)

    def body(me_ref, p_ref, own_ref, *rest):
        g = None
        for s in range(N_DEV):
            part = jnp.where(me_ref[0] == s, own_ref[...], p_ref[s]).astype(F32)
            g = part if g is None else g + part
        _adam_update(g, *rest)

    blk = pl.BlockSpec((tr, C), lambda i, me: (i, 0))
    return pl.pallas_call(
        body, name=name, out_shape=[jax.ShapeDtypeStruct((R, C), F32)] * 4,
        grid_spec=pltpu.PrefetchScalarGridSpec(
            num_scalar_prefetch=1, grid=(R // tr,),
            in_specs=[pl.BlockSpec((N_DEV, tr, C), lambda i, me: (0, i, 0)),
                      pl.BlockSpec((None, tr, C), lambda i, me: (me[0], i, 0)), blk, blk, blk],
            out_specs=[blk] * 4),
        compiler_params=_cparams("parallel"),
    )(me, parts, own, w, m, v)


def kernel(x, p, positions, w_in, ln_pre, ln_post, w_dw, b_dw, conv_ln_g, conv_ln_b, w_pw, sinks, w_br_conv, w_br_attn, w_out, w_ple_gate, w_ple_proj, loss_target, m_w_in, m_ln_pre, m_ln_post, m_w_dw, m_b_dw, m_conv_ln_g, m_conv_ln_b, m_w_pw, m_sinks, m_w_br_conv, m_w_br_attn, m_w_out, m_w_ple_gate, m_w_ple_proj, v_w_in, v_ln_pre, v_ln_post, v_w_dw, v_b_dw, v_conv_ln_g, v_conv_ln_b, v_w_pw, v_sinks, v_w_br_conv, v_w_br_attn, v_w_out, v_w_ple_gate, v_w_ple_proj):
    NS, S, _ = x.shape
    T = NS * S
    x2 = x.reshape(T, D)
    p2 = p.reshape(T, PLE)
    tgt = loss_target.reshape(T, D)
    pos = positions.reshape(T, 1)

    row_sharded = [w_pw[0], w_br_conv[0], w_br_attn[0], w_out[0], w_ple_gate[0]]
    sh_rows = D // N_DEV
    w_t, m_t, v_t = (jnp.swapaxes(a[0], 0, 1) for a in (w_in, m_w_in, v_w_in))
    cos, sa, sb, g_in = _rope_tables(pos, _TwoLevelGather([w_t.astype(BF16)]))
    w_in_f = g_in.reshape(NW, D)
    gather_rest = _Exchange([], [w.astype(BF16) for w in row_sharded] + [
        w_ple_proj[0].astype(BF16), jnp.pad(w_dw[0], ((0, CONV_HALO - CONV_K), (0, 0)))])

    in_out = _in_proj(x2, ln_pre, w_in_f, cos, sa, sb, gather_rest)
    z2, h, qr, kv2 = in_out[:4]
    g_rows, g_pp, g_dw = in_out[4:9], in_out[9], in_out[10]
    full = [g.reshape(D, D) for g in g_rows]
    w_pp_f = g_pp.transpose(1, 0, 2).reshape(PLE, D)
    w_dw_f = g_dw.transpose(1, 0, 2).reshape(CONV_HALO, D)
    z3 = z2.reshape(NS, S, NW)
    c3, cs3 = _conv_fwd(z3, w_dw_f, b_dw, conv_ln_g, conv_ln_b)
    qr3, kv3 = qr.reshape(NS, S, D), kv2.reshape(NS, S, 4 * N_KV * KV_W)
    sinks1 = sinks.reshape(N_HEADS)
    o3, lse3 = _attn_fwd(qr3, kv3, sinks1)
    o = o3.reshape(T, D)
    cs = cs3.reshape(T, D)
    (dya0, ya1, dya, yb0, dyb, m, dmo, x1, dgl, dpp, dcs, do, dcg, dag, dgc, dga, dx1, loss_blk, glp_post) = _mid(
        cs, o, z2, x2, p2, tgt, ln_post, full + [w_pp_f])

    gp_rows = [_grad_matmul(a, b, nm).reshape(N_DEV, sh_rows, D) for a, b, nm in (
        (cs, dya0, "grad_w_pw"), (ya1, dya, "grad_w_br_conv"), (yb0, dyb, "grad_w_br_attn"),
        (m, dmo, "grad_w_out"), (x1, dgl, "grad_w_ple_gate"))]
    gp_pp = _grad_matmul(p2, dpp, "grad_w_ple_proj").reshape(PLE, N_DEV, D // N_DEV).transpose(1, 0, 2)
    conv_out = _conv_bwd(z3, c3, dcs.reshape(NS, S, D), w_dw_f, conv_ln_g, conv_ln_b,
                         _Exchange(gp_rows + [gp_pp], [loss_blk]))
    dzvu3, gw, gvec, r_rows, r_pp, r_loss = conv_out[0], conv_out[1], conv_out[2], conv_out[3:8], conv_out[8], conv_out[9]
    loss = jnp.sum(r_loss[:, 0, 0])
    tab3 = [t.reshape(NS, S, 128) for t in (cos, sa, sb)]
    dq3, dkc3, dkp3, dvc3, dvp3, dsk3 = _attn_bwd(qr3, kv3, do.reshape(NS, S, D), o3, lse3, sinks1, *tab3)
    dkv3 = _attn_post(dkc3, dkp3, dvc3, dvp3, *tab3)
    segs = [dzvu3.reshape(T, 2 * D), dcg, dq3.reshape(T, D), dag, dgc, dga, dkv3.reshape(T, 2 * KV_W)]
    gt = [_grad_matmul(s, h, f"grad_w_in_{j}") for j, s in enumerate(segs)]
    gp_in = jnp.concatenate(gt[:3] + gt[6:] + gt[3:6], axis=0).reshape(N_DEV, NW // N_DEV, D)
    in_send, in_recv, in_own, in_land, sent = _scatter_send(gp_in, "w_in_grad_send")
    grad_x2, glp_pre = _in_bwd(segs, w_in_f, x2, dx1, ln_pre, sent)
    gp_dw, gp_small = _pack_small(gw, gvec, glp_pre, glp_post, dsk3.reshape(T, N_HEADS))
    r_dw, r_small = _Exchange([gp_dw], [gp_small]).alone("small_grad_exchange")

    res = {}
    names_rows = ["w_pw", "w_br_conv", "w_br_attn", "w_out", "w_ple_gate"]
    wmv = {"w_pw": (w_pw, m_w_pw, v_w_pw), "w_br_conv": (w_br_conv, m_w_br_conv, v_w_br_conv),
           "w_br_attn": (w_br_attn, m_w_br_attn, v_w_br_attn), "w_out": (w_out, m_w_out, v_w_out),
           "w_ple_gate": (w_ple_gate, m_w_ple_gate, v_w_ple_gate)}
    for nm, parts in zip(names_rows, r_rows):
        w_, m_, v_ = wmv[nm]
        res[nm] = _adamw(parts, w_[0], m_[0], v_[0], "adamw_" + nm)
    res["w_ple_proj"] = _adamw(r_pp, w_ple_proj[0], m_w_ple_proj[0], v_w_ple_proj[0], "adamw_w_ple_proj")
    pad_dw = lambda a: jnp.pad(a[0], ((0, CONV_HALO - CONV_K), (0, 0)))
    res["w_dw"] = [a[:CONV_K] for a in _adamw(r_dw, pad_dw(w_dw), pad_dw(m_w_dw), pad_dw(v_w_dw), "adamw_w_dw")]

    def stack_small(a_pre, a_post, a_b, a_g, a_bb, a_s):
        sk = jnp.pad(a_s, ((0, 0), (0, D - N_HEADS)))
        return jnp.concatenate([a_pre, a_post, a_b, a_g, a_bb, sk, jnp.zeros((2, D), F32)], axis=0)

    small = _adamw(
        r_small, stack_small(ln_pre, ln_post, b_dw, conv_ln_g, conv_ln_b, sinks),
        stack_small(m_ln_pre, m_ln_post, m_b_dw, m_conv_ln_g, m_conv_ln_b, m_sinks),
        stack_small(v_ln_pre, v_ln_post, v_b_dw, v_conv_ln_g, v_conv_ln_b, v_sinks), "adamw_small")
    for j, nm in enumerate(["ln_pre", "ln_post", "b_dw", "conv_ln_g", "conv_ln_b"]):
        res[nm] = [a[j] for a in small]
    res["sinks"] = [a[5, :N_HEADS] for a in small]
    in_own, in_land = _scatter_wait(in_send, in_recv, in_own, in_land, small[0], "w_in_grad_wait")
    me = _slot(*_my_place()).astype(jnp.int32).reshape(1)
    res["w_in"] = [jnp.swapaxes(a, 0, 1) for a in _adamw_own(in_land, in_own, me, w_t, m_t, v_t, "adamw_w_in")]

    order = ["w_in", "ln_pre", "ln_post", "w_dw", "b_dw", "conv_ln_g", "conv_ln_b", "w_pw", "sinks", "w_br_conv",
             "w_br_attn", "w_out", "w_ple_gate", "w_ple_proj"]
    outs = [loss, grad_x2.reshape(NS, S, D)]
    for kind in range(4):
        outs += [res[nm][kind][None] for nm in order]
    return tuple(outs)
```

```python
import functools

import numpy as np

import jax
import jax.numpy as jnp
from jax import lax
from jax.experimental import pallas as pl
from jax.experimental.pallas import tpu as pltpu

F32 = jnp.float32
BF16 = jnp.bfloat16

D = 1024
N_HEADS = 16
N_KV = 2
HEAD_DIM = 64
GROUP = N_HEADS // N_KV
KV_W = N_KV * HEAD_DIM
CONV_K = 31
CONV_HALO = 32
BLK = 128
ROPE_DIM = 16
ROPE_THETA = 500000.0
EPS = 1e-6
PLE = 256
NW = 7 * D + 2 * KV_W
N_DEV = 8

CB_VAL, CB_GLU, CB_CGATE, CB_Q, CB_AGATE, CB_GCONV, CB_GATTN = range(7)
CB_K = 7 * D // KV_W

ADAM_LR, ADAM_B1, ADAM_B2, ADAM_EPS, ADAM_WD, ADAM_STEP = 0.001, 0.9, 0.999, 1e-08, 0.01, 10

VMEM_LIMIT = 56 * 1024 * 1024
ROW_TILE = 256
CONV_TILE = 256


def _cparams(*sem):
    return pltpu.CompilerParams(dimension_semantics=sem if sem else None, vmem_limit_bytes=VMEM_LIMIT)


def _sig(v):
    return 1.0 / (1.0 + jnp.exp(-v))


def _rowsum8(a):
    return a.reshape(a.shape[0] // 8, 8, a.shape[1]).sum(axis=0)


def _dot(a, b):
    return jnp.dot(a, b, preferred_element_type=F32)


def _dot_nt(a, b):
    return lax.dot_general(a, b, (((1,), (1,)), ((), ())), preferred_element_type=F32)


def _dot_tn(a, b):
    return lax.dot_general(a, b, (((0,), (0,)), ((), ())), preferred_element_type=F32)


def _orig_col(zc):
    if zc < 4 * D:
        return zc
    return zc - 7 * D + 4 * D if zc >= 7 * D else zc + 2 * KV_W


def _my_place():
    return lax.axis_index("x"), lax.axis_index("y"), lax.axis_index("c")


def _slot(px, py, pc):
    return 4 * px + 2 * py + pc


class _TwoLevelGather:
    def __init__(self, shards):
        self.arrs = list(shards)
        self.n = len(self.arrs)
        self.out_shape = [jax.ShapeDtypeStruct((N_DEV,) + s.shape, s.dtype) for s in shards]
        self.specs = [pl.BlockSpec(memory_space=pl.ANY)] * self.n
        self.scratch = [pltpu.SemaphoreType.DMA((self.n, 7)), pltpu.SemaphoreType.DMA((self.n, 7)),
                        pltpu.SemaphoreType.DMA((self.n,))]

    def _plan(self, ins, outs, sems):
        send_sems, recv_sems, local_sems = sems
        x, y, c = _my_place()
        me, sibling = (x, y, c), (x, y, 1 - c)
        chips = [(1 - x, y), (x, 1 - y), (1 - x, 1 - y)]

        def copy(a, k, block, to, src=None):
            rows = outs[a].at[_slot(*block)]
            return pltpu.make_async_remote_copy(
                src_ref=rows if src is None else src, dst_ref=rows, send_sem=send_sems.at[a, k],
                recv_sem=recv_sems.at[a, k], device_id=to, device_id_type=pl.DeviceIdType.MESH)

        mine = [pltpu.make_async_copy(ins[a], outs[a].at[_slot(*me)], local_sems.at[a]) for a in range(self.n)]
        first = []
        for a in range(self.n):
            first.append(copy(a, 0, me, sibling, src=ins[a]))
            first += [copy(a, 1 + j, me, (*chip, c), src=ins[a]) for j, chip in enumerate(chips)]
        return copy, mine, first, me, sibling, chips, c

    def start(self, ins, outs, sems):
        _, mine, first, *_ = self._plan(ins, outs, sems)
        for cp in mine + first:
            cp.start()

    def finish(self, ins, outs, sems):
        copy, mine, first, me, sibling, chips, c = self._plan(ins, outs, sems)
        passed = []
        for j, chip in enumerate(chips):
            for a in range(self.n):
                copy(a, 1 + j, (*chip, c), me).wait_recv()
                cp = copy(a, 4 + j, (*chip, c), sibling)
                cp.start()
                passed.append(cp)
        for a in range(self.n):
            copy(a, 0, sibling, me).wait_recv()
            for j, chip in enumerate(chips):
                copy(a, 4 + j, (*chip, 1 - c), me).wait_recv()
        for cp in first + passed:
            cp.wait_send()
        for cp in mine:
            cp.wait()

    def carried(self, refs_in, refs_out, sems, first, last):
        @pl.when(first)
        def _():
            self.start(refs_in, refs_out, sems)

        @pl.when(last)
        def _():
            self.finish(refs_in, refs_out, sems)


class _Exchange:
    def __init__(self, scatter, bcast):
        self.arrs = list(scatter) + list(bcast)
        self.n, self.n_sc = len(self.arrs), len(scatter)
        self.out_shape = [jax.ShapeDtypeStruct(a.shape, a.dtype) for a in scatter]
        self.out_shape += [jax.ShapeDtypeStruct((N_DEV,) + a.shape, a.dtype) for a in bcast]
        self.specs = [pl.BlockSpec(memory_space=pl.ANY)] * self.n
        self.scratch = [pltpu.SemaphoreType.DMA((self.n, 7)), pltpu.SemaphoreType.DMA((self.n, 7)),
                        pltpu.SemaphoreType.DMA((self.n,))]

    def _copies(self, ins, outs, sems):
        send_sems, recv_sems, local_sems = sems
        x, y, c = _my_place()
        me = _slot(x, y, c)
        peers = _peers(x, y, c)
        mine, sends, arrivals = [], [], []
        for a in range(self.n):
            src = ins[a].at[me] if a < self.n_sc else ins[a]
            mine.append(pltpu.make_async_copy(src, outs[a].at[me], local_sems.at[a]))
        for k, peer in enumerate(peers):
            for a in range(self.n):
                src = ins[a].at[_slot(*peer)] if a < self.n_sc else ins[a]
                sends.append(pltpu.make_async_remote_copy(
                    src_ref=src, dst_ref=outs[a].at[me], send_sem=send_sems.at[a, k], recv_sem=recv_sems.at[a, k],
                    device_id=peer, device_id_type=pl.DeviceIdType.MESH))
                rows = outs[a].at[_slot(*peer)]
                arrivals.append(pltpu.make_async_remote_copy(
                    src_ref=rows, dst_ref=rows, send_sem=send_sems.at[a, k], recv_sem=recv_sems.at[a, k],
                    device_id=peer, device_id_type=pl.DeviceIdType.MESH))
        return mine, sends, arrivals

    def start(self, ins, outs, sems):
        mine, sends, _ = self._copies(ins, outs, sems)
        for cp in mine + sends:
            cp.start()

    def finish(self, ins, outs, sems):
        mine, sends, arrivals = self._copies(ins, outs, sems)
        for cp in arrivals:
            cp.wait_recv()
        for cp in sends:
            cp.wait_send()
        for cp in mine:
            cp.wait()

    def carried(self, refs_in, refs_out, sems, first, last):
        @pl.when(first)
        def _():
            self.start(refs_in, refs_out, sems)

        @pl.when(last)
        def _():
            self.finish(refs_in, refs_out, sems)

    def alone(self, name):
        n = self.n

        def body(*refs):
            ins, outs, sems = refs[:n], refs[n : 2 * n], refs[2 * n :]
            self.start(ins, outs, sems)
            self.finish(ins, outs, sems)

        return pl.pallas_call(body, name=name, out_shape=self.out_shape, in_specs=self.specs, out_specs=self.specs,
                              scratch_shapes=self.scratch)(*self.arrs)


def _peers(x, y, c):
    return [(1 - x if k & 4 else x, 1 - y if k & 2 else y, 1 - c if k & 1 else c) for k in range(1, N_DEV)]


def _scatter_send(g, name):
    hbm = pl.BlockSpec(memory_space=pltpu.HBM)
    sem = pl.BlockSpec(memory_space=pltpu.SEMAPHORE)

    def body(g_ref, land_ref, send_sems, recv_sems, g_thru, land_thru, token):
        x, y, c = _my_place()
        me = _slot(x, y, c)
        for k, peer in enumerate(_peers(x, y, c)):
            pltpu.make_async_remote_copy(
                src_ref=g_ref.at[_slot(*peer)], dst_ref=land_ref.at[me], send_sem=send_sems.at[k], recv_sem=recv_sems.at[k],
                device_id=peer, device_id_type=pl.DeviceIdType.MESH).start()
        token[...] = jnp.zeros_like(token)

    return pl.pallas_call(
        body, name=name,
        out_shape=(pltpu.SemaphoreType.DMA((N_DEV - 1,)), pltpu.SemaphoreType.DMA((N_DEV - 1,)),
                   pltpu.HBM(g.shape, g.dtype), pltpu.HBM(g.shape, g.dtype), jax.ShapeDtypeStruct((8, 128), F32)),
        in_specs=(hbm, hbm), out_specs=(sem, sem, hbm, hbm, pl.BlockSpec(memory_space=pltpu.VMEM)),
        input_output_aliases={0: 2, 1: 3},
        compiler_params=pltpu.CompilerParams(has_side_effects=pltpu.SideEffectType.DATAFLOW_SIDE_EFFECTING),
    )(pltpu.with_memory_space_constraint(g, pltpu.HBM),
      pltpu.with_memory_space_constraint(lax.empty(g.shape, g.dtype), pltpu.HBM))


def _scatter_wait(send_sems, recv_sems, g_thru, land_thru, after, name):
    hbm = pl.BlockSpec(memory_space=pltpu.HBM)
    sem = pl.BlockSpec(memory_space=pltpu.SEMAPHORE)

    def body(g_ref, land_ref, send_sems, recv_sems, after_ref, g_out, land_out):
        x, y, c = _my_place()
        for k, peer in enumerate(_peers(x, y, c)):
            cp = pltpu.make_async_remote_copy(
                src_ref=g_ref.at[_slot(*peer)], dst_ref=land_ref.at[_slot(*peer)], send_sem=send_sems.at[k],
                recv_sem=recv_sems.at[k], device_id=peer, device_id_type=pl.DeviceIdType.MESH)
            cp.wait_send()
            cp.wait_recv()

    return pl.pallas_call(
        body, name=name,
        out_shape=(pltpu.HBM(g_thru.shape, g_thru.dtype), pltpu.HBM(land_thru.shape, land_thru.dtype)),
        in_specs=(hbm, hbm, sem, sem, pl.BlockSpec(memory_space=pl.ANY)), out_specs=(hbm, hbm),
        input_output_aliases={0: 0, 1: 1},
        compiler_params=pltpu.CompilerParams(has_side_effects=pltpu.SideEffectType.DATAFLOW_SIDE_EFFECTING),
    )(g_thru, land_thru, send_sems, recv_sems, after)


def _host_split(refs, n_in, n_out, n_scratch, ex):
    k = ex.n if ex is not None else 0
    a = n_in
    b = a + k
    c = b + n_out
    d = c + k
    e = d + n_scratch
    return refs[:a], refs[a:b], refs[b:c], refs[c:d], refs[d:e], refs[e:]


def _rope_tables(pos, ex):
    T = pos.shape[0]
    tm = min(1024, T)
    nt = T // tm
    lane = np.arange(128) % HEAD_DIM
    inv = np.power(np.float32(ROPE_THETA), -np.arange(0, ROPE_DIM, 2, dtype=np.float32) / np.float32(ROPE_DIM)).astype(np.float32)
    half = ROPE_DIM // 2
    invf = np.where(lane < ROPE_DIM, inv[lane % half], 0.0).astype(np.float32)[None, :]
    m_a = (lane < half).astype(np.float32)[None, :]
    m_b = ((lane >= half) & (lane < ROPE_DIM)).astype(np.float32)[None, :]

    def body(*refs):
        ins, ex_in, (cos_ref, sa_ref, sb_ref), ex_out, _, ex_sems = _host_split(refs, 4, 3, 0, ex)
        pos_ref, invf_ref, ma_ref, mb_ref = ins
        i = pl.program_id(0)
        ex.carried(ex_in, ex_out, ex_sems, i == 0, i == nt - 1)
        ang = pos_ref[...].astype(F32) * invf_ref[...]
        sn = jnp.sin(ang)
        cos_ref[...] = jnp.cos(ang)
        sa_ref[...] = -sn * ma_ref[...]
        sb_ref[...] = sn * mb_ref[...]

    row = pl.BlockSpec((tm, 128), lambda i: (i, 0))
    cst = pl.BlockSpec((1, 128), lambda i: (0, 0))
    return pl.pallas_call(
        body, name="rope_tables", grid=(nt,), out_shape=[jax.ShapeDtypeStruct((T, 128), F32)] * 3 + ex.out_shape,
        in_specs=[pl.BlockSpec((tm, 1), lambda i: (i, 0)), cst, cst, cst] + ex.specs, out_specs=[row] * 3 + ex.specs,
        scratch_shapes=ex.scratch,
        compiler_params=_cparams("arbitrary"),
    )(pos, jnp.asarray(invf), jnp.asarray(m_a), jnp.asarray(m_b), *ex.arrs)


def _rope(t, cos, sa, sb, sign=1.0):
    parts = []
    for i in range(t.shape[1] // 128):
        ti = t[:, 128 * i : 128 * (i + 1)]
        up = pltpu.roll(ti, 128 - ROPE_DIM // 2, 1)
        dn = pltpu.roll(ti, ROPE_DIM // 2, 1)
        parts.append(ti * cos + sign * (up * sa + dn * sb))
    return parts[0] if len(parts) == 1 else jnp.concatenate(parts, axis=-1)


def _in_proj(x2, ln_pre, w_in, cos, sa, sb, ex):
    T = x2.shape[0]
    tm = min(512, T)
    nt = T // tm
    chunk = D
    kv_cols = 4 * N_KV * KV_W

    def body(*refs):
        ins, ex_in, (z_ref, h_ref, qr_ref, kv_ref), ex_out, _, ex_sems = _host_split(refs, 6, 4, 0, ex)
        x_ref, g_ref, w_ref, cos_ref, sa_ref, sb_ref = ins
        i = pl.program_id(0)
        ex.carried(ex_in, ex_out, ex_sems, i == 0, i == nt - 1)
        xv = x_ref[...]
        r = lax.rsqrt(jnp.mean(xv * xv, axis=-1, keepdims=True) + EPS)
        h = (xv * r * g_ref[...]).astype(BF16)
        h_ref[...] = h
        tabs = cos_ref[...], sa_ref[...], sb_ref[...]
        for c0 in range(0, NW, chunk):
            cw = min(chunk, NW - c0)
            zc = _dot_nt(h, w_ref[_orig_col(c0) : _orig_col(c0) + cw, :])
            z_ref[:, c0 : c0 + cw] = zc.astype(BF16)
            if c0 == CB_Q * D:
                qr_ref[...] = (_rope(zc, *tabs) * (HEAD_DIM ** -0.5)).astype(BF16)
            if c0 == CB_K * KV_W:
                low = lax.broadcasted_iota(jnp.int32, (tm, KV_W), 1) < HEAD_DIM
                for kind, t in ((0, _rope(zc[:, 0:KV_W], *tabs)), (1, zc[:, KV_W:])):
                    swapped = pltpu.roll(t, HEAD_DIM, 1)
                    cols = {(0, 0): jnp.where(low, t, 0.0), (0, 1): jnp.where(low, 0.0, swapped),
                            (1, 0): jnp.where(low, swapped, 0.0), (1, 1): jnp.where(low, 0.0, t)}
                    for (g, par), val in cols.items():
                        c = _kv2_col(kind, g, par)
                        kv_ref[:, c * KV_W : (c + 1) * KV_W] = val.astype(BF16)

    def row(w):
        return pl.BlockSpec((tm, w), lambda i: (i, 0))

    bf = lambda w: jax.ShapeDtypeStruct((T, w), BF16)
    return pl.pallas_call(
        body, name="in_proj", grid=(nt,),
        out_shape=[bf(NW), bf(D), bf(D), bf(kv_cols)] + ex.out_shape,
        in_specs=[row(D), _weight_spec((1, D)), _weight_spec((NW, D)), row(128), row(128), row(128)] + ex.specs,
        out_specs=[row(NW), row(D), row(D), row(kv_cols)] + ex.specs,
        scratch_shapes=ex.scratch,
        compiler_params=_cparams("arbitrary"),
    )(x2, ln_pre, w_in, cos, sa, sb, *ex.arrs)


def _conv_tiles(S):
    tm = min(CONV_TILE, S)
    return tm, S // tm, tm // CONV_HALO


CONV_ROWS_FWD = 32
CONV_ROWS = 16


def _fill_shifted(sh, rows):
    for b in range(1, 8):
        sh[b, 0:rows, :] = sh[0, b : b + rows, :]


def _conv_fwd(z3, w_dw, b_dw, ln_g, ln_b):
    NS, S, _ = z3.shape
    tm, nt, r = _conv_tiles(S)

    def body(val_ref, glu_ref, hval_ref, hglu_ref, w_ref, b_ref, g_ref, bb_ref, c_ref, cs_ref, ush):
        i = pl.program_id(1)
        for r0 in range(0, tm, CONV_ROWS_FWD):
            rows = slice(r0, r0 + CONV_ROWS_FWD)
            ush[0, CONV_HALO + r0 : CONV_HALO + r0 + CONV_ROWS_FWD, :] = (
                val_ref[rows, :].astype(F32) * _sig(glu_ref[rows, :].astype(F32)))
        uh = hval_ref[...].astype(F32) * _sig(hglu_ref[...].astype(F32))
        ush[0, 0:CONV_HALO, :] = jnp.where(i > 0, uh, 0.0)
        _fill_shifted(ush, tm + CONV_HALO - 8)
        for r0 in range(0, tm, CONV_ROWS_FWD):
            rows = slice(r0, r0 + CONV_ROWS_FWD)
            acc = jnp.zeros((CONV_ROWS_FWD, D), F32)
            for k in range(CONV_K):
                a, b = divmod(CONV_HALO - (CONV_K - 1) + k, 8)
                acc = acc + w_ref[k : k + 1, :] * ush[b, r0 + 8 * a : r0 + 8 * a + CONV_ROWS_FWD, :]
            cv = acc + b_ref[...]
            mu = jnp.mean(cv, axis=-1, keepdims=True)
            xc = cv - mu
            var = jnp.mean(xc * xc, axis=-1, keepdims=True)
            cl = xc * lax.rsqrt(var + EPS) * g_ref[...] + bb_ref[...]
            c_ref[rows, :] = cv.astype(BF16)
            cs_ref[rows, :] = (cl * _sig(cl)).astype(BF16)

    def cur(cb):
        return pl.BlockSpec((None, tm, D), lambda s, i: (s, i, cb))

    def halo(cb):
        return pl.BlockSpec((None, CONV_HALO, D), lambda s, i: (s, jnp.maximum(i * r - 1, 0), cb))

    vec = pl.BlockSpec((1, D), lambda s, i: (0, 0))
    out = pl.BlockSpec((None, tm, D), lambda s, i: (s, i, 0))
    return pl.pallas_call(
        body, name="conv_fwd", grid=(NS, nt),
        out_shape=[jax.ShapeDtypeStruct((NS, S, D), BF16)] * 2,
        in_specs=[cur(CB_VAL), cur(CB_GLU), halo(CB_VAL), halo(CB_GLU),
                  pl.BlockSpec((CONV_HALO, D), lambda s, i: (0, 0)), vec, vec, vec],
        out_specs=[out, out],
        scratch_shapes=[pltpu.VMEM((8, tm + CONV_HALO, D), F32)],
        compiler_params=_cparams("parallel", "parallel"),
    )(z3, z3, z3, z3, w_dw, b_dw, ln_g, ln_b)


PAIRS = GROUP // 2
QROWS = PAIRS * BLK


def _kv2_col(kind, g, par):
    return kind * 2 * N_KV + g * 2 + par


def _attn_mask(has_prev):
    qi = lax.broadcasted_iota(jnp.int32, (QROWS, 2 * BLK), 0) & (BLK - 1)
    kj = lax.broadcasted_iota(jnp.int32, (QROWS, 2 * BLK), 1)
    first_key = jnp.where(has_prev, 0, BLK)
    return (kj > qi) & (kj <= qi + BLK) & (kj >= first_key)


NEG_BIG = -1e30


def _attn_specs():
    q = pl.BlockSpec((None, BLK, D), lambda s, n: (s, n, 0))
    kv_cur = pl.BlockSpec((None, BLK, 4 * N_KV * KV_W), lambda s, n: (s, n, 0))
    kv_prev = pl.BlockSpec((None, BLK, 4 * N_KV * KV_W), lambda s, n: (s, jnp.maximum(n - 1, 0), 0))
    sink = pl.BlockSpec(memory_space=pltpu.SMEM)
    return sink, q, kv_cur, kv_prev


def _stack_pairs(ref, g):
    return jnp.concatenate([ref[:, (PAIRS * g + j) * 128 : (PAIRS * g + j + 1) * 128] for j in range(PAIRS)], axis=0)


def _unstack_pairs(ref, g, val):
    for j in range(PAIRS):
        ref[:, (PAIRS * g + j) * 128 : (PAIRS * g + j + 1) * 128] = val[j * BLK : (j + 1) * BLK].astype(ref.dtype)


def _pair_heads(g, par):
    return [GROUP * g + 2 * j + par for j in range(PAIRS)]


def _head_col_load(ref, g, par):
    return jnp.concatenate([ref[:, h : h + 1] for h in _pair_heads(g, par)], axis=0)


def _head_col_store(ref, g, par, col):
    for j, h in enumerate(_pair_heads(g, par)):
        ref[:, h : h + 1] = col[j * BLK : (j + 1) * BLK]


def _sink_col(sink_ref, g, par):
    return jnp.concatenate([jnp.full((BLK, 1), sink_ref[h], F32) for h in _pair_heads(g, par)], axis=0)


def _kv2_block(kvc_ref, kvp_ref, has_prev, c):
    col = slice(c * KV_W, (c + 1) * KV_W)
    prev = jnp.where(has_prev, kvp_ref[:, col], jnp.zeros((BLK, KV_W), BF16))
    return jnp.concatenate([prev, kvc_ref[:, col]], axis=0)


def _attn_fwd(qr3, kv3, sinks):
    NS, S, _ = qr3.shape

    def body(sink_ref, q_ref, kvc_ref, kvp_ref, o_ref, lse_ref):
        has_prev = pl.program_id(1) > 0
        mask = _attn_mask(has_prev)[0:BLK]
        for g in range(N_KV):
            kv = [[_kv2_block(kvc_ref, kvp_ref, has_prev, _kv2_col(kind, g, par)) for par in range(2)] for kind in range(2)]
            for j in range(PAIRS):
                cols = slice((PAIRS * g + j) * 128, (PAIRS * g + j + 1) * 128)
                q2 = q_ref[:, cols]
                o_pair = None
                for par in range(2):
                    h = GROUP * g + 2 * j + par
                    s = jnp.where(mask, _dot_nt(q2, kv[0][par]), NEG_BIG)
                    sk = sink_ref[h]
                    mx = jnp.maximum(jnp.max(s, axis=-1, keepdims=True), sk)
                    e = jnp.exp(s - mx)
                    den = jnp.sum(e, axis=-1, keepdims=True) + jnp.exp(sk - mx)
                    pv = _dot(e.astype(BF16), kv[1][par]) * (1.0 / den)
                    o_pair = pv if o_pair is None else o_pair + pv
                    lse_ref[:, h : h + 1] = mx + jnp.log(den)
                o_ref[:, cols] = o_pair.astype(BF16)

    return pl.pallas_call(
        body, name="attn_fwd", grid=(NS, S // BLK),
        out_shape=[jax.ShapeDtypeStruct((NS, S, D), BF16), jax.ShapeDtypeStruct((NS, S, N_HEADS), F32)],
        in_specs=list(_attn_specs()),
        out_specs=[pl.BlockSpec((None, BLK, D), lambda s, n: (s, n, 0)),
                   pl.BlockSpec((None, BLK, N_HEADS), lambda s, n: (s, n, 0))],
        compiler_params=_cparams("parallel", "parallel"),
    )(sinks, qr3, kv3, kv3)


def _weight_spec(shape):
    return pl.BlockSpec(shape, lambda i: (0,) * len(shape), pipeline_mode=pl.Buffered(1))


def _dsilu(v, s):
    return s * (1.0 + v * (1.0 - s))


MID_TILE = 256


def _mid(cs, o, z2, x2, p2, tgt, ln_post, weights):
    T = cs.shape[0]
    tm = min(MID_TILE, T)
    nt = T // tm
    n_bf = 16

    def body(cs_ref, o_ref, cg_ref, ag_ref, gc_ref, ga_ref, x_ref, p_ref, t_ref, g_ref,
             wpw, wbrc, wbra, wout, wpg, wpp,
             dya0_ref, ya1_ref, dya_ref, yb0_ref, dyb_ref, m_ref, dmo_ref, x1_ref, dgl_ref, dpp_ref,
             dcs_ref, do_ref, dcg_ref, dag_ref, dgc_ref, dga_ref, dx1_ref, loss_ref, glp_ref, lacc):
        i = pl.program_id(0)

        @pl.when(i == 0)
        def _():
            lacc[...] = jnp.zeros_like(lacc)
            glp_ref[...] = jnp.zeros_like(glp_ref)

        cg = cg_ref[...].astype(F32)
        scg = _sig(cg)
        silu_c = cg * scg
        ya0 = _dot(cs_ref[...], wpw[...])
        ya1 = (ya0 * silu_c).astype(BF16)
        ya1_ref[...] = ya1
        ya = _dot(ya1, wbrc[...])
        ag = ag_ref[...].astype(F32)
        sag = _sig(ag)
        silu_a = ag * sag
        ov = o_ref[...].astype(F32)
        yb0 = (ov * silu_a).astype(BF16)
        yb0_ref[...] = yb0
        yb = _dot(yb0, wbra[...])
        sgc = _sig(gc_ref[...].astype(F32))
        sga = _sig(ga_ref[...].astype(F32))
        mb = (sgc * ya + sga * yb).astype(BF16)
        m_ref[...] = mb
        mo = _dot(mb, wout[...])
        r2 = lax.rsqrt(jnp.mean(mo * mo, axis=-1, keepdims=True) + EPS)
        nrm = mo * r2
        x1 = x_ref[...] + nrm * g_ref[...]
        x1b = x1.astype(BF16)
        x1_ref[...] = x1b
        gate = _sig(_dot(x1b, wpg[...]))
        pp = _dot(p_ref[...].astype(BF16), wpp[...])
        e = x1 + gate * pp - t_ref[...]
        lacc[...] += _rowsum8(e * e)
        dy = e * (1.0 / D)

        dgl = (dy * pp * gate * (1.0 - gate)).astype(BF16)
        dgl_ref[...] = dgl
        dpp_ref[...] = (dy * gate).astype(BF16)
        dx1 = dy + _dot_nt(dgl, wpg[...])
        dx1_ref[...] = dx1
        glp_ref[...] += _rowsum8(dx1 * nrm)
        dn = dx1 * g_ref[...]
        dmo = (r2 * (dn - nrm * jnp.mean(dn * nrm, axis=-1, keepdims=True))).astype(BF16)
        dmo_ref[...] = dmo
        dm = _dot_nt(dmo, wout[...])
        dya = (dm * sgc).astype(BF16)
        dyb = (dm * sga).astype(BF16)
        dya_ref[...] = dya
        dyb_ref[...] = dyb
        dgc_ref[...] = (dm * ya * sgc * (1.0 - sgc)).astype(BF16)
        dga_ref[...] = (dm * yb * sga * (1.0 - sga)).astype(BF16)
        dya1 = _dot_nt(dya, wbrc[...])
        dya0 = (dya1 * silu_c).astype(BF16)
        dya0_ref[...] = dya0
        dcg_ref[...] = (dya1 * ya0 * _dsilu(cg, scg)).astype(BF16)
        dcs_ref[...] = _dot_nt(dya0, wpw[...]).astype(BF16)
        dyb0 = _dot_nt(dyb, wbra[...])
        do_ref[...] = (dyb0 * silu_a).astype(BF16)
        dag_ref[...] = (dyb0 * ov * _dsilu(ag, sag)).astype(BF16)

        @pl.when(i == nt - 1)
        def _():
            loss_ref[...] = jnp.full(loss_ref.shape, jnp.sum(lacc[...]) * (0.5 / D), F32)

    row = pl.BlockSpec((tm, D), lambda i: (i, 0))

    def zcol(cb):
        return pl.BlockSpec((tm, D), lambda i: (i, cb))

    bf = jax.ShapeDtypeStruct((T, D), BF16)
    return pl.pallas_call(
        body, name="mid_fwd_bwd", grid=(nt,),
        out_shape=[bf] * n_bf + [jax.ShapeDtypeStruct((T, D), F32), jax.ShapeDtypeStruct((8, 128), F32),
                                 jax.ShapeDtypeStruct((8, D), F32)],
        in_specs=[row, row, zcol(CB_CGATE), zcol(CB_AGATE), zcol(CB_GCONV), zcol(CB_GATTN), row,
                  pl.BlockSpec((tm, PLE), lambda i: (i, 0)), row, _weight_spec((1, D))]
        + [_weight_spec((D, D))] * 5 + [_weight_spec((PLE, D))],
        out_specs=[row] * (n_bf + 1) + [pl.BlockSpec((8, 128), lambda i: (0, 0)), pl.BlockSpec((8, D), lambda i: (0, 0))],
        scratch_shapes=[pltpu.VMEM((8, D), F32)],
        compiler_params=_cparams("arbitrary"),
    )(cs, o, z2, z2, z2, z2, x2, p2, tgt, ln_post, *weights)


def _conv_bwd(z3, c3, dcs3, w_dw, ln_g, ln_b, ex):
    NS, S, _ = z3.shape
    tm, nt, r = _conv_tiles(S)

    def body(*refs):
        ins, ex_in, outs, ex_out, scratch, ex_sems = _host_split(refs, 9, 3, 3, ex)
        val_ref, glu_ref, c_ref, dcs_ref, hc_ref, hdcs_ref, w_ref, g_ref, bb_ref = ins
        dz_ref, gw_ref, gvec_ref = outs
        dsh, ubuf, dubuf = scratch
        i = pl.program_id(1)
        first = (pl.program_id(0) == 0) & (i == 0)
        ex.carried(ex_in, ex_out, ex_sems, first, (pl.program_id(0) == NS - 1) & (i == nt - 1))

        @pl.when(first)
        def _():
            gw_ref[...] = jnp.zeros_like(gw_ref)
            gvec_ref[...] = jnp.zeros_like(gvec_ref)

        val = val_ref[...].astype(F32)
        sg = _sig(glu_ref[...].astype(F32))
        ubuf[...] = val * sg

        def ln_bwd(cv, dcs):
            cv = cv.astype(F32)
            mu = jnp.mean(cv, axis=-1, keepdims=True)
            xc = cv - mu
            rstd = lax.rsqrt(jnp.mean(xc * xc, axis=-1, keepdims=True) + EPS)
            xhat = xc * rstd
            cl = xhat * g_ref[...] + bb_ref[...]
            s = _sig(cl)
            dcl = dcs.astype(F32) * _dsilu(cl, s)
            dxh = dcl * g_ref[...]
            dc = rstd * (dxh - jnp.mean(dxh, axis=-1, keepdims=True) - xhat * jnp.mean(dxh * xhat, axis=-1, keepdims=True))
            return dc, dcl, xhat

        dc, dcl, xhat = ln_bwd(c_ref[...], dcs_ref[...])
        dsh[0, 0:tm, :] = dc
        dch, _, _ = ln_bwd(hc_ref[...], hdcs_ref[...])
        dsh[0, tm:, :] = jnp.where(i < nt - 1, dch, 0.0)
        gvec_ref[0:8, :] += _rowsum8(dcl * xhat)
        gvec_ref[8:16, :] += _rowsum8(dcl)
        gvec_ref[16:24, :] += _rowsum8(dc)
        _fill_shifted(dsh, tm + CONV_HALO - 8)

        def dc_ahead(r0, k):
            a, b = divmod(CONV_K - 1 - k, 8)
            return dsh[b, r0 + 8 * a : r0 + 8 * a + CONV_ROWS, :]

        for r0 in range(0, tm, CONV_ROWS):
            acc = jnp.zeros((CONV_ROWS, D), F32)
            for k in range(CONV_K):
                acc = acc + w_ref[k : k + 1, :] * dc_ahead(r0, k)
            dubuf[r0 : r0 + CONV_ROWS, :] = acc
        for r0 in range(0, tm, CONV_ROWS):
            ur = ubuf[r0 : r0 + CONV_ROWS, :]
            for k in range(CONV_K):
                gw_ref[8 * k : 8 * k + 8, :] += _rowsum8(ur * dc_ahead(r0, k))
        du = dubuf[...]
        dz_ref[:, 0:D] = (du * sg).astype(BF16)
        dz_ref[:, D:] = (du * val * sg * (1.0 - sg)).astype(BF16)

    def cur(cb):
        return pl.BlockSpec((None, tm, D), lambda s, i: (s, i, cb))

    nxt = pl.BlockSpec((None, CONV_HALO, D), lambda s, i: (s, jnp.minimum((i + 1) * r, S // CONV_HALO - 1), 0))
    vec = pl.BlockSpec((1, D), lambda s, i: (0, 0))
    return pl.pallas_call(
        body, name="conv_bwd", grid=(NS, nt),
        out_shape=[jax.ShapeDtypeStruct((NS, S, 2 * D), BF16), jax.ShapeDtypeStruct((CONV_HALO * 8, D), F32),
                   jax.ShapeDtypeStruct((24, D), F32)] + ex.out_shape,
        in_specs=[cur(CB_VAL), cur(CB_GLU), cur(0), cur(0), nxt, nxt,
                  pl.BlockSpec((CONV_HALO, D), lambda s, i: (0, 0)), vec, vec] + ex.specs,
        out_specs=[pl.BlockSpec((None, tm, 2 * D), lambda s, i: (s, i, 0)),
                   pl.BlockSpec((CONV_HALO * 8, D), lambda s, i: (0, 0)), pl.BlockSpec((24, D), lambda s, i: (0, 0))] + ex.specs,
        scratch_shapes=[pltpu.VMEM((8, tm + CONV_HALO, D), F32), pltpu.VMEM((tm, D), F32), pltpu.VMEM((tm, D), F32)] + ex.scratch,
        compiler_params=_cparams("arbitrary", "arbitrary"),
    )(z3, z3, c3, dcs3, c3, dcs3, w_dw, ln_g, ln_b, *ex.arrs)


def _attn_bwd(qr3, kv3, do3, o3, lse3, sinks, cos3, sa3, sb3):
    NS, S, _ = qr3.shape

    def body(sink_ref, q_ref, kvc_ref, kvp_ref, do_ref, o_ref, lse_ref, cos_ref, sa_ref, sb_ref,
             dq_ref, dkc_ref, dkp_ref, dvc_ref, dvp_ref, dsk_ref):
        has_prev = pl.program_id(1) > 0
        mask = _attn_mask(has_prev)
        tabs = cos_ref[...], sa_ref[...], sb_ref[...]
        low_q = lax.broadcasted_iota(jnp.int32, (QROWS, 128), 1) < HEAD_DIM
        dk_g, dv_g = [], []
        for g in range(N_KV):
            qs = _stack_pairs(q_ref, g)
            dos = _stack_pairs(do_ref, g)
            prod = dos.astype(F32) * _stack_pairs(o_ref, g).astype(F32)
            deltas = [jnp.sum(jnp.where(low_q, prod, 0.0), axis=-1, keepdims=True),
                      jnp.sum(jnp.where(low_q, 0.0, prod), axis=-1, keepdims=True)]
            dq_acc, dk_par, dv_par = None, [], []
            for par in range(2):
                k2 = _kv2_block(kvc_ref, kvp_ref, has_prev, _kv2_col(0, g, par))
                v2 = _kv2_block(kvc_ref, kvp_ref, has_prev, _kv2_col(1, g, par))
                lse = _head_col_load(lse_ref, g, par)
                p = jnp.exp(jnp.where(mask, _dot_nt(qs, k2), NEG_BIG) - lse)
                ds = (p * (_dot_nt(dos, v2) - deltas[par])).astype(BF16)
                dq = _dot(ds, k2)
                dq_acc = dq if dq_acc is None else dq_acc + dq
                dk_par.append(_dot_tn(ds, qs))
                dv_par.append(_dot_tn(p.astype(BF16), dos))
                _head_col_store(dsk_ref, g, par, -jnp.exp(_sink_col(sink_ref, g, par) - lse) * deltas[par])
            for j in range(PAIRS):
                dq_pair = _rope(dq_acc[j * BLK : (j + 1) * BLK], *tabs, sign=-1.0) * (HEAD_DIM ** -0.5)
                dq_ref[:, (PAIRS * g + j) * 128 : (PAIRS * g + j + 1) * 128] = dq_pair.astype(BF16)
            dk_g.append(dk_par[0] + pltpu.roll(dk_par[1], HEAD_DIM, 1))
            dv_g.append(dv_par[0] + pltpu.roll(dv_par[1], HEAD_DIM, 1))
        low_k = lax.broadcasted_iota(jnp.int32, (2 * BLK, KV_W), 1) < HEAD_DIM
        dk = jnp.where(low_k, dk_g[0], pltpu.roll(dk_g[1], HEAD_DIM, 1))
        dv = jnp.where(low_k, dv_g[0], pltpu.roll(dv_g[1], HEAD_DIM, 1))
        dkp_ref[...] = dk[0:BLK]
        dkc_ref[...] = dk[BLK:]
        dvp_ref[...] = dv[0:BLK]
        dvc_ref[...] = dv[BLK:]

    qspec = pl.BlockSpec((None, BLK, D), lambda s, n: (s, n, 0))
    kvspec = pl.BlockSpec((None, BLK, KV_W), lambda s, n: (s, n, 0))
    hspec = pl.BlockSpec((None, BLK, N_HEADS), lambda s, n: (s, n, 0))
    kv = jax.ShapeDtypeStruct((NS, S, KV_W), F32)
    return pl.pallas_call(
        body, name="attn_bwd", grid=(NS, S // BLK),
        out_shape=[jax.ShapeDtypeStruct((NS, S, D), BF16), kv, kv, kv, kv, jax.ShapeDtypeStruct((NS, S, N_HEADS), F32)],
        in_specs=list(_attn_specs()) + [qspec, qspec, hspec, kvspec, kvspec, kvspec],
        out_specs=[qspec, kvspec, kvspec, kvspec, kvspec, hspec],
        compiler_params=_cparams("parallel", "parallel"),
    )(sinks, qr3, kv3, kv3, do3, o3, lse3, cos3, sa3, sb3)


def _attn_post(dkc3, dkp3, dvc3, dvp3, cos3, sa3, sb3):
    NS, S, _ = dkc3.shape
    tm = min(4 * BLK, S)
    nt = S // tm

    def body(dkc_ref, dkp_ref, dkn_ref, dvc_ref, dvp_ref, dvn_ref, cos_ref, sa_ref, sb_ref, dkv_ref):
        has_next = pl.program_id(1) < nt - 1

        def join(cur_ref, prev_ref, next_ref):
            ahead = jnp.where(has_next, next_ref[...], 0.0)
            shifted = ahead if tm == BLK else jnp.concatenate([prev_ref[BLK:, :], ahead], axis=0)
            return cur_ref[...] + shifted

        tabs = cos_ref[...], sa_ref[...], sb_ref[...]
        dkv_ref[:, 0:KV_W] = _rope(join(dkc_ref, dkp_ref, dkn_ref), *tabs, sign=-1.0).astype(BF16)
        dkv_ref[:, KV_W:] = join(dvc_ref, dvp_ref, dvn_ref).astype(BF16)

    cur = pl.BlockSpec((None, tm, KV_W), lambda s, j: (s, j, 0))
    nxt = pl.BlockSpec((None, BLK, KV_W), lambda s, j: (s, jnp.minimum((j + 1) * (tm // BLK), S // BLK - 1), 0))
    return pl.pallas_call(
        body, name="attn_post", grid=(NS, nt),
        out_shape=jax.ShapeDtypeStruct((NS, S, 2 * KV_W), BF16),
        in_specs=[cur, cur, nxt, cur, cur, nxt, cur, cur, cur],
        out_specs=pl.BlockSpec((None, tm, 2 * KV_W), lambda s, j: (s, j, 0)),
        compiler_params=_cparams("parallel", "parallel"),
    )(dkc3, dkp3, dkp3, dvc3, dvp3, dvp3, cos3, sa3, sb3)


def _in_bwd(segs, w_in_t, x2, dx1, ln_pre, after):
    T = x2.shape[0]
    tm = min(ROW_TILE, T)
    nt = T // tm
    ns = len(segs)
    widths = [s.shape[1] for s in segs]

    def body(*refs):
        seg_refs = refs[:ns]
        w_ref, x_ref, dx1_ref, g_ref, _, gx_ref, glp_ref = refs[ns:]
        i = pl.program_id(0)

        @pl.when(i == 0)
        def _():
            glp_ref[...] = jnp.zeros_like(glp_ref)

        dh = None
        zc = 0
        for sref, w in zip(seg_refs, widths):
            part = _dot(sref[...], w_ref[_orig_col(zc) : _orig_col(zc) + w, :])
            dh = part if dh is None else dh + part
            zc += w
        xv = x_ref[...]
        r1 = lax.rsqrt(jnp.mean(xv * xv, axis=-1, keepdims=True) + EPS)
        xhat = xv * r1
        glp_ref[...] += _rowsum8(dh * xhat)
        dhg = dh * g_ref[...]
        gx_ref[...] = dx1_ref[...] + r1 * (dhg - xhat * jnp.mean(dhg * xhat, axis=-1, keepdims=True))

    row = pl.BlockSpec((tm, D), lambda i: (i, 0))
    return pl.pallas_call(
        body, name="in_bwd", grid=(nt,),
        out_shape=[jax.ShapeDtypeStruct((T, D), F32), jax.ShapeDtypeStruct((8, D), F32)],
        in_specs=[pl.BlockSpec((tm, w), lambda i: (i, 0)) for w in widths]
        + [_weight_spec((NW, D)), row, row, _weight_spec((1, D)), pl.BlockSpec(memory_space=pl.ANY)],
        out_specs=[row, pl.BlockSpec((8, D), lambda i: (0, 0))],
        compiler_params=_cparams("arbitrary"),
    )(*segs, w_in_t, x2, dx1, ln_pre, after)


def _grad_matmul(a, b, name, rows=None, into=None):
    T, M = a.shape
    N = b.shape[1]
    tk = min(1024, T)
    nk = T // tk

    def body(a_ref, b_ref, *rest):
        o_ref, acc = rest[-2:]
        k = pl.program_id(0)

        @pl.when(k == 0)
        def _():
            acc[...] = jnp.zeros_like(acc)

        acc[...] += _dot_tn(a_ref[...].astype(BF16), b_ref[...])

        @pl.when(k == nk - 1)
        def _():
            o_ref[...] = acc[...].astype(BF16)

    in_specs = [pl.BlockSpec((tk, M), lambda k: (k, 0)), pl.BlockSpec((tk, N), lambda k: (k, 0))]
    if rows is None:
        out_shape, out_spec = (M, N), pl.BlockSpec((M, N), lambda k: (0, 0))
    else:
        out_shape, out_spec = (rows[1], N), pl.BlockSpec((pl.Element(M), pl.Element(N)), lambda k: (rows[0], 0))
    operands = (a, b) if into is None else (a, b, into)
    return pl.pallas_call(
        body, name=name, grid=(nk,), out_shape=jax.ShapeDtypeStruct(out_shape, BF16),
        in_specs=in_specs if into is None else in_specs + [pl.BlockSpec(memory_space=pl.ANY)],
        out_specs=out_spec, input_output_aliases={} if into is None else {2: 0},
        scratch_shapes=[pltpu.VMEM((M, N), F32)],
        compiler_params=_cparams("arbitrary"),
    )(*operands)


def _pack_small(gw, gvec, glp_pre, glp_post, dsk):
    T = dsk.shape[0]

    def body(gw_ref, gvec_ref, pre_ref, post_ref, dsk_ref, gdw_ref, gs_ref):
        gwf = gw_ref[...].reshape(CONV_HALO, 8, D).sum(axis=1)
        for d in range(N_DEV):
            gdw_ref[d] = gwf[:, 128 * d : 128 * (d + 1)]
        gs_ref[...] = jnp.zeros_like(gs_ref)
        gs_ref[0:1, :] = jnp.sum(pre_ref[...], axis=0, keepdims=True)
        gs_ref[1:2, :] = jnp.sum(post_ref[...], axis=0, keepdims=True)
        gs_ref[2:3, :] = jnp.sum(gvec_ref[16:24, :], axis=0, keepdims=True)
        gs_ref[3:4, :] = jnp.sum(gvec_ref[0:8, :], axis=0, keepdims=True)
        gs_ref[4:5, :] = jnp.sum(gvec_ref[8:16, :], axis=0, keepdims=True)
        gs_ref[5:6, 0:N_HEADS] = jnp.sum(dsk_ref[...], axis=0, keepdims=True)

    return pl.pallas_call(
        body, name="pack_small",
        out_shape=[jax.ShapeDtypeStruct((N_DEV, CONV_HALO, 128), F32), jax.ShapeDtypeStruct((8, D), F32)],
        compiler_params=_cparams(),
    )(gw, gvec, glp_pre, glp_post, dsk)


def _adam_update(g, w_ref, m_ref, v_ref, g_ref, d_ref, nm_ref, nv_ref):
    nm = ADAM_B1 * m_ref[...] + (1.0 - ADAM_B1) * g
    nv = ADAM_B2 * v_ref[...] + (1.0 - ADAM_B2) * (g * g)
    m_hat = nm / (1.0 - ADAM_B1 ** ADAM_STEP)
    v_hat = nv / (1.0 - ADAM_B2 ** ADAM_STEP)
    g_ref[...] = g
    d_ref[...] = -ADAM_LR * (m_hat / (jnp.sqrt(v_hat) + ADAM_EPS) + ADAM_WD * w_ref[...])
    nm_ref[...] = nm
    nv_ref[...] = nv


def _adamw(parts, w, m, v, name):
    R, C = w.shape
    tr = R if R <= 256 else 128

    def body(p_ref, *rest):
        g = p_ref[0].astype(F32)
        for s in range(1, N_DEV):
            g = g + p_ref[s].astype(F32)
        _adam_update(g, *rest)

    blk = pl.BlockSpec((tr, C), lambda i: (i, 0))
    return pl.pallas_call(
        body, name=name, grid=(R // tr,), out_shape=[jax.ShapeDtypeStruct((R, C), F32)] * 4,
        in_specs=[pl.BlockSpec((N_DEV, tr, C), lambda i: (0, i, 0)), blk, blk, blk], out_specs=[blk] * 4,
        compiler_params=_cparams("parallel"),
    )(parts, w, m, v)


def _adamw_own(parts, own, me, w, m, v, name):
    R, C = w.shape
    tr = max(t for t in range(16, 513, 16) if R % t == 0)

    def body(me_ref, p_ref, own_ref, *rest):
        g = None
        for s in range(N_DEV):
            part = jnp.where(me_ref[0] == s, own_ref[...], p_ref[s]).astype(F32)
            g = part if g is None else g + part
        _adam_update(g, *rest)

    blk = pl.BlockSpec((tr, C), lambda i, me: (i, 0))
    return pl.pallas_call(
        body, name=name, out_shape=[jax.ShapeDtypeStruct((R, C), F32)] * 4,
        grid_spec=pltpu.PrefetchScalarGridSpec(
            num_scalar_prefetch=1, grid=(R // tr,),
            in_specs=[pl.BlockSpec((N_DEV, tr, C), lambda i, me: (0, i, 0)),
                      pl.BlockSpec((None, tr, C), lambda i, me: (me[0], i, 0)), blk, blk, blk],
            out_specs=[blk] * 4),
        compiler_params=_cparams("parallel"),
    )(me, parts, own, w, m, v)


def kernel(x, p, positions, w_in, ln_pre, ln_post, w_dw, b_dw, conv_ln_g, conv_ln_b, w_pw, sinks, w_br_conv, w_br_attn, w_out, w_ple_gate, w_ple_proj, loss_target, m_w_in, m_ln_pre, m_ln_post, m_w_dw, m_b_dw, m_conv_ln_g, m_conv_ln_b, m_w_pw, m_sinks, m_w_br_conv, m_w_br_attn, m_w_out, m_w_ple_gate, m_w_ple_proj, v_w_in, v_ln_pre, v_ln_post, v_w_dw, v_b_dw, v_conv_ln_g, v_conv_ln_b, v_w_pw, v_sinks, v_w_br_conv, v_w_br_attn, v_w_out, v_w_ple_gate, v_w_ple_proj):
    NS, S, _ = x.shape
    T = NS * S
    x2 = x.reshape(T, D)
    p2 = p.reshape(T, PLE)
    tgt = loss_target.reshape(T, D)
    pos = positions.reshape(T, 1)

    row_sharded = [w_pw[0], w_br_conv[0], w_br_attn[0], w_out[0], w_ple_gate[0]]
    sh_rows = D // N_DEV
    w_t, m_t, v_t = (jnp.swapaxes(a[0], 0, 1) for a in (w_in, m_w_in, v_w_in))
    cos, sa, sb, g_in = _rope_tables(pos, _TwoLevelGather([w_t.astype(BF16)]))
    w_in_f = g_in.reshape(NW, D)
    gather_rest = _Exchange([], [w.astype(BF16) for w in row_sharded] + [
        w_ple_proj[0].astype(BF16), jnp.pad(w_dw[0], ((0, CONV_HALO - CONV_K), (0, 0)))])

    in_out = _in_proj(x2, ln_pre, w_in_f, cos, sa, sb, gather_rest)
    z2, h, qr, kv2 = in_out[:4]
    g_rows, g_pp, g_dw = in_out[4:9], in_out[9], in_out[10]
    full = [g.reshape(D, D) for g in g_rows]
    w_pp_f = g_pp.transpose(1, 0, 2).reshape(PLE, D)
    w_dw_f = g_dw.transpose(1, 0, 2).reshape(CONV_HALO, D)
    z3 = z2.reshape(NS, S, NW)
    c3, cs3 = _conv_fwd(z3, w_dw_f, b_dw, conv_ln_g, conv_ln_b)
    qr3, kv3 = qr.reshape(NS, S, D), kv2.reshape(NS, S, 4 * N_KV * KV_W)
    sinks1 = sinks.reshape(N_HEADS)
    o3, lse3 = _attn_fwd(qr3, kv3, sinks1)
    o = o3.reshape(T, D)
    cs = cs3.reshape(T, D)
    (dya0, ya1, dya, yb0, dyb, m, dmo, x1, dgl, dpp, dcs, do, dcg, dag, dgc, dga, dx1, loss_blk, glp_post) = _mid(
        cs, o, z2, x2, p2, tgt, ln_post, full + [w_pp_f])

    gp_rows = [_grad_matmul(a, b, nm).reshape(N_DEV, sh_rows, D) for a, b, nm in (
        (cs, dya0, "grad_w_pw"), (ya1, dya, "grad_w_br_conv"), (yb0, dyb, "grad_w_br_attn"),
        (m, dmo, "grad_w_out"), (x1, dgl, "grad_w_ple_gate"))]
    gp_pp = _grad_matmul(p2, dpp, "grad_w_ple_proj").reshape(PLE, N_DEV, D // N_DEV).transpose(1, 0, 2)
    conv_out = _conv_bwd(z3, c3, dcs.reshape(NS, S, D), w_dw_f, conv_ln_g, conv_ln_b,
                         _Exchange(gp_rows + [gp_pp], [loss_blk]))
    dzvu3, gw, gvec, r_rows, r_pp, r_loss = conv_out[0], conv_out[1], conv_out[2], conv_out[3:8], conv_out[8], conv_out[9]
    loss = jnp.sum(r_loss[:, 0, 0])
    tab3 = [t.reshape(NS, S, 128) for t in (cos, sa, sb)]
    dq3, dkc3, dkp3, dvc3, dvp3, dsk3 = _attn_bwd(qr3, kv3, do.reshape(NS, S, D), o3, lse3, sinks1, *tab3)
    dkv3 = _attn_post(dkc3, dkp3, dvc3, dvp3, *tab3)
    segs = [dzvu3.reshape(T, 2 * D), dcg, dq3.reshape(T, D), dag, dgc, dga, dkv3.reshape(T, 2 * KV_W)]
    gp_in, zc = None, 0
    for j, s in enumerate(segs):
        gp_in = _grad_matmul(s, h, f"grad_w_in_{j}", rows=(_orig_col(zc), NW), into=gp_in)
        zc += s.shape[1]
    gp_in = gp_in.reshape(N_DEV, NW // N_DEV, D)
    in_send, in_recv, in_own, in_land, sent = _scatter_send(gp_in, "w_in_grad_send")
    grad_x2, glp_pre = _in_bwd(segs, w_in_f, x2, dx1, ln_pre, sent)
    gp_dw, gp_small = _pack_small(gw, gvec, glp_pre, glp_post, dsk3.reshape(T, N_HEADS))
    r_dw, r_small = _Exchange([gp_dw], [gp_small]).alone("small_grad_exchange")

    res = {}
    names_rows = ["w_pw", "w_br_conv", "w_br_attn", "w_out", "w_ple_gate"]
    wmv = {"w_pw": (w_pw, m_w_pw, v_w_pw), "w_br_conv": (w_br_conv, m_w_br_conv, v_w_br_conv),
           "w_br_attn": (w_br_attn, m_w_br_attn, v_w_br_attn), "w_out": (w_out, m_w_out, v_w_out),
           "w_ple_gate": (w_ple_gate, m_w_ple_gate, v_w_ple_gate)}
    for nm, parts in zip(names_rows, r_rows):
        w_, m_, v_ = wmv[nm]
        res[nm] = _adamw(parts, w_[0], m_[0], v_[0], "adamw_" + nm)
    res["w_ple_proj"] = _adamw(r_pp, w_ple_proj[0], m_w_ple_proj[0], v_w_ple_proj[0], "adamw_w_ple_proj")
    pad_dw = lambda a: jnp.pad(a[0], ((0, CONV_HALO - CONV_K), (0, 0)))
    res["w_dw"] = [a[:CONV_K] for a in _adamw(r_dw, pad_dw(w_dw), pad_dw(m_w_dw), pad_dw(v_w_dw), "adamw_w_dw")]

    def stack_small(a_pre, a_post, a_b, a_g, a_bb, a_s):
        sk = jnp.pad(a_s, ((0, 0), (0, D - N_HEADS)))
        return jnp.concatenate([a_pre, a_post, a_b, a_g, a_bb, sk, jnp.zeros((2, D), F32)], axis=0)

    small = _adamw(
        r_small, stack_small(ln_pre, ln_post, b_dw, conv_ln_g, conv_ln_b, sinks),
        stack_small(m_ln_pre, m_ln_post, m_b_dw, m_conv_ln_g, m_conv_ln_b, m_sinks),
        stack_small(v_ln_pre, v_ln_post, v_b_dw, v_conv_ln_g, v_conv_ln_b, v_sinks), "adamw_small")
    for j, nm in enumerate(["ln_pre", "ln_post", "b_dw", "conv_ln_g", "conv_ln_b"]):
        res[nm] = [a[j] for a in small]
    res["sinks"] = [a[5, :N_HEADS] for a in small]
    in_own, in_land = _scatter_wait(in_send, in_recv, in_own, in_land, small[0], "w_in_grad_wait")
    me = _slot(*_my_place()).astype(jnp.int32).reshape(1)
    res["w_in"] = [jnp.swapaxes(a, 0, 1) for a in _adamw_own(in_land, in_own, me, w_t, m_t, v_t, "adamw_w_in")]

    order = ["w_in", "ln_pre", "ln_post", "w_dw", "b_dw", "conv_ln_g", "conv_ln_b", "w_pw", "sinks", "w_br_conv",
             "w_br_attn", "w_out", "w_ple_gate", "w_ple_proj"]
    outs = [loss, grad_x2.reshape(NS, S, D)]
    for kind in range(4):
        outs += [res[nm][kind][None] for nm in order]
    return tuple(outs)
```

```python
import functools

import numpy as np

import jax
import jax.numpy as jnp
from jax import lax
from jax.experimental import pallas as pl
from jax.experimental.pallas import tpu as pltpu

F32 = jnp.float32
BF16 = jnp.bfloat16

D = 1024
N_HEADS = 16
N_KV = 2
HEAD_DIM = 64
GROUP = N_HEADS // N_KV
KV_W = N_KV * HEAD_DIM
CONV_K = 31
CONV_HALO = 32
BLK = 128
ROPE_DIM = 16
ROPE_THETA = 500000.0
EPS = 1e-6
PLE = 256
NW = 7 * D + 2 * KV_W
N_DEV = 8

CB_VAL, CB_GLU, CB_CGATE, CB_Q, CB_AGATE, CB_GCONV, CB_GATTN = range(7)
CB_K = 7 * D // KV_W

ADAM_LR, ADAM_B1, ADAM_B2, ADAM_EPS, ADAM_WD, ADAM_STEP = 0.001, 0.9, 0.999, 1e-08, 0.01, 10

VMEM_LIMIT = 56 * 1024 * 1024
ROW_TILE = 256
CONV_TILE = 256


def _cparams(*sem):
    return pltpu.CompilerParams(dimension_semantics=sem if sem else None, vmem_limit_bytes=VMEM_LIMIT)


def _sig(v):
    return 1.0 / (1.0 + jnp.exp(-v))


def _rowsum8(a):
    return a.reshape(a.shape[0] // 8, 8, a.shape[1]).sum(axis=0)


def _dot(a, b):
    return jnp.dot(a, b, preferred_element_type=F32)


def _dot_nt(a, b):
    return lax.dot_general(a, b, (((1,), (1,)), ((), ())), preferred_element_type=F32)


def _dot_tn(a, b):
    return lax.dot_general(a, b, (((0,), (0,)), ((), ())), preferred_element_type=F32)


def _orig_col(zc):
    if zc < 4 * D:
        return zc
    return zc - 7 * D + 4 * D if zc >= 7 * D else zc + 2 * KV_W


def _my_place():
    return lax.axis_index("x"), lax.axis_index("y"), lax.axis_index("c")


def _slot(px, py, pc):
    return 4 * px + 2 * py + pc


class _TwoLevelGather:
    def __init__(self, shards):
        self.arrs = list(shards)
        self.n = len(self.arrs)
        self.out_shape = [jax.ShapeDtypeStruct((N_DEV,) + s.shape, s.dtype) for s in shards]
        self.specs = [pl.BlockSpec(memory_space=pl.ANY)] * self.n
        self.scratch = [pltpu.SemaphoreType.DMA((self.n, 7)), pltpu.SemaphoreType.DMA((self.n, 7)),
                        pltpu.SemaphoreType.DMA((self.n,))]

    def _plan(self, ins, outs, sems):
        send_sems, recv_sems, local_sems = sems
        x, y, c = _my_place()
        me, sibling = (x, y, c), (x, y, 1 - c)
        chips = [(1 - x, y), (x, 1 - y), (1 - x, 1 - y)]

        def copy(a, k, block, to, src=None):
            rows = outs[a].at[_slot(*block)]
            return pltpu.make_async_remote_copy(
                src_ref=rows if src is None else src, dst_ref=rows, send_sem=send_sems.at[a, k],
                recv_sem=recv_sems.at[a, k], device_id=to, device_id_type=pl.DeviceIdType.MESH)

        mine = [pltpu.make_async_copy(ins[a], outs[a].at[_slot(*me)], local_sems.at[a]) for a in range(self.n)]
        first = []
        for a in range(self.n):
            first.append(copy(a, 0, me, sibling, src=ins[a]))
            first += [copy(a, 1 + j, me, (*chips[j], c), src=ins[a]) for j in range(2)]
        return copy, mine, first, me, sibling, chips, c

    def start(self, ins, outs, sems):
        _, mine, first, *_ = self._plan(ins, outs, sems)
        for cp in mine + first:
            cp.start()

    def finish(self, ins, outs, sems):
        copy, mine, first, me, sibling, chips, c = self._plan(ins, outs, sems)

        def land_and_pass(a, j, relay_to=None):
            copy(a, 1 + j, (*chips[j], c), me).wait_recv()
            if relay_to is not None:
                copy(a, 3, (*chips[j], c), (*chips[relay_to], c)).start()
            copy(a, 4 + j, (*chips[j], c), sibling).start()

        for a in range(self.n):
            @pl.when(c == 0)
            def _():
                land_and_pass(a, 0, relay_to=1)
                land_and_pass(a, 1)

            @pl.when(c == 1)
            def _():
                land_and_pass(a, 1, relay_to=0)
                land_and_pass(a, 0)

            copy(a, 3, (*chips[2], c), me).wait_recv()
            copy(a, 6, (*chips[2], c), sibling).start()
        for a in range(self.n):
            copy(a, 0, sibling, me).wait_recv()
            for j in range(3):
                copy(a, 4 + j, (*chips[j], 1 - c), me).wait_recv()
        for cp in first:
            cp.wait_send()
        for a in range(self.n):
            for k in (3, 4, 5, 6):
                copy(a, k, me, me, src=ins[a]).wait_send()
        for cp in mine:
            cp.wait()

    def carried(self, refs_in, refs_out, sems, first, last):
        @pl.when(first)
        def _():
            self.start(refs_in, refs_out, sems)

        @pl.when(last)
        def _():
            self.finish(refs_in, refs_out, sems)


class _Exchange:
    def __init__(self, scatter, bcast):
        self.arrs = list(scatter) + list(bcast)
        self.n, self.n_sc = len(self.arrs), len(scatter)
        self.out_shape = [jax.ShapeDtypeStruct(a.shape, a.dtype) for a in scatter]
        self.out_shape += [jax.ShapeDtypeStruct((N_DEV,) + a.shape, a.dtype) for a in bcast]
        self.specs = [pl.BlockSpec(memory_space=pl.ANY)] * self.n
        self.scratch = [pltpu.SemaphoreType.DMA((self.n, 7)), pltpu.SemaphoreType.DMA((self.n, 7)),
                        pltpu.SemaphoreType.DMA((self.n,))]

    def _copies(self, ins, outs, sems):
        send_sems, recv_sems, local_sems = sems
        x, y, c = _my_place()
        me = _slot(x, y, c)
        peers = _peers(x, y, c)
        mine, sends, arrivals = [], [], []
        for a in range(self.n):
            src = ins[a].at[me] if a < self.n_sc else ins[a]
            mine.append(pltpu.make_async_copy(src, outs[a].at[me], local_sems.at[a]))
        for k, peer in enumerate(peers):
            for a in range(self.n):
                src = ins[a].at[_slot(*peer)] if a < self.n_sc else ins[a]
                sends.append(pltpu.make_async_remote_copy(
                    src_ref=src, dst_ref=outs[a].at[me], send_sem=send_sems.at[a, k], recv_sem=recv_sems.at[a, k],
                    device_id=peer, device_id_type=pl.DeviceIdType.MESH))
                rows = outs[a].at[_slot(*peer)]
                arrivals.append(pltpu.make_async_remote_copy(
                    src_ref=rows, dst_ref=rows, send_sem=send_sems.at[a, k], recv_sem=recv_sems.at[a, k],
                    device_id=peer, device_id_type=pl.DeviceIdType.MESH))
        return mine, sends, arrivals

    def start(self, ins, outs, sems):
        mine, sends, _ = self._copies(ins, outs, sems)
        for cp in mine + sends:
            cp.start()

    def finish(self, ins, outs, sems):
        mine, sends, arrivals = self._copies(ins, outs, sems)
        for cp in arrivals:
            cp.wait_recv()
        for cp in sends:
            cp.wait_send()
        for cp in mine:
            cp.wait()

    def carried(self, refs_in, refs_out, sems, first, last):
        @pl.when(first)
        def _():
            self.start(refs_in, refs_out, sems)

        @pl.when(last)
        def _():
            self.finish(refs_in, refs_out, sems)

    def alone(self, name):
        n = self.n

        def body(*refs):
            ins, outs, sems = refs[:n], refs[n : 2 * n], refs[2 * n :]
            self.start(ins, outs, sems)
            self.finish(ins, outs, sems)

        return pl.pallas_call(body, name=name, out_shape=self.out_shape, in_specs=self.specs, out_specs=self.specs,
                              scratch_shapes=self.scratch)(*self.arrs)


def _peers(x, y, c):
    return [(1 - x if k & 4 else x, 1 - y if k & 2 else y, 1 - c if k & 1 else c) for k in range(1, N_DEV)]


def _scatter_send(g, name):
    hbm = pl.BlockSpec(memory_space=pltpu.HBM)
    sem = pl.BlockSpec(memory_space=pltpu.SEMAPHORE)

    def body(g_ref, land_ref, send_sems, recv_sems, g_thru, land_thru, token):
        x, y, c = _my_place()
        me = _slot(x, y, c)
        for k, peer in enumerate(_peers(x, y, c)):
            pltpu.make_async_remote_copy(
                src_ref=g_ref.at[_slot(*peer)], dst_ref=land_ref.at[me], send_sem=send_sems.at[k], recv_sem=recv_sems.at[k],
                device_id=peer, device_id_type=pl.DeviceIdType.MESH).start()
        token[...] = jnp.zeros_like(token)

    return pl.pallas_call(
        body, name=name,
        out_shape=(pltpu.SemaphoreType.DMA((N_DEV - 1,)), pltpu.SemaphoreType.DMA((N_DEV - 1,)),
                   pltpu.HBM(g.shape, g.dtype), pltpu.HBM(g.shape, g.dtype), jax.ShapeDtypeStruct((8, 128), F32)),
        in_specs=(hbm, hbm), out_specs=(sem, sem, hbm, hbm, pl.BlockSpec(memory_space=pltpu.VMEM)),
        input_output_aliases={0: 2, 1: 3},
        compiler_params=pltpu.CompilerParams(has_side_effects=pltpu.SideEffectType.DATAFLOW_SIDE_EFFECTING),
    )(pltpu.with_memory_space_constraint(g, pltpu.HBM),
      pltpu.with_memory_space_constraint(lax.empty(g.shape, g.dtype), pltpu.HBM))


def _scatter_wait(send_sems, recv_sems, g_thru, land_thru, after, name):
    hbm = pl.BlockSpec(memory_space=pltpu.HBM)
    sem = pl.BlockSpec(memory_space=pltpu.SEMAPHORE)

    def body(g_ref, land_ref, send_sems, recv_sems, after_ref, g_out, land_out):
        x, y, c = _my_place()
        for k, peer in enumerate(_peers(x, y, c)):
            cp = pltpu.make_async_remote_copy(
                src_ref=g_ref.at[_slot(*peer)], dst_ref=land_ref.at[_slot(*peer)], send_sem=send_sems.at[k],
                recv_sem=recv_sems.at[k], device_id=peer, device_id_type=pl.DeviceIdType.MESH)
            cp.wait_send()
            cp.wait_recv()

    return pl.pallas_call(
        body, name=name,
        out_shape=(pltpu.HBM(g_thru.shape, g_thru.dtype), pltpu.HBM(land_thru.shape, land_thru.dtype)),
        in_specs=(hbm, hbm, sem, sem, pl.BlockSpec(memory_space=pl.ANY)), out_specs=(hbm, hbm),
        input_output_aliases={0: 0, 1: 1},
        compiler_params=pltpu.CompilerParams(has_side_effects=pltpu.SideEffectType.DATAFLOW_SIDE_EFFECTING),
    )(g_thru, land_thru, send_sems, recv_sems, after)


def _host_split(refs, n_in, n_out, n_scratch, ex):
    k = ex.n if ex is not None else 0
    a = n_in
    b = a + k
    c = b + n_out
    d = c + k
    e = d + n_scratch
    return refs[:a], refs[a:b], refs[b:c], refs[c:d], refs[d:e], refs[e:]


def _rope_tables(pos, ex):
    T = pos.shape[0]
    tm = min(1024, T)
    nt = T // tm
    lane = np.arange(128) % HEAD_DIM
    inv = np.power(np.float32(ROPE_THETA), -np.arange(0, ROPE_DIM, 2, dtype=np.float32) / np.float32(ROPE_DIM)).astype(np.float32)
    half = ROPE_DIM // 2
    invf = np.where(lane < ROPE_DIM, inv[lane % half], 0.0).astype(np.float32)[None, :]
    m_a = (lane < half).astype(np.float32)[None, :]
    m_b = ((lane >= half) & (lane < ROPE_DIM)).astype(np.float32)[None, :]

    def body(*refs):
        ins, ex_in, (cos_ref, sa_ref, sb_ref), ex_out, _, ex_sems = _host_split(refs, 4, 3, 0, ex)
        pos_ref, invf_ref, ma_ref, mb_ref = ins
        i = pl.program_id(0)
        ex.carried(ex_in, ex_out, ex_sems, i == 0, i == nt - 1)
        ang = pos_ref[...].astype(F32) * invf_ref[...]
        sn = jnp.sin(ang)
        cos_ref[...] = jnp.cos(ang)
        sa_ref[...] = -sn * ma_ref[...]
        sb_ref[...] = sn * mb_ref[...]

    row = pl.BlockSpec((tm, 128), lambda i: (i, 0))
    cst = pl.BlockSpec((1, 128), lambda i: (0, 0))
    return pl.pallas_call(
        body, name="rope_tables", grid=(nt,), out_shape=[jax.ShapeDtypeStruct((T, 128), F32)] * 3 + ex.out_shape,
        in_specs=[pl.BlockSpec((tm, 1), lambda i: (i, 0)), cst, cst, cst] + ex.specs, out_specs=[row] * 3 + ex.specs,
        scratch_shapes=ex.scratch,
        compiler_params=_cparams("arbitrary"),
    )(pos, jnp.asarray(invf), jnp.asarray(m_a), jnp.asarray(m_b), *ex.arrs)


def _rope(t, cos, sa, sb, sign=1.0):
    parts = []
    for i in range(t.shape[1] // 128):
        ti = t[:, 128 * i : 128 * (i + 1)]
        up = pltpu.roll(ti, 128 - ROPE_DIM // 2, 1)
        dn = pltpu.roll(ti, ROPE_DIM // 2, 1)
        parts.append(ti * cos + sign * (up * sa + dn * sb))
    return parts[0] if len(parts) == 1 else jnp.concatenate(parts, axis=-1)


def _in_proj(x2, ln_pre, w_in, cos, sa, sb, ex):
    T = x2.shape[0]
    tm = min(512, T)
    nt = T // tm
    chunk = D
    kv_cols = 4 * N_KV * KV_W

    def body(*refs):
        ins, ex_in, (z_ref, h_ref, qr_ref, kv_ref), ex_out, _, ex_sems = _host_split(refs, 6, 4, 0, ex)
        x_ref, g_ref, w_ref, cos_ref, sa_ref, sb_ref = ins
        i = pl.program_id(0)
        ex.carried(ex_in, ex_out, ex_sems, i == 0, i == nt - 1)
        xv = x_ref[...]
        r = lax.rsqrt(jnp.mean(xv * xv, axis=-1, keepdims=True) + EPS)
        h = (xv * r * g_ref[...]).astype(BF16)
        h_ref[...] = h
        tabs = cos_ref[...], sa_ref[...], sb_ref[...]
        for c0 in range(0, NW, chunk):
            cw = min(chunk, NW - c0)
            zc = _dot_nt(h, w_ref[_orig_col(c0) : _orig_col(c0) + cw, :])
            z_ref[:, c0 : c0 + cw] = zc.astype(BF16)
            if c0 == CB_Q * D:
                qr_ref[...] = (_rope(zc, *tabs) * (HEAD_DIM ** -0.5)).astype(BF16)
            if c0 == CB_K * KV_W:
                low = lax.broadcasted_iota(jnp.int32, (tm, KV_W), 1) < HEAD_DIM
                for kind, t in ((0, _rope(zc[:, 0:KV_W], *tabs)), (1, zc[:, KV_W:])):
                    swapped = pltpu.roll(t, HEAD_DIM, 1)
                    cols = {(0, 0): jnp.where(low, t, 0.0), (0, 1): jnp.where(low, 0.0, swapped),
                            (1, 0): jnp.where(low, swapped, 0.0), (1, 1): jnp.where(low, 0.0, t)}
                    for (g, par), val in cols.items():
                        c = _kv2_col(kind, g, par)
                        kv_ref[:, c * KV_W : (c + 1) * KV_W] = val.astype(BF16)

    def row(w):
        return pl.BlockSpec((tm, w), lambda i: (i, 0))

    bf = lambda w: jax.ShapeDtypeStruct((T, w), BF16)
    return pl.pallas_call(
        body, name="in_proj", grid=(nt,),
        out_shape=[bf(NW), bf(D), bf(D), bf(kv_cols)] + ex.out_shape,
        in_specs=[row(D), _weight_spec((1, D)), _weight_spec((NW, D)), row(128), row(128), row(128)] + ex.specs,
        out_specs=[row(NW), row(D), row(D), row(kv_cols)] + ex.specs,
        scratch_shapes=ex.scratch,
        compiler_params=_cparams("arbitrary"),
    )(x2, ln_pre, w_in, cos, sa, sb, *ex.arrs)


def _conv_tiles(S):
    tm = min(CONV_TILE, S)
    return tm, S // tm, tm // CONV_HALO


CONV_ROWS_FWD = 32
CONV_ROWS = 16


def _fill_shifted(sh, rows):
    for b in range(1, 8):
        sh[b, 0:rows, :] = sh[0, b : b + rows, :]


def _conv_fwd(z3, w_dw, b_dw, ln_g, ln_b):
    NS, S, _ = z3.shape
    tm, nt, r = _conv_tiles(S)

    def body(val_ref, glu_ref, hval_ref, hglu_ref, w_ref, b_ref, g_ref, bb_ref, c_ref, cs_ref, ush, cbuf):
        i = pl.program_id(1)
        ush[0, CONV_HALO:, :] = val_ref[...].astype(F32) * _sig(glu_ref[...].astype(F32))
        uh = hval_ref[...].astype(F32) * _sig(hglu_ref[...].astype(F32))
        ush[0, 0:CONV_HALO, :] = jnp.where(i > 0, uh, 0.0)
        _fill_shifted(ush, tm + CONV_HALO - 8)
        for r0 in range(0, tm, CONV_ROWS_FWD):
            acc = jnp.zeros((CONV_ROWS_FWD, D), F32)
            for k in range(CONV_K):
                a, b = divmod(CONV_HALO - (CONV_K - 1) + k, 8)
                acc = acc + w_ref[k : k + 1, :] * ush[b, r0 + 8 * a : r0 + 8 * a + CONV_ROWS_FWD, :]
            cbuf[r0 : r0 + CONV_ROWS_FWD, :] = acc + b_ref[...]
        cv = cbuf[...]
        mu = jnp.mean(cv, axis=-1, keepdims=True)
        xc = cv - mu
        var = jnp.mean(xc * xc, axis=-1, keepdims=True)
        cl = xc * lax.rsqrt(var + EPS) * g_ref[...] + bb_ref[...]
        c_ref[...] = cv.astype(BF16)
        cs_ref[...] = (cl * _sig(cl)).astype(BF16)

    def cur(cb):
        return pl.BlockSpec((None, tm, D), lambda s, i: (s, i, cb))

    def halo(cb):
        return pl.BlockSpec((None, CONV_HALO, D), lambda s, i: (s, jnp.maximum(i * r - 1, 0), cb))

    vec = pl.BlockSpec((1, D), lambda s, i: (0, 0))
    out = pl.BlockSpec((None, tm, D), lambda s, i: (s, i, 0))
    return pl.pallas_call(
        body, name="conv_fwd", grid=(NS, nt),
        out_shape=[jax.ShapeDtypeStruct((NS, S, D), BF16)] * 2,
        in_specs=[cur(CB_VAL), cur(CB_GLU), halo(CB_VAL), halo(CB_GLU),
                  pl.BlockSpec((CONV_HALO, D), lambda s, i: (0, 0)), vec, vec, vec],
        out_specs=[out, out],
        scratch_shapes=[pltpu.VMEM((8, tm + CONV_HALO, D), F32), pltpu.VMEM((tm, D), F32)],
        compiler_params=_cparams("parallel", "parallel"),
    )(z3, z3, z3, z3, w_dw, b_dw, ln_g, ln_b)


PAIRS = GROUP // 2
QROWS = PAIRS * BLK


def _kv2_col(kind, g, par):
    return kind * 2 * N_KV + g * 2 + par


def _attn_mask(has_prev):
    qi = lax.broadcasted_iota(jnp.int32, (QROWS, 2 * BLK), 0) & (BLK - 1)
    kj = lax.broadcasted_iota(jnp.int32, (QROWS, 2 * BLK), 1)
    first_key = jnp.where(has_prev, 0, BLK)
    return (kj > qi) & (kj <= qi + BLK) & (kj >= first_key)


NEG_BIG = -1e30


def _attn_specs():
    q = pl.BlockSpec((None, BLK, D), lambda s, n: (s, n, 0))
    kv_cur = pl.BlockSpec((None, BLK, 4 * N_KV * KV_W), lambda s, n: (s, n, 0))
    kv_prev = pl.BlockSpec((None, BLK, 4 * N_KV * KV_W), lambda s, n: (s, jnp.maximum(n - 1, 0), 0))
    sink = pl.BlockSpec(memory_space=pltpu.SMEM)
    return sink, q, kv_cur, kv_prev


def _stack_pairs(ref, g):
    return jnp.concatenate([ref[:, (PAIRS * g + j) * 128 : (PAIRS * g + j + 1) * 128] for j in range(PAIRS)], axis=0)


def _unstack_pairs(ref, g, val):
    for j in range(PAIRS):
        ref[:, (PAIRS * g + j) * 128 : (PAIRS * g + j + 1) * 128] = val[j * BLK : (j + 1) * BLK].astype(ref.dtype)


def _pair_heads(g, par):
    return [GROUP * g + 2 * j + par for j in range(PAIRS)]


def _head_col_load(ref, g, par):
    return jnp.concatenate([ref[:, h : h + 1] for h in _pair_heads(g, par)], axis=0)


def _head_col_store(ref, g, par, col):
    for j, h in enumerate(_pair_heads(g, par)):
        ref[:, h : h + 1] = col[j * BLK : (j + 1) * BLK]


def _sink_col(sink_ref, g, par):
    return jnp.concatenate([jnp.full((BLK, 1), sink_ref[h], F32) for h in _pair_heads(g, par)], axis=0)


def _kv2_block(kvc_ref, kvp_ref, has_prev, c):
    col = slice(c * KV_W, (c + 1) * KV_W)
    prev = jnp.where(has_prev, kvp_ref[:, col], jnp.zeros((BLK, KV_W), BF16))
    return jnp.concatenate([prev, kvc_ref[:, col]], axis=0)


def _attn_fwd(qr3, kv3, sinks):
    NS, S, _ = qr3.shape

    def body(sink_ref, q_ref, kvc_ref, kvp_ref, o_ref, lse_ref):
        has_prev = pl.program_id(1) > 0
        mask = _attn_mask(has_prev)[0:BLK]
        for g in range(N_KV):
            kv = [[_kv2_block(kvc_ref, kvp_ref, has_prev, _kv2_col(kind, g, par)) for par in range(2)] for kind in range(2)]
            for j in range(PAIRS):
                cols = slice((PAIRS * g + j) * 128, (PAIRS * g + j + 1) * 128)
                q2 = q_ref[:, cols]
                o_pair = None
                for par in range(2):
                    h = GROUP * g + 2 * j + par
                    s = jnp.where(mask, _dot_nt(q2, kv[0][par]), NEG_BIG)
                    sk = sink_ref[h]
                    mx = jnp.maximum(jnp.max(s, axis=-1, keepdims=True), sk)
                    e = jnp.exp(s - mx)
                    den = jnp.sum(e, axis=-1, keepdims=True) + jnp.exp(sk - mx)
                    pv = _dot(e.astype(BF16), kv[1][par]) * (1.0 / den)
                    o_pair = pv if o_pair is None else o_pair + pv
                    lse_ref[:, h : h + 1] = mx + jnp.log(den)
                o_ref[:, cols] = o_pair.astype(BF16)

    return pl.pallas_call(
        body, name="attn_fwd", grid=(NS, S // BLK),
        out_shape=[jax.ShapeDtypeStruct((NS, S, D), BF16), jax.ShapeDtypeStruct((NS, S, N_HEADS), F32)],
        in_specs=list(_attn_specs()),
        out_specs=[pl.BlockSpec((None, BLK, D), lambda s, n: (s, n, 0)),
                   pl.BlockSpec((None, BLK, N_HEADS), lambda s, n: (s, n, 0))],
        compiler_params=_cparams("parallel", "parallel"),
    )(sinks, qr3, kv3, kv3)


def _weight_spec(shape):
    return pl.BlockSpec(shape, lambda i: (0,) * len(shape), pipeline_mode=pl.Buffered(1))


def _dsilu(v, s):
    return s * (1.0 + v * (1.0 - s))


MID_TILE = 256


def _mid(cs, o, z2, x2, p2, tgt, ln_post, weights):
    T = cs.shape[0]
    tm = min(MID_TILE, T)
    nt = T // tm
    n_bf = 16

    def body(cs_ref, o_ref, cg_ref, ag_ref, gc_ref, ga_ref, x_ref, p_ref, t_ref, g_ref,
             wpw, wbrc, wbra, wout, wpg, wpp,
             dya0_ref, ya1_ref, dya_ref, yb0_ref, dyb_ref, m_ref, dmo_ref, x1_ref, dgl_ref, dpp_ref,
             dcs_ref, do_ref, dcg_ref, dag_ref, dgc_ref, dga_ref, dx1_ref, loss_ref, glp_ref, lacc):
        i = pl.program_id(0)

        @pl.when(i == 0)
        def _():
            lacc[...] = jnp.zeros_like(lacc)
            glp_ref[...] = jnp.zeros_like(glp_ref)

        cg = cg_ref[...].astype(F32)
        scg = _sig(cg)
        silu_c = cg * scg
        ya0 = _dot(cs_ref[...], wpw[...])
        ya1 = (ya0 * silu_c).astype(BF16)
        ya1_ref[...] = ya1
        ya = _dot(ya1, wbrc[...])
        ag = ag_ref[...].astype(F32)
        sag = _sig(ag)
        silu_a = ag * sag
        ov = o_ref[...].astype(F32)
        yb0 = (ov * silu_a).astype(BF16)
        yb0_ref[...] = yb0
        yb = _dot(yb0, wbra[...])
        sgc = _sig(gc_ref[...].astype(F32))
        sga = _sig(ga_ref[...].astype(F32))
        mb = (sgc * ya + sga * yb).astype(BF16)
        m_ref[...] = mb
        mo = _dot(mb, wout[...])
        r2 = lax.rsqrt(jnp.mean(mo * mo, axis=-1, keepdims=True) + EPS)
        nrm = mo * r2
        x1 = x_ref[...] + nrm * g_ref[...]
        x1b = x1.astype(BF16)
        x1_ref[...] = x1b
        gate = _sig(_dot(x1b, wpg[...]))
        pp = _dot(p_ref[...].astype(BF16), wpp[...])
        e = x1 + gate * pp - t_ref[...]
        lacc[...] += _rowsum8(e * e)
        dy = e * (1.0 / D)

        dgl = (dy * pp * gate * (1.0 - gate)).astype(BF16)
        dgl_ref[...] = dgl
        dpp_ref[...] = (dy * gate).astype(BF16)
        dx1 = dy + _dot_nt(dgl, wpg[...])
        dx1_ref[...] = dx1
        glp_ref[...] += _rowsum8(dx1 * nrm)
        dn = dx1 * g_ref[...]
        dmo = (r2 * (dn - nrm * jnp.mean(dn * nrm, axis=-1, keepdims=True))).astype(BF16)
        dmo_ref[...] = dmo
        dm = _dot_nt(dmo, wout[...])
        dya = (dm * sgc).astype(BF16)
        dyb = (dm * sga).astype(BF16)
        dya_ref[...] = dya
        dyb_ref[...] = dyb
        dgc_ref[...] = (dm * ya * sgc * (1.0 - sgc)).astype(BF16)
        dga_ref[...] = (dm * yb * sga * (1.0 - sga)).astype(BF16)
        dya1 = _dot_nt(dya, wbrc[...])
        dya0 = (dya1 * silu_c).astype(BF16)
        dya0_ref[...] = dya0
        dcg_ref[...] = (dya1 * ya0 * _dsilu(cg, scg)).astype(BF16)
        dcs_ref[...] = _dot_nt(dya0, wpw[...]).astype(BF16)
        dyb0 = _dot_nt(dyb, wbra[...])
        do_ref[...] = (dyb0 * silu_a).astype(BF16)
        dag_ref[...] = (dyb0 * ov * _dsilu(ag, sag)).astype(BF16)

        @pl.when(i == nt - 1)
        def _():
            loss_ref[...] = jnp.full(loss_ref.shape, jnp.sum(lacc[...]) * (0.5 / D), F32)

    row = pl.BlockSpec((tm, D), lambda i: (i, 0))

    def zcol(cb):
        return pl.BlockSpec((tm, D), lambda i: (i, cb))

    bf = jax.ShapeDtypeStruct((T, D), BF16)
    return pl.pallas_call(
        body, name="mid_fwd_bwd", grid=(nt,),
        out_shape=[bf] * n_bf + [jax.ShapeDtypeStruct((T, D), F32), jax.ShapeDtypeStruct((8, 128), F32),
                                 jax.ShapeDtypeStruct((8, D), F32)],
        in_specs=[row, row, zcol(CB_CGATE), zcol(CB_AGATE), zcol(CB_GCONV), zcol(CB_GATTN), row,
                  pl.BlockSpec((tm, PLE), lambda i: (i, 0)), row, _weight_spec((1, D))]
        + [_weight_spec((D, D))] * 5 + [_weight_spec((PLE, D))],
        out_specs=[row] * (n_bf + 1) + [pl.BlockSpec((8, 128), lambda i: (0, 0)), pl.BlockSpec((8, D), lambda i: (0, 0))],
        scratch_shapes=[pltpu.VMEM((8, D), F32)],
        compiler_params=_cparams("arbitrary"),
    )(cs, o, z2, z2, z2, z2, x2, p2, tgt, ln_post, *weights)


def _conv_bwd(z3, c3, dcs3, w_dw, ln_g, ln_b, ex):
    NS, S, _ = z3.shape
    tm, nt, r = _conv_tiles(S)

    def body(*refs):
        ins, ex_in, outs, ex_out, scratch, ex_sems = _host_split(refs, 9, 3, 3, ex)
        val_ref, glu_ref, c_ref, dcs_ref, hc_ref, hdcs_ref, w_ref, g_ref, bb_ref = ins
        dz_ref, gw_ref, gvec_ref = outs
        dsh, ubuf, dubuf = scratch
        i = pl.program_id(1)
        first = (pl.program_id(0) == 0) & (i == 0)
        ex.carried(ex_in, ex_out, ex_sems, first, (pl.program_id(0) == NS - 1) & (i == nt - 1))

        @pl.when(first)
        def _():
            gw_ref[...] = jnp.zeros_like(gw_ref)
            gvec_ref[...] = jnp.zeros_like(gvec_ref)

        val = val_ref[...].astype(F32)
        sg = _sig(glu_ref[...].astype(F32))
        ubuf[...] = val * sg

        def ln_bwd(cv, dcs):
            cv = cv.astype(F32)
            mu = jnp.mean(cv, axis=-1, keepdims=True)
            xc = cv - mu
            rstd = lax.rsqrt(jnp.mean(xc * xc, axis=-1, keepdims=True) + EPS)
            xhat = xc * rstd
            cl = xhat * g_ref[...] + bb_ref[...]
            s = _sig(cl)
            dcl = dcs.astype(F32) * _dsilu(cl, s)
            dxh = dcl * g_ref[...]
            dc = rstd * (dxh - jnp.mean(dxh, axis=-1, keepdims=True) - xhat * jnp.mean(dxh * xhat, axis=-1, keepdims=True))
            return dc, dcl, xhat

        dc, dcl, xhat = ln_bwd(c_ref[...], dcs_ref[...])
        dsh[0, 0:tm, :] = dc
        dch, _, _ = ln_bwd(hc_ref[...], hdcs_ref[...])
        dsh[0, tm:, :] = jnp.where(i < nt - 1, dch, 0.0)
        gvec_ref[0:8, :] += _rowsum8(dcl * xhat)
        gvec_ref[8:16, :] += _rowsum8(dcl)
        gvec_ref[16:24, :] += _rowsum8(dc)
        _fill_shifted(dsh, tm + CONV_HALO - 8)

        def dc_ahead(r0, k):
            a, b = divmod(CONV_K - 1 - k, 8)
            return dsh[b, r0 + 8 * a : r0 + 8 * a + CONV_ROWS, :]

        for r0 in range(0, tm, CONV_ROWS):
            acc = jnp.zeros((CONV_ROWS, D), F32)
            for k in range(CONV_K):
                acc = acc + w_ref[k : k + 1, :] * dc_ahead(r0, k)
            dubuf[r0 : r0 + CONV_ROWS, :] = acc
        for r0 in range(0, tm, CONV_ROWS):
            ur = ubuf[r0 : r0 + CONV_ROWS, :]
            for k in range(CONV_K):
                gw_ref[8 * k : 8 * k + 8, :] += _rowsum8(ur * dc_ahead(r0, k))
        du = dubuf[...]
        dz_ref[:, 0:D] = (du * sg).astype(BF16)
        dz_ref[:, D:] = (du * val * sg * (1.0 - sg)).astype(BF16)

    def cur(cb):
        return pl.BlockSpec((None, tm, D), lambda s, i: (s, i, cb))

    nxt = pl.BlockSpec((None, CONV_HALO, D), lambda s, i: (s, jnp.minimum((i + 1) * r, S // CONV_HALO - 1), 0))
    vec = pl.BlockSpec((1, D), lambda s, i: (0, 0))
    return pl.pallas_call(
        body, name="conv_bwd", grid=(NS, nt),
        out_shape=[jax.ShapeDtypeStruct((NS, S, 2 * D), BF16), jax.ShapeDtypeStruct((CONV_HALO * 8, D), F32),
                   jax.ShapeDtypeStruct((24, D), F32)] + ex.out_shape,
        in_specs=[cur(CB_VAL), cur(CB_GLU), cur(0), cur(0), nxt, nxt,
                  pl.BlockSpec((CONV_HALO, D), lambda s, i: (0, 0)), vec, vec] + ex.specs,
        out_specs=[pl.BlockSpec((None, tm, 2 * D), lambda s, i: (s, i, 0)),
                   pl.BlockSpec((CONV_HALO * 8, D), lambda s, i: (0, 0)), pl.BlockSpec((24, D), lambda s, i: (0, 0))] + ex.specs,
        scratch_shapes=[pltpu.VMEM((8, tm + CONV_HALO, D), F32), pltpu.VMEM((tm, D), F32), pltpu.VMEM((tm, D), F32)] + ex.scratch,
        compiler_params=_cparams("arbitrary", "arbitrary"),
    )(z3, z3, c3, dcs3, c3, dcs3, w_dw, ln_g, ln_b, *ex.arrs)


def _attn_bwd(qr3, kv3, do3, o3, lse3, sinks, cos3, sa3, sb3):
    NS, S, _ = qr3.shape

    def body(sink_ref, q_ref, kvc_ref, kvp_ref, do_ref, o_ref, lse_ref, cos_ref, sa_ref, sb_ref,
             dq_ref, dkc_ref, dkp_ref, dvc_ref, dvp_ref, dsk_ref):
        has_prev = pl.program_id(1) > 0
        mask = _attn_mask(has_prev)
        tabs = cos_ref[...], sa_ref[...], sb_ref[...]
        low_q = lax.broadcasted_iota(jnp.int32, (QROWS, 128), 1) < HEAD_DIM
        dk_g, dv_g = [], []
        for g in range(N_KV):
            qs = _stack_pairs(q_ref, g)
            dos = _stack_pairs(do_ref, g)
            prod = dos.astype(F32) * _stack_pairs(o_ref, g).astype(F32)
            deltas = [jnp.sum(jnp.where(low_q, prod, 0.0), axis=-1, keepdims=True),
                      jnp.sum(jnp.where(low_q, 0.0, prod), axis=-1, keepdims=True)]
            dq_acc, dk_par, dv_par = None, [], []
            for par in range(2):
                k2 = _kv2_block(kvc_ref, kvp_ref, has_prev, _kv2_col(0, g, par))
                v2 = _kv2_block(kvc_ref, kvp_ref, has_prev, _kv2_col(1, g, par))
                lse = _head_col_load(lse_ref, g, par)
                p = jnp.exp(jnp.where(mask, _dot_nt(qs, k2), NEG_BIG) - lse)
                ds = (p * (_dot_nt(dos, v2) - deltas[par])).astype(BF16)
                dq = _dot(ds, k2)
                dq_acc = dq if dq_acc is None else dq_acc + dq
                dk_par.append(_dot_tn(ds, qs))
                dv_par.append(_dot_tn(p.astype(BF16), dos))
                _head_col_store(dsk_ref, g, par, -jnp.exp(_sink_col(sink_ref, g, par) - lse) * deltas[par])
            for j in range(PAIRS):
                dq_pair = _rope(dq_acc[j * BLK : (j + 1) * BLK], *tabs, sign=-1.0) * (HEAD_DIM ** -0.5)
                dq_ref[:, (PAIRS * g + j) * 128 : (PAIRS * g + j + 1) * 128] = dq_pair.astype(BF16)
            dk_g.append(dk_par[0] + pltpu.roll(dk_par[1], HEAD_DIM, 1))
            dv_g.append(dv_par[0] + pltpu.roll(dv_par[1], HEAD_DIM, 1))
        low_k = lax.broadcasted_iota(jnp.int32, (2 * BLK, KV_W), 1) < HEAD_DIM
        dk = jnp.where(low_k, dk_g[0], pltpu.roll(dk_g[1], HEAD_DIM, 1))
        dv = jnp.where(low_k, dv_g[0], pltpu.roll(dv_g[1], HEAD_DIM, 1))
        dkp_ref[...] = dk[0:BLK]
        dkc_ref[...] = dk[BLK:]
        dvp_ref[...] = dv[0:BLK]
        dvc_ref[...] = dv[BLK:]

    qspec = pl.BlockSpec((None, BLK, D), lambda s, n: (s, n, 0))
    kvspec = pl.BlockSpec((None, BLK, KV_W), lambda s, n: (s, n, 0))
    hspec = pl.BlockSpec((None, BLK, N_HEADS), lambda s, n: (s, n, 0))
    kv = jax.ShapeDtypeStruct((NS, S, KV_W), F32)
    return pl.pallas_call(
        body, name="attn_bwd", grid=(NS, S // BLK),
        out_shape=[jax.ShapeDtypeStruct((NS, S, D), BF16), kv, kv, kv, kv, jax.ShapeDtypeStruct((NS, S, N_HEADS), F32)],
        in_specs=list(_attn_specs()) + [qspec, qspec, hspec, kvspec, kvspec, kvspec],
        out_specs=[qspec, kvspec, kvspec, kvspec, kvspec, hspec],
        compiler_params=_cparams("parallel", "parallel"),
    )(sinks, qr3, kv3, kv3, do3, o3, lse3, cos3, sa3, sb3)


def _attn_post(dkc3, dkp3, dvc3, dvp3, cos3, sa3, sb3):
    NS, S, _ = dkc3.shape
    tm = min(4 * BLK, S)
    nt = S // tm

    def body(dkc_ref, dkp_ref, dkn_ref, dvc_ref, dvp_ref, dvn_ref, cos_ref, sa_ref, sb_ref, dkv_ref):
        has_next = pl.program_id(1) < nt - 1

        def join(cur_ref, prev_ref, next_ref):
            ahead = jnp.where(has_next, next_ref[...], 0.0)
            shifted = ahead if tm == BLK else jnp.concatenate([prev_ref[BLK:, :], ahead], axis=0)
            return cur_ref[...] + shifted

        tabs = cos_ref[...], sa_ref[...], sb_ref[...]
        dkv_ref[:, 0:KV_W] = _rope(join(dkc_ref, dkp_ref, dkn_ref), *tabs, sign=-1.0).astype(BF16)
        dkv_ref[:, KV_W:] = join(dvc_ref, dvp_ref, dvn_ref).astype(BF16)

    cur = pl.BlockSpec((None, tm, KV_W), lambda s, j: (s, j, 0))
    nxt = pl.BlockSpec((None, BLK, KV_W), lambda s, j: (s, jnp.minimum((j + 1) * (tm // BLK), S // BLK - 1), 0))
    return pl.pallas_call(
        body, name="attn_post", grid=(NS, nt),
        out_shape=jax.ShapeDtypeStruct((NS, S, 2 * KV_W), BF16),
        in_specs=[cur, cur, nxt, cur, cur, nxt, cur, cur, cur],
        out_specs=pl.BlockSpec((None, tm, 2 * KV_W), lambda s, j: (s, j, 0)),
        compiler_params=_cparams("parallel", "parallel"),
    )(dkc3, dkp3, dkp3, dvc3, dvp3, dvp3, cos3, sa3, sb3)


def _in_bwd(segs, w_in_t, x2, dx1, ln_pre, after):
    T = x2.shape[0]
    tm = min(ROW_TILE, T)
    nt = T // tm
    ns = len(segs)
    widths = [s.shape[1] for s in segs]

    def body(*refs):
        seg_refs = refs[:ns]
        w_ref, x_ref, dx1_ref, g_ref, _, gx_ref, glp_ref = refs[ns:]
        i = pl.program_id(0)

        @pl.when(i == 0)
        def _():
            glp_ref[...] = jnp.zeros_like(glp_ref)

        dh = None
        zc = 0
        for sref, w in zip(seg_refs, widths):
            part = _dot(sref[...], w_ref[_orig_col(zc) : _orig_col(zc) + w, :])
            dh = part if dh is None else dh + part
            zc += w
        xv = x_ref[...]
        r1 = lax.rsqrt(jnp.mean(xv * xv, axis=-1, keepdims=True) + EPS)
        xhat = xv * r1
        glp_ref[...] += _rowsum8(dh * xhat)
        dhg = dh * g_ref[...]
        gx_ref[...] = dx1_ref[...] + r1 * (dhg - xhat * jnp.mean(dhg * xhat, axis=-1, keepdims=True))

    row = pl.BlockSpec((tm, D), lambda i: (i, 0))
    return pl.pallas_call(
        body, name="in_bwd", grid=(nt,),
        out_shape=[jax.ShapeDtypeStruct((T, D), F32), jax.ShapeDtypeStruct((8, D), F32)],
        in_specs=[pl.BlockSpec((tm, w), lambda i: (i, 0)) for w in widths]
        + [_weight_spec((NW, D)), row, row, _weight_spec((1, D)), pl.BlockSpec(memory_space=pl.ANY)],
        out_specs=[row, pl.BlockSpec((8, D), lambda i: (0, 0))],
        compiler_params=_cparams("arbitrary"),
    )(*segs, w_in_t, x2, dx1, ln_pre, after)


def _grad_matmul(a, b, name, rows=None, into=None):
    T, M = a.shape
    N = b.shape[1]
    tk = min(1024, T)
    nk = T // tk

    def body(a_ref, b_ref, *rest):
        o_ref, acc = rest[-2:]
        k = pl.program_id(0)

        @pl.when(k == 0)
        def _():
            acc[...] = jnp.zeros_like(acc)

        acc[...] += _dot_tn(a_ref[...].astype(BF16), b_ref[...])

        @pl.when(k == nk - 1)
        def _():
            o_ref[...] = acc[...].astype(BF16)

    in_specs = [pl.BlockSpec((tk, M), lambda k: (k, 0)), pl.BlockSpec((tk, N), lambda k: (k, 0))]
    if rows is None:
        out_shape, out_spec = (M, N), pl.BlockSpec((M, N), lambda k: (0, 0))
    else:
        out_shape, out_spec = (rows[1], N), pl.BlockSpec((pl.Element(M), pl.Element(N)), lambda k: (rows[0], 0))
    operands = (a, b) if into is None else (a, b, into)
    return pl.pallas_call(
        body, name=name, grid=(nk,), out_shape=jax.ShapeDtypeStruct(out_shape, BF16),
        in_specs=in_specs if into is None else in_specs + [pl.BlockSpec(memory_space=pl.ANY)],
        out_specs=out_spec, input_output_aliases={} if into is None else {2: 0},
        scratch_shapes=[pltpu.VMEM((M, N), F32)],
        compiler_params=_cparams("arbitrary"),
    )(*operands)


def _pack_small(gw, gvec, glp_pre, glp_post, dsk):
    T = dsk.shape[0]

    def body(gw_ref, gvec_ref, pre_ref, post_ref, dsk_ref, gdw_ref, gs_ref):
        gwf = gw_ref[...].reshape(CONV_HALO, 8, D).sum(axis=1)
        for d in range(N_DEV):
            gdw_ref[d] = gwf[:, 128 * d : 128 * (d + 1)]
        gs_ref[...] = jnp.zeros_like(gs_ref)
        gs_ref[0:1, :] = jnp.sum(pre_ref[...], axis=0, keepdims=True)
        gs_ref[1:2, :] = jnp.sum(post_ref[...], axis=0, keepdims=True)
        gs_ref[2:3, :] = jnp.sum(gvec_ref[16:24, :], axis=0, keepdims=True)
        gs_ref[3:4, :] = jnp.sum(gvec_ref[0:8, :], axis=0, keepdims=True)
        gs_ref[4:5, :] = jnp.sum(gvec_ref[8:16, :], axis=0, keepdims=True)
        gs_ref[5:6, 0:N_HEADS] = jnp.sum(dsk_ref[...], axis=0, keepdims=True)

    return pl.pallas_call(
        body, name="pack_small",
        out_shape=[jax.ShapeDtypeStruct((N_DEV, CONV_HALO, 128), F32), jax.ShapeDtypeStruct((8, D), F32)],
        compiler_params=_cparams(),
    )(gw, gvec, glp_pre, glp_post, dsk)


def _adam_update(g, w_ref, m_ref, v_ref, g_ref, d_ref, nm_ref, nv_ref):
    nm = ADAM_B1 * m_ref[...] + (1.0 - ADAM_B1) * g
    nv = ADAM_B2 * v_ref[...] + (1.0 - ADAM_B2) * (g * g)
    m_hat = nm / (1.0 - ADAM_B1 ** ADAM_STEP)
    v_hat = nv / (1.0 - ADAM_B2 ** ADAM_STEP)
    g_ref[...] = g
    d_ref[...] = -ADAM_LR * (m_hat / (jnp.sqrt(v_hat) + ADAM_EPS) + ADAM_WD * w_ref[...])
    nm_ref[...] = nm
    nv_ref[...] = nv


def _adamw(parts, w, m, v, name):
    R, C = w.shape
    tr = R if R <= 256 else 128

    def body(p_ref, *rest):
        g = p_ref[0].astype(F32)
        for s in range(1, N_DEV):
            g = g + p_ref[s].astype(F32)
        _adam_update(g, *rest)

    blk = pl.BlockSpec((tr, C), lambda i: (i, 0))
    return pl.pallas_call(
        body, name=name, grid=(R // tr,), out_shape=[jax.ShapeDtypeStruct((R, C), F32)] * 4,
        in_specs=[pl.BlockSpec((N_DEV, tr, C), lambda i: (0, i, 0)), blk, blk, blk], out_specs=[blk] * 4,
        compiler_params=_cparams("parallel"),
    )(parts, w, m, v)


def _adamw_own(parts, own, me, w, m, v, name):
    R, C = w.shape
    tr = max(t for t in range(16, 513, 16) if R % t == 0)

    def body(me_ref, p_ref, own_ref, *rest):
        g = None
        for s in range(N_DEV):
            part = jnp.where(me_ref[0] == s, own_ref[...], p_ref[s]).astype(F32)
            g = part if g is None else g + part
        _adam_update(g, *rest)

    blk = pl.BlockSpec((tr, C), lambda i, me: (i, 0))
    return pl.pallas_call(
        body, name=name, out_shape=[jax.ShapeDtypeStruct((R, C), F32)] * 4,
        grid_spec=pltpu.PrefetchScalarGridSpec(
            num_scalar_prefetch=1, grid=(R // tr,),
            in_specs=[pl.BlockSpec((N_DEV, tr, C), lambda i, me: (0, i, 0)),
                      pl.BlockSpec((None, tr, C), lambda i, me: (me[0], i, 0)), blk, blk, blk],
            out_specs=[blk] * 4),
        compiler_params=_cparams("parallel"),
    )(me, parts, own, w, m, v)


def kernel(x, p, positions, w_in, ln_pre, ln_post, w_dw, b_dw, conv_ln_g, conv_ln_b, w_pw, sinks, w_br_conv, w_br_attn, w_out, w_ple_gate, w_ple_proj, loss_target, m_w_in, m_ln_pre, m_ln_post, m_w_dw, m_b_dw, m_conv_ln_g, m_conv_ln_b, m_w_pw, m_sinks, m_w_br_conv, m_w_br_attn, m_w_out, m_w_ple_gate, m_w_ple_proj, v_w_in, v_ln_pre, v_ln_post, v_w_dw, v_b_dw, v_conv_ln_g, v_conv_ln_b, v_w_pw, v_sinks, v_w_br_conv, v_w_br_attn, v_w_out, v_w_ple_gate, v_w_ple_proj):
    NS, S, _ = x.shape
    T = NS * S
    x2 = x.reshape(T, D)
    p2 = p.reshape(T, PLE)
    tgt = loss_target.reshape(T, D)
    pos = positions.reshape(T, 1)

    row_sharded = [w_pw[0], w_br_conv[0], w_br_attn[0], w_out[0], w_ple_gate[0]]
    sh_rows = D // N_DEV
    w_t, m_t, v_t = (jnp.swapaxes(a[0], 0, 1) for a in (w_in, m_w_in, v_w_in))
    cos, sa, sb, g_in = _rope_tables(pos, _TwoLevelGather([w_t.astype(BF16)]))
    w_in_f = g_in.reshape(NW, D)
    gather_rest = _Exchange([], [w.astype(BF16) for w in row_sharded] + [
        w_ple_proj[0].astype(BF16), jnp.pad(w_dw[0], ((0, CONV_HALO - CONV_K), (0, 0)))])

    in_out = _in_proj(x2, ln_pre, w_in_f, cos, sa, sb, gather_rest)
    z2, h, qr, kv2 = in_out[:4]
    g_rows, g_pp, g_dw = in_out[4:9], in_out[9], in_out[10]
    full = [g.reshape(D, D) for g in g_rows]
    w_pp_f = g_pp.transpose(1, 0, 2).reshape(PLE, D)
    w_dw_f = g_dw.transpose(1, 0, 2).reshape(CONV_HALO, D)
    z3 = z2.reshape(NS, S, NW)
    c3, cs3 = _conv_fwd(z3, w_dw_f, b_dw, conv_ln_g, conv_ln_b)
    qr3, kv3 = qr.reshape(NS, S, D), kv2.reshape(NS, S, 4 * N_KV * KV_W)
    sinks1 = sinks.reshape(N_HEADS)
    o3, lse3 = _attn_fwd(qr3, kv3, sinks1)
    o = o3.reshape(T, D)
    cs = cs3.reshape(T, D)
    (dya0, ya1, dya, yb0, dyb, m, dmo, x1, dgl, dpp, dcs, do, dcg, dag, dgc, dga, dx1, loss_blk, glp_post) = _mid(
        cs, o, z2, x2, p2, tgt, ln_post, full + [w_pp_f])

    gp_rows = [_grad_matmul(a, b, nm).reshape(N_DEV, sh_rows, D) for a, b, nm in (
        (cs, dya0, "grad_w_pw"), (ya1, dya, "grad_w_br_conv"), (yb0, dyb, "grad_w_br_attn"),
        (m, dmo, "grad_w_out"), (x1, dgl, "grad_w_ple_gate"))]
    gp_pp = _grad_matmul(p2, dpp, "grad_w_ple_proj").reshape(PLE, N_DEV, D // N_DEV).transpose(1, 0, 2)
    conv_out = _conv_bwd(z3, c3, dcs.reshape(NS, S, D), w_dw_f, conv_ln_g, conv_ln_b,
                         _Exchange(gp_rows + [gp_pp], [loss_blk]))
    dzvu3, gw, gvec, r_rows, r_pp, r_loss = conv_out[0], conv_out[1], conv_out[2], conv_out[3:8], conv_out[8], conv_out[9]
    loss = jnp.sum(r_loss[:, 0, 0])
    tab3 = [t.reshape(NS, S, 128) for t in (cos, sa, sb)]
    dq3, dkc3, dkp3, dvc3, dvp3, dsk3 = _attn_bwd(qr3, kv3, do.reshape(NS, S, D), o3, lse3, sinks1, *tab3)
    dkv3 = _attn_post(dkc3, dkp3, dvc3, dvp3, *tab3)
    segs = [dzvu3.reshape(T, 2 * D), dcg, dq3.reshape(T, D), dag, dgc, dga, dkv3.reshape(T, 2 * KV_W)]
    gp_in, zc = None, 0
    for j, s in enumerate(segs):
        gp_in = _grad_matmul(s, h, f"grad_w_in_{j}", rows=(_orig_col(zc), NW), into=gp_in)
        zc += s.shape[1]
    gp_in = gp_in.reshape(N_DEV, NW // N_DEV, D)
    in_send, in_recv, in_own, in_land, sent = _scatter_send(gp_in, "w_in_grad_send")
    grad_x2, glp_pre = _in_bwd(segs, w_in_f, x2, dx1, ln_pre, sent)
    gp_dw, gp_small = _pack_small(gw, gvec, glp_pre, glp_post, dsk3.reshape(T, N_HEADS))
    r_dw, r_small = _Exchange([gp_dw], [gp_small]).alone("small_grad_exchange")

    res = {}
    names_rows = ["w_pw", "w_br_conv", "w_br_attn", "w_out", "w_ple_gate"]
    wmv = {"w_pw": (w_pw, m_w_pw, v_w_pw), "w_br_conv": (w_br_conv, m_w_br_conv, v_w_br_conv),
           "w_br_attn": (w_br_attn, m_w_br_attn, v_w_br_attn), "w_out": (w_out, m_w_out, v_w_out),
           "w_ple_gate": (w_ple_gate, m_w_ple_gate, v_w_ple_gate)}
    for nm, parts in zip(names_rows, r_rows):
        w_, m_, v_ = wmv[nm]
        res[nm] = _adamw(parts, w_[0], m_[0], v_[0], "adamw_" + nm)
    res["w_ple_proj"] = _adamw(r_pp, w_ple_proj[0], m_w_ple_proj[0], v_w_ple_proj[0], "adamw_w_ple_proj")
    pad_dw = lambda a: jnp.pad(a[0], ((0, CONV_HALO - CONV_K), (0, 0)))
    res["w_dw"] = [a[:CONV_K] for a in _adamw(r_dw, pad_dw(w_dw), pad_dw(m_w_dw), pad_dw(v_w_dw), "adamw_w_dw")]

    def stack_small(a_pre, a_post, a_b, a_g, a_bb, a_s):
        sk = jnp.pad(a_s, ((0, 0), (0, D - N_HEADS)))
        return jnp.concatenate([a_pre, a_post, a_b, a_g, a_bb, sk, jnp.zeros((2, D), F32)], axis=0)

    small = _adamw(
        r_small, stack_small(ln_pre, ln_post, b_dw, conv_ln_g, conv_ln_b, sinks),
        stack_small(m_ln_pre, m_ln_post, m_b_dw, m_conv_ln_g, m_conv_ln_b, m_sinks),
        stack_small(v_ln_pre, v_ln_post, v_b_dw, v_conv_ln_g, v_conv_ln_b, v_sinks), "adamw_small")
    for j, nm in enumerate(["ln_pre", "ln_post", "b_dw", "conv_ln_g", "conv_ln_b"]):
        res[nm] = [a[j] for a in small]
    res["sinks"] = [a[5, :N_HEADS] for a in small]
    in_own, in_land = _scatter_wait(in_send, in_recv, in_own, in_land, small[0], "w_in_grad_wait")
    me = _slot(*_my_place()).astype(jnp.int32).reshape(1)
    res["w_in"] = [jnp.swapaxes(a, 0, 1) for a in _adamw_own(in_land, in_own, me, w_t, m_t, v_t, "adamw_w_in")]

    order = ["w_in", "ln_pre", "ln_post", "w_dw", "b_dw", "conv_ln_g", "conv_ln_b", "w_pw", "sinks", "w_br_conv",
             "w_br_attn", "w_out", "w_ple_gate", "w_ple_proj"]
    outs = [loss, grad_x2.reshape(NS, S, D)]
    for kind in range(4):
        outs += [res[nm][kind][None] for nm in order]
    return tuple(outs)
```

```python
import functools

import numpy as np

import jax
import jax.numpy as jnp
from jax import lax
from jax.experimental import pallas as pl
from jax.experimental.pallas import tpu as pltpu

F32 = jnp.float32
BF16 = jnp.bfloat16

D = 1024
N_HEADS = 16
N_KV = 2
HEAD_DIM = 64
GROUP = N_HEADS // N_KV
KV_W = N_KV * HEAD_DIM
CONV_K = 31
CONV_HALO = 32
BLK = 128
ROPE_DIM = 16
ROPE_THETA = 500000.0
EPS = 1e-6
PLE = 256
NW = 7 * D + 2 * KV_W
N_DEV = 8

CB_VAL, CB_GLU, CB_CGATE, CB_Q, CB_AGATE, CB_GCONV, CB_GATTN = range(7)
CB_K = 7 * D // KV_W

ADAM_LR, ADAM_B1, ADAM_B2, ADAM_EPS, ADAM_WD, ADAM_STEP = 0.001, 0.9, 0.999, 1e-08, 0.01, 10

VMEM_LIMIT = 56 * 1024 * 1024
ROW_TILE = 256
CONV_TILE = 256


def _cparams(*sem):
    return pltpu.CompilerParams(dimension_semantics=sem if sem else None, vmem_limit_bytes=VMEM_LIMIT)


def _sig(v):
    return 1.0 / (1.0 + jnp.exp(-v))


def _rowsum8(a):
    return a.reshape(a.shape[0] // 8, 8, a.shape[1]).sum(axis=0)


def _dot(a, b):
    return jnp.dot(a, b, preferred_element_type=F32)


def _dot_nt(a, b):
    return lax.dot_general(a, b, (((1,), (1,)), ((), ())), preferred_element_type=F32)


def _dot_tn(a, b):
    return lax.dot_general(a, b, (((0,), (0,)), ((), ())), preferred_element_type=F32)


def _orig_col(zc):
    if zc < 4 * D:
        return zc
    return zc - 7 * D + 4 * D if zc >= 7 * D else zc + 2 * KV_W


def _my_place():
    return lax.axis_index("x"), lax.axis_index("y"), lax.axis_index("c")


def _slot(px, py, pc):
    return 4 * px + 2 * py + pc


class _TwoLevelGather:
    def __init__(self, shards):
        self.arrs = list(shards)
        self.n = len(self.arrs)
        self.out_shape = [jax.ShapeDtypeStruct((N_DEV,) + s.shape, s.dtype) for s in shards]
        self.specs = [pl.BlockSpec(memory_space=pl.ANY)] * self.n
        self.scratch = [pltpu.SemaphoreType.DMA((self.n, 7)), pltpu.SemaphoreType.DMA((self.n, 7)),
                        pltpu.SemaphoreType.DMA((self.n,))]

    def _plan(self, ins, outs, sems):
        send_sems, recv_sems, local_sems = sems
        x, y, c = _my_place()
        me, sibling = (x, y, c), (x, y, 1 - c)
        chips = [(1 - x, y), (x, 1 - y), (1 - x, 1 - y)]

        def copy(a, k, block, to, src=None):
            rows = outs[a].at[_slot(*block)]
            return pltpu.make_async_remote_copy(
                src_ref=rows if src is None else src, dst_ref=rows, send_sem=send_sems.at[a, k],
                recv_sem=recv_sems.at[a, k], device_id=to, device_id_type=pl.DeviceIdType.MESH)

        mine = [pltpu.make_async_copy(ins[a], outs[a].at[_slot(*me)], local_sems.at[a]) for a in range(self.n)]
        first = []
        for a in range(self.n):
            first.append(copy(a, 0, me, sibling, src=ins[a]))
            first += [copy(a, 1 + j, me, (*chips[j], c), src=ins[a]) for j in range(2)]
        return copy, mine, first, me, sibling, chips, c

    def start(self, ins, outs, sems):
        _, mine, first, *_ = self._plan(ins, outs, sems)
        for cp in mine + first:
            cp.start()

    def finish(self, ins, outs, sems):
        copy, mine, first, me, sibling, chips, c = self._plan(ins, outs, sems)

        def land_and_pass(a, j, relay_to=None):
            copy(a, 1 + j, (*chips[j], c), me).wait_recv()
            if relay_to is not None:
                copy(a, 3, (*chips[j], c), (*chips[relay_to], c)).start()
            copy(a, 4 + j, (*chips[j], c), sibling).start()

        for a in range(self.n):
            @pl.when(c == 0)
            def _():
                land_and_pass(a, 0, relay_to=1)
                land_and_pass(a, 1)

            @pl.when(c == 1)
            def _():
                land_and_pass(a, 1, relay_to=0)
                land_and_pass(a, 0)

            copy(a, 3, (*chips[2], c), me).wait_recv()
            copy(a, 6, (*chips[2], c), sibling).start()
        for a in range(self.n):
            copy(a, 0, sibling, me).wait_recv()
            for j in range(3):
                copy(a, 4 + j, (*chips[j], 1 - c), me).wait_recv()
        for cp in first:
            cp.wait_send()
        for a in range(self.n):
            for k in (3, 4, 5, 6):
                copy(a, k, me, me, src=ins[a]).wait_send()
        for cp in mine:
            cp.wait()

    def carried(self, refs_in, refs_out, sems, first, last):
        @pl.when(first)
        def _():
            self.start(refs_in, refs_out, sems)

        @pl.when(last)
        def _():
            self.finish(refs_in, refs_out, sems)


class _Exchange:
    def __init__(self, scatter, bcast):
        self.arrs = list(scatter) + list(bcast)
        self.n, self.n_sc = len(self.arrs), len(scatter)
        self.out_shape = [jax.ShapeDtypeStruct(a.shape, a.dtype) for a in scatter]
        self.out_shape += [jax.ShapeDtypeStruct((N_DEV,) + a.shape, a.dtype) for a in bcast]
        self.specs = [pl.BlockSpec(memory_space=pl.ANY)] * self.n
        self.scratch = [pltpu.SemaphoreType.DMA((self.n, 7)), pltpu.SemaphoreType.DMA((self.n, 7)),
                        pltpu.SemaphoreType.DMA((self.n,))]

    def _copies(self, ins, outs, sems):
        send_sems, recv_sems, local_sems = sems
        x, y, c = _my_place()
        me = _slot(x, y, c)
        peers = _peers(x, y, c)
        mine, sends, arrivals = [], [], []
        for a in range(self.n):
            src = ins[a].at[me] if a < self.n_sc else ins[a]
            mine.append(pltpu.make_async_copy(src, outs[a].at[me], local_sems.at[a]))
        for k, peer in enumerate(peers):
            for a in range(self.n):
                src = ins[a].at[_slot(*peer)] if a < self.n_sc else ins[a]
                sends.append(pltpu.make_async_remote_copy(
                    src_ref=src, dst_ref=outs[a].at[me], send_sem=send_sems.at[a, k], recv_sem=recv_sems.at[a, k],
                    device_id=peer, device_id_type=pl.DeviceIdType.MESH))
                rows = outs[a].at[_slot(*peer)]
                arrivals.append(pltpu.make_async_remote_copy(
                    src_ref=rows, dst_ref=rows, send_sem=send_sems.at[a, k], recv_sem=recv_sems.at[a, k],
                    device_id=peer, device_id_type=pl.DeviceIdType.MESH))
        return mine, sends, arrivals

    def start(self, ins, outs, sems):
        mine, sends, _ = self._copies(ins, outs, sems)
        for cp in mine + sends:
            cp.start()

    def finish(self, ins, outs, sems):
        mine, sends, arrivals = self._copies(ins, outs, sems)
        for cp in arrivals:
            cp.wait_recv()
        for cp in sends:
            cp.wait_send()
        for cp in mine:
            cp.wait()

    def carried(self, refs_in, refs_out, sems, first, last):
        @pl.when(first)
        def _():
            self.start(refs_in, refs_out, sems)

        @pl.when(last)
        def _():
            self.finish(refs_in, refs_out, sems)

    def alone(self, name):
        n = self.n

        def body(*refs):
            ins, outs, sems = refs[:n], refs[n : 2 * n], refs[2 * n :]
            self.start(ins, outs, sems)
            self.finish(ins, outs, sems)

        return pl.pallas_call(body, name=name, out_shape=self.out_shape, in_specs=self.specs, out_specs=self.specs,
                              scratch_shapes=self.scratch)(*self.arrs)


def _peers(x, y, c):
    return [(1 - x if k & 4 else x, 1 - y if k & 2 else y, 1 - c if k & 1 else c) for k in range(1, N_DEV)]


def _scatter_send(g, name):
    hbm = pl.BlockSpec(memory_space=pltpu.HBM)
    sem = pl.BlockSpec(memory_space=pltpu.SEMAPHORE)

    def body(g_ref, land_ref, send_sems, recv_sems, g_thru, land_thru, token):
        x, y, c = _my_place()
        me = _slot(x, y, c)
        for k, peer in enumerate(_peers(x, y, c)):
            pltpu.make_async_remote_copy(
                src_ref=g_ref.at[_slot(*peer)], dst_ref=land_ref.at[me], send_sem=send_sems.at[k], recv_sem=recv_sems.at[k],
                device_id=peer, device_id_type=pl.DeviceIdType.MESH).start()
        token[...] = jnp.zeros_like(token)

    return pl.pallas_call(
        body, name=name,
        out_shape=(pltpu.SemaphoreType.DMA((N_DEV - 1,)), pltpu.SemaphoreType.DMA((N_DEV - 1,)),
                   pltpu.HBM(g.shape, g.dtype), pltpu.HBM(g.shape, g.dtype), jax.ShapeDtypeStruct((8, 128), F32)),
        in_specs=(hbm, hbm), out_specs=(sem, sem, hbm, hbm, pl.BlockSpec(memory_space=pltpu.VMEM)),
        input_output_aliases={0: 2, 1: 3},
        compiler_params=pltpu.CompilerParams(has_side_effects=pltpu.SideEffectType.DATAFLOW_SIDE_EFFECTING),
    )(pltpu.with_memory_space_constraint(g, pltpu.HBM),
      pltpu.with_memory_space_constraint(lax.empty(g.shape, g.dtype), pltpu.HBM))


def _scatter_wait(send_sems, recv_sems, g_thru, land_thru, after, name):
    hbm = pl.BlockSpec(memory_space=pltpu.HBM)
    sem = pl.BlockSpec(memory_space=pltpu.SEMAPHORE)

    def body(g_ref, land_ref, send_sems, recv_sems, after_ref, g_out, land_out):
        x, y, c = _my_place()
        for k, peer in enumerate(_peers(x, y, c)):
            cp = pltpu.make_async_remote_copy(
                src_ref=g_ref.at[_slot(*peer)], dst_ref=land_ref.at[_slot(*peer)], send_sem=send_sems.at[k],
                recv_sem=recv_sems.at[k], device_id=peer, device_id_type=pl.DeviceIdType.MESH)
            cp.wait_send()
            cp.wait_recv()

    return pl.pallas_call(
        body, name=name,
        out_shape=(pltpu.HBM(g_thru.shape, g_thru.dtype), pltpu.HBM(land_thru.shape, land_thru.dtype)),
        in_specs=(hbm, hbm, sem, sem, pl.BlockSpec(memory_space=pl.ANY)), out_specs=(hbm, hbm),
        input_output_aliases={0: 0, 1: 1},
        compiler_params=pltpu.CompilerParams(has_side_effects=pltpu.SideEffectType.DATAFLOW_SIDE_EFFECTING),
    )(g_thru, land_thru, send_sems, recv_sems, after)


def _host_split(refs, n_in, n_out, n_scratch, ex):
    k = ex.n if ex is not None else 0
    a = n_in
    b = a + k
    c = b + n_out
    d = c + k
    e = d + n_scratch
    return refs[:a], refs[a:b], refs[b:c], refs[c:d], refs[d:e], refs[e:]


def _rope_tables(pos, ex):
    T = pos.shape[0]
    tm = min(1024, T)
    nt = T // tm
    lane = np.arange(128) % HEAD_DIM
    inv = np.power(np.float32(ROPE_THETA), -np.arange(0, ROPE_DIM, 2, dtype=np.float32) / np.float32(ROPE_DIM)).astype(np.float32)
    half = ROPE_DIM // 2
    invf = np.where(lane < ROPE_DIM, inv[lane % half], 0.0).astype(np.float32)[None, :]
    m_a = (lane < half).astype(np.float32)[None, :]
    m_b = ((lane >= half) & (lane < ROPE_DIM)).astype(np.float32)[None, :]

    def body(*refs):
        ins, ex_in, (cos_ref, sa_ref, sb_ref), ex_out, _, ex_sems = _host_split(refs, 4, 3, 0, ex)
        pos_ref, invf_ref, ma_ref, mb_ref = ins
        i = pl.program_id(0)
        ex.carried(ex_in, ex_out, ex_sems, i == 0, i == nt - 1)
        ang = pos_ref[...].astype(F32) * invf_ref[...]
        sn = jnp.sin(ang)
        cos_ref[...] = jnp.cos(ang)
        sa_ref[...] = -sn * ma_ref[...]
        sb_ref[...] = sn * mb_ref[...]

    row = pl.BlockSpec((tm, 128), lambda i: (i, 0))
    cst = pl.BlockSpec((1, 128), lambda i: (0, 0))
    return pl.pallas_call(
        body, name="rope_tables", grid=(nt,), out_shape=[jax.ShapeDtypeStruct((T, 128), F32)] * 3 + ex.out_shape,
        in_specs=[pl.BlockSpec((tm, 1), lambda i: (i, 0)), cst, cst, cst] + ex.specs, out_specs=[row] * 3 + ex.specs,
        scratch_shapes=ex.scratch,
        compiler_params=_cparams("arbitrary"),
    )(pos, jnp.asarray(invf), jnp.asarray(m_a), jnp.asarray(m_b), *ex.arrs)


def _rope(t, cos, sa, sb, sign=1.0):
    parts = []
    for i in range(t.shape[1] // 128):
        ti = t[:, 128 * i : 128 * (i + 1)]
        up = pltpu.roll(ti, 128 - ROPE_DIM // 2, 1)
        dn = pltpu.roll(ti, ROPE_DIM // 2, 1)
        parts.append(ti * cos + sign * (up * sa + dn * sb))
    return parts[0] if len(parts) == 1 else jnp.concatenate(parts, axis=-1)


def _in_proj(x2, ln_pre, w_in, cos, sa, sb, ex):
    T = x2.shape[0]
    tm = min(512, T)
    nt = T // tm
    chunk = D
    kv_cols = 4 * N_KV * KV_W

    def body(*refs):
        ins, ex_in, (z_ref, h_ref, qr_ref, kv_ref), ex_out, _, ex_sems = _host_split(refs, 6, 4, 0, ex)
        x_ref, g_ref, w_ref, cos_ref, sa_ref, sb_ref = ins
        i = pl.program_id(0)
        ex.carried(ex_in, ex_out, ex_sems, i == 0, i == nt - 1)
        xv = x_ref[...]
        r = lax.rsqrt(jnp.mean(xv * xv, axis=-1, keepdims=True) + EPS)
        h = (xv * r * g_ref[...]).astype(BF16)
        h_ref[...] = h
        tabs = cos_ref[...], sa_ref[...], sb_ref[...]
        for c0 in range(0, NW, chunk):
            cw = min(chunk, NW - c0)
            zc = _dot_nt(h, w_ref[_orig_col(c0) : _orig_col(c0) + cw, :])
            z_ref[:, c0 : c0 + cw] = zc.astype(BF16)
            if c0 == CB_Q * D:
                qr_ref[...] = (_rope(zc, *tabs) * (HEAD_DIM ** -0.5)).astype(BF16)
            if c0 == CB_K * KV_W:
                low = lax.broadcasted_iota(jnp.int32, (tm, KV_W), 1) < HEAD_DIM
                for kind, t in ((0, _rope(zc[:, 0:KV_W], *tabs)), (1, zc[:, KV_W:])):
                    swapped = pltpu.roll(t, HEAD_DIM, 1)
                    cols = {(0, 0): jnp.where(low, t, 0.0), (0, 1): jnp.where(low, 0.0, swapped),
                            (1, 0): jnp.where(low, swapped, 0.0), (1, 1): jnp.where(low, 0.0, t)}
                    for (g, par), val in cols.items():
                        c = _kv2_col(kind, g, par)
                        kv_ref[:, c * KV_W : (c + 1) * KV_W] = val.astype(BF16)

    def row(w):
        return pl.BlockSpec((tm, w), lambda i: (i, 0))

    bf = lambda w: jax.ShapeDtypeStruct((T, w), BF16)
    return pl.pallas_call(
        body, name="in_proj", grid=(nt,),
        out_shape=[bf(NW), bf(D), bf(D), bf(kv_cols)] + ex.out_shape,
        in_specs=[row(D), _weight_spec((1, D)), _weight_spec((NW, D)), row(128), row(128), row(128)] + ex.specs,
        out_specs=[row(NW), row(D), row(D), row(kv_cols)] + ex.specs,
        scratch_shapes=ex.scratch,
        compiler_params=_cparams("arbitrary"),
    )(x2, ln_pre, w_in, cos, sa, sb, *ex.arrs)


def _conv_tiles(S):
    tm = min(CONV_TILE, S)
    return tm, S // tm, tm // CONV_HALO


CONV_ROWS_FWD = 32
CONV_ROWS = 16


def _fill_shifted(sh, rows):
    for b in range(1, 8):
        sh[b, 0:rows, :] = sh[0, b : b + rows, :]


def _conv_fwd(z3, w_dw, b_dw, ln_g, ln_b):
    NS, S, _ = z3.shape
    tm, nt, r = _conv_tiles(S)

    def body(val_ref, glu_ref, hval_ref, hglu_ref, w_ref, b_ref, g_ref, bb_ref, c_ref, cs_ref, ush, cbuf):
        i = pl.program_id(1)
        ush[0, CONV_HALO:, :] = val_ref[...].astype(F32) * _sig(glu_ref[...].astype(F32))
        uh = hval_ref[...].astype(F32) * _sig(hglu_ref[...].astype(F32))
        ush[0, 0:CONV_HALO, :] = jnp.where(i > 0, uh, 0.0)
        _fill_shifted(ush, tm + CONV_HALO - 8)
        for r0 in range(0, tm, CONV_ROWS_FWD):
            acc = jnp.zeros((CONV_ROWS_FWD, D), F32)
            for k in range(CONV_K):
                a, b = divmod(CONV_HALO - (CONV_K - 1) + k, 8)
                acc = acc + w_ref[k : k + 1, :] * ush[b, r0 + 8 * a : r0 + 8 * a + CONV_ROWS_FWD, :]
            cbuf[r0 : r0 + CONV_ROWS_FWD, :] = acc + b_ref[...]
        cv = cbuf[...]
        mu = jnp.mean(cv, axis=-1, keepdims=True)
        xc = cv - mu
        var = jnp.mean(xc * xc, axis=-1, keepdims=True)
        cl = xc * lax.rsqrt(var + EPS) * g_ref[...] + bb_ref[...]
        c_ref[...] = cv.astype(BF16)
        cs_ref[...] = (cl * _sig(cl)).astype(BF16)

    def cur(cb):
        return pl.BlockSpec((None, tm, D), lambda s, i: (s, i, cb))

    def halo(cb):
        return pl.BlockSpec((None, CONV_HALO, D), lambda s, i: (s, jnp.maximum(i * r - 1, 0), cb))

    vec = pl.BlockSpec((1, D), lambda s, i: (0, 0))
    out = pl.BlockSpec((None, tm, D), lambda s, i: (s, i, 0))
    return pl.pallas_call(
        body, name="conv_fwd", grid=(NS, nt),
        out_shape=[jax.ShapeDtypeStruct((NS, S, D), BF16)] * 2,
        in_specs=[cur(CB_VAL), cur(CB_GLU), halo(CB_VAL), halo(CB_GLU),
                  pl.BlockSpec((CONV_HALO, D), lambda s, i: (0, 0)), vec, vec, vec],
        out_specs=[out, out],
        scratch_shapes=[pltpu.VMEM((8, tm + CONV_HALO, D), F32), pltpu.VMEM((tm, D), F32)],
        compiler_params=_cparams("parallel", "parallel"),
    )(z3, z3, z3, z3, w_dw, b_dw, ln_g, ln_b)


PAIRS = GROUP // 2
QROWS = PAIRS * BLK


def _kv2_col(kind, g, par):
    return kind * 2 * N_KV + g * 2 + par


def _attn_mask(has_prev):
    qi = lax.broadcasted_iota(jnp.int32, (QROWS, 2 * BLK), 0) & (BLK - 1)
    kj = lax.broadcasted_iota(jnp.int32, (QROWS, 2 * BLK), 1)
    first_key = jnp.where(has_prev, 0, BLK)
    return (kj > qi) & (kj <= qi + BLK) & (kj >= first_key)


NEG_BIG = -1e30


def _attn_specs():
    q = pl.BlockSpec((None, BLK, D), lambda s, n: (s, n, 0))
    kv_cur = pl.BlockSpec((None, BLK, 4 * N_KV * KV_W), lambda s, n: (s, n, 0))
    kv_prev = pl.BlockSpec((None, BLK, 4 * N_KV * KV_W), lambda s, n: (s, jnp.maximum(n - 1, 0), 0))
    sink = pl.BlockSpec(memory_space=pltpu.SMEM)
    return sink, q, kv_cur, kv_prev


def _stack_pairs(ref, g):
    return jnp.concatenate([ref[:, (PAIRS * g + j) * 128 : (PAIRS * g + j + 1) * 128] for j in range(PAIRS)], axis=0)


def _unstack_pairs(ref, g, val):
    for j in range(PAIRS):
        ref[:, (PAIRS * g + j) * 128 : (PAIRS * g + j + 1) * 128] = val[j * BLK : (j + 1) * BLK].astype(ref.dtype)


def _pair_heads(g, par):
    return [GROUP * g + 2 * j + par for j in range(PAIRS)]


def _head_col_load(ref, g, par):
    return jnp.concatenate([ref[:, h : h + 1] for h in _pair_heads(g, par)], axis=0)


def _head_col_store(ref, g, par, col):
    for j, h in enumerate(_pair_heads(g, par)):
        ref[:, h : h + 1] = col[j * BLK : (j + 1) * BLK]


def _sink_col(sink_ref, g, par):
    return jnp.concatenate([jnp.full((BLK, 1), sink_ref[h], F32) for h in _pair_heads(g, par)], axis=0)


def _kv2_block(kvc_ref, kvp_ref, has_prev, c):
    col = slice(c * KV_W, (c + 1) * KV_W)
    prev = jnp.where(has_prev, kvp_ref[:, col], jnp.zeros((BLK, KV_W), BF16))
    return jnp.concatenate([prev, kvc_ref[:, col]], axis=0)


def _attn_fwd(qr3, kv3, sinks):
    NS, S, _ = qr3.shape

    def body(sink_ref, q_ref, kvc_ref, kvp_ref, o_ref, lse_ref):
        has_prev = pl.program_id(1) > 0
        mask = _attn_mask(has_prev)[0:BLK]
        for g in range(N_KV):
            kv = [[_kv2_block(kvc_ref, kvp_ref, has_prev, _kv2_col(kind, g, par)) for par in range(2)] for kind in range(2)]
            for j in range(PAIRS):
                cols = slice((PAIRS * g + j) * 128, (PAIRS * g + j + 1) * 128)
                q2 = q_ref[:, cols]
                o_pair = None
                for par in range(2):
                    h = GROUP * g + 2 * j + par
                    s = jnp.where(mask, _dot_nt(q2, kv[0][par]), NEG_BIG)
                    sk = sink_ref[h]
                    mx = jnp.maximum(jnp.max(s, axis=-1, keepdims=True), sk)
                    e = jnp.exp(s - mx)
                    den = jnp.sum(e, axis=-1, keepdims=True) + jnp.exp(sk - mx)
                    pv = _dot(e.astype(BF16), kv[1][par]) * (1.0 / den)
                    o_pair = pv if o_pair is None else o_pair + pv
                    lse_ref[:, h : h + 1] = mx + jnp.log(den)
                o_ref[:, cols] = o_pair.astype(BF16)

    return pl.pallas_call(
        body, name="attn_fwd", grid=(NS, S // BLK),
        out_shape=[jax.ShapeDtypeStruct((NS, S, D), BF16), jax.ShapeDtypeStruct((NS, S, N_HEADS), F32)],
        in_specs=list(_attn_specs()),
        out_specs=[pl.BlockSpec((None, BLK, D), lambda s, n: (s, n, 0)),
                   pl.BlockSpec((None, BLK, N_HEADS), lambda s, n: (s, n, 0))],
        compiler_params=_cparams("parallel", "parallel"),
    )(sinks, qr3, kv3, kv3)


def _weight_spec(shape):
    return pl.BlockSpec(shape, lambda i: (0,) * len(shape), pipeline_mode=pl.Buffered(1))


def _dsilu(v, s):
    return s * (1.0 + v * (1.0 - s))


MID_TILE = 256


def _mid(cs, o, z2, x2, p2, tgt, ln_post, weights):
    T = cs.shape[0]
    tm = min(MID_TILE, T)
    nt = T // tm
    n_bf = 16

    def body(cs_ref, o_ref, cg_ref, ag_ref, gc_ref, ga_ref, x_ref, p_ref, t_ref, g_ref,
             wpw, wbrc, wbra, wout, wpg, wpp,
             dya0_ref, ya1_ref, dya_ref, yb0_ref, dyb_ref, m_ref, dmo_ref, x1_ref, dgl_ref, dpp_ref,
             dcs_ref, do_ref, dcg_ref, dag_ref, dgc_ref, dga_ref, dx1_ref, loss_ref, glp_ref, lacc):
        i = pl.program_id(0)

        @pl.when(i == 0)
        def _():
            lacc[...] = jnp.zeros_like(lacc)
            glp_ref[...] = jnp.zeros_like(glp_ref)

        cg = cg_ref[...].astype(F32)
        scg = _sig(cg)
        silu_c = cg * scg
        ya0 = _dot(cs_ref[...], wpw[...])
        ya1 = (ya0 * silu_c).astype(BF16)
        ya1_ref[...] = ya1
        ya = _dot(ya1, wbrc[...])
        ag = ag_ref[...].astype(F32)
        sag = _sig(ag)
        silu_a = ag * sag
        ov = o_ref[...].astype(F32)
        yb0 = (ov * silu_a).astype(BF16)
        yb0_ref[...] = yb0
        yb = _dot(yb0, wbra[...])
        sgc = _sig(gc_ref[...].astype(F32))
        sga = _sig(ga_ref[...].astype(F32))
        mb = (sgc * ya + sga * yb).astype(BF16)
        m_ref[...] = mb
        mo = _dot(mb, wout[...])
        r2 = lax.rsqrt(jnp.mean(mo * mo, axis=-1, keepdims=True) + EPS)
        nrm = mo * r2
        x1 = x_ref[...] + nrm * g_ref[...]
        x1b = x1.astype(BF16)
        x1_ref[...] = x1b
        gate = _sig(_dot(x1b, wpg[...]))
        pp = _dot(p_ref[...].astype(BF16), wpp[...])
        e = x1 + gate * pp - t_ref[...]
        lacc[...] += _rowsum8(e * e)
        dy = e * (1.0 / D)

        dgl = (dy * pp * gate * (1.0 - gate)).astype(BF16)
        dgl_ref[...] = dgl
        dpp_ref[...] = (dy * gate).astype(BF16)
        dx1 = dy + _dot_nt(dgl, wpg[...])
        dx1_ref[...] = dx1
        glp_ref[...] += _rowsum8(dx1 * nrm)
        dn = dx1 * g_ref[...]
        dmo = (r2 * (dn - nrm * jnp.mean(dn * nrm, axis=-1, keepdims=True))).astype(BF16)
        dmo_ref[...] = dmo
        dm = _dot_nt(dmo, wout[...])
        dya = (dm * sgc).astype(BF16)
        dyb = (dm * sga).astype(BF16)
        dya_ref[...] = dya
        dyb_ref[...] = dyb
        dgc_ref[...] = (dm * ya * sgc * (1.0 - sgc)).astype(BF16)
        dga_ref[...] = (dm * yb * sga * (1.0 - sga)).astype(BF16)
        dya1 = _dot_nt(dya, wbrc[...])
        dya0 = (dya1 * silu_c).astype(BF16)
        dya0_ref[...] = dya0
        dcg_ref[...] = (dya1 * ya0 * _dsilu(cg, scg)).astype(BF16)
        dcs_ref[...] = _dot_nt(dya0, wpw[...]).astype(BF16)
        dyb0 = _dot_nt(dyb, wbra[...])
        do_ref[...] = (dyb0 * silu_a).astype(BF16)
        dag_ref[...] = (dyb0 * ov * _dsilu(ag, sag)).astype(BF16)

        @pl.when(i == nt - 1)
        def _():
            loss_ref[...] = jnp.full(loss_ref.shape, jnp.sum(lacc[...]) * (0.5 / D), F32)

    row = pl.BlockSpec((tm, D), lambda i: (i, 0))

    def zcol(cb):
        return pl.BlockSpec((tm, D), lambda i: (i, cb))

    bf = jax.ShapeDtypeStruct((T, D), BF16)
    return pl.pallas_call(
        body, name="mid_fwd_bwd", grid=(nt,),
        out_shape=[bf] * n_bf + [jax.ShapeDtypeStruct((T, D), F32), jax.ShapeDtypeStruct((8, 128), F32),
                                 jax.ShapeDtypeStruct((8, D), F32)],
        in_specs=[row, row, zcol(CB_CGATE), zcol(CB_AGATE), zcol(CB_GCONV), zcol(CB_GATTN), row,
                  pl.BlockSpec((tm, PLE), lambda i: (i, 0)), row, _weight_spec((1, D))]
        + [_weight_spec((D, D))] * 5 + [_weight_spec((PLE, D))],
        out_specs=[row] * (n_bf + 1) + [pl.BlockSpec((8, 128), lambda i: (0, 0)), pl.BlockSpec((8, D), lambda i: (0, 0))],
        scratch_shapes=[pltpu.VMEM((8, D), F32)],
        compiler_params=_cparams("arbitrary"),
    )(cs, o, z2, z2, z2, z2, x2, p2, tgt, ln_post, *weights)


def _conv_bwd(z3, c3, dcs3, w_dw, ln_g, ln_b, ex):
    NS, S, _ = z3.shape
    tm, nt, r = _conv_tiles(S)

    def body(*refs):
        ins, ex_in, outs, ex_out, scratch, ex_sems = _host_split(refs, 9, 3, 3, ex)
        val_ref, glu_ref, c_ref, dcs_ref, hc_ref, hdcs_ref, w_ref, g_ref, bb_ref = ins
        dz_ref, gw_ref, gvec_ref = outs
        dsh, ubuf, dubuf = scratch
        i = pl.program_id(1)
        first = (pl.program_id(0) == 0) & (i == 0)
        ex.carried(ex_in, ex_out, ex_sems, first, (pl.program_id(0) == NS - 1) & (i == nt - 1))

        @pl.when(first)
        def _():
            gw_ref[...] = jnp.zeros_like(gw_ref)
            gvec_ref[...] = jnp.zeros_like(gvec_ref)

        val = val_ref[...].astype(F32)
        sg = _sig(glu_ref[...].astype(F32))
        ubuf[...] = val * sg

        def ln_bwd(cv, dcs):
            cv = cv.astype(F32)
            mu = jnp.mean(cv, axis=-1, keepdims=True)
            xc = cv - mu
            rstd = lax.rsqrt(jnp.mean(xc * xc, axis=-1, keepdims=True) + EPS)
            xhat = xc * rstd
            cl = xhat * g_ref[...] + bb_ref[...]
            s = _sig(cl)
            dcl = dcs.astype(F32) * _dsilu(cl, s)
            dxh = dcl * g_ref[...]
            dc = rstd * (dxh - jnp.mean(dxh, axis=-1, keepdims=True) - xhat * jnp.mean(dxh * xhat, axis=-1, keepdims=True))
            return dc, dcl, xhat

        dc, dcl, xhat = ln_bwd(c_ref[...], dcs_ref[...])
        dsh[0, 0:tm, :] = dc
        dch, _, _ = ln_bwd(hc_ref[...], hdcs_ref[...])
        dsh[0, tm:, :] = jnp.where(i < nt - 1, dch, 0.0)
        gvec_ref[0:8, :] += _rowsum8(dcl * xhat)
        gvec_ref[8:16, :] += _rowsum8(dcl)
        gvec_ref[16:24, :] += _rowsum8(dc)
        _fill_shifted(dsh, tm + CONV_HALO - 8)

        def dc_ahead(r0, k):
            a, b = divmod(CONV_K - 1 - k, 8)
            return dsh[b, r0 + 8 * a : r0 + 8 * a + CONV_ROWS, :]

        for r0 in range(0, tm, CONV_ROWS):
            acc = jnp.zeros((CONV_ROWS, D), F32)
            for k in range(CONV_K):
                acc = acc + w_ref[k : k + 1, :] * dc_ahead(r0, k)
            dubuf[r0 : r0 + CONV_ROWS, :] = acc
        for r0 in range(0, tm, CONV_ROWS):
            ur = ubuf[r0 : r0 + CONV_ROWS, :]
            for k in range(CONV_K):
                gw_ref[8 * k : 8 * k + 8, :] += _rowsum8(ur * dc_ahead(r0, k))
        du = dubuf[...]
        dz_ref[:, 0:D] = (du * sg).astype(BF16)
        dz_ref[:, D:] = (du * val * sg * (1.0 - sg)).astype(BF16)

    def cur(cb):
        return pl.BlockSpec((None, tm, D), lambda s, i: (s, i, cb))

    nxt = pl.BlockSpec((None, CONV_HALO, D), lambda s, i: (s, jnp.minimum((i + 1) * r, S // CONV_HALO - 1), 0))
    vec = pl.BlockSpec((1, D), lambda s, i: (0, 0))
    return pl.pallas_call(
        body, name="conv_bwd", grid=(NS, nt),
        out_shape=[jax.ShapeDtypeStruct((NS, S, 2 * D), BF16), jax.ShapeDtypeStruct((CONV_HALO * 8, D), F32),
                   jax.ShapeDtypeStruct((24, D), F32)] + ex.out_shape,
        in_specs=[cur(CB_VAL), cur(CB_GLU), cur(0), cur(0), nxt, nxt,
                  pl.BlockSpec((CONV_HALO, D), lambda s, i: (0, 0)), vec, vec] + ex.specs,
        out_specs=[pl.BlockSpec((None, tm, 2 * D), lambda s, i: (s, i, 0)),
                   pl.BlockSpec((CONV_HALO * 8, D), lambda s, i: (0, 0)), pl.BlockSpec((24, D), lambda s, i: (0, 0))] + ex.specs,
        scratch_shapes=[pltpu.VMEM((8, tm + CONV_HALO, D), F32), pltpu.VMEM((tm, D), F32), pltpu.VMEM((tm, D), F32)] + ex.scratch,
        compiler_params=_cparams("arbitrary", "arbitrary"),
    )(z3, z3, c3, dcs3, c3, dcs3, w_dw, ln_g, ln_b, *ex.arrs)


def _attn_bwd(qr3, kv3, do3, o3, lse3, sinks, cos3, sa3, sb3):
    NS, S, _ = qr3.shape

    def body(sink_ref, q_ref, kvc_ref, kvp_ref, do_ref, o_ref, lse_ref, cos_ref, sa_ref, sb_ref,
             dq_ref, dkc_ref, dkp_ref, dvc_ref, dvp_ref, dsk_ref):
        has_prev = pl.program_id(1) > 0
        mask = _attn_mask(has_prev)
        tabs = cos_ref[...], sa_ref[...], sb_ref[...]
        low_q = lax.broadcasted_iota(jnp.int32, (QROWS, 128), 1) < HEAD_DIM
        dk_g, dv_g = [], []
        for g in range(N_KV):
            qs = _stack_pairs(q_ref, g)
            dos = _stack_pairs(do_ref, g)
            prod = dos.astype(F32) * _stack_pairs(o_ref, g).astype(F32)
            deltas = [jnp.sum(jnp.where(low_q, prod, 0.0), axis=-1, keepdims=True),
                      jnp.sum(jnp.where(low_q, 0.0, prod), axis=-1, keepdims=True)]
            dq_acc, dk_par, dv_par = None, [], []
            for par in range(2):
                k2 = _kv2_block(kvc_ref, kvp_ref, has_prev, _kv2_col(0, g, par))
                v2 = _kv2_block(kvc_ref, kvp_ref, has_prev, _kv2_col(1, g, par))
                lse = _head_col_load(lse_ref, g, par)
                p = jnp.exp(jnp.where(mask, _dot_nt(qs, k2), NEG_BIG) - lse)
                ds = (p * (_dot_nt(dos, v2) - deltas[par])).astype(BF16)
                dq = _dot(ds, k2)
                dq_acc = dq if dq_acc is None else dq_acc + dq
                dk_par.append(_dot_tn(ds, qs))
                dv_par.append(_dot_tn(p.astype(BF16), dos))
                _head_col_store(dsk_ref, g, par, -jnp.exp(_sink_col(sink_ref, g, par) - lse) * deltas[par])
            for j in range(PAIRS):
                dq_pair = _rope(dq_acc[j * BLK : (j + 1) * BLK], *tabs, sign=-1.0) * (HEAD_DIM ** -0.5)
                dq_ref[:, (PAIRS * g + j) * 128 : (PAIRS * g + j + 1) * 128] = dq_pair.astype(BF16)
            dk_g.append(dk_par[0] + pltpu.roll(dk_par[1], HEAD_DIM, 1))
            dv_g.append(dv_par[0] + pltpu.roll(dv_par[1], HEAD_DIM, 1))
        low_k = lax.broadcasted_iota(jnp.int32, (2 * BLK, KV_W), 1) < HEAD_DIM
        dk = jnp.where(low_k, dk_g[0], pltpu.roll(dk_g[1], HEAD_DIM, 1))
        dv = jnp.where(low_k, dv_g[0], pltpu.roll(dv_g[1], HEAD_DIM, 1))
        dkp_ref[...] = dk[0:BLK]
        dkc_ref[...] = dk[BLK:]
        dvp_ref[...] = dv[0:BLK]
        dvc_ref[...] = dv[BLK:]

    qspec = pl.BlockSpec((None, BLK, D), lambda s, n: (s, n, 0))
    kvspec = pl.BlockSpec((None, BLK, KV_W), lambda s, n: (s, n, 0))
    hspec = pl.BlockSpec((None, BLK, N_HEADS), lambda s, n: (s, n, 0))
    kv = jax.ShapeDtypeStruct((NS, S, KV_W), F32)
    return pl.pallas_call(
        body, name="attn_bwd", grid=(NS, S // BLK),
        out_shape=[jax.ShapeDtypeStruct((NS, S, D), BF16), kv, kv, kv, kv, jax.ShapeDtypeStruct((NS, S, N_HEADS), F32)],
        in_specs=list(_attn_specs()) + [qspec, qspec, hspec, kvspec, kvspec, kvspec],
        out_specs=[qspec, kvspec, kvspec, kvspec, kvspec, hspec],
        compiler_params=_cparams("parallel", "parallel"),
    )(sinks, qr3, kv3, kv3, do3, o3, lse3, cos3, sa3, sb3)


def _attn_post(dkc3, dkp3, dvc3, dvp3, cos3, sa3, sb3):
    NS, S, _ = dkc3.shape
    tm = min(8 * BLK, S)
    nt = S // tm

    def body(dkc_ref, dkp_ref, dkn_ref, dvc_ref, dvp_ref, dvn_ref, cos_ref, sa_ref, sb_ref, dkv_ref):
        has_next = pl.program_id(1) < nt - 1

        def join(cur_ref, prev_ref, next_ref):
            ahead = jnp.where(has_next, next_ref[...], 0.0)
            shifted = ahead if tm == BLK else jnp.concatenate([prev_ref[BLK:, :], ahead], axis=0)
            return cur_ref[...] + shifted

        tabs = cos_ref[...], sa_ref[...], sb_ref[...]
        dkv_ref[:, 0:KV_W] = _rope(join(dkc_ref, dkp_ref, dkn_ref), *tabs, sign=-1.0).astype(BF16)
        dkv_ref[:, KV_W:] = join(dvc_ref, dvp_ref, dvn_ref).astype(BF16)

    cur = pl.BlockSpec((None, tm, KV_W), lambda s, j: (s, j, 0))
    nxt = pl.BlockSpec((None, BLK, KV_W), lambda s, j: (s, jnp.minimum((j + 1) * (tm // BLK), S // BLK - 1), 0))
    return pl.pallas_call(
        body, name="attn_post", grid=(NS, nt),
        out_shape=jax.ShapeDtypeStruct((NS, S, 2 * KV_W), BF16),
        in_specs=[cur, cur, nxt, cur, cur, nxt, cur, cur, cur],
        out_specs=pl.BlockSpec((None, tm, 2 * KV_W), lambda s, j: (s, j, 0)),
        compiler_params=_cparams("parallel", "parallel"),
    )(dkc3, dkp3, dkp3, dvc3, dvp3, dvp3, cos3, sa3, sb3)


def _in_bwd(segs, w_in_t, x2, dx1, ln_pre, after):
    T = x2.shape[0]
    tm = min(ROW_TILE, T)
    nt = T // tm
    ns = len(segs)
    widths = [s.shape[1] for s in segs]

    def body(*refs):
        seg_refs = refs[:ns]
        w_ref, x_ref, dx1_ref, g_ref, _, gx_ref, glp_ref = refs[ns:]
        i = pl.program_id(0)

        @pl.when(i == 0)
        def _():
            glp_ref[...] = jnp.zeros_like(glp_ref)

        dh = None
        zc = 0
        for sref, w in zip(seg_refs, widths):
            part = _dot(sref[...], w_ref[_orig_col(zc) : _orig_col(zc) + w, :])
            dh = part if dh is None else dh + part
            zc += w
        xv = x_ref[...]
        r1 = lax.rsqrt(jnp.mean(xv * xv, axis=-1, keepdims=True) + EPS)
        xhat = xv * r1
        glp_ref[...] += _rowsum8(dh * xhat)
        dhg = dh * g_ref[...]
        gx_ref[...] = dx1_ref[...] + r1 * (dhg - xhat * jnp.mean(dhg * xhat, axis=-1, keepdims=True))

    row = pl.BlockSpec((tm, D), lambda i: (i, 0))
    return pl.pallas_call(
        body, name="in_bwd", grid=(nt,),
        out_shape=[jax.ShapeDtypeStruct((T, D), F32), jax.ShapeDtypeStruct((8, D), F32)],
        in_specs=[pl.BlockSpec((tm, w), lambda i: (i, 0)) for w in widths]
        + [_weight_spec((NW, D)), row, row, _weight_spec((1, D)), pl.BlockSpec(memory_space=pl.ANY)],
        out_specs=[row, pl.BlockSpec((8, D), lambda i: (0, 0))],
        compiler_params=_cparams("arbitrary"),
    )(*segs, w_in_t, x2, dx1, ln_pre, after)


def _grad_matmul(a, b, name, rows=None, into=None):
    T, M = a.shape
    N = b.shape[1]
    tk = min(1024, T)
    nk = T // tk

    def body(a_ref, b_ref, *rest):
        o_ref, acc = rest[-2:]
        k = pl.program_id(0)

        @pl.when(k == 0)
        def _():
            acc[...] = jnp.zeros_like(acc)

        acc[...] += _dot_tn(a_ref[...].astype(BF16), b_ref[...])

        @pl.when(k == nk - 1)
        def _():
            o_ref[...] = acc[...].astype(BF16)

    in_specs = [pl.BlockSpec((tk, M), lambda k: (k, 0)), pl.BlockSpec((tk, N), lambda k: (k, 0))]
    if rows is None:
        out_shape, out_spec = (M, N), pl.BlockSpec((M, N), lambda k: (0, 0))
    else:
        out_shape, out_spec = (rows[1], N), pl.BlockSpec((pl.Element(M), pl.Element(N)), lambda k: (rows[0], 0))
    operands = (a, b) if into is None else (a, b, into)
    return pl.pallas_call(
        body, name=name, grid=(nk,), out_shape=jax.ShapeDtypeStruct(out_shape, BF16),
        in_specs=in_specs if into is None else in_specs + [pl.BlockSpec(memory_space=pl.ANY)],
        out_specs=out_spec, input_output_aliases={} if into is None else {2: 0},
        scratch_shapes=[pltpu.VMEM((M, N), F32)],
        compiler_params=_cparams("arbitrary"),
    )(*operands)


def _pack_small(gw, gvec, glp_pre, glp_post, dsk):
    T = dsk.shape[0]

    def body(gw_ref, gvec_ref, pre_ref, post_ref, dsk_ref, gdw_ref, gs_ref):
        gwf = gw_ref[...].reshape(CONV_HALO, 8, D).sum(axis=1)
        for d in range(N_DEV):
            gdw_ref[d] = gwf[:, 128 * d : 128 * (d + 1)]
        gs_ref[...] = jnp.zeros_like(gs_ref)
        gs_ref[0:1, :] = jnp.sum(pre_ref[...], axis=0, keepdims=True)
        gs_ref[1:2, :] = jnp.sum(post_ref[...], axis=0, keepdims=True)
        gs_ref[2:3, :] = jnp.sum(gvec_ref[16:24, :], axis=0, keepdims=True)
        gs_ref[3:4, :] = jnp.sum(gvec_ref[0:8, :], axis=0, keepdims=True)
        gs_ref[4:5, :] = jnp.sum(gvec_ref[8:16, :], axis=0, keepdims=True)
        gs_ref[5:6, 0:N_HEADS] = jnp.sum(dsk_ref[...], axis=0, keepdims=True)

    return pl.pallas_call(
        body, name="pack_small",
        out_shape=[jax.ShapeDtypeStruct((N_DEV, CONV_HALO, 128), F32), jax.ShapeDtypeStruct((8, D), F32)],
        compiler_params=_cparams(),
    )(gw, gvec, glp_pre, glp_post, dsk)


def _adam_update(g, w_ref, m_ref, v_ref, g_ref, d_ref, nm_ref, nv_ref):
    nm = ADAM_B1 * m_ref[...] + (1.0 - ADAM_B1) * g
    nv = ADAM_B2 * v_ref[...] + (1.0 - ADAM_B2) * (g * g)
    m_hat = nm / (1.0 - ADAM_B1 ** ADAM_STEP)
    v_hat = nv / (1.0 - ADAM_B2 ** ADAM_STEP)
    g_ref[...] = g
    d_ref[...] = -ADAM_LR * (m_hat / (jnp.sqrt(v_hat) + ADAM_EPS) + ADAM_WD * w_ref[...])
    nm_ref[...] = nm
    nv_ref[...] = nv


def _adamw(parts, w, m, v, name):
    R, C = w.shape
    tr = R if R <= 256 else 128

    def body(p_ref, *rest):
        g = p_ref[0].astype(F32)
        for s in range(1, N_DEV):
            g = g + p_ref[s].astype(F32)
        _adam_update(g, *rest)

    blk = pl.BlockSpec((tr, C), lambda i: (i, 0))
    return pl.pallas_call(
        body, name=name, grid=(R // tr,), out_shape=[jax.ShapeDtypeStruct((R, C), F32)] * 4,
        in_specs=[pl.BlockSpec((N_DEV, tr, C), lambda i: (0, i, 0)), blk, blk, blk], out_specs=[blk] * 4,
        compiler_params=_cparams("parallel"),
    )(parts, w, m, v)


def _adamw_many(groups, name):
    n = len(groups)
    R, C = groups[0][1].shape
    tr = 32

    def body(*refs):
        ins, outs = refs[: 4 * n], refs[4 * n :]
        for j in range(n):
            p_ref = ins[4 * j]
            g = p_ref[0].astype(F32)
            for s in range(1, N_DEV):
                g = g + p_ref[s].astype(F32)
            _adam_update(g, *ins[4 * j + 1 : 4 * j + 4], *outs[4 * j : 4 * j + 4])

    blk = pl.BlockSpec((tr, C), lambda i: (i, 0))
    res = pl.pallas_call(
        body, name=name, grid=(R // tr,), out_shape=[jax.ShapeDtypeStruct((R, C), F32)] * (4 * n),
        in_specs=[pl.BlockSpec((N_DEV, tr, C), lambda i: (0, i, 0)), blk, blk, blk] * n, out_specs=[blk] * (4 * n),
        compiler_params=_cparams("parallel"),
    )(*[a for grp in groups for a in grp])
    return [res[4 * j : 4 * j + 4] for j in range(n)]


def _adamw_own(parts, own, me, w, m, v, name):
    R, C = w.shape
    tr = max(t for t in range(16, 513, 16) if R % t == 0)

    def body(me_ref, p_ref, own_ref, *rest):
        g = None
        for s in range(N_DEV):
            part = jnp.where(me_ref[0] == s, own_ref[...], p_ref[s]).astype(F32)
            g = part if g is None else g + part
        _adam_update(g, *rest)

    blk = pl.BlockSpec((tr, C), lambda i, me: (i, 0))
    return pl.pallas_call(
        body, name=name, out_shape=[jax.ShapeDtypeStruct((R, C), F32)] * 4,
        grid_spec=pltpu.PrefetchScalarGridSpec(
            num_scalar_prefetch=1, grid=(R // tr,),
            in_specs=[pl.BlockSpec((N_DEV, tr, C), lambda i, me: (0, i, 0)),
                      pl.BlockSpec((None, tr, C), lambda i, me: (me[0], i, 0)), blk, blk, blk],
            out_specs=[blk] * 4),
        compiler_params=_cparams("parallel"),
    )(me, parts, own, w, m, v)


def kernel(x, p, positions, w_in, ln_pre, ln_post, w_dw, b_dw, conv_ln_g, conv_ln_b, w_pw, sinks, w_br_conv, w_br_attn, w_out, w_ple_gate, w_ple_proj, loss_target, m_w_in, m_ln_pre, m_ln_post, m_w_dw, m_b_dw, m_conv_ln_g, m_conv_ln_b, m_w_pw, m_sinks, m_w_br_conv, m_w_br_attn, m_w_out, m_w_ple_gate, m_w_ple_proj, v_w_in, v_ln_pre, v_ln_post, v_w_dw, v_b_dw, v_conv_ln_g, v_conv_ln_b, v_w_pw, v_sinks, v_w_br_conv, v_w_br_attn, v_w_out, v_w_ple_gate, v_w_ple_proj):
    NS, S, _ = x.shape
    T = NS * S
    x2 = x.reshape(T, D)
    p2 = p.reshape(T, PLE)
    tgt = loss_target.reshape(T, D)
    pos = positions.reshape(T, 1)

    row_sharded = [w_pw[0], w_br_conv[0], w_br_attn[0], w_out[0], w_ple_gate[0]]
    sh_rows = D // N_DEV
    w_t, m_t, v_t = (jnp.swapaxes(a[0], 0, 1) for a in (w_in, m_w_in, v_w_in))
    cos, sa, sb, g_in = _rope_tables(pos, _TwoLevelGather([w_t.astype(BF16)]))
    w_in_f = g_in.reshape(NW, D)
    gather_rest = _Exchange([], [w.astype(BF16) for w in row_sharded] + [
        w_ple_proj[0].astype(BF16), jnp.pad(w_dw[0], ((0, CONV_HALO - CONV_K), (0, 0)))])

    in_out = _in_proj(x2, ln_pre, w_in_f, cos, sa, sb, gather_rest)
    z2, h, qr, kv2 = in_out[:4]
    g_rows, g_pp, g_dw = in_out[4:9], in_out[9], in_out[10]
    full = [g.reshape(D, D) for g in g_rows]
    w_pp_f = g_pp.transpose(1, 0, 2).reshape(PLE, D)
    w_dw_f = g_dw.transpose(1, 0, 2).reshape(CONV_HALO, D)
    z3 = z2.reshape(NS, S, NW)
    c3, cs3 = _conv_fwd(z3, w_dw_f, b_dw, conv_ln_g, conv_ln_b)
    qr3, kv3 = qr.reshape(NS, S, D), kv2.reshape(NS, S, 4 * N_KV * KV_W)
    sinks1 = sinks.reshape(N_HEADS)
    o3, lse3 = _attn_fwd(qr3, kv3, sinks1)
    o = o3.reshape(T, D)
    cs = cs3.reshape(T, D)
    (dya0, ya1, dya, yb0, dyb, m, dmo, x1, dgl, dpp, dcs, do, dcg, dag, dgc, dga, dx1, loss_blk, glp_post) = _mid(
        cs, o, z2, x2, p2, tgt, ln_post, full + [w_pp_f])

    gp_rows = [_grad_matmul(a, b, nm).reshape(N_DEV, sh_rows, D) for a, b, nm in (
        (cs, dya0, "grad_w_pw"), (ya1, dya, "grad_w_br_conv"), (yb0, dyb, "grad_w_br_attn"),
        (m, dmo, "grad_w_out"), (x1, dgl, "grad_w_ple_gate"))]
    gp_pp = _grad_matmul(p2, dpp, "grad_w_ple_proj").reshape(PLE, N_DEV, D // N_DEV).transpose(1, 0, 2)
    conv_out = _conv_bwd(z3, c3, dcs.reshape(NS, S, D), w_dw_f, conv_ln_g, conv_ln_b,
                         _Exchange(gp_rows + [gp_pp], [loss_blk]))
    dzvu3, gw, gvec, r_rows, r_pp, r_loss = conv_out[0], conv_out[1], conv_out[2], conv_out[3:8], conv_out[8], conv_out[9]
    loss = jnp.sum(r_loss[:, 0, 0])
    tab3 = [t.reshape(NS, S, 128) for t in (cos, sa, sb)]
    dq3, dkc3, dkp3, dvc3, dvp3, dsk3 = _attn_bwd(qr3, kv3, do.reshape(NS, S, D), o3, lse3, sinks1, *tab3)
    dkv3 = _attn_post(dkc3, dkp3, dvc3, dvp3, *tab3)
    segs = [dzvu3.reshape(T, 2 * D), dcg, dq3.reshape(T, D), dag, dgc, dga, dkv3.reshape(T, 2 * KV_W)]
    gp_in, zc = None, 0
    for j, s in enumerate(segs):
        gp_in = _grad_matmul(s, h, f"grad_w_in_{j}", rows=(_orig_col(zc), NW), into=gp_in)
        zc += s.shape[1]
    gp_in = gp_in.reshape(N_DEV, NW // N_DEV, D)
    in_send, in_recv, in_own, in_land, sent = _scatter_send(gp_in, "w_in_grad_send")
    grad_x2, glp_pre = _in_bwd(segs, w_in_f, x2, dx1, ln_pre, sent)
    gp_dw, gp_small = _pack_small(gw, gvec, glp_pre, glp_post, dsk3.reshape(T, N_HEADS))
    r_dw, r_small = _Exchange([gp_dw], [gp_small]).alone("small_grad_exchange")

    res = {}
    names_rows = ["w_pw", "w_br_conv", "w_br_attn", "w_out", "w_ple_gate"]
    wmv = {"w_pw": (w_pw, m_w_pw, v_w_pw), "w_br_conv": (w_br_conv, m_w_br_conv, v_w_br_conv),
           "w_br_attn": (w_br_attn, m_w_br_attn, v_w_br_attn), "w_out": (w_out, m_w_out, v_w_out),
           "w_ple_gate": (w_ple_gate, m_w_ple_gate, v_w_ple_gate)}
    rows_res = _adamw_many([(parts, *(a[0] for a in wmv[nm])) for nm, parts in zip(names_rows, r_rows)], "adamw_rows")
    res.update(zip(names_rows, rows_res))
    res["w_ple_proj"] = _adamw(r_pp, w_ple_proj[0], m_w_ple_proj[0], v_w_ple_proj[0], "adamw_w_ple_proj")
    pad_dw = lambda a: jnp.pad(a[0], ((0, CONV_HALO - CONV_K), (0, 0)))
    res["w_dw"] = [a[:CONV_K] for a in _adamw(r_dw, pad_dw(w_dw), pad_dw(m_w_dw), pad_dw(v_w_dw), "adamw_w_dw")]

    def stack_small(a_pre, a_post, a_b, a_g, a_bb, a_s):
        sk = jnp.pad(a_s, ((0, 0), (0, D - N_HEADS)))
        return jnp.concatenate([a_pre, a_post, a_b, a_g, a_bb, sk, jnp.zeros((2, D), F32)], axis=0)

    small = _adamw(
        r_small, stack_small(ln_pre, ln_post, b_dw, conv_ln_g, conv_ln_b, sinks),
        stack_small(m_ln_pre, m_ln_post, m_b_dw, m_conv_ln_g, m_conv_ln_b, m_sinks),
        stack_small(v_ln_pre, v_ln_post, v_b_dw, v_conv_ln_g, v_conv_ln_b, v_sinks), "adamw_small")
    for j, nm in enumerate(["ln_pre", "ln_post", "b_dw", "conv_ln_g", "conv_ln_b"]):
        res[nm] = [a[j] for a in small]
    res["sinks"] = [a[5, :N_HEADS] for a in small]
    in_own, in_land = _scatter_wait(in_send, in_recv, in_own, in_land, small[0], "w_in_grad_wait")
    me = _slot(*_my_place()).astype(jnp.int32).reshape(1)
    res["w_in"] = [jnp.swapaxes(a, 0, 1) for a in _adamw_own(in_land, in_own, me, w_t, m_t, v_t, "adamw_w_in")]

    order = ["w_in", "ln_pre", "ln_post", "w_dw", "b_dw", "conv_ln_g", "conv_ln_b", "w_pw", "sinks", "w_br_conv",
             "w_br_attn", "w_out", "w_ple_gate", "w_ple_proj"]
    outs = [loss, grad_x2.reshape(NS, S, D)]
    for kind in range(4):
        outs += [res[nm][kind][None] for nm in order]
    return tuple(outs)
```

```python
import numpy as np

import jax
import jax.numpy as jnp
from jax import lax
from jax.experimental import pallas as pl
from jax.experimental.pallas import tpu as pltpu

F32 = jnp.float32
BF16 = jnp.bfloat16

D = 1024
N_HEADS = 16
N_KV = 2
HEAD_DIM = 64
GROUP = N_HEADS // N_KV
KV_W = N_KV * HEAD_DIM
CONV_K = 31
CONV_HALO = 32
BLK = 128
ROPE_DIM = 16
ROPE_THETA = 500000.0
EPS = 1e-6
PLE = 256
NW = 7 * D + 2 * KV_W
N_DEV = 8

CB_VAL, CB_GLU, CB_CGATE, CB_Q, CB_AGATE, CB_GCONV, CB_GATTN = range(7)
CB_K = 7 * D // KV_W

ADAM_LR, ADAM_B1, ADAM_B2, ADAM_EPS, ADAM_WD, ADAM_STEP = 0.001, 0.9, 0.999, 1e-08, 0.01, 10

VMEM_LIMIT = 56 * 1024 * 1024
ROW_TILE = 256
CONV_TILE = 256


def _cparams(*sem):
    return pltpu.CompilerParams(dimension_semantics=sem if sem else None, vmem_limit_bytes=VMEM_LIMIT)


def _sig(v):
    return 1.0 / (1.0 + jnp.exp(-v))


def _rowsum8(a):
    return a.reshape(a.shape[0] // 8, 8, a.shape[1]).sum(axis=0)


def _dot(a, b):
    return jnp.dot(a, b, preferred_element_type=F32)


def _dot_nt(a, b):
    return lax.dot_general(a, b, (((1,), (1,)), ((), ())), preferred_element_type=F32)


def _dot_tn(a, b):
    return lax.dot_general(a, b, (((0,), (0,)), ((), ())), preferred_element_type=F32)


def _orig_col(zc):
    if zc < 4 * D:
        return zc
    return zc - 7 * D + 4 * D if zc >= 7 * D else zc + 2 * KV_W


def _my_place():
    return lax.axis_index("x"), lax.axis_index("y"), lax.axis_index("c")


def _slot(px, py, pc):
    return 4 * px + 2 * py + pc


class _TwoLevelGather:
    def __init__(self, shards):
        self.arrs = list(shards)
        self.n = len(self.arrs)
        self.out_shape = [jax.ShapeDtypeStruct((N_DEV,) + s.shape, s.dtype) for s in shards]
        self.specs = [pl.BlockSpec(memory_space=pl.ANY)] * self.n
        self.scratch = [pltpu.SemaphoreType.DMA((self.n, 7)), pltpu.SemaphoreType.DMA((self.n, 7)),
                        pltpu.SemaphoreType.DMA((self.n,))]

    def _plan(self, ins, outs, sems):
        send_sems, recv_sems, local_sems = sems
        x, y, c = _my_place()
        me, sibling = (x, y, c), (x, y, 1 - c)
        chips = [(1 - x, y), (x, 1 - y), (1 - x, 1 - y)]

        def copy(a, k, block, to, src=None):
            rows = outs[a].at[_slot(*block)]
            return pltpu.make_async_remote_copy(
                src_ref=rows if src is None else src, dst_ref=rows, send_sem=send_sems.at[a, k],
                recv_sem=recv_sems.at[a, k], device_id=to, device_id_type=pl.DeviceIdType.MESH)

        mine = [pltpu.make_async_copy(ins[a], outs[a].at[_slot(*me)], local_sems.at[a]) for a in range(self.n)]
        first = []
        for a in range(self.n):
            first.append(copy(a, 0, me, sibling, src=ins[a]))
            first += [copy(a, 1 + j, me, (*chips[j], c), src=ins[a]) for j in range(2)]
        return copy, mine, first, me, sibling, chips, c

    def start(self, ins, outs, sems):
        _, mine, first, *_ = self._plan(ins, outs, sems)
        for cp in mine + first:
            cp.start()

    def finish(self, ins, outs, sems):
        copy, mine, first, me, sibling, chips, c = self._plan(ins, outs, sems)

        def land_and_pass(a, j, relay_to=None):
            copy(a, 1 + j, (*chips[j], c), me).wait_recv()
            if relay_to is not None:
                copy(a, 3, (*chips[j], c), (*chips[relay_to], c)).start()
            copy(a, 4 + j, (*chips[j], c), sibling).start()

        for a in range(self.n):
            @pl.when(c == 0)
            def _():
                land_and_pass(a, 0, relay_to=1)
                land_and_pass(a, 1)

            @pl.when(c == 1)
            def _():
                land_and_pass(a, 1, relay_to=0)
                land_and_pass(a, 0)

            copy(a, 3, (*chips[2], c), me).wait_recv()
            copy(a, 6, (*chips[2], c), sibling).start()
        for a in range(self.n):
            copy(a, 0, sibling, me).wait_recv()
            for j in range(3):
                copy(a, 4 + j, (*chips[j], 1 - c), me).wait_recv()
        for cp in first:
            cp.wait_send()
        for a in range(self.n):
            for k in (3, 4, 5, 6):
                copy(a, k, me, me, src=ins[a]).wait_send()
        for cp in mine:
            cp.wait()

    def carried(self, refs_in, refs_out, sems, first, last):
        @pl.when(first)
        def _():
            self.start(refs_in, refs_out, sems)

        @pl.when(last)
        def _():
            self.finish(refs_in, refs_out, sems)


class _Exchange:
    def __init__(self, scatter, bcast):
        self.arrs = list(scatter) + list(bcast)
        self.n, self.n_sc = len(self.arrs), len(scatter)
        self.out_shape = [jax.ShapeDtypeStruct(a.shape, a.dtype) for a in scatter]
        self.out_shape += [jax.ShapeDtypeStruct((N_DEV,) + a.shape, a.dtype) for a in bcast]
        self.specs = [pl.BlockSpec(memory_space=pl.ANY)] * self.n
        self.scratch = [pltpu.SemaphoreType.DMA((self.n, 7)), pltpu.SemaphoreType.DMA((self.n, 7)),
                        pltpu.SemaphoreType.DMA((self.n,))]

    def _copies(self, ins, outs, sems):
        send_sems, recv_sems, local_sems = sems
        x, y, c = _my_place()
        me = _slot(x, y, c)
        peers = _peers(x, y, c)
        mine, sends, arrivals = [], [], []
        for a in range(self.n):
            src = ins[a].at[me] if a < self.n_sc else ins[a]
            mine.append(pltpu.make_async_copy(src, outs[a].at[me], local_sems.at[a]))
        for k, peer in enumerate(peers):
            for a in range(self.n):
                src = ins[a].at[_slot(*peer)] if a < self.n_sc else ins[a]
                sends.append(pltpu.make_async_remote_copy(
                    src_ref=src, dst_ref=outs[a].at[me], send_sem=send_sems.at[a, k], recv_sem=recv_sems.at[a, k],
                    device_id=peer, device_id_type=pl.DeviceIdType.MESH))
                rows = outs[a].at[_slot(*peer)]
                arrivals.append(pltpu.make_async_remote_copy(
                    src_ref=rows, dst_ref=rows, send_sem=send_sems.at[a, k], recv_sem=recv_sems.at[a, k],
                    device_id=peer, device_id_type=pl.DeviceIdType.MESH))
        return mine, sends, arrivals

    def start(self, ins, outs, sems):
        mine, sends, _ = self._copies(ins, outs, sems)
        for cp in mine + sends:
            cp.start()

    def finish(self, ins, outs, sems):
        mine, sends, arrivals = self._copies(ins, outs, sems)
        for cp in arrivals:
            cp.wait_recv()
        for cp in sends:
            cp.wait_send()
        for cp in mine:
            cp.wait()

    def carried(self, refs_in, refs_out, sems, first, last):
        @pl.when(first)
        def _():
            self.start(refs_in, refs_out, sems)

        @pl.when(last)
        def _():
            self.finish(refs_in, refs_out, sems)

    def alone(self, name):
        n = self.n

        def body(*refs):
            ins, outs, sems = refs[:n], refs[n : 2 * n], refs[2 * n :]
            self.start(ins, outs, sems)
            self.finish(ins, outs, sems)

        return pl.pallas_call(body, name=name, out_shape=self.out_shape, in_specs=self.specs, out_specs=self.specs,
                              scratch_shapes=self.scratch)(*self.arrs)


def _peers(x, y, c):
    return [(1 - x if k & 4 else x, 1 - y if k & 2 else y, 1 - c if k & 1 else c) for k in range(1, N_DEV)]


def _scatter_send(g, name):
    hbm = pl.BlockSpec(memory_space=pltpu.HBM)
    sem = pl.BlockSpec(memory_space=pltpu.SEMAPHORE)

    def body(g_ref, land_ref, send_sems, recv_sems, g_thru, land_thru, token):
        x, y, c = _my_place()
        me = _slot(x, y, c)
        for k, peer in enumerate(_peers(x, y, c)):
            pltpu.make_async_remote_copy(
                src_ref=g_ref.at[_slot(*peer)], dst_ref=land_ref.at[me], send_sem=send_sems.at[k], recv_sem=recv_sems.at[k],
                device_id=peer, device_id_type=pl.DeviceIdType.MESH).start()
        token[...] = jnp.zeros_like(token)

    return pl.pallas_call(
        body, name=name,
        out_shape=(pltpu.SemaphoreType.DMA((N_DEV - 1,)), pltpu.SemaphoreType.DMA((N_DEV - 1,)),
                   pltpu.HBM(g.shape, g.dtype), pltpu.HBM(g.shape, g.dtype), jax.ShapeDtypeStruct((8, 128), F32)),
        in_specs=(hbm, hbm), out_specs=(sem, sem, hbm, hbm, pl.BlockSpec(memory_space=pltpu.VMEM)),
        input_output_aliases={0: 2, 1: 3},
        compiler_params=pltpu.CompilerParams(has_side_effects=pltpu.SideEffectType.DATAFLOW_SIDE_EFFECTING),
    )(pltpu.with_memory_space_constraint(g, pltpu.HBM),
      pltpu.with_memory_space_constraint(lax.empty(g.shape, g.dtype), pltpu.HBM))


def _scatter_wait(send_sems, recv_sems, g_thru, land_thru, after, name):
    hbm = pl.BlockSpec(memory_space=pltpu.HBM)
    sem = pl.BlockSpec(memory_space=pltpu.SEMAPHORE)

    def body(g_ref, land_ref, send_sems, recv_sems, after_ref, g_out, land_out):
        x, y, c = _my_place()
        for k, peer in enumerate(_peers(x, y, c)):
            cp = pltpu.make_async_remote_copy(
                src_ref=g_ref.at[_slot(*peer)], dst_ref=land_ref.at[_slot(*peer)], send_sem=send_sems.at[k],
                recv_sem=recv_sems.at[k], device_id=peer, device_id_type=pl.DeviceIdType.MESH)
            cp.wait_send()
            cp.wait_recv()

    return pl.pallas_call(
        body, name=name,
        out_shape=(pltpu.HBM(g_thru.shape, g_thru.dtype), pltpu.HBM(land_thru.shape, land_thru.dtype)),
        in_specs=(hbm, hbm, sem, sem, pl.BlockSpec(memory_space=pl.ANY)), out_specs=(hbm, hbm),
        input_output_aliases={0: 0, 1: 1},
        compiler_params=pltpu.CompilerParams(has_side_effects=pltpu.SideEffectType.DATAFLOW_SIDE_EFFECTING),
    )(g_thru, land_thru, send_sems, recv_sems, after)


def _host_split(refs, n_in, n_out, n_scratch, ex):
    k = ex.n if ex is not None else 0
    a = n_in
    b = a + k
    c = b + n_out
    d = c + k
    e = d + n_scratch
    return refs[:a], refs[a:b], refs[b:c], refs[c:d], refs[d:e], refs[e:]


def _rope_tables(pos, ex):
    T = pos.shape[0]
    tm = min(1024, T)
    nt = T // tm
    lane = np.arange(128) % HEAD_DIM
    inv = np.power(np.float32(ROPE_THETA), -np.arange(0, ROPE_DIM, 2, dtype=np.float32) / np.float32(ROPE_DIM)).astype(np.float32)
    half = ROPE_DIM // 2
    invf = np.where(lane < ROPE_DIM, inv[lane % half], 0.0).astype(np.float32)[None, :]
    m_a = (lane < half).astype(np.float32)[None, :]
    m_b = ((lane >= half) & (lane < ROPE_DIM)).astype(np.float32)[None, :]

    def body(*refs):
        ins, ex_in, (cos_ref, sa_ref, sb_ref), ex_out, _, ex_sems = _host_split(refs, 4, 3, 0, ex)
        pos_ref, invf_ref, ma_ref, mb_ref = ins
        i = pl.program_id(0)
        ex.carried(ex_in, ex_out, ex_sems, i == 0, i == nt - 1)
        ang = pos_ref[...].astype(F32) * invf_ref[...]
        sn = jnp.sin(ang)
        cos_ref[...] = jnp.cos(ang)
        sa_ref[...] = -sn * ma_ref[...]
        sb_ref[...] = sn * mb_ref[...]

    row = pl.BlockSpec((tm, 128), lambda i: (i, 0))
    cst = pl.BlockSpec((1, 128), lambda i: (0, 0))
    return pl.pallas_call(
        body, name="rope_tables", grid=(nt,), out_shape=[jax.ShapeDtypeStruct((T, 128), F32)] * 3 + ex.out_shape,
        in_specs=[pl.BlockSpec((tm, 1), lambda i: (i, 0)), cst, cst, cst] + ex.specs, out_specs=[row] * 3 + ex.specs,
        scratch_shapes=ex.scratch,
        compiler_params=_cparams("arbitrary"),
    )(pos, jnp.asarray(invf), jnp.asarray(m_a), jnp.asarray(m_b), *ex.arrs)


def _rope(t, cos, sa, sb, sign=1.0):
    parts = []
    for i in range(t.shape[1] // 128):
        ti = t[:, 128 * i : 128 * (i + 1)]
        up = pltpu.roll(ti, 128 - ROPE_DIM // 2, 1)
        dn = pltpu.roll(ti, ROPE_DIM // 2, 1)
        parts.append(ti * cos + sign * (up * sa + dn * sb))
    return parts[0] if len(parts) == 1 else jnp.concatenate(parts, axis=-1)


def _in_proj(x2, ln_pre, w_in, cos, sa, sb, ex):
    T = x2.shape[0]
    tm = min(512, T)
    nt = T // tm
    chunk = D
    kv_cols = 4 * N_KV * KV_W

    def body(*refs):
        ins, ex_in, (z_ref, h_ref, qr_ref, kv_ref), ex_out, _, ex_sems = _host_split(refs, 6, 4, 0, ex)
        x_ref, g_ref, w_ref, cos_ref, sa_ref, sb_ref = ins
        i = pl.program_id(0)
        ex.carried(ex_in, ex_out, ex_sems, i == 0, i == nt - 1)
        xv = x_ref[...]
        r = lax.rsqrt(jnp.mean(xv * xv, axis=-1, keepdims=True) + EPS)
        h = (xv * r * g_ref[...]).astype(BF16)
        h_ref[...] = h
        tabs = cos_ref[...], sa_ref[...], sb_ref[...]
        for c0 in range(0, NW, chunk):
            cw = min(chunk, NW - c0)
            zc = _dot_nt(h, w_ref[_orig_col(c0) : _orig_col(c0) + cw, :])
            z_ref[:, c0 : c0 + cw] = zc.astype(BF16)
            if c0 == CB_Q * D:
                qr_ref[...] = (_rope(zc, *tabs) * (HEAD_DIM ** -0.5)).astype(BF16)
            if c0 == CB_K * KV_W:
                low = lax.broadcasted_iota(jnp.int32, (tm, KV_W), 1) < HEAD_DIM
                for kind, t in ((0, _rope(zc[:, 0:KV_W], *tabs)), (1, zc[:, KV_W:])):
                    swapped = pltpu.roll(t, HEAD_DIM, 1)
                    cols = {(0, 0): jnp.where(low, t, 0.0), (0, 1): jnp.where(low, 0.0, swapped),
                            (1, 0): jnp.where(low, swapped, 0.0), (1, 1): jnp.where(low, 0.0, t)}
                    for (g, par), val in cols.items():
                        c = _kv2_col(kind, g, par)
                        kv_ref[:, c * KV_W : (c + 1) * KV_W] = val.astype(BF16)

    def row(w):
        return pl.BlockSpec((tm, w), lambda i: (i, 0))

    bf = lambda w: jax.ShapeDtypeStruct((T, w), BF16)
    return pl.pallas_call(
        body, name="in_proj", grid=(nt,),
        out_shape=[bf(NW), bf(D), bf(D), bf(kv_cols)] + ex.out_shape,
        in_specs=[row(D), _weight_spec((1, D)), _weight_spec((NW, D)), row(128), row(128), row(128)] + ex.specs,
        out_specs=[row(NW), row(D), row(D), row(kv_cols)] + ex.specs,
        scratch_shapes=ex.scratch,
        compiler_params=_cparams("arbitrary"),
    )(x2, ln_pre, w_in, cos, sa, sb, *ex.arrs)


def _conv_tiles(S):
    tm = min(CONV_TILE, S)
    return tm, S // tm, tm // CONV_HALO


CONV_ROWS_FWD = 32
CONV_ROWS = 16


def _fill_shifted(sh, rows):
    for b in range(1, 8):
        sh[b, 0:rows, :] = sh[0, b : b + rows, :]


def _conv_fwd(z3, w_dw, b_dw, ln_g, ln_b):
    NS, S, _ = z3.shape
    tm, nt, r = _conv_tiles(S)

    def body(val_ref, glu_ref, hval_ref, hglu_ref, w_ref, b_ref, g_ref, bb_ref, c_ref, cs_ref, ush, cbuf):
        i = pl.program_id(1)
        ush[0, CONV_HALO:, :] = val_ref[...].astype(F32) * _sig(glu_ref[...].astype(F32))
        uh = hval_ref[...].astype(F32) * _sig(hglu_ref[...].astype(F32))
        ush[0, 0:CONV_HALO, :] = jnp.where(i > 0, uh, 0.0)
        _fill_shifted(ush, tm + CONV_HALO - 8)
        for r0 in range(0, tm, CONV_ROWS_FWD):
            acc = jnp.zeros((CONV_ROWS_FWD, D), F32)
            for k in range(CONV_K):
                a, b = divmod(CONV_HALO - (CONV_K - 1) + k, 8)
                acc = acc + w_ref[k : k + 1, :] * ush[b, r0 + 8 * a : r0 + 8 * a + CONV_ROWS_FWD, :]
            cbuf[r0 : r0 + CONV_ROWS_FWD, :] = acc + b_ref[...]
        cv = cbuf[...]
        mu = jnp.mean(cv, axis=-1, keepdims=True)
        xc = cv - mu
        var = jnp.mean(xc * xc, axis=-1, keepdims=True)
        cl = xc * lax.rsqrt(var + EPS) * g_ref[...] + bb_ref[...]
        c_ref[...] = cv.astype(BF16)
        cs_ref[...] = (cl * _sig(cl)).astype(BF16)

    def cur(cb):
        return pl.BlockSpec((None, tm, D), lambda s, i: (s, i, cb))

    def halo(cb):
        return pl.BlockSpec((None, CONV_HALO, D), lambda s, i: (s, jnp.maximum(i * r - 1, 0), cb))

    vec = pl.BlockSpec((1, D), lambda s, i: (0, 0))
    out = pl.BlockSpec((None, tm, D), lambda s, i: (s, i, 0))
    return pl.pallas_call(
        body, name="conv_fwd", grid=(NS, nt),
        out_shape=[jax.ShapeDtypeStruct((NS, S, D), BF16)] * 2,
        in_specs=[cur(CB_VAL), cur(CB_GLU), halo(CB_VAL), halo(CB_GLU),
                  pl.BlockSpec((CONV_HALO, D), lambda s, i: (0, 0)), vec, vec, vec],
        out_specs=[out, out],
        scratch_shapes=[pltpu.VMEM((8, tm + CONV_HALO, D), F32), pltpu.VMEM((tm, D), F32)],
        compiler_params=_cparams("parallel", "parallel"),
    )(z3, z3, z3, z3, w_dw, b_dw, ln_g, ln_b)


PAIRS = GROUP // 2
QROWS = PAIRS * BLK


def _kv2_col(kind, g, par):
    return kind * 2 * N_KV + g * 2 + par


def _attn_mask(has_prev):
    qi = lax.broadcasted_iota(jnp.int32, (QROWS, 2 * BLK), 0) & (BLK - 1)
    kj = lax.broadcasted_iota(jnp.int32, (QROWS, 2 * BLK), 1)
    first_key = jnp.where(has_prev, 0, BLK)
    return (kj > qi) & (kj <= qi + BLK) & (kj >= first_key)


NEG_BIG = -1e30


def _attn_specs():
    q = pl.BlockSpec((None, BLK, D), lambda s, n: (s, n, 0))
    kv_cur = pl.BlockSpec((None, BLK, 4 * N_KV * KV_W), lambda s, n: (s, n, 0))
    kv_prev = pl.BlockSpec((None, BLK, 4 * N_KV * KV_W), lambda s, n: (s, jnp.maximum(n - 1, 0), 0))
    sink = pl.BlockSpec(memory_space=pltpu.SMEM)
    return sink, q, kv_cur, kv_prev


def _stack_pairs(ref, g):
    return jnp.concatenate([ref[:, (PAIRS * g + j) * 128 : (PAIRS * g + j + 1) * 128] for j in range(PAIRS)], axis=0)


def _unstack_pairs(ref, g, val):
    for j in range(PAIRS):
        ref[:, (PAIRS * g + j) * 128 : (PAIRS * g + j + 1) * 128] = val[j * BLK : (j + 1) * BLK].astype(ref.dtype)


def _pair_heads(g, par):
    return [GROUP * g + 2 * j + par for j in range(PAIRS)]


def _head_col_load(ref, g, par):
    return jnp.concatenate([ref[:, h : h + 1] for h in _pair_heads(g, par)], axis=0)


def _head_col_store(ref, g, par, col):
    for j, h in enumerate(_pair_heads(g, par)):
        ref[:, h : h + 1] = col[j * BLK : (j + 1) * BLK]


def _sink_col(sink_ref, g, par):
    return jnp.concatenate([jnp.full((BLK, 1), sink_ref[h], F32) for h in _pair_heads(g, par)], axis=0)


def _kv2_block(kvc_ref, kvp_ref, has_prev, c):
    col = slice(c * KV_W, (c + 1) * KV_W)
    prev = jnp.where(has_prev, kvp_ref[:, col], jnp.zeros((BLK, KV_W), BF16))
    return jnp.concatenate([prev, kvc_ref[:, col]], axis=0)


def _attn_fwd(qr3, kv3, sinks):
    NS, S, _ = qr3.shape

    def body(sink_ref, q_ref, kvc_ref, kvp_ref, o_ref, lse_ref):
        has_prev = pl.program_id(1) > 0
        mask = _attn_mask(has_prev)[0:BLK]
        for g in range(N_KV):
            kv = [[_kv2_block(kvc_ref, kvp_ref, has_prev, _kv2_col(kind, g, par)) for par in range(2)] for kind in range(2)]
            for j in range(PAIRS):
                cols = slice((PAIRS * g + j) * 128, (PAIRS * g + j + 1) * 128)
                q2 = q_ref[:, cols]
                o_pair = None
                for par in range(2):
                    h = GROUP * g + 2 * j + par
                    s = jnp.where(mask, _dot_nt(q2, kv[0][par]), NEG_BIG)
                    sk = sink_ref[h]
                    mx = jnp.maximum(jnp.max(s, axis=-1, keepdims=True), sk)
                    e = jnp.exp(s - mx)
                    den = jnp.sum(e, axis=-1, keepdims=True) + jnp.exp(sk - mx)
                    pv = _dot(e.astype(BF16), kv[1][par]) * (1.0 / den)
                    o_pair = pv if o_pair is None else o_pair + pv
                    lse_ref[:, h : h + 1] = mx + jnp.log(den)
                o_ref[:, cols] = o_pair.astype(BF16)

    return pl.pallas_call(
        body, name="attn_fwd", grid=(NS, S // BLK),
        out_shape=[jax.ShapeDtypeStruct((NS, S, D), BF16), jax.ShapeDtypeStruct((NS, S, N_HEADS), F32)],
        in_specs=list(_attn_specs()),
        out_specs=[pl.BlockSpec((None, BLK, D), lambda s, n: (s, n, 0)),
                   pl.BlockSpec((None, BLK, N_HEADS), lambda s, n: (s, n, 0))],
        compiler_params=_cparams("parallel", "parallel"),
    )(sinks, qr3, kv3, kv3)


def _weight_spec(shape):
    return pl.BlockSpec(shape, lambda i: (0,) * len(shape), pipeline_mode=pl.Buffered(1))


def _dsilu(v, s):
    return s * (1.0 + v * (1.0 - s))


MID_TILE = 256


def _mid(cs, o, z2, x2, p2, tgt, ln_post, weights):
    T = cs.shape[0]
    tm = min(MID_TILE, T)
    nt = T // tm
    n_bf = 16

    def body(cs_ref, o_ref, cg_ref, ag_ref, gc_ref, ga_ref, x_ref, p_ref, t_ref, g_ref,
             wpw, wbrc, wbra, wout, wpg, wpp,
             dya0_ref, ya1_ref, dya_ref, yb0_ref, dyb_ref, m_ref, dmo_ref, x1_ref, dgl_ref, dpp_ref,
             dcs_ref, do_ref, dcg_ref, dag_ref, dgc_ref, dga_ref, dx1_ref, loss_ref, glp_ref, lacc):
        i = pl.program_id(0)

        @pl.when(i == 0)
        def _():
            lacc[...] = jnp.zeros_like(lacc)
            glp_ref[...] = jnp.zeros_like(glp_ref)

        cg = cg_ref[...].astype(F32)
        scg = _sig(cg)
        silu_c = cg * scg
        ya0 = _dot(cs_ref[...], wpw[...])
        ya1 = (ya0 * silu_c).astype(BF16)
        ya1_ref[...] = ya1
        ya = _dot(ya1, wbrc[...])
        ag = ag_ref[...].astype(F32)
        sag = _sig(ag)
        silu_a = ag * sag
        ov = o_ref[...].astype(F32)
        yb0 = (ov * silu_a).astype(BF16)
        yb0_ref[...] = yb0
        yb = _dot(yb0, wbra[...])
        sgc = _sig(gc_ref[...].astype(F32))
        sga = _sig(ga_ref[...].astype(F32))
        mb = (sgc * ya + sga * yb).astype(BF16)
        m_ref[...] = mb
        mo = _dot(mb, wout[...])
        r2 = lax.rsqrt(jnp.mean(mo * mo, axis=-1, keepdims=True) + EPS)
        nrm = mo * r2
        x1 = x_ref[...] + nrm * g_ref[...]
        x1b = x1.astype(BF16)
        x1_ref[...] = x1b
        gate = _sig(_dot(x1b, wpg[...]))
        pp = _dot(p_ref[...].astype(BF16), wpp[...])
        e = x1 + gate * pp - t_ref[...]
        lacc[...] += _rowsum8(e * e)
        dy = e * (1.0 / D)

        dgl = (dy * pp * gate * (1.0 - gate)).astype(BF16)
        dgl_ref[...] = dgl
        dpp_ref[...] = (dy * gate).astype(BF16)
        dx1 = dy + _dot_nt(dgl, wpg[...])
        dx1_ref[...] = dx1
        glp_ref[...] += _rowsum8(dx1 * nrm)
        dn = dx1 * g_ref[...]
        dmo = (r2 * (dn - nrm * jnp.mean(dn * nrm, axis=-1, keepdims=True))).astype(BF16)
        dmo_ref[...] = dmo
        dm = _dot_nt(dmo, wout[...])
        dya = (dm * sgc).astype(BF16)
        dyb = (dm * sga).astype(BF16)
        dya_ref[...] = dya
        dyb_ref[...] = dyb
        dgc_ref[...] = (dm * ya * sgc * (1.0 - sgc)).astype(BF16)
        dga_ref[...] = (dm * yb * sga * (1.0 - sga)).astype(BF16)
        dya1 = _dot_nt(dya, wbrc[...])
        dya0 = (dya1 * silu_c).astype(BF16)
        dya0_ref[...] = dya0
        dcg_ref[...] = (dya1 * ya0 * _dsilu(cg, scg)).astype(BF16)
        dcs_ref[...] = _dot_nt(dya0, wpw[...]).astype(BF16)
        dyb0 = _dot_nt(dyb, wbra[...])
        do_ref[...] = (dyb0 * silu_a).astype(BF16)
        dag_ref[...] = (dyb0 * ov * _dsilu(ag, sag)).astype(BF16)

        @pl.when(i == nt - 1)
        def _():
            loss_ref[...] = jnp.full(loss_ref.shape, jnp.sum(lacc[...]) * (0.5 / D), F32)

    row = pl.BlockSpec((tm, D), lambda i: (i, 0))

    def zcol(cb):
        return pl.BlockSpec((tm, D), lambda i: (i, cb))

    bf = jax.ShapeDtypeStruct((T, D), BF16)
    return pl.pallas_call(
        body, name="mid_fwd_bwd", grid=(nt,),
        out_shape=[bf] * n_bf + [jax.ShapeDtypeStruct((T, D), F32), jax.ShapeDtypeStruct((8, 128), F32),
                                 jax.ShapeDtypeStruct((8, D), F32)],
        in_specs=[row, row, zcol(CB_CGATE), zcol(CB_AGATE), zcol(CB_GCONV), zcol(CB_GATTN), row,
                  pl.BlockSpec((tm, PLE), lambda i: (i, 0)), row, _weight_spec((1, D))]
        + [_weight_spec((D, D))] * 5 + [_weight_spec((PLE, D))],
        out_specs=[row] * (n_bf + 1) + [pl.BlockSpec((8, 128), lambda i: (0, 0)), pl.BlockSpec((8, D), lambda i: (0, 0))],
        scratch_shapes=[pltpu.VMEM((8, D), F32)],
        compiler_params=_cparams("arbitrary"),
    )(cs, o, z2, z2, z2, z2, x2, p2, tgt, ln_post, *weights)


def _conv_bwd(z3, c3, dcs3, w_dw, ln_g, ln_b, ex):
    NS, S, _ = z3.shape
    tm, nt, r = _conv_tiles(S)

    def body(*refs):
        ins, ex_in, outs, ex_out, scratch, ex_sems = _host_split(refs, 9, 3, 3, ex)
        val_ref, glu_ref, c_ref, dcs_ref, hc_ref, hdcs_ref, w_ref, g_ref, bb_ref = ins
        dz_ref, gw_ref, gvec_ref = outs
        dsh, ubuf, dubuf = scratch
        i = pl.program_id(1)
        first = (pl.program_id(0) == 0) & (i == 0)
        ex.carried(ex_in, ex_out, ex_sems, first, (pl.program_id(0) == NS - 1) & (i == nt - 1))

        @pl.when(first)
        def _():
            gw_ref[...] = jnp.zeros_like(gw_ref)
            gvec_ref[...] = jnp.zeros_like(gvec_ref)

        val = val_ref[...].astype(F32)
        sg = _sig(glu_ref[...].astype(F32))
        ubuf[...] = val * sg

        def ln_bwd(cv, dcs):
            cv = cv.astype(F32)
            mu = jnp.mean(cv, axis=-1, keepdims=True)
            xc = cv - mu
            rstd = lax.rsqrt(jnp.mean(xc * xc, axis=-1, keepdims=True) + EPS)
            xhat = xc * rstd
            cl = xhat * g_ref[...] + bb_ref[...]
            s = _sig(cl)
            dcl = dcs.astype(F32) * _dsilu(cl, s)
            dxh = dcl * g_ref[...]
            dc = rstd * (dxh - jnp.mean(dxh, axis=-1, keepdims=True) - xhat * jnp.mean(dxh * xhat, axis=-1, keepdims=True))
            return dc, dcl, xhat

        dc, dcl, xhat = ln_bwd(c_ref[...], dcs_ref[...])
        dsh[0, 0:tm, :] = dc
        dch, _, _ = ln_bwd(hc_ref[...], hdcs_ref[...])
        dsh[0, tm:, :] = jnp.where(i < nt - 1, dch, 0.0)
        gvec_ref[0:8, :] += _rowsum8(dcl * xhat)
        gvec_ref[8:16, :] += _rowsum8(dcl)
        gvec_ref[16:24, :] += _rowsum8(dc)
        _fill_shifted(dsh, tm + CONV_HALO - 8)

        def dc_ahead(r0, k):
            a, b = divmod(CONV_K - 1 - k, 8)
            return dsh[b, r0 + 8 * a : r0 + 8 * a + CONV_ROWS, :]

        for r0 in range(0, tm, CONV_ROWS):
            acc = jnp.zeros((CONV_ROWS, D), F32)
            for k in range(CONV_K):
                acc = acc + w_ref[k : k + 1, :] * dc_ahead(r0, k)
            dubuf[r0 : r0 + CONV_ROWS, :] = acc
        for r0 in range(0, tm, CONV_ROWS):
            ur = ubuf[r0 : r0 + CONV_ROWS, :]
            for k in range(CONV_K):
                gw_ref[8 * k : 8 * k + 8, :] += _rowsum8(ur * dc_ahead(r0, k))
        du = dubuf[...]
        dz_ref[:, 0:D] = (du * sg).astype(BF16)
        dz_ref[:, D:] = (du * val * sg * (1.0 - sg)).astype(BF16)

    def cur(cb):
        return pl.BlockSpec((None, tm, D), lambda s, i: (s, i, cb))

    nxt = pl.BlockSpec((None, CONV_HALO, D), lambda s, i: (s, jnp.minimum((i + 1) * r, S // CONV_HALO - 1), 0))
    vec = pl.BlockSpec((1, D), lambda s, i: (0, 0))
    return pl.pallas_call(
        body, name="conv_bwd", grid=(NS, nt),
        out_shape=[jax.ShapeDtypeStruct((NS, S, 2 * D), BF16), jax.ShapeDtypeStruct((CONV_HALO * 8, D), F32),
                   jax.ShapeDtypeStruct((24, D), F32)] + ex.out_shape,
        in_specs=[cur(CB_VAL), cur(CB_GLU), cur(0), cur(0), nxt, nxt,
                  pl.BlockSpec((CONV_HALO, D), lambda s, i: (0, 0)), vec, vec] + ex.specs,
        out_specs=[pl.BlockSpec((None, tm, 2 * D), lambda s, i: (s, i, 0)),
                   pl.BlockSpec((CONV_HALO * 8, D), lambda s, i: (0, 0)), pl.BlockSpec((24, D), lambda s, i: (0, 0))] + ex.specs,
        scratch_shapes=[pltpu.VMEM((8, tm + CONV_HALO, D), F32), pltpu.VMEM((tm, D), F32), pltpu.VMEM((tm, D), F32)] + ex.scratch,
        compiler_params=_cparams("arbitrary", "arbitrary"),
    )(z3, z3, c3, dcs3, c3, dcs3, w_dw, ln_g, ln_b, *ex.arrs)


def _attn_bwd(qr3, kv3, do3, o3, lse3, sinks, cos3, sa3, sb3):
    NS, S, _ = qr3.shape

    def body(sink_ref, q_ref, kvc_ref, kvp_ref, do_ref, o_ref, lse_ref, cos_ref, sa_ref, sb_ref,
             dq_ref, dkc_ref, dkp_ref, dvc_ref, dvp_ref, dsk_ref):
        has_prev = pl.program_id(1) > 0
        mask = _attn_mask(has_prev)
        tabs = cos_ref[...], sa_ref[...], sb_ref[...]
        low_q = lax.broadcasted_iota(jnp.int32, (QROWS, 128), 1) < HEAD_DIM
        dk_g, dv_g = [], []
        for g in range(N_KV):
            qs = _stack_pairs(q_ref, g)
            dos = _stack_pairs(do_ref, g)
            prod = dos.astype(F32) * _stack_pairs(o_ref, g).astype(F32)
            deltas = [jnp.sum(jnp.where(low_q, prod, 0.0), axis=-1, keepdims=True),
                      jnp.sum(jnp.where(low_q, 0.0, prod), axis=-1, keepdims=True)]
            dq_acc, dk_par, dv_par = None, [], []
            for par in range(2):
                k2 = _kv2_block(kvc_ref, kvp_ref, has_prev, _kv2_col(0, g, par))
                v2 = _kv2_block(kvc_ref, kvp_ref, has_prev, _kv2_col(1, g, par))
                lse = _head_col_load(lse_ref, g, par)
                p = jnp.exp(jnp.where(mask, _dot_nt(qs, k2), NEG_BIG) - lse)
                ds = (p * (_dot_nt(dos, v2) - deltas[par])).astype(BF16)
                dq = _dot(ds, k2)
                dq_acc = dq if dq_acc is None else dq_acc + dq
                dk_par.append(_dot_tn(ds, qs))
                dv_par.append(_dot_tn(p.astype(BF16), dos))
                _head_col_store(dsk_ref, g, par, -jnp.exp(_sink_col(sink_ref, g, par) - lse) * deltas[par])
            for j in range(PAIRS):
                dq_pair = _rope(dq_acc[j * BLK : (j + 1) * BLK], *tabs, sign=-1.0) * (HEAD_DIM ** -0.5)
                dq_ref[:, (PAIRS * g + j) * 128 : (PAIRS * g + j + 1) * 128] = dq_pair.astype(BF16)
            dk_g.append(dk_par[0] + pltpu.roll(dk_par[1], HEAD_DIM, 1))
            dv_g.append(dv_par[0] + pltpu.roll(dv_par[1], HEAD_DIM, 1))
        low_k = lax.broadcasted_iota(jnp.int32, (2 * BLK, KV_W), 1) < HEAD_DIM
        dk = jnp.where(low_k, dk_g[0], pltpu.roll(dk_g[1], HEAD_DIM, 1))
        dv = jnp.where(low_k, dv_g[0], pltpu.roll(dv_g[1], HEAD_DIM, 1))
        dkp_ref[...] = dk[0:BLK]
        dkc_ref[...] = dk[BLK:]
        dvp_ref[...] = dv[0:BLK]
        dvc_ref[...] = dv[BLK:]

    qspec = pl.BlockSpec((None, BLK, D), lambda s, n: (s, n, 0))
    kvspec = pl.BlockSpec((None, BLK, KV_W), lambda s, n: (s, n, 0))
    hspec = pl.BlockSpec((None, BLK, N_HEADS), lambda s, n: (s, n, 0))
    kv = jax.ShapeDtypeStruct((NS, S, KV_W), F32)
    return pl.pallas_call(
        body, name="attn_bwd", grid=(NS, S // BLK),
        out_shape=[jax.ShapeDtypeStruct((NS, S, D), BF16), kv, kv, kv, kv, jax.ShapeDtypeStruct((NS, S, N_HEADS), F32)],
        in_specs=list(_attn_specs()) + [qspec, qspec, hspec, kvspec, kvspec, kvspec],
        out_specs=[qspec, kvspec, kvspec, kvspec, kvspec, hspec],
        compiler_params=_cparams("parallel", "parallel"),
    )(sinks, qr3, kv3, kv3, do3, o3, lse3, cos3, sa3, sb3)


def _attn_post(dkc3, dkp3, dvc3, dvp3, cos3, sa3, sb3):
    NS, S, _ = dkc3.shape
    tm = min(8 * BLK, S)
    nt = S // tm

    def body(dkc_ref, dkp_ref, dkn_ref, dvc_ref, dvp_ref, dvn_ref, cos_ref, sa_ref, sb_ref, dkv_ref):
        has_next = pl.program_id(1) < nt - 1

        def join(cur_ref, prev_ref, next_ref):
            ahead = jnp.where(has_next, next_ref[...], 0.0)
            shifted = ahead if tm == BLK else jnp.concatenate([prev_ref[BLK:, :], ahead], axis=0)
            return cur_ref[...] + shifted

        tabs = cos_ref[...], sa_ref[...], sb_ref[...]
        dkv_ref[:, 0:KV_W] = _rope(join(dkc_ref, dkp_ref, dkn_ref), *tabs, sign=-1.0).astype(BF16)
        dkv_ref[:, KV_W:] = join(dvc_ref, dvp_ref, dvn_ref).astype(BF16)

    cur = pl.BlockSpec((None, tm, KV_W), lambda s, j: (s, j, 0))
    nxt = pl.BlockSpec((None, BLK, KV_W), lambda s, j: (s, jnp.minimum((j + 1) * (tm // BLK), S // BLK - 1), 0))
    return pl.pallas_call(
        body, name="attn_post", grid=(NS, nt),
        out_shape=jax.ShapeDtypeStruct((NS, S, 2 * KV_W), BF16),
        in_specs=[cur, cur, nxt, cur, cur, nxt, cur, cur, cur],
        out_specs=pl.BlockSpec((None, tm, 2 * KV_W), lambda s, j: (s, j, 0)),
        compiler_params=_cparams("parallel", "parallel"),
    )(dkc3, dkp3, dkp3, dvc3, dvp3, dvp3, cos3, sa3, sb3)


def _in_bwd(segs, w_in_t, x2, dx1, ln_pre, after):
    T = x2.shape[0]
    tm = min(ROW_TILE, T)
    nt = T // tm
    ns = len(segs)
    widths = [s.shape[1] for s in segs]

    def body(*refs):
        seg_refs = refs[:ns]
        w_ref, x_ref, dx1_ref, g_ref, _, gx_ref, glp_ref = refs[ns:]
        i = pl.program_id(0)

        @pl.when(i == 0)
        def _():
            glp_ref[...] = jnp.zeros_like(glp_ref)

        dh = None
        zc = 0
        for sref, w in zip(seg_refs, widths):
            part = _dot(sref[...], w_ref[_orig_col(zc) : _orig_col(zc) + w, :])
            dh = part if dh is None else dh + part
            zc += w
        xv = x_ref[...]
        r1 = lax.rsqrt(jnp.mean(xv * xv, axis=-1, keepdims=True) + EPS)
        xhat = xv * r1
        glp_ref[...] += _rowsum8(dh * xhat)
        dhg = dh * g_ref[...]
        gx_ref[...] = dx1_ref[...] + r1 * (dhg - xhat * jnp.mean(dhg * xhat, axis=-1, keepdims=True))

    row = pl.BlockSpec((tm, D), lambda i: (i, 0))
    return pl.pallas_call(
        body, name="in_bwd", grid=(nt,),
        out_shape=[jax.ShapeDtypeStruct((T, D), F32), jax.ShapeDtypeStruct((8, D), F32)],
        in_specs=[pl.BlockSpec((tm, w), lambda i: (i, 0)) for w in widths]
        + [_weight_spec((NW, D)), row, row, _weight_spec((1, D)), pl.BlockSpec(memory_space=pl.ANY)],
        out_specs=[row, pl.BlockSpec((8, D), lambda i: (0, 0))],
        compiler_params=_cparams("arbitrary"),
    )(*segs, w_in_t, x2, dx1, ln_pre, after)


def _grad_matmul(a, b, name, rows=None, into=None):
    T, M = a.shape
    N = b.shape[1]
    tk = min(2048, T)
    nk = T // tk

    def body(a_ref, b_ref, *rest):
        o_ref, acc = rest[-2:]
        k = pl.program_id(0)

        @pl.when(k == 0)
        def _():
            acc[...] = jnp.zeros_like(acc)

        acc[...] += _dot_tn(a_ref[...].astype(BF16), b_ref[...])

        @pl.when(k == nk - 1)
        def _():
            o_ref[...] = acc[...].astype(BF16)

    in_specs = [pl.BlockSpec((tk, M), lambda k: (k, 0)), pl.BlockSpec((tk, N), lambda k: (k, 0))]
    if rows is None:
        out_shape, out_spec = (M, N), pl.BlockSpec((M, N), lambda k: (0, 0))
    else:
        out_shape, out_spec = (rows[1], N), pl.BlockSpec((pl.Element(M), pl.Element(N)), lambda k: (rows[0], 0))
    operands = (a, b) if into is None else (a, b, into)
    return pl.pallas_call(
        body, name=name, grid=(nk,), out_shape=jax.ShapeDtypeStruct(out_shape, BF16),
        in_specs=in_specs if into is None else in_specs + [pl.BlockSpec(memory_space=pl.ANY)],
        out_specs=out_spec, input_output_aliases={} if into is None else {2: 0},
        scratch_shapes=[pltpu.VMEM((M, N), F32)],
        compiler_params=_cparams("arbitrary"),
    )(*operands)


def _pack_small(gw, gvec, glp_pre, glp_post, dsk):
    T = dsk.shape[0]

    def body(gw_ref, gvec_ref, pre_ref, post_ref, dsk_ref, gdw_ref, gs_ref):
        gwf = gw_ref[...].reshape(CONV_HALO, 8, D).sum(axis=1)
        for d in range(N_DEV):
            gdw_ref[d] = gwf[:, 128 * d : 128 * (d + 1)]
        gs_ref[...] = jnp.zeros_like(gs_ref)
        gs_ref[0:1, :] = jnp.sum(pre_ref[...], axis=0, keepdims=True)
        gs_ref[1:2, :] = jnp.sum(post_ref[...], axis=0, keepdims=True)
        gs_ref[2:3, :] = jnp.sum(gvec_ref[16:24, :], axis=0, keepdims=True)
        gs_ref[3:4, :] = jnp.sum(gvec_ref[0:8, :], axis=0, keepdims=True)
        gs_ref[4:5, :] = jnp.sum(gvec_ref[8:16, :], axis=0, keepdims=True)
        gs_ref[5:6, 0:N_HEADS] = jnp.sum(dsk_ref[...], axis=0, keepdims=True)

    return pl.pallas_call(
        body, name="pack_small",
        out_shape=[jax.ShapeDtypeStruct((N_DEV, CONV_HALO, 128), F32), jax.ShapeDtypeStruct((8, D), F32)],
        compiler_params=_cparams(),
    )(gw, gvec, glp_pre, glp_post, dsk)


def _adam_update(g, w_ref, m_ref, v_ref, g_ref, d_ref, nm_ref, nv_ref):
    nm = ADAM_B1 * m_ref[...] + (1.0 - ADAM_B1) * g
    nv = ADAM_B2 * v_ref[...] + (1.0 - ADAM_B2) * (g * g)
    m_hat = nm / (1.0 - ADAM_B1 ** ADAM_STEP)
    v_hat = nv / (1.0 - ADAM_B2 ** ADAM_STEP)
    g_ref[...] = g
    d_ref[...] = -ADAM_LR * (m_hat / (jnp.sqrt(v_hat) + ADAM_EPS) + ADAM_WD * w_ref[...])
    nm_ref[...] = nm
    nv_ref[...] = nv


def _adamw(parts, w, m, v, name):
    R, C = w.shape
    tr = R if R <= 256 else 128

    def body(p_ref, *rest):
        g = p_ref[0].astype(F32)
        for s in range(1, N_DEV):
            g = g + p_ref[s].astype(F32)
        _adam_update(g, *rest)

    blk = pl.BlockSpec((tr, C), lambda i: (i, 0))
    return pl.pallas_call(
        body, name=name, grid=(R // tr,), out_shape=[jax.ShapeDtypeStruct((R, C), F32)] * 4,
        in_specs=[pl.BlockSpec((N_DEV, tr, C), lambda i: (0, i, 0)), blk, blk, blk], out_specs=[blk] * 4,
        compiler_params=_cparams("parallel"),
    )(parts, w, m, v)


def _adamw_many(groups, name):
    n = len(groups)
    R, C = groups[0][1].shape
    tr = 32

    def body(*refs):
        ins, outs = refs[: 4 * n], refs[4 * n :]
        for j in range(n):
            p_ref = ins[4 * j]
            g = p_ref[0].astype(F32)
            for s in range(1, N_DEV):
                g = g + p_ref[s].astype(F32)
            _adam_update(g, *ins[4 * j + 1 : 4 * j + 4], *outs[4 * j : 4 * j + 4])

    blk = pl.BlockSpec((tr, C), lambda i: (i, 0))
    res = pl.pallas_call(
        body, name=name, grid=(R // tr,), out_shape=[jax.ShapeDtypeStruct((R, C), F32)] * (4 * n),
        in_specs=[pl.BlockSpec((N_DEV, tr, C), lambda i: (0, i, 0)), blk, blk, blk] * n, out_specs=[blk] * (4 * n),
        compiler_params=_cparams("parallel"),
    )(*[a for grp in groups for a in grp])
    return [res[4 * j : 4 * j + 4] for j in range(n)]


def _adamw_own(parts, own, me, w, m, v, name):
    R, C = w.shape
    tr = max(t for t in range(16, 513, 16) if R % t == 0)

    def body(me_ref, p_ref, own_ref, *rest):
        g = None
        for s in range(N_DEV):
            part = jnp.where(me_ref[0] == s, own_ref[...], p_ref[s]).astype(F32)
            g = part if g is None else g + part
        _adam_update(g, *rest)

    blk = pl.BlockSpec((tr, C), lambda i, me: (i, 0))
    return pl.pallas_call(
        body, name=name, out_shape=[jax.ShapeDtypeStruct((R, C), F32)] * 4,
        grid_spec=pltpu.PrefetchScalarGridSpec(
            num_scalar_prefetch=1, grid=(R // tr,),
            in_specs=[pl.BlockSpec((N_DEV, tr, C), lambda i, me: (0, i, 0)),
                      pl.BlockSpec((None, tr, C), lambda i, me: (me[0], i, 0)), blk, blk, blk],
            out_specs=[blk] * 4),
        compiler_params=_cparams("parallel"),
    )(me, parts, own, w, m, v)


def kernel(x, p, positions, w_in, ln_pre, ln_post, w_dw, b_dw, conv_ln_g, conv_ln_b, w_pw, sinks, w_br_conv, w_br_attn, w_out, w_ple_gate, w_ple_proj, loss_target, m_w_in, m_ln_pre, m_ln_post, m_w_dw, m_b_dw, m_conv_ln_g, m_conv_ln_b, m_w_pw, m_sinks, m_w_br_conv, m_w_br_attn, m_w_out, m_w_ple_gate, m_w_ple_proj, v_w_in, v_ln_pre, v_ln_post, v_w_dw, v_b_dw, v_conv_ln_g, v_conv_ln_b, v_w_pw, v_sinks, v_w_br_conv, v_w_br_attn, v_w_out, v_w_ple_gate, v_w_ple_proj):
    NS, S, _ = x.shape
    T = NS * S
    x2 = x.reshape(T, D)
    p2 = p.reshape(T, PLE)
    tgt = loss_target.reshape(T, D)
    pos = positions.reshape(T, 1)

    row_sharded = [w_pw[0], w_br_conv[0], w_br_attn[0], w_out[0], w_ple_gate[0]]
    sh_rows = D // N_DEV
    w_t, m_t, v_t = (jnp.swapaxes(a[0], 0, 1) for a in (w_in, m_w_in, v_w_in))
    cos, sa, sb, g_in = _rope_tables(pos, _TwoLevelGather([w_t.astype(BF16)]))
    w_in_f = g_in.reshape(NW, D)
    gather_rest = _Exchange([], [w.astype(BF16) for w in row_sharded] + [
        w_ple_proj[0].astype(BF16), jnp.pad(w_dw[0], ((0, CONV_HALO - CONV_K), (0, 0)))])

    in_out = _in_proj(x2, ln_pre, w_in_f, cos, sa, sb, gather_rest)
    z2, h, qr, kv2 = in_out[:4]
    g_rows, g_pp, g_dw = in_out[4:9], in_out[9], in_out[10]
    full = [g.reshape(D, D) for g in g_rows]
    w_pp_f = g_pp.transpose(1, 0, 2).reshape(PLE, D)
    w_dw_f = g_dw.transpose(1, 0, 2).reshape(CONV_HALO, D)
    z3 = z2.reshape(NS, S, NW)
    c3, cs3 = _conv_fwd(z3, w_dw_f, b_dw, conv_ln_g, conv_ln_b)
    qr3, kv3 = qr.reshape(NS, S, D), kv2.reshape(NS, S, 4 * N_KV * KV_W)
    sinks1 = sinks.reshape(N_HEADS)
    o3, lse3 = _attn_fwd(qr3, kv3, sinks1)
    o = o3.reshape(T, D)
    cs = cs3.reshape(T, D)
    (dya0, ya1, dya, yb0, dyb, m, dmo, x1, dgl, dpp, dcs, do, dcg, dag, dgc, dga, dx1, loss_blk, glp_post) = _mid(
        cs, o, z2, x2, p2, tgt, ln_post, full + [w_pp_f])

    gp_rows = [_grad_matmul(a, b, nm).reshape(N_DEV, sh_rows, D) for a, b, nm in (
        (cs, dya0, "grad_w_pw"), (ya1, dya, "grad_w_br_conv"), (yb0, dyb, "grad_w_br_attn"),
        (m, dmo, "grad_w_out"), (x1, dgl, "grad_w_ple_gate"))]
    gp_pp = _grad_matmul(p2, dpp, "grad_w_ple_proj").reshape(PLE, N_DEV, D // N_DEV).transpose(1, 0, 2)
    conv_out = _conv_bwd(z3, c3, dcs.reshape(NS, S, D), w_dw_f, conv_ln_g, conv_ln_b,
                         _Exchange(gp_rows + [gp_pp], [loss_blk]))
    dzvu3, gw, gvec, r_rows, r_pp, r_loss = conv_out[0], conv_out[1], conv_out[2], conv_out[3:8], conv_out[8], conv_out[9]
    loss = jnp.sum(r_loss[:, 0, 0])
    tab3 = [t.reshape(NS, S, 128) for t in (cos, sa, sb)]
    dq3, dkc3, dkp3, dvc3, dvp3, dsk3 = _attn_bwd(qr3, kv3, do.reshape(NS, S, D), o3, lse3, sinks1, *tab3)
    dkv3 = _attn_post(dkc3, dkp3, dvc3, dvp3, *tab3)
    segs = [dzvu3.reshape(T, 2 * D), dcg, dq3.reshape(T, D), dag, dgc, dga, dkv3.reshape(T, 2 * KV_W)]
    gp_in, zc = None, 0
    for j, s in enumerate(segs):
        gp_in = _grad_matmul(s, h, f"grad_w_in_{j}", rows=(_orig_col(zc), NW), into=gp_in)
        zc += s.shape[1]
    gp_in = gp_in.reshape(N_DEV, NW // N_DEV, D)
    in_send, in_recv, in_own, in_land, sent = _scatter_send(gp_in, "w_in_grad_send")
    grad_x2, glp_pre = _in_bwd(segs, w_in_f, x2, dx1, ln_pre, sent)
    gp_dw, gp_small = _pack_small(gw, gvec, glp_pre, glp_post, dsk3.reshape(T, N_HEADS))
    r_dw, r_small = _Exchange([gp_dw], [gp_small]).alone("small_grad_exchange")

    res = {}
    names_rows = ["w_pw", "w_br_conv", "w_br_attn", "w_out", "w_ple_gate"]
    wmv = {"w_pw": (w_pw, m_w_pw, v_w_pw), "w_br_conv": (w_br_conv, m_w_br_conv, v_w_br_conv),
           "w_br_attn": (w_br_attn, m_w_br_attn, v_w_br_attn), "w_out": (w_out, m_w_out, v_w_out),
           "w_ple_gate": (w_ple_gate, m_w_ple_gate, v_w_ple_gate)}
    rows_res = _adamw_many([(parts, *(a[0] for a in wmv[nm])) for nm, parts in zip(names_rows, r_rows)], "adamw_rows")
    res.update(zip(names_rows, rows_res))
    res["w_ple_proj"] = _adamw(r_pp, w_ple_proj[0], m_w_ple_proj[0], v_w_ple_proj[0], "adamw_w_ple_proj")
    pad_dw = lambda a: jnp.pad(a[0], ((0, CONV_HALO - CONV_K), (0, 0)))
    res["w_dw"] = [a[:CONV_K] for a in _adamw(r_dw, pad_dw(w_dw), pad_dw(m_w_dw), pad_dw(v_w_dw), "adamw_w_dw")]

    def stack_small(a_pre, a_post, a_b, a_g, a_bb, a_s):
        sk = jnp.pad(a_s, ((0, 0), (0, D - N_HEADS)))
        return jnp.concatenate([a_pre, a_post, a_b, a_g, a_bb, sk, jnp.zeros((2, D), F32)], axis=0)

    small = _adamw(
        r_small, stack_small(ln_pre, ln_post, b_dw, conv_ln_g, conv_ln_b, sinks),
        stack_small(m_ln_pre, m_ln_post, m_b_dw, m_conv_ln_g, m_conv_ln_b, m_sinks),
        stack_small(v_ln_pre, v_ln_post, v_b_dw, v_conv_ln_g, v_conv_ln_b, v_sinks), "adamw_small")
    for j, nm in enumerate(["ln_pre", "ln_post", "b_dw", "conv_ln_g", "conv_ln_b"]):
        res[nm] = [a[j] for a in small]
    res["sinks"] = [a[5, :N_HEADS] for a in small]
    in_own, in_land = _scatter_wait(in_send, in_recv, in_own, in_land, small[0], "w_in_grad_wait")
    me = _slot(*_my_place()).astype(jnp.int32).reshape(1)
    res["w_in"] = [jnp.swapaxes(a, 0, 1) for a in _adamw_own(in_land, in_own, me, w_t, m_t, v_t, "adamw_w_in")]

    order = ["w_in", "ln_pre", "ln_post", "w_dw", "b_dw", "conv_ln_g", "conv_ln_b", "w_pw", "sinks", "w_br_conv",
             "w_br_attn", "w_out", "w_ple_gate", "w_ple_proj"]
    outs = [loss, grad_x2.reshape(NS, S, D)]
    for kind in range(4):
        outs += [res[nm][kind][None] for nm in order]
    return tuple(outs)
```

```python
import numpy as np

import jax
import jax.numpy as jnp
from jax import lax
from jax.experimental import pallas as pl
from jax.experimental.pallas import tpu as pltpu

F32 = jnp.float32
BF16 = jnp.bfloat16

D = 1024
N_HEADS = 16
N_KV = 2
HEAD_DIM = 64
GROUP = N_HEADS // N_KV
KV_W = N_KV * HEAD_DIM
CONV_K = 31
CONV_HALO = 32
BLK = 128
ROPE_DIM = 16
ROPE_THETA = 500000.0
EPS = 1e-6
PLE = 256
NW = 7 * D + 2 * KV_W
N_DEV = 8

CB_VAL, CB_GLU, CB_CGATE, CB_Q, CB_AGATE, CB_GCONV, CB_GATTN = range(7)
CB_K = 7 * D // KV_W

ADAM_LR, ADAM_B1, ADAM_B2, ADAM_EPS, ADAM_WD, ADAM_STEP = 0.001, 0.9, 0.999, 1e-08, 0.01, 10

VMEM_LIMIT = 56 * 1024 * 1024
ROW_TILE = 512
CONV_TILE = 256


def _cparams(*sem):
    return pltpu.CompilerParams(dimension_semantics=sem if sem else None, vmem_limit_bytes=VMEM_LIMIT)


def _sig(v):
    return 1.0 / (1.0 + jnp.exp(-v))


def _rowsum8(a):
    return a.reshape(a.shape[0] // 8, 8, a.shape[1]).sum(axis=0)


def _dot(a, b):
    return jnp.dot(a, b, preferred_element_type=F32)


def _dot_nt(a, b):
    return lax.dot_general(a, b, (((1,), (1,)), ((), ())), preferred_element_type=F32)


def _dot_tn(a, b):
    return lax.dot_general(a, b, (((0,), (0,)), ((), ())), preferred_element_type=F32)


def _orig_col(zc):
    if zc < 4 * D:
        return zc
    return zc - 7 * D + 4 * D if zc >= 7 * D else zc + 2 * KV_W


def _my_place():
    return lax.axis_index("x"), lax.axis_index("y"), lax.axis_index("c")


def _slot(px, py, pc):
    return 4 * px + 2 * py + pc


class _TwoLevelGather:
    def __init__(self, shards):
        self.arrs = list(shards)
        self.n = len(self.arrs)
        self.out_shape = [jax.ShapeDtypeStruct((N_DEV,) + s.shape, s.dtype) for s in shards]
        self.specs = [pl.BlockSpec(memory_space=pl.ANY)] * self.n
        self.scratch = [pltpu.SemaphoreType.DMA((self.n, 7)), pltpu.SemaphoreType.DMA((self.n, 7)),
                        pltpu.SemaphoreType.DMA((self.n,))]

    def _plan(self, ins, outs, sems):
        send_sems, recv_sems, local_sems = sems
        x, y, c = _my_place()
        me, sibling = (x, y, c), (x, y, 1 - c)
        chips = [(1 - x, y), (x, 1 - y), (1 - x, 1 - y)]

        def copy(a, k, block, to, src=None):
            rows = outs[a].at[_slot(*block)]
            return pltpu.make_async_remote_copy(
                src_ref=rows if src is None else src, dst_ref=rows, send_sem=send_sems.at[a, k],
                recv_sem=recv_sems.at[a, k], device_id=to, device_id_type=pl.DeviceIdType.MESH)

        mine = [pltpu.make_async_copy(ins[a], outs[a].at[_slot(*me)], local_sems.at[a]) for a in range(self.n)]
        first = []
        for a in range(self.n):
            first.append(copy(a, 0, me, sibling, src=ins[a]))
            first += [copy(a, 1 + j, me, (*chips[j], c), src=ins[a]) for j in range(2)]
        return copy, mine, first, me, sibling, chips, c

    def start(self, ins, outs, sems):
        _, mine, first, *_ = self._plan(ins, outs, sems)
        for cp in mine + first:
            cp.start()

    def finish(self, ins, outs, sems):
        copy, mine, first, me, sibling, chips, c = self._plan(ins, outs, sems)

        def land_and_pass(a, j, relay_to=None):
            copy(a, 1 + j, (*chips[j], c), me).wait_recv()
            if relay_to is not None:
                copy(a, 3, (*chips[j], c), (*chips[relay_to], c)).start()
            copy(a, 4 + j, (*chips[j], c), sibling).start()

        for a in range(self.n):
            @pl.when(c == 0)
            def _():
                land_and_pass(a, 0, relay_to=1)
                land_and_pass(a, 1)

            @pl.when(c == 1)
            def _():
                land_and_pass(a, 1, relay_to=0)
                land_and_pass(a, 0)

            copy(a, 3, (*chips[2], c), me).wait_recv()
            copy(a, 6, (*chips[2], c), sibling).start()
        for a in range(self.n):
            copy(a, 0, sibling, me).wait_recv()
            for j in range(3):
                copy(a, 4 + j, (*chips[j], 1 - c), me).wait_recv()
        for cp in first:
            cp.wait_send()
        for a in range(self.n):
            for k in (3, 4, 5, 6):
                copy(a, k, me, me, src=ins[a]).wait_send()
        for cp in mine:
            cp.wait()

    def carried(self, refs_in, refs_out, sems, first, last):
        @pl.when(first)
        def _():
            self.start(refs_in, refs_out, sems)

        @pl.when(last)
        def _():
            self.finish(refs_in, refs_out, sems)


class _Exchange:
    def __init__(self, scatter, bcast):
        self.arrs = list(scatter) + list(bcast)
        self.n, self.n_sc = len(self.arrs), len(scatter)
        self.out_shape = [jax.ShapeDtypeStruct(a.shape, a.dtype) for a in scatter]
        self.out_shape += [jax.ShapeDtypeStruct((N_DEV,) + a.shape, a.dtype) for a in bcast]
        self.specs = [pl.BlockSpec(memory_space=pl.ANY)] * self.n
        self.scratch = [pltpu.SemaphoreType.DMA((self.n, 7)), pltpu.SemaphoreType.DMA((self.n, 7)),
                        pltpu.SemaphoreType.DMA((self.n,))]

    def _copies(self, ins, outs, sems):
        send_sems, recv_sems, local_sems = sems
        x, y, c = _my_place()
        me = _slot(x, y, c)
        peers = _peers(x, y, c)
        mine, sends, arrivals = [], [], []
        for a in range(self.n):
            src = ins[a].at[me] if a < self.n_sc else ins[a]
            mine.append(pltpu.make_async_copy(src, outs[a].at[me], local_sems.at[a]))
        for k, peer in enumerate(peers):
            for a in range(self.n):
                src = ins[a].at[_slot(*peer)] if a < self.n_sc else ins[a]
                sends.append(pltpu.make_async_remote_copy(
                    src_ref=src, dst_ref=outs[a].at[me], send_sem=send_sems.at[a, k], recv_sem=recv_sems.at[a, k],
                    device_id=peer, device_id_type=pl.DeviceIdType.MESH))
                rows = outs[a].at[_slot(*peer)]
                arrivals.append(pltpu.make_async_remote_copy(
                    src_ref=rows, dst_ref=rows, send_sem=send_sems.at[a, k], recv_sem=recv_sems.at[a, k],
                    device_id=peer, device_id_type=pl.DeviceIdType.MESH))
        return mine, sends, arrivals

    def start(self, ins, outs, sems):
        mine, sends, _ = self._copies(ins, outs, sems)
        for cp in mine + sends:
            cp.start()

    def finish(self, ins, outs, sems):
        mine, sends, arrivals = self._copies(ins, outs, sems)
        for cp in arrivals:
            cp.wait_recv()
        for cp in sends:
            cp.wait_send()
        for cp in mine:
            cp.wait()

    def carried(self, refs_in, refs_out, sems, first, last):
        @pl.when(first)
        def _():
            self.start(refs_in, refs_out, sems)

        @pl.when(last)
        def _():
            self.finish(refs_in, refs_out, sems)

    def alone(self, name):
        n = self.n

        def body(*refs):
            ins, outs, sems = refs[:n], refs[n : 2 * n], refs[2 * n :]
            self.start(ins, outs, sems)
            self.finish(ins, outs, sems)

        return pl.pallas_call(body, name=name, out_shape=self.out_shape, in_specs=self.specs, out_specs=self.specs,
                              scratch_shapes=self.scratch)(*self.arrs)


def _peers(x, y, c):
    return [(1 - x if k & 4 else x, 1 - y if k & 2 else y, 1 - c if k & 1 else c) for k in range(1, N_DEV)]


def _scatter_send(g, name):
    hbm = pl.BlockSpec(memory_space=pltpu.HBM)
    sem = pl.BlockSpec(memory_space=pltpu.SEMAPHORE)

    def body(g_ref, land_ref, send_sems, recv_sems, g_thru, land_thru, token):
        x, y, c = _my_place()
        me = _slot(x, y, c)
        for k, peer in enumerate(_peers(x, y, c)):
            pltpu.make_async_remote_copy(
                src_ref=g_ref.at[_slot(*peer)], dst_ref=land_ref.at[me], send_sem=send_sems.at[k], recv_sem=recv_sems.at[k],
                device_id=peer, device_id_type=pl.DeviceIdType.MESH).start()
        token[...] = jnp.zeros_like(token)

    return pl.pallas_call(
        body, name=name,
        out_shape=(pltpu.SemaphoreType.DMA((N_DEV - 1,)), pltpu.SemaphoreType.DMA((N_DEV - 1,)),
                   pltpu.HBM(g.shape, g.dtype), pltpu.HBM(g.shape, g.dtype), jax.ShapeDtypeStruct((8, 128), F32)),
        in_specs=(hbm, hbm), out_specs=(sem, sem, hbm, hbm, pl.BlockSpec(memory_space=pltpu.VMEM)),
        input_output_aliases={0: 2, 1: 3},
        compiler_params=pltpu.CompilerParams(has_side_effects=pltpu.SideEffectType.DATAFLOW_SIDE_EFFECTING),
    )(pltpu.with_memory_space_constraint(g, pltpu.HBM),
      pltpu.with_memory_space_constraint(lax.empty(g.shape, g.dtype), pltpu.HBM))


def _scatter_wait(send_sems, recv_sems, g_thru, land_thru, after, name):
    hbm = pl.BlockSpec(memory_space=pltpu.HBM)
    sem = pl.BlockSpec(memory_space=pltpu.SEMAPHORE)

    def body(g_ref, land_ref, send_sems, recv_sems, after_ref, g_out, land_out):
        x, y, c = _my_place()
        for k, peer in enumerate(_peers(x, y, c)):
            cp = pltpu.make_async_remote_copy(
                src_ref=g_ref.at[_slot(*peer)], dst_ref=land_ref.at[_slot(*peer)], send_sem=send_sems.at[k],
                recv_sem=recv_sems.at[k], device_id=peer, device_id_type=pl.DeviceIdType.MESH)
            cp.wait_send()
            cp.wait_recv()

    return pl.pallas_call(
        body, name=name,
        out_shape=(pltpu.HBM(g_thru.shape, g_thru.dtype), pltpu.HBM(land_thru.shape, land_thru.dtype)),
        in_specs=(hbm, hbm, sem, sem, pl.BlockSpec(memory_space=pl.ANY)), out_specs=(hbm, hbm),
        input_output_aliases={0: 0, 1: 1},
        compiler_params=pltpu.CompilerParams(has_side_effects=pltpu.SideEffectType.DATAFLOW_SIDE_EFFECTING),
    )(g_thru, land_thru, send_sems, recv_sems, after)


def _host_split(refs, n_in, n_out, n_scratch, ex):
    k = ex.n if ex is not None else 0
    a = n_in
    b = a + k
    c = b + n_out
    d = c + k
    e = d + n_scratch
    return refs[:a], refs[a:b], refs[b:c], refs[c:d], refs[d:e], refs[e:]


def _rope_tables(pos, ex):
    T = pos.shape[0]
    tm = min(1024, T)
    nt = T // tm
    lane = np.arange(128) % HEAD_DIM
    inv = np.power(np.float32(ROPE_THETA), -np.arange(0, ROPE_DIM, 2, dtype=np.float32) / np.float32(ROPE_DIM)).astype(np.float32)
    half = ROPE_DIM // 2
    invf = np.where(lane < ROPE_DIM, inv[lane % half], 0.0).astype(np.float32)[None, :]
    m_a = (lane < half).astype(np.float32)[None, :]
    m_b = ((lane >= half) & (lane < ROPE_DIM)).astype(np.float32)[None, :]

    def body(*refs):
        ins, ex_in, (cos_ref, sa_ref, sb_ref), ex_out, _, ex_sems = _host_split(refs, 4, 3, 0, ex)
        pos_ref, invf_ref, ma_ref, mb_ref = ins
        i = pl.program_id(0)
        ex.carried(ex_in, ex_out, ex_sems, i == 0, i == nt - 1)
        ang = pos_ref[...].astype(F32) * invf_ref[...]
        sn = jnp.sin(ang)
        cos_ref[...] = jnp.cos(ang)
        sa_ref[...] = -sn * ma_ref[...]
        sb_ref[...] = sn * mb_ref[...]

    row = pl.BlockSpec((tm, 128), lambda i: (i, 0))
    cst = pl.BlockSpec((1, 128), lambda i: (0, 0))
    return pl.pallas_call(
        body, name="rope_tables", grid=(nt,), out_shape=[jax.ShapeDtypeStruct((T, 128), F32)] * 3 + ex.out_shape,
        in_specs=[pl.BlockSpec((tm, 1), lambda i: (i, 0)), cst, cst, cst] + ex.specs, out_specs=[row] * 3 + ex.specs,
        scratch_shapes=ex.scratch,
        compiler_params=_cparams("arbitrary"),
    )(pos, jnp.asarray(invf), jnp.asarray(m_a), jnp.asarray(m_b), *ex.arrs)


def _rope(t, cos, sa, sb, sign=1.0):
    parts = []
    for i in range(t.shape[1] // 128):
        ti = t[:, 128 * i : 128 * (i + 1)]
        up = pltpu.roll(ti, 128 - ROPE_DIM // 2, 1)
        dn = pltpu.roll(ti, ROPE_DIM // 2, 1)
        parts.append(ti * cos + sign * (up * sa + dn * sb))
    return parts[0] if len(parts) == 1 else jnp.concatenate(parts, axis=-1)


def _in_proj(x2, ln_pre, w_in, cos, sa, sb, ex):
    T = x2.shape[0]
    tm = min(512, T)
    nt = T // tm
    chunk = D
    kv_cols = 4 * N_KV * KV_W

    def body(*refs):
        ins, ex_in, (z_ref, h_ref, qr_ref, kv_ref), ex_out, _, ex_sems = _host_split(refs, 6, 4, 0, ex)
        x_ref, g_ref, w_ref, cos_ref, sa_ref, sb_ref = ins
        i = pl.program_id(0)
        ex.carried(ex_in, ex_out, ex_sems, i == 0, i == nt - 1)
        xv = x_ref[...]
        r = lax.rsqrt(jnp.mean(xv * xv, axis=-1, keepdims=True) + EPS)
        h = (xv * r * g_ref[...]).astype(BF16)
        h_ref[...] = h
        tabs = cos_ref[...], sa_ref[...], sb_ref[...]
        for c0 in range(0, NW, chunk):
            cw = min(chunk, NW - c0)
            zc = _dot_nt(h, w_ref[_orig_col(c0) : _orig_col(c0) + cw, :])
            z_ref[:, c0 : c0 + cw] = zc.astype(BF16)
            if c0 == CB_Q * D:
                qr_ref[...] = (_rope(zc, *tabs) * (HEAD_DIM ** -0.5)).astype(BF16)
            if c0 == CB_K * KV_W:
                low = lax.broadcasted_iota(jnp.int32, (tm, KV_W), 1) < HEAD_DIM
                for kind, t in ((0, _rope(zc[:, 0:KV_W], *tabs)), (1, zc[:, KV_W:])):
                    swapped = pltpu.roll(t, HEAD_DIM, 1)
                    cols = {(0, 0): jnp.where(low, t, 0.0), (0, 1): jnp.where(low, 0.0, swapped),
                            (1, 0): jnp.where(low, swapped, 0.0), (1, 1): jnp.where(low, 0.0, t)}
                    for (g, par), val in cols.items():
                        c = _kv2_col(kind, g, par)
                        kv_ref[:, c * KV_W : (c + 1) * KV_W] = val.astype(BF16)

    def row(w):
        return pl.BlockSpec((tm, w), lambda i: (i, 0))

    bf = lambda w: jax.ShapeDtypeStruct((T, w), BF16)
    return pl.pallas_call(
        body, name="in_proj", grid=(nt,),
        out_shape=[bf(NW), bf(D), bf(D), bf(kv_cols)] + ex.out_shape,
        in_specs=[row(D), _weight_spec((1, D)), _weight_spec((NW, D)), row(128), row(128), row(128)] + ex.specs,
        out_specs=[row(NW), row(D), row(D), row(kv_cols)] + ex.specs,
        scratch_shapes=ex.scratch,
        compiler_params=_cparams("arbitrary"),
    )(x2, ln_pre, w_in, cos, sa, sb, *ex.arrs)


def _conv_tiles(S):
    tm = min(CONV_TILE, S)
    return tm, S // tm, tm // CONV_HALO


CONV_ROWS_FWD = 32
CONV_ROWS = 16


def _fill_shifted(sh, rows):
    for b in range(1, 8):
        sh[b, 0:rows, :] = sh[0, b : b + rows, :]


def _conv_fwd(z3, w_dw, b_dw, ln_g, ln_b):
    NS, S, _ = z3.shape
    tm, nt, r = _conv_tiles(S)

    def body(val_ref, glu_ref, hval_ref, hglu_ref, w_ref, b_ref, g_ref, bb_ref, c_ref, cs_ref, ush, cbuf):
        i = pl.program_id(1)
        ush[0, CONV_HALO:, :] = val_ref[...].astype(F32) * _sig(glu_ref[...].astype(F32))
        uh = hval_ref[...].astype(F32) * _sig(hglu_ref[...].astype(F32))
        ush[0, 0:CONV_HALO, :] = jnp.where(i > 0, uh, 0.0)
        _fill_shifted(ush, tm + CONV_HALO - 8)
        for r0 in range(0, tm, CONV_ROWS_FWD):
            acc = jnp.zeros((CONV_ROWS_FWD, D), F32)
            for k in range(CONV_K):
                a, b = divmod(CONV_HALO - (CONV_K - 1) + k, 8)
                acc = acc + w_ref[k : k + 1, :] * ush[b, r0 + 8 * a : r0 + 8 * a + CONV_ROWS_FWD, :]
            cbuf[r0 : r0 + CONV_ROWS_FWD, :] = acc + b_ref[...]
        cv = cbuf[...]
        mu = jnp.mean(cv, axis=-1, keepdims=True)
        xc = cv - mu
        var = jnp.mean(xc * xc, axis=-1, keepdims=True)
        cl = xc * lax.rsqrt(var + EPS) * g_ref[...] + bb_ref[...]
        c_ref[...] = cv.astype(BF16)
        cs_ref[...] = (cl * _sig(cl)).astype(BF16)

    def cur(cb):
        return pl.BlockSpec((None, tm, D), lambda s, i: (s, i, cb))

    def halo(cb):
        return pl.BlockSpec((None, CONV_HALO, D), lambda s, i: (s, jnp.maximum(i * r - 1, 0), cb))

    vec = pl.BlockSpec((1, D), lambda s, i: (0, 0))
    out = pl.BlockSpec((None, tm, D), lambda s, i: (s, i, 0))
    return pl.pallas_call(
        body, name="conv_fwd", grid=(NS, nt),
        out_shape=[jax.ShapeDtypeStruct((NS, S, D), BF16)] * 2,
        in_specs=[cur(CB_VAL), cur(CB_GLU), halo(CB_VAL), halo(CB_GLU),
                  pl.BlockSpec((CONV_HALO, D), lambda s, i: (0, 0)), vec, vec, vec],
        out_specs=[out, out],
        scratch_shapes=[pltpu.VMEM((8, tm + CONV_HALO, D), F32), pltpu.VMEM((tm, D), F32)],
        compiler_params=_cparams("parallel", "parallel"),
    )(z3, z3, z3, z3, w_dw, b_dw, ln_g, ln_b)


PAIRS = GROUP // 2
QROWS = PAIRS * BLK


def _kv2_col(kind, g, par):
    return kind * 2 * N_KV + g * 2 + par


def _attn_mask(has_prev):
    qi = lax.broadcasted_iota(jnp.int32, (QROWS, 2 * BLK), 0) & (BLK - 1)
    kj = lax.broadcasted_iota(jnp.int32, (QROWS, 2 * BLK), 1)
    first_key = jnp.where(has_prev, 0, BLK)
    return (kj > qi) & (kj <= qi + BLK) & (kj >= first_key)


NEG_BIG = -1e30


def _attn_specs():
    q = pl.BlockSpec((None, BLK, D), lambda s, n: (s, n, 0))
    kv_cur = pl.BlockSpec((None, BLK, 4 * N_KV * KV_W), lambda s, n: (s, n, 0))
    kv_prev = pl.BlockSpec((None, BLK, 4 * N_KV * KV_W), lambda s, n: (s, jnp.maximum(n - 1, 0), 0))
    sink = pl.BlockSpec(memory_space=pltpu.SMEM)
    return sink, q, kv_cur, kv_prev


def _stack_pairs(ref, g):
    return jnp.concatenate([ref[:, (PAIRS * g + j) * 128 : (PAIRS * g + j + 1) * 128] for j in range(PAIRS)], axis=0)


def _unstack_pairs(ref, g, val):
    for j in range(PAIRS):
        ref[:, (PAIRS * g + j) * 128 : (PAIRS * g + j + 1) * 128] = val[j * BLK : (j + 1) * BLK].astype(ref.dtype)


def _pair_heads(g, par):
    return [GROUP * g + 2 * j + par for j in range(PAIRS)]


def _head_col_load(ref, g, par):
    return jnp.concatenate([ref[:, h : h + 1] for h in _pair_heads(g, par)], axis=0)


def _head_col_store(ref, g, par, col):
    for j, h in enumerate(_pair_heads(g, par)):
        ref[:, h : h + 1] = col[j * BLK : (j + 1) * BLK]


def _sink_col(sink_ref, g, par):
    return jnp.concatenate([jnp.full((BLK, 1), sink_ref[h], F32) for h in _pair_heads(g, par)], axis=0)


def _kv2_block(kvc_ref, kvp_ref, has_prev, c):
    col = slice(c * KV_W, (c + 1) * KV_W)
    prev = jnp.where(has_prev, kvp_ref[:, col], jnp.zeros((BLK, KV_W), BF16))
    return jnp.concatenate([prev, kvc_ref[:, col]], axis=0)


def _attn_fwd(qr3, kv3, sinks):
    NS, S, _ = qr3.shape

    def body(sink_ref, q_ref, kvc_ref, kvp_ref, o_ref, lse_ref):
        has_prev = pl.program_id(1) > 0
        mask = _attn_mask(has_prev)[0:BLK]
        for g in range(N_KV):
            kv = [[_kv2_block(kvc_ref, kvp_ref, has_prev, _kv2_col(kind, g, par)) for par in range(2)] for kind in range(2)]
            for j in range(PAIRS):
                cols = slice((PAIRS * g + j) * 128, (PAIRS * g + j + 1) * 128)
                q2 = q_ref[:, cols]
                o_pair = None
                for par in range(2):
                    h = GROUP * g + 2 * j + par
                    s = jnp.where(mask, _dot_nt(q2, kv[0][par]), NEG_BIG)
                    sk = sink_ref[h]
                    mx = jnp.maximum(jnp.max(s, axis=-1, keepdims=True), sk)
                    e = jnp.exp(s - mx)
                    den = jnp.sum(e, axis=-1, keepdims=True) + jnp.exp(sk - mx)
                    pv = _dot(e.astype(BF16), kv[1][par]) * (1.0 / den)
                    o_pair = pv if o_pair is None else o_pair + pv
                    lse_ref[:, h : h + 1] = mx + jnp.log(den)
                o_ref[:, cols] = o_pair.astype(BF16)

    return pl.pallas_call(
        body, name="attn_fwd", grid=(NS, S // BLK),
        out_shape=[jax.ShapeDtypeStruct((NS, S, D), BF16), jax.ShapeDtypeStruct((NS, S, N_HEADS), F32)],
        in_specs=list(_attn_specs()),
        out_specs=[pl.BlockSpec((None, BLK, D), lambda s, n: (s, n, 0)),
                   pl.BlockSpec((None, BLK, N_HEADS), lambda s, n: (s, n, 0))],
        compiler_params=_cparams("parallel", "parallel"),
    )(sinks, qr3, kv3, kv3)


def _weight_spec(shape):
    return pl.BlockSpec(shape, lambda i: (0,) * len(shape), pipeline_mode=pl.Buffered(1))


def _dsilu(v, s):
    return s * (1.0 + v * (1.0 - s))


MID_TILE = 256


def _mid(cs, o, z2, x2, p2, tgt, ln_post, weights):
    T = cs.shape[0]
    tm = min(MID_TILE, T)
    nt = T // tm
    n_bf = 16

    def body(cs_ref, o_ref, cg_ref, ag_ref, gc_ref, ga_ref, x_ref, p_ref, t_ref, g_ref,
             wpw, wbrc, wbra, wout, wpg, wpp,
             dya0_ref, ya1_ref, dya_ref, yb0_ref, dyb_ref, m_ref, dmo_ref, x1_ref, dgl_ref, dpp_ref,
             dcs_ref, do_ref, dcg_ref, dag_ref, dgc_ref, dga_ref, dx1_ref, loss_ref, glp_ref, lacc):
        i = pl.program_id(0)

        @pl.when(i == 0)
        def _():
            lacc[...] = jnp.zeros_like(lacc)
            glp_ref[...] = jnp.zeros_like(glp_ref)

        cg = cg_ref[...].astype(F32)
        scg = _sig(cg)
        silu_c = cg * scg
        ya0 = _dot(cs_ref[...], wpw[...])
        ya1 = (ya0 * silu_c).astype(BF16)
        ya1_ref[...] = ya1
        ya = _dot(ya1, wbrc[...])
        ag = ag_ref[...].astype(F32)
        sag = _sig(ag)
        silu_a = ag * sag
        ov = o_ref[...].astype(F32)
        yb0 = (ov * silu_a).astype(BF16)
        yb0_ref[...] = yb0
        yb = _dot(yb0, wbra[...])
        sgc = _sig(gc_ref[...].astype(F32))
        sga = _sig(ga_ref[...].astype(F32))
        mb = (sgc * ya + sga * yb).astype(BF16)
        m_ref[...] = mb
        mo = _dot(mb, wout[...])
        r2 = lax.rsqrt(jnp.mean(mo * mo, axis=-1, keepdims=True) + EPS)
        nrm = mo * r2
        x1 = x_ref[...] + nrm * g_ref[...]
        x1b = x1.astype(BF16)
        x1_ref[...] = x1b
        gate = _sig(_dot(x1b, wpg[...]))
        pp = _dot(p_ref[...].astype(BF16), wpp[...])
        e = x1 + gate * pp - t_ref[...]
        lacc[...] += _rowsum8(e * e)
        dy = e * (1.0 / D)

        dgl = (dy * pp * gate * (1.0 - gate)).astype(BF16)
        dgl_ref[...] = dgl
        dpp_ref[...] = (dy * gate).astype(BF16)
        dx1 = dy + _dot_nt(dgl, wpg[...])
        dx1_ref[...] = dx1
        glp_ref[...] += _rowsum8(dx1 * nrm)
        dn = dx1 * g_ref[...]
        dmo = (r2 * (dn - nrm * jnp.mean(dn * nrm, axis=-1, keepdims=True))).astype(BF16)
        dmo_ref[...] = dmo
        dm = _dot_nt(dmo, wout[...])
        dya = (dm * sgc).astype(BF16)
        dyb = (dm * sga).astype(BF16)
        dya_ref[...] = dya
        dyb_ref[...] = dyb
        dgc_ref[...] = (dm * ya * sgc * (1.0 - sgc)).astype(BF16)
        dga_ref[...] = (dm * yb * sga * (1.0 - sga)).astype(BF16)
        dya1 = _dot_nt(dya, wbrc[...])
        dya0 = (dya1 * silu_c).astype(BF16)
        dya0_ref[...] = dya0
        dcg_ref[...] = (dya1 * ya0 * _dsilu(cg, scg)).astype(BF16)
        dcs_ref[...] = _dot_nt(dya0, wpw[...]).astype(BF16)
        dyb0 = _dot_nt(dyb, wbra[...])
        do_ref[...] = (dyb0 * silu_a).astype(BF16)
        dag_ref[...] = (dyb0 * ov * _dsilu(ag, sag)).astype(BF16)

        @pl.when(i == nt - 1)
        def _():
            loss_ref[...] = jnp.full(loss_ref.shape, jnp.sum(lacc[...]) * (0.5 / D), F32)

    row = pl.BlockSpec((tm, D), lambda i: (i, 0))

    def zcol(cb):
        return pl.BlockSpec((tm, D), lambda i: (i, cb))

    bf = jax.ShapeDtypeStruct((T, D), BF16)
    return pl.pallas_call(
        body, name="mid_fwd_bwd", grid=(nt,),
        out_shape=[bf] * n_bf + [jax.ShapeDtypeStruct((T, D), F32), jax.ShapeDtypeStruct((8, 128), F32),
                                 jax.ShapeDtypeStruct((8, D), F32)],
        in_specs=[row, row, zcol(CB_CGATE), zcol(CB_AGATE), zcol(CB_GCONV), zcol(CB_GATTN), row,
                  pl.BlockSpec((tm, PLE), lambda i: (i, 0)), row, _weight_spec((1, D))]
        + [_weight_spec((D, D))] * 5 + [_weight_spec((PLE, D))],
        out_specs=[row] * (n_bf + 1) + [pl.BlockSpec((8, 128), lambda i: (0, 0)), pl.BlockSpec((8, D), lambda i: (0, 0))],
        scratch_shapes=[pltpu.VMEM((8, D), F32)],
        compiler_params=_cparams("arbitrary"),
    )(cs, o, z2, z2, z2, z2, x2, p2, tgt, ln_post, *weights)


def _conv_bwd(z3, c3, dcs3, w_dw, ln_g, ln_b, ex):
    NS, S, _ = z3.shape
    tm, nt, r = _conv_tiles(S)

    def body(*refs):
        ins, ex_in, outs, ex_out, scratch, ex_sems = _host_split(refs, 9, 3, 3, ex)
        val_ref, glu_ref, c_ref, dcs_ref, hc_ref, hdcs_ref, w_ref, g_ref, bb_ref = ins
        dz_ref, gw_ref, gvec_ref = outs
        dsh, ubuf, dubuf = scratch
        i = pl.program_id(1)
        first = (pl.program_id(0) == 0) & (i == 0)
        ex.carried(ex_in, ex_out, ex_sems, first, (pl.program_id(0) == NS - 1) & (i == nt - 1))

        @pl.when(first)
        def _():
            gw_ref[...] = jnp.zeros_like(gw_ref)
            gvec_ref[...] = jnp.zeros_like(gvec_ref)

        val = val_ref[...].astype(F32)
        sg = _sig(glu_ref[...].astype(F32))
        ubuf[...] = val * sg

        def ln_bwd(cv, dcs):
            cv = cv.astype(F32)
            mu = jnp.mean(cv, axis=-1, keepdims=True)
            xc = cv - mu
            rstd = lax.rsqrt(jnp.mean(xc * xc, axis=-1, keepdims=True) + EPS)
            xhat = xc * rstd
            cl = xhat * g_ref[...] + bb_ref[...]
            s = _sig(cl)
            dcl = dcs.astype(F32) * _dsilu(cl, s)
            dxh = dcl * g_ref[...]
            dc = rstd * (dxh - jnp.mean(dxh, axis=-1, keepdims=True) - xhat * jnp.mean(dxh * xhat, axis=-1, keepdims=True))
            return dc, dcl, xhat

        dc, dcl, xhat = ln_bwd(c_ref[...], dcs_ref[...])
        dsh[0, 0:tm, :] = dc
        dch, _, _ = ln_bwd(hc_ref[...], hdcs_ref[...])
        dsh[0, tm:, :] = jnp.where(i < nt - 1, dch, 0.0)
        gvec_ref[0:8, :] += _rowsum8(dcl * xhat)
        gvec_ref[8:16, :] += _rowsum8(dcl)
        gvec_ref[16:24, :] += _rowsum8(dc)
        _fill_shifted(dsh, tm + CONV_HALO - 8)

        def dc_ahead(r0, k):
            a, b = divmod(CONV_K - 1 - k, 8)
            return dsh[b, r0 + 8 * a : r0 + 8 * a + CONV_ROWS, :]

        for r0 in range(0, tm, CONV_ROWS):
            acc = jnp.zeros((CONV_ROWS, D), F32)
            for k in range(CONV_K):
                acc = acc + w_ref[k : k + 1, :] * dc_ahead(r0, k)
            dubuf[r0 : r0 + CONV_ROWS, :] = acc
        for r0 in range(0, tm, CONV_ROWS):
            ur = ubuf[r0 : r0 + CONV_ROWS, :]
            for k in range(CONV_K):
                gw_ref[8 * k : 8 * k + 8, :] += _rowsum8(ur * dc_ahead(r0, k))
        du = dubuf[...]
        dz_ref[:, 0:D] = (du * sg).astype(BF16)
        dz_ref[:, D:] = (du * val * sg * (1.0 - sg)).astype(BF16)

    def cur(cb):
        return pl.BlockSpec((None, tm, D), lambda s, i: (s, i, cb))

    nxt = pl.BlockSpec((None, CONV_HALO, D), lambda s, i: (s, jnp.minimum((i + 1) * r, S // CONV_HALO - 1), 0))
    vec = pl.BlockSpec((1, D), lambda s, i: (0, 0))
    return pl.pallas_call(
        body, name="conv_bwd", grid=(NS, nt),
        out_shape=[jax.ShapeDtypeStruct((NS, S, 2 * D), BF16), jax.ShapeDtypeStruct((CONV_HALO * 8, D), F32),
                   jax.ShapeDtypeStruct((24, D), F32)] + ex.out_shape,
        in_specs=[cur(CB_VAL), cur(CB_GLU), cur(0), cur(0), nxt, nxt,
                  pl.BlockSpec((CONV_HALO, D), lambda s, i: (0, 0)), vec, vec] + ex.specs,
        out_specs=[pl.BlockSpec((None, tm, 2 * D), lambda s, i: (s, i, 0)),
                   pl.BlockSpec((CONV_HALO * 8, D), lambda s, i: (0, 0)), pl.BlockSpec((24, D), lambda s, i: (0, 0))] + ex.specs,
        scratch_shapes=[pltpu.VMEM((8, tm + CONV_HALO, D), F32), pltpu.VMEM((tm, D), F32), pltpu.VMEM((tm, D), F32)] + ex.scratch,
        compiler_params=_cparams("arbitrary", "arbitrary"),
    )(z3, z3, c3, dcs3, c3, dcs3, w_dw, ln_g, ln_b, *ex.arrs)


def _attn_bwd(qr3, kv3, do3, o3, lse3, sinks, cos3, sa3, sb3):
    NS, S, _ = qr3.shape

    def body(sink_ref, q_ref, kvc_ref, kvp_ref, do_ref, o_ref, lse_ref, cos_ref, sa_ref, sb_ref,
             dq_ref, dkc_ref, dkp_ref, dvc_ref, dvp_ref, dsk_ref):
        has_prev = pl.program_id(1) > 0
        mask = _attn_mask(has_prev)
        tabs = cos_ref[...], sa_ref[...], sb_ref[...]
        low_q = lax.broadcasted_iota(jnp.int32, (QROWS, 128), 1) < HEAD_DIM
        dk_g, dv_g = [], []
        for g in range(N_KV):
            qs = _stack_pairs(q_ref, g)
            dos = _stack_pairs(do_ref, g)
            prod = dos.astype(F32) * _stack_pairs(o_ref, g).astype(F32)
            deltas = [jnp.sum(jnp.where(low_q, prod, 0.0), axis=-1, keepdims=True),
                      jnp.sum(jnp.where(low_q, 0.0, prod), axis=-1, keepdims=True)]
            dq_acc, dk_par, dv_par = None, [], []
            for par in range(2):
                k2 = _kv2_block(kvc_ref, kvp_ref, has_prev, _kv2_col(0, g, par))
                v2 = _kv2_block(kvc_ref, kvp_ref, has_prev, _kv2_col(1, g, par))
                lse = _head_col_load(lse_ref, g, par)
                p = jnp.exp(jnp.where(mask, _dot_nt(qs, k2), NEG_BIG) - lse)
                ds = (p * (_dot_nt(dos, v2) - deltas[par])).astype(BF16)
                dq = _dot(ds, k2)
                dq_acc = dq if dq_acc is None else dq_acc + dq
                dk_par.append(_dot_tn(ds, qs))
                dv_par.append(_dot_tn(p.astype(BF16), dos))
                _head_col_store(dsk_ref, g, par, -jnp.exp(_sink_col(sink_ref, g, par) - lse) * deltas[par])
            for j in range(PAIRS):
                dq_pair = _rope(dq_acc[j * BLK : (j + 1) * BLK], *tabs, sign=-1.0) * (HEAD_DIM ** -0.5)
                dq_ref[:, (PAIRS * g + j) * 128 : (PAIRS * g + j + 1) * 128] = dq_pair.astype(BF16)
            dk_g.append(dk_par[0] + pltpu.roll(dk_par[1], HEAD_DIM, 1))
            dv_g.append(dv_par[0] + pltpu.roll(dv_par[1], HEAD_DIM, 1))
        low_k = lax.broadcasted_iota(jnp.int32, (2 * BLK, KV_W), 1) < HEAD_DIM
        dk = jnp.where(low_k, dk_g[0], pltpu.roll(dk_g[1], HEAD_DIM, 1))
        dv = jnp.where(low_k, dv_g[0], pltpu.roll(dv_g[1], HEAD_DIM, 1))
        dkp_ref[...] = dk[0:BLK]
        dkc_ref[...] = dk[BLK:]
        dvp_ref[...] = dv[0:BLK]
        dvc_ref[...] = dv[BLK:]

    qspec = pl.BlockSpec((None, BLK, D), lambda s, n: (s, n, 0))
    kvspec = pl.BlockSpec((None, BLK, KV_W), lambda s, n: (s, n, 0))
    hspec = pl.BlockSpec((None, BLK, N_HEADS), lambda s, n: (s, n, 0))
    kv = jax.ShapeDtypeStruct((NS, S, KV_W), F32)
    return pl.pallas_call(
        body, name="attn_bwd", grid=(NS, S // BLK),
        out_shape=[jax.ShapeDtypeStruct((NS, S, D), BF16), kv, kv, kv, kv, jax.ShapeDtypeStruct((NS, S, N_HEADS), F32)],
        in_specs=list(_attn_specs()) + [qspec, qspec, hspec, kvspec, kvspec, kvspec],
        out_specs=[qspec, kvspec, kvspec, kvspec, kvspec, hspec],
        compiler_params=_cparams("parallel", "parallel"),
    )(sinks, qr3, kv3, kv3, do3, o3, lse3, cos3, sa3, sb3)


def _attn_post(dkc3, dkp3, dvc3, dvp3, cos3, sa3, sb3):
    NS, S, _ = dkc3.shape
    tm = min(16 * BLK, S)
    nt = S // tm

    def body(dkc_ref, dkp_ref, dkn_ref, dvc_ref, dvp_ref, dvn_ref, cos_ref, sa_ref, sb_ref, dkv_ref):
        has_next = pl.program_id(1) < nt - 1

        def join(cur_ref, prev_ref, next_ref):
            ahead = jnp.where(has_next, next_ref[...], 0.0)
            shifted = ahead if tm == BLK else jnp.concatenate([prev_ref[BLK:, :], ahead], axis=0)
            return cur_ref[...] + shifted

        tabs = cos_ref[...], sa_ref[...], sb_ref[...]
        dkv_ref[:, 0:KV_W] = _rope(join(dkc_ref, dkp_ref, dkn_ref), *tabs, sign=-1.0).astype(BF16)
        dkv_ref[:, KV_W:] = join(dvc_ref, dvp_ref, dvn_ref).astype(BF16)

    cur = pl.BlockSpec((None, tm, KV_W), lambda s, j: (s, j, 0))
    nxt = pl.BlockSpec((None, BLK, KV_W), lambda s, j: (s, jnp.minimum((j + 1) * (tm // BLK), S // BLK - 1), 0))
    return pl.pallas_call(
        body, name="attn_post", grid=(NS, nt),
        out_shape=jax.ShapeDtypeStruct((NS, S, 2 * KV_W), BF16),
        in_specs=[cur, cur, nxt, cur, cur, nxt, cur, cur, cur],
        out_specs=pl.BlockSpec((None, tm, 2 * KV_W), lambda s, j: (s, j, 0)),
        compiler_params=_cparams("parallel", "parallel"),
    )(dkc3, dkp3, dkp3, dvc3, dvp3, dvp3, cos3, sa3, sb3)


def _in_bwd(segs, w_in_t, x2, dx1, ln_pre, after):
    T = x2.shape[0]
    tm = min(ROW_TILE, T)
    nt = T // tm
    ns = len(segs)
    widths = [s.shape[1] for s in segs]

    def body(*refs):
        seg_refs = refs[:ns]
        w_ref, x_ref, dx1_ref, g_ref, _, gx_ref, glp_ref = refs[ns:]
        i = pl.program_id(0)

        @pl.when(i == 0)
        def _():
            glp_ref[...] = jnp.zeros_like(glp_ref)

        dh = None
        zc = 0
        for sref, w in zip(seg_refs, widths):
            part = _dot(sref[...], w_ref[_orig_col(zc) : _orig_col(zc) + w, :])
            dh = part if dh is None else dh + part
            zc += w
        xv = x_ref[...]
        r1 = lax.rsqrt(jnp.mean(xv * xv, axis=-1, keepdims=True) + EPS)
        xhat = xv * r1
        glp_ref[...] += _rowsum8(dh * xhat)
        dhg = dh * g_ref[...]
        gx_ref[...] = dx1_ref[...] + r1 * (dhg - xhat * jnp.mean(dhg * xhat, axis=-1, keepdims=True))

    row = pl.BlockSpec((tm, D), lambda i: (i, 0))
    return pl.pallas_call(
        body, name="in_bwd", grid=(nt,),
        out_shape=[jax.ShapeDtypeStruct((T, D), F32), jax.ShapeDtypeStruct((8, D), F32)],
        in_specs=[pl.BlockSpec((tm, w), lambda i: (i, 0)) for w in widths]
        + [_weight_spec((NW, D)), row, row, _weight_spec((1, D)), pl.BlockSpec(memory_space=pl.ANY)],
        out_specs=[row, pl.BlockSpec((8, D), lambda i: (0, 0))],
        compiler_params=_cparams("arbitrary"),
    )(*segs, w_in_t, x2, dx1, ln_pre, after)


def _grad_matmul(a, b, name, rows=None, into=None):
    T, M = a.shape
    N = b.shape[1]
    tk = min(2048, T)
    nk = T // tk

    def body(a_ref, b_ref, *rest):
        o_ref, acc = rest[-2:]
        k = pl.program_id(0)

        @pl.when(k == 0)
        def _():
            acc[...] = jnp.zeros_like(acc)

        acc[...] += _dot_tn(a_ref[...].astype(BF16), b_ref[...])

        @pl.when(k == nk - 1)
        def _():
            o_ref[...] = acc[...].astype(BF16)

    in_specs = [pl.BlockSpec((tk, M), lambda k: (k, 0)), pl.BlockSpec((tk, N), lambda k: (k, 0))]
    if rows is None:
        out_shape, out_spec = (M, N), pl.BlockSpec((M, N), lambda k: (0, 0))
    else:
        out_shape, out_spec = (rows[1], N), pl.BlockSpec((pl.Element(M), pl.Element(N)), lambda k: (rows[0], 0))
    operands = (a, b) if into is None else (a, b, into)
    return pl.pallas_call(
        body, name=name, grid=(nk,), out_shape=jax.ShapeDtypeStruct(out_shape, BF16),
        in_specs=in_specs if into is None else in_specs + [pl.BlockSpec(memory_space=pl.ANY)],
        out_specs=out_spec, input_output_aliases={} if into is None else {2: 0},
        scratch_shapes=[pltpu.VMEM((M, N), F32)],
        compiler_params=_cparams("arbitrary"),
    )(*operands)


def _pack_small(gw, gvec, glp_pre, glp_post, dsk):
    T = dsk.shape[0]

    def body(gw_ref, gvec_ref, pre_ref, post_ref, dsk_ref, gdw_ref, gs_ref):
        gwf = gw_ref[...].reshape(CONV_HALO, 8, D).sum(axis=1)
        for d in range(N_DEV):
            gdw_ref[d] = gwf[:, 128 * d : 128 * (d + 1)]
        gs_ref[...] = jnp.zeros_like(gs_ref)
        gs_ref[0:1, :] = jnp.sum(pre_ref[...], axis=0, keepdims=True)
        gs_ref[1:2, :] = jnp.sum(post_ref[...], axis=0, keepdims=True)
        gs_ref[2:3, :] = jnp.sum(gvec_ref[16:24, :], axis=0, keepdims=True)
        gs_ref[3:4, :] = jnp.sum(gvec_ref[0:8, :], axis=0, keepdims=True)
        gs_ref[4:5, :] = jnp.sum(gvec_ref[8:16, :], axis=0, keepdims=True)
        gs_ref[5:6, 0:N_HEADS] = jnp.sum(dsk_ref[...], axis=0, keepdims=True)

    return pl.pallas_call(
        body, name="pack_small",
        out_shape=[jax.ShapeDtypeStruct((N_DEV, CONV_HALO, 128), F32), jax.ShapeDtypeStruct((8, D), F32)],
        compiler_params=_cparams(),
    )(gw, gvec, glp_pre, glp_post, dsk)


def _adam_update(g, w_ref, m_ref, v_ref, g_ref, d_ref, nm_ref, nv_ref):
    nm = ADAM_B1 * m_ref[...] + (1.0 - ADAM_B1) * g
    nv = ADAM_B2 * v_ref[...] + (1.0 - ADAM_B2) * (g * g)
    m_hat = nm / (1.0 - ADAM_B1 ** ADAM_STEP)
    v_hat = nv / (1.0 - ADAM_B2 ** ADAM_STEP)
    g_ref[...] = g
    d_ref[...] = -ADAM_LR * (m_hat / (jnp.sqrt(v_hat) + ADAM_EPS) + ADAM_WD * w_ref[...])
    nm_ref[...] = nm
    nv_ref[...] = nv


def _adamw(parts, w, m, v, name):
    R, C = w.shape
    tr = R if R <= 256 else 128

    def body(p_ref, *rest):
        g = p_ref[0].astype(F32)
        for s in range(1, N_DEV):
            g = g + p_ref[s].astype(F32)
        _adam_update(g, *rest)

    blk = pl.BlockSpec((tr, C), lambda i: (i, 0))
    return pl.pallas_call(
        body, name=name, grid=(R // tr,), out_shape=[jax.ShapeDtypeStruct((R, C), F32)] * 4,
        in_specs=[pl.BlockSpec((N_DEV, tr, C), lambda i: (0, i, 0)), blk, blk, blk], out_specs=[blk] * 4,
        compiler_params=_cparams("parallel"),
    )(parts, w, m, v)


def _adamw_many(groups, name):
    n = len(groups)
    R, C = groups[0][1].shape
    tr = 32

    def body(*refs):
        ins, outs = refs[: 4 * n], refs[4 * n :]
        for j in range(n):
            p_ref = ins[4 * j]
            g = p_ref[0].astype(F32)
            for s in range(1, N_DEV):
                g = g + p_ref[s].astype(F32)
            _adam_update(g, *ins[4 * j + 1 : 4 * j + 4], *outs[4 * j : 4 * j + 4])

    blk = pl.BlockSpec((tr, C), lambda i: (i, 0))
    res = pl.pallas_call(
        body, name=name, grid=(R // tr,), out_shape=[jax.ShapeDtypeStruct((R, C), F32)] * (4 * n),
        in_specs=[pl.BlockSpec((N_DEV, tr, C), lambda i: (0, i, 0)), blk, blk, blk] * n, out_specs=[blk] * (4 * n),
        compiler_params=_cparams("parallel"),
    )(*[a for grp in groups for a in grp])
    return [res[4 * j : 4 * j + 4] for j in range(n)]


def _adamw_own(parts, own, me, w, m, v, name):
    R, C = w.shape
    tr = max(t for t in range(16, 513, 16) if R % t == 0)

    def body(me_ref, p_ref, own_ref, *rest):
        g = None
        for s in range(N_DEV):
            part = jnp.where(me_ref[0] == s, own_ref[...], p_ref[s]).astype(F32)
            g = part if g is None else g + part
        _adam_update(g, *rest)

    blk = pl.BlockSpec((tr, C), lambda i, me: (i, 0))
    return pl.pallas_call(
        body, name=name, out_shape=[jax.ShapeDtypeStruct((R, C), F32)] * 4,
        grid_spec=pltpu.PrefetchScalarGridSpec(
            num_scalar_prefetch=1, grid=(R // tr,),
            in_specs=[pl.BlockSpec((N_DEV, tr, C), lambda i, me: (0, i, 0)),
                      pl.BlockSpec((None, tr, C), lambda i, me: (me[0], i, 0)), blk, blk, blk],
            out_specs=[blk] * 4),
        compiler_params=_cparams("parallel"),
    )(me, parts, own, w, m, v)


def kernel(x, p, positions, w_in, ln_pre, ln_post, w_dw, b_dw, conv_ln_g, conv_ln_b, w_pw, sinks, w_br_conv, w_br_attn, w_out, w_ple_gate, w_ple_proj, loss_target, m_w_in, m_ln_pre, m_ln_post, m_w_dw, m_b_dw, m_conv_ln_g, m_conv_ln_b, m_w_pw, m_sinks, m_w_br_conv, m_w_br_attn, m_w_out, m_w_ple_gate, m_w_ple_proj, v_w_in, v_ln_pre, v_ln_post, v_w_dw, v_b_dw, v_conv_ln_g, v_conv_ln_b, v_w_pw, v_sinks, v_w_br_conv, v_w_br_attn, v_w_out, v_w_ple_gate, v_w_ple_proj):
    NS, S, _ = x.shape
    T = NS * S
    x2 = x.reshape(T, D)
    p2 = p.reshape(T, PLE)
    tgt = loss_target.reshape(T, D)
    pos = positions.reshape(T, 1)

    row_sharded = [w_pw[0], w_br_conv[0], w_br_attn[0], w_out[0], w_ple_gate[0]]
    sh_rows = D // N_DEV
    w_t, m_t, v_t = (jnp.swapaxes(a[0], 0, 1) for a in (w_in, m_w_in, v_w_in))
    cos, sa, sb, g_in = _rope_tables(pos, _TwoLevelGather([w_t.astype(BF16)]))
    w_in_f = g_in.reshape(NW, D)
    gather_rest = _Exchange([], [w.astype(BF16) for w in row_sharded] + [
        w_ple_proj[0].astype(BF16), jnp.pad(w_dw[0], ((0, CONV_HALO - CONV_K), (0, 0)))])

    in_out = _in_proj(x2, ln_pre, w_in_f, cos, sa, sb, gather_rest)
    z2, h, qr, kv2 = in_out[:4]
    g_rows, g_pp, g_dw = in_out[4:9], in_out[9], in_out[10]
    full = [g.reshape(D, D) for g in g_rows]
    w_pp_f = g_pp.transpose(1, 0, 2).reshape(PLE, D)
    w_dw_f = g_dw.transpose(1, 0, 2).reshape(CONV_HALO, D)
    z3 = z2.reshape(NS, S, NW)
    c3, cs3 = _conv_fwd(z3, w_dw_f, b_dw, conv_ln_g, conv_ln_b)
    qr3, kv3 = qr.reshape(NS, S, D), kv2.reshape(NS, S, 4 * N_KV * KV_W)
    sinks1 = sinks.reshape(N_HEADS)
    o3, lse3 = _attn_fwd(qr3, kv3, sinks1)
    o = o3.reshape(T, D)
    cs = cs3.reshape(T, D)
    (dya0, ya1, dya, yb0, dyb, m, dmo, x1, dgl, dpp, dcs, do, dcg, dag, dgc, dga, dx1, loss_blk, glp_post) = _mid(
        cs, o, z2, x2, p2, tgt, ln_post, full + [w_pp_f])

    gp_rows = [_grad_matmul(a, b, nm).reshape(N_DEV, sh_rows, D) for a, b, nm in (
        (cs, dya0, "grad_w_pw"), (ya1, dya, "grad_w_br_conv"), (yb0, dyb, "grad_w_br_attn"),
        (m, dmo, "grad_w_out"), (x1, dgl, "grad_w_ple_gate"))]
    gp_pp = _grad_matmul(p2, dpp, "grad_w_ple_proj").reshape(PLE, N_DEV, D // N_DEV).transpose(1, 0, 2)
    conv_out = _conv_bwd(z3, c3, dcs.reshape(NS, S, D), w_dw_f, conv_ln_g, conv_ln_b,
                         _Exchange(gp_rows + [gp_pp], [loss_blk]))
    dzvu3, gw, gvec, r_rows, r_pp, r_loss = conv_out[0], conv_out[1], conv_out[2], conv_out[3:8], conv_out[8], conv_out[9]
    loss = jnp.sum(r_loss[:, 0, 0])
    tab3 = [t.reshape(NS, S, 128) for t in (cos, sa, sb)]
    dq3, dkc3, dkp3, dvc3, dvp3, dsk3 = _attn_bwd(qr3, kv3, do.reshape(NS, S, D), o3, lse3, sinks1, *tab3)
    dkv3 = _attn_post(dkc3, dkp3, dvc3, dvp3, *tab3)
    segs = [dzvu3.reshape(T, 2 * D), dcg, dq3.reshape(T, D), dag, dgc, dga, dkv3.reshape(T, 2 * KV_W)]
    gp_in, zc = None, 0
    for j, s in enumerate(segs):
        gp_in = _grad_matmul(s, h, f"grad_w_in_{j}", rows=(_orig_col(zc), NW), into=gp_in)
        zc += s.shape[1]
    gp_in = gp_in.reshape(N_DEV, NW // N_DEV, D)
    in_send, in_recv, in_own, in_land, sent = _scatter_send(gp_in, "w_in_grad_send")
    grad_x2, glp_pre = _in_bwd(segs, w_in_f, x2, dx1, ln_pre, sent)
    gp_dw, gp_small = _pack_small(gw, gvec, glp_pre, glp_post, dsk3.reshape(T, N_HEADS))
    r_dw, r_small = _Exchange([gp_dw], [gp_small]).alone("small_grad_exchange")

    res = {}
    names_rows = ["w_pw", "w_br_conv", "w_br_attn", "w_out", "w_ple_gate"]
    wmv = {"w_pw": (w_pw, m_w_pw, v_w_pw), "w_br_conv": (w_br_conv, m_w_br_conv, v_w_br_conv),
           "w_br_attn": (w_br_attn, m_w_br_attn, v_w_br_attn), "w_out": (w_out, m_w_out, v_w_out),
           "w_ple_gate": (w_ple_gate, m_w_ple_gate, v_w_ple_gate)}
    rows_res = _adamw_many([(parts, *(a[0] for a in wmv[nm])) for nm, parts in zip(names_rows, r_rows)], "adamw_rows")
    res.update(zip(names_rows, rows_res))
    res["w_ple_proj"] = _adamw(r_pp, w_ple_proj[0], m_w_ple_proj[0], v_w_ple_proj[0], "adamw_w_ple_proj")
    pad_dw = lambda a: jnp.pad(a[0], ((0, CONV_HALO - CONV_K), (0, 0)))
    res["w_dw"] = [a[:CONV_K] for a in _adamw(r_dw, pad_dw(w_dw), pad_dw(m_w_dw), pad_dw(v_w_dw), "adamw_w_dw")]

    def stack_small(a_pre, a_post, a_b, a_g, a_bb, a_s):
        sk = jnp.pad(a_s, ((0, 0), (0, D - N_HEADS)))
        return jnp.concatenate([a_pre, a_post, a_b, a_g, a_bb, sk, jnp.zeros((2, D), F32)], axis=0)

    small = _adamw(
        r_small, stack_small(ln_pre, ln_post, b_dw, conv_ln_g, conv_ln_b, sinks),
        stack_small(m_ln_pre, m_ln_post, m_b_dw, m_conv_ln_g, m_conv_ln_b, m_sinks),
        stack_small(v_ln_pre, v_ln_post, v_b_dw, v_conv_ln_g, v_conv_ln_b, v_sinks), "adamw_small")
    for j, nm in enumerate(["ln_pre", "ln_post", "b_dw", "conv_ln_g", "conv_ln_b"]):
        res[nm] = [a[j] for a in small]
    res["sinks"] = [a[5, :N_HEADS] for a in small]
    in_own, in_land = _scatter_wait(in_send, in_recv, in_own, in_land, small[0], "w_in_grad_wait")
    me = _slot(*_my_place()).astype(jnp.int32).reshape(1)
    res["w_in"] = [jnp.swapaxes(a, 0, 1) for a in _adamw_own(in_land, in_own, me, w_t, m_t, v_t, "adamw_w_in")]

    order = ["w_in", "ln_pre", "ln_post", "w_dw", "b_dw", "conv_ln_g", "conv_ln_b", "w_pw", "sinks", "w_br_conv",
             "w_br_attn", "w_out", "w_ple_gate", "w_ple_proj"]
    outs = [loss, grad_x2.reshape(NS, S, D)]
    for kind in range(4):
        outs += [res[nm][kind][None] for nm in order]
    return tuple(outs)
```

```python
import numpy as np

import jax
import jax.numpy as jnp
from jax import lax
from jax.experimental import pallas as pl
from jax.experimental.pallas import tpu as pltpu

F32 = jnp.float32
BF16 = jnp.bfloat16

D = 1024
N_HEADS = 16
N_KV = 2
HEAD_DIM = 64
GROUP = N_HEADS // N_KV
KV_W = N_KV * HEAD_DIM
CONV_K = 31
CONV_HALO = 32
BLK = 128
ROPE_DIM = 16
ROPE_THETA = 500000.0
EPS = 1e-6
PLE = 256
NW = 7 * D + 2 * KV_W
N_DEV = 8

CB_VAL, CB_GLU, CB_CGATE, CB_Q, CB_AGATE, CB_GCONV, CB_GATTN = range(7)
CB_K = 7 * D // KV_W

ADAM_LR, ADAM_B1, ADAM_B2, ADAM_EPS, ADAM_WD, ADAM_STEP = 0.001, 0.9, 0.999, 1e-08, 0.01, 10

VMEM_LIMIT = 56 * 1024 * 1024
ROW_TILE = 512
CONV_TILE = 256


def _cparams(*sem):
    return pltpu.CompilerParams(dimension_semantics=sem if sem else None, vmem_limit_bytes=VMEM_LIMIT)


def _sig(v):
    return 1.0 / (1.0 + jnp.exp(-v))


def _rowsum8(a):
    return a.reshape(a.shape[0] // 8, 8, a.shape[1]).sum(axis=0)


def _dot(a, b):
    return jnp.dot(a, b, preferred_element_type=F32)


def _dot_nt(a, b):
    return lax.dot_general(a, b, (((1,), (1,)), ((), ())), preferred_element_type=F32)


def _dot_tn(a, b):
    return lax.dot_general(a, b, (((0,), (0,)), ((), ())), preferred_element_type=F32)


def _orig_col(zc):
    if zc < 4 * D:
        return zc
    return zc - 7 * D + 4 * D if zc >= 7 * D else zc + 2 * KV_W


def _my_place():
    return lax.axis_index("x"), lax.axis_index("y"), lax.axis_index("c")


def _slot(px, py, pc):
    return 4 * px + 2 * py + pc


class _TwoLevelGather:
    def __init__(self, shards):
        self.arrs = list(shards)
        self.n = len(self.arrs)
        self.out_shape = [jax.ShapeDtypeStruct((N_DEV,) + s.shape, s.dtype) for s in shards]
        self.specs = [pl.BlockSpec(memory_space=pl.ANY)] * self.n
        self.scratch = [pltpu.SemaphoreType.DMA((self.n, 7)), pltpu.SemaphoreType.DMA((self.n, 7)),
                        pltpu.SemaphoreType.DMA((self.n,))]

    def _plan(self, ins, outs, sems):
        send_sems, recv_sems, local_sems = sems
        x, y, c = _my_place()
        me, sibling = (x, y, c), (x, y, 1 - c)
        chips = [(1 - x, y), (x, 1 - y), (1 - x, 1 - y)]

        def copy(a, k, block, to, src=None):
            rows = outs[a].at[_slot(*block)]
            return pltpu.make_async_remote_copy(
                src_ref=rows if src is None else src, dst_ref=rows, send_sem=send_sems.at[a, k],
                recv_sem=recv_sems.at[a, k], device_id=to, device_id_type=pl.DeviceIdType.MESH)

        mine = [pltpu.make_async_copy(ins[a], outs[a].at[_slot(*me)], local_sems.at[a]) for a in range(self.n)]
        first = []
        for a in range(self.n):
            first.append(copy(a, 0, me, sibling, src=ins[a]))
            first += [copy(a, 1 + j, me, (*chips[j], c), src=ins[a]) for j in range(2)]
        return copy, mine, first, me, sibling, chips, c

    def start(self, ins, outs, sems):
        _, mine, first, *_ = self._plan(ins, outs, sems)
        for cp in mine + first:
            cp.start()

    def finish(self, ins, outs, sems):
        copy, mine, first, me, sibling, chips, c = self._plan(ins, outs, sems)

        def land_and_pass(a, j, relay_to=None):
            copy(a, 1 + j, (*chips[j], c), me).wait_recv()
            if relay_to is not None:
                copy(a, 3, (*chips[j], c), (*chips[relay_to], c)).start()
            copy(a, 4 + j, (*chips[j], c), sibling).start()

        for a in range(self.n):
            @pl.when(c == 0)
            def _():
                land_and_pass(a, 0, relay_to=1)
                land_and_pass(a, 1)

            @pl.when(c == 1)
            def _():
                land_and_pass(a, 1, relay_to=0)
                land_and_pass(a, 0)

            copy(a, 3, (*chips[2], c), me).wait_recv()
            copy(a, 6, (*chips[2], c), sibling).start()
        for a in range(self.n):
            copy(a, 0, sibling, me).wait_recv()
            for j in range(3):
                copy(a, 4 + j, (*chips[j], 1 - c), me).wait_recv()
        for cp in first:
            cp.wait_send()
        for a in range(self.n):
            for k in (3, 4, 5, 6):
                copy(a, k, me, me, src=ins[a]).wait_send()
        for cp in mine:
            cp.wait()

    def carried(self, refs_in, refs_out, sems, first, last):
        @pl.when(first)
        def _():
            self.start(refs_in, refs_out, sems)

        @pl.when(last)
        def _():
            self.finish(refs_in, refs_out, sems)


class _Exchange:
    def __init__(self, scatter, bcast):
        self.arrs = list(scatter) + list(bcast)
        self.n, self.n_sc = len(self.arrs), len(scatter)
        self.out_shape = [jax.ShapeDtypeStruct(a.shape, a.dtype) for a in scatter]
        self.out_shape += [jax.ShapeDtypeStruct((N_DEV,) + a.shape, a.dtype) for a in bcast]
        self.specs = [pl.BlockSpec(memory_space=pl.ANY)] * self.n
        self.scratch = [pltpu.SemaphoreType.DMA((self.n, 7)), pltpu.SemaphoreType.DMA((self.n, 7)),
                        pltpu.SemaphoreType.DMA((self.n,))]

    def _copies(self, ins, outs, sems):
        send_sems, recv_sems, local_sems = sems
        x, y, c = _my_place()
        me = _slot(x, y, c)
        peers = _peers(x, y, c)
        mine, sends, arrivals = [], [], []
        for a in range(self.n):
            src = ins[a].at[me] if a < self.n_sc else ins[a]
            mine.append(pltpu.make_async_copy(src, outs[a].at[me], local_sems.at[a]))
        for k, peer in enumerate(peers):
            for a in range(self.n):
                src = ins[a].at[_slot(*peer)] if a < self.n_sc else ins[a]
                sends.append(pltpu.make_async_remote_copy(
                    src_ref=src, dst_ref=outs[a].at[me], send_sem=send_sems.at[a, k], recv_sem=recv_sems.at[a, k],
                    device_id=peer, device_id_type=pl.DeviceIdType.MESH))
                rows = outs[a].at[_slot(*peer)]
                arrivals.append(pltpu.make_async_remote_copy(
                    src_ref=rows, dst_ref=rows, send_sem=send_sems.at[a, k], recv_sem=recv_sems.at[a, k],
                    device_id=peer, device_id_type=pl.DeviceIdType.MESH))
        return mine, sends, arrivals

    def start(self, ins, outs, sems):
        mine, sends, _ = self._copies(ins, outs, sems)
        for cp in mine + sends:
            cp.start()

    def finish(self, ins, outs, sems):
        mine, sends, arrivals = self._copies(ins, outs, sems)
        for cp in arrivals:
            cp.wait_recv()
        for cp in sends:
            cp.wait_send()
        for cp in mine:
            cp.wait()

    def carried(self, refs_in, refs_out, sems, first, last):
        @pl.when(first)
        def _():
            self.start(refs_in, refs_out, sems)

        @pl.when(last)
        def _():
            self.finish(refs_in, refs_out, sems)

    def alone(self, name):
        n = self.n

        def body(*refs):
            ins, outs, sems = refs[:n], refs[n : 2 * n], refs[2 * n :]
            self.start(ins, outs, sems)
            self.finish(ins, outs, sems)

        return pl.pallas_call(body, name=name, out_shape=self.out_shape, in_specs=self.specs, out_specs=self.specs,
                              scratch_shapes=self.scratch)(*self.arrs)


def _peers(x, y, c):
    return [(1 - x if k & 4 else x, 1 - y if k & 2 else y, 1 - c if k & 1 else c) for k in range(1, N_DEV)]


def _scatter_send(g, name):
    hbm = pl.BlockSpec(memory_space=pltpu.HBM)
    sem = pl.BlockSpec(memory_space=pltpu.SEMAPHORE)

    def body(g_ref, land_ref, send_sems, recv_sems, g_thru, land_thru, token):
        x, y, c = _my_place()
        me = _slot(x, y, c)
        for k, peer in enumerate(_peers(x, y, c)):
            pltpu.make_async_remote_copy(
                src_ref=g_ref.at[_slot(*peer)], dst_ref=land_ref.at[me], send_sem=send_sems.at[k], recv_sem=recv_sems.at[k],
                device_id=peer, device_id_type=pl.DeviceIdType.MESH).start()
        token[...] = jnp.zeros_like(token)

    return pl.pallas_call(
        body, name=name,
        out_shape=(pltpu.SemaphoreType.DMA((N_DEV - 1,)), pltpu.SemaphoreType.DMA((N_DEV - 1,)),
                   pltpu.HBM(g.shape, g.dtype), pltpu.HBM(g.shape, g.dtype), jax.ShapeDtypeStruct((8, 128), F32)),
        in_specs=(hbm, hbm), out_specs=(sem, sem, hbm, hbm, pl.BlockSpec(memory_space=pltpu.VMEM)),
        input_output_aliases={0: 2, 1: 3},
        compiler_params=pltpu.CompilerParams(has_side_effects=pltpu.SideEffectType.DATAFLOW_SIDE_EFFECTING),
    )(pltpu.with_memory_space_constraint(g, pltpu.HBM),
      pltpu.with_memory_space_constraint(lax.empty(g.shape, g.dtype), pltpu.HBM))


def _scatter_wait(send_sems, recv_sems, g_thru, land_thru, after, name):
    hbm = pl.BlockSpec(memory_space=pltpu.HBM)
    sem = pl.BlockSpec(memory_space=pltpu.SEMAPHORE)

    def body(g_ref, land_ref, send_sems, recv_sems, after_ref, g_out, land_out):
        x, y, c = _my_place()
        for k, peer in enumerate(_peers(x, y, c)):
            cp = pltpu.make_async_remote_copy(
                src_ref=g_ref.at[_slot(*peer)], dst_ref=land_ref.at[_slot(*peer)], send_sem=send_sems.at[k],
                recv_sem=recv_sems.at[k], device_id=peer, device_id_type=pl.DeviceIdType.MESH)
            cp.wait_send()
            cp.wait_recv()

    return pl.pallas_call(
        body, name=name,
        out_shape=(pltpu.HBM(g_thru.shape, g_thru.dtype), pltpu.HBM(land_thru.shape, land_thru.dtype)),
        in_specs=(hbm, hbm, sem, sem, pl.BlockSpec(memory_space=pl.ANY)), out_specs=(hbm, hbm),
        input_output_aliases={0: 0, 1: 1},
        compiler_params=pltpu.CompilerParams(has_side_effects=pltpu.SideEffectType.DATAFLOW_SIDE_EFFECTING),
    )(g_thru, land_thru, send_sems, recv_sems, after)


def _host_split(refs, n_in, n_out, n_scratch, ex):
    k = ex.n if ex is not None else 0
    a = n_in
    b = a + k
    c = b + n_out
    d = c + k
    e = d + n_scratch
    return refs[:a], refs[a:b], refs[b:c], refs[c:d], refs[d:e], refs[e:]


def _rope_tables(pos, ex):
    T = pos.shape[0]
    tm = min(1024, T)
    nt = T // tm
    lane = np.arange(128) % HEAD_DIM
    inv = np.power(np.float32(ROPE_THETA), -np.arange(0, ROPE_DIM, 2, dtype=np.float32) / np.float32(ROPE_DIM)).astype(np.float32)
    half = ROPE_DIM // 2
    invf = np.where(lane < ROPE_DIM, inv[lane % half], 0.0).astype(np.float32)[None, :]
    m_a = (lane < half).astype(np.float32)[None, :]
    m_b = ((lane >= half) & (lane < ROPE_DIM)).astype(np.float32)[None, :]

    def body(*refs):
        ins, ex_in, (cos_ref, sa_ref, sb_ref), ex_out, _, ex_sems = _host_split(refs, 4, 3, 0, ex)
        pos_ref, invf_ref, ma_ref, mb_ref = ins
        i = pl.program_id(0)
        ex.carried(ex_in, ex_out, ex_sems, i == 0, i == nt - 1)
        ang = pos_ref[...].astype(F32) * invf_ref[...]
        sn = jnp.sin(ang)
        cos_ref[...] = jnp.cos(ang)
        sa_ref[...] = -sn * ma_ref[...]
        sb_ref[...] = sn * mb_ref[...]

    row = pl.BlockSpec((tm, 128), lambda i: (i, 0))
    cst = pl.BlockSpec((1, 128), lambda i: (0, 0))
    return pl.pallas_call(
        body, name="rope_tables", grid=(nt,), out_shape=[jax.ShapeDtypeStruct((T, 128), F32)] * 3 + ex.out_shape,
        in_specs=[pl.BlockSpec((tm, 1), lambda i: (i, 0)), cst, cst, cst] + ex.specs, out_specs=[row] * 3 + ex.specs,
        scratch_shapes=ex.scratch,
        compiler_params=_cparams("arbitrary"),
    )(pos, jnp.asarray(invf), jnp.asarray(m_a), jnp.asarray(m_b), *ex.arrs)


def _rope(t, cos, sa, sb, sign=1.0):
    parts = []
    for i in range(t.shape[1] // 128):
        ti = t[:, 128 * i : 128 * (i + 1)]
        up = pltpu.roll(ti, 128 - ROPE_DIM // 2, 1)
        dn = pltpu.roll(ti, ROPE_DIM // 2, 1)
        parts.append(ti * cos + sign * (up * sa + dn * sb))
    return parts[0] if len(parts) == 1 else jnp.concatenate(parts, axis=-1)


def _in_proj(x2, ln_pre, w_in, cos, sa, sb, ex):
    T = x2.shape[0]
    tm = min(512, T)
    nt = T // tm
    chunk = D
    kv_cols = 4 * N_KV * KV_W

    def body(*refs):
        ins, ex_in, (z_ref, h_ref, qr_ref, kv_ref), ex_out, _, ex_sems = _host_split(refs, 6, 4, 0, ex)
        x_ref, g_ref, w_ref, cos_ref, sa_ref, sb_ref = ins
        i = pl.program_id(0)
        ex.carried(ex_in, ex_out, ex_sems, i == 0, i == nt - 1)
        xv = x_ref[...]
        r = lax.rsqrt(jnp.mean(xv * xv, axis=-1, keepdims=True) + EPS)
        h = (xv * r * g_ref[...]).astype(BF16)
        h_ref[...] = h
        tabs = cos_ref[...], sa_ref[...], sb_ref[...]
        for c0 in range(0, NW, chunk):
            cw = min(chunk, NW - c0)
            zc = _dot_nt(h, w_ref[_orig_col(c0) : _orig_col(c0) + cw, :])
            z_ref[:, c0 : c0 + cw] = zc.astype(BF16)
            if c0 == CB_Q * D:
                qr_ref[...] = (_rope(zc, *tabs) * (HEAD_DIM ** -0.5)).astype(BF16)
            if c0 == CB_K * KV_W:
                low = lax.broadcasted_iota(jnp.int32, (tm, KV_W), 1) < HEAD_DIM
                for kind, t in ((0, _rope(zc[:, 0:KV_W], *tabs)), (1, zc[:, KV_W:])):
                    swapped = pltpu.roll(t, HEAD_DIM, 1)
                    cols = {(0, 0): jnp.where(low, t, 0.0), (0, 1): jnp.where(low, 0.0, swapped),
                            (1, 0): jnp.where(low, swapped, 0.0), (1, 1): jnp.where(low, 0.0, t)}
                    for (g, par), val in cols.items():
                        c = _kv2_col(kind, g, par)
                        kv_ref[:, c * KV_W : (c + 1) * KV_W] = val.astype(BF16)

    def row(w):
        return pl.BlockSpec((tm, w), lambda i: (i, 0))

    bf = lambda w: jax.ShapeDtypeStruct((T, w), BF16)
    return pl.pallas_call(
        body, name="in_proj", grid=(nt,),
        out_shape=[bf(NW), bf(D), bf(D), bf(kv_cols)] + ex.out_shape,
        in_specs=[row(D), _weight_spec((1, D)), _weight_spec((NW, D)), row(128), row(128), row(128)] + ex.specs,
        out_specs=[row(NW), row(D), row(D), row(kv_cols)] + ex.specs,
        scratch_shapes=ex.scratch,
        compiler_params=_cparams("arbitrary"),
    )(x2, ln_pre, w_in, cos, sa, sb, *ex.arrs)


def _conv_tiles(S):
    tm = min(CONV_TILE, S)
    return tm, S // tm, tm // CONV_HALO


CONV_ROWS_FWD = 32
CONV_ROWS = 16


def _fill_shifted(sh, rows):
    for b in range(1, 8):
        sh[b, 0:rows, :] = sh[0, b : b + rows, :]


def _conv_fwd(z3, w_dw, b_dw, ln_g, ln_b):
    NS, S, _ = z3.shape
    tm, nt, r = _conv_tiles(S)

    def body(val_ref, glu_ref, hval_ref, hglu_ref, w_ref, b_ref, g_ref, bb_ref, c_ref, cs_ref, ush, cbuf):
        i = pl.program_id(1)
        ush[0, CONV_HALO:, :] = val_ref[...].astype(F32) * _sig(glu_ref[...].astype(F32))
        uh = hval_ref[...].astype(F32) * _sig(hglu_ref[...].astype(F32))
        ush[0, 0:CONV_HALO, :] = jnp.where(i > 0, uh, 0.0)
        _fill_shifted(ush, tm + CONV_HALO - 8)
        for r0 in range(0, tm, CONV_ROWS_FWD):
            acc = jnp.zeros((CONV_ROWS_FWD, D), F32)
            for k in range(CONV_K):
                a, b = divmod(CONV_HALO - (CONV_K - 1) + k, 8)
                acc = acc + w_ref[k : k + 1, :] * ush[b, r0 + 8 * a : r0 + 8 * a + CONV_ROWS_FWD, :]
            cbuf[r0 : r0 + CONV_ROWS_FWD, :] = acc + b_ref[...]
        cv = cbuf[...]
        mu = jnp.mean(cv, axis=-1, keepdims=True)
        xc = cv - mu
        var = jnp.mean(xc * xc, axis=-1, keepdims=True)
        cl = xc * lax.rsqrt(var + EPS) * g_ref[...] + bb_ref[...]
        c_ref[...] = cv.astype(BF16)
        cs_ref[...] = (cl * _sig(cl)).astype(BF16)

    def cur(cb):
        return pl.BlockSpec((None, tm, D), lambda s, i: (s, i, cb))

    def halo(cb):
        return pl.BlockSpec((None, CONV_HALO, D), lambda s, i: (s, jnp.maximum(i * r - 1, 0), cb))

    vec = pl.BlockSpec((1, D), lambda s, i: (0, 0))
    out = pl.BlockSpec((None, tm, D), lambda s, i: (s, i, 0))
    return pl.pallas_call(
        body, name="conv_fwd", grid=(NS, nt),
        out_shape=[jax.ShapeDtypeStruct((NS, S, D), BF16)] * 2,
        in_specs=[cur(CB_VAL), cur(CB_GLU), halo(CB_VAL), halo(CB_GLU),
                  pl.BlockSpec((CONV_HALO, D), lambda s, i: (0, 0)), vec, vec, vec],
        out_specs=[out, out],
        scratch_shapes=[pltpu.VMEM((8, tm + CONV_HALO, D), F32), pltpu.VMEM((tm, D), F32)],
        compiler_params=_cparams("parallel", "parallel"),
    )(z3, z3, z3, z3, w_dw, b_dw, ln_g, ln_b)


PAIRS = GROUP // 2
QROWS = PAIRS * BLK


def _kv2_col(kind, g, par):
    return kind * 2 * N_KV + g * 2 + par


def _attn_mask(has_prev):
    qi = lax.broadcasted_iota(jnp.int32, (QROWS, 2 * BLK), 0) & (BLK - 1)
    kj = lax.broadcasted_iota(jnp.int32, (QROWS, 2 * BLK), 1)
    first_key = jnp.where(has_prev, 0, BLK)
    return (kj > qi) & (kj <= qi + BLK) & (kj >= first_key)


NEG_BIG = -1e30


def _attn_specs():
    q = pl.BlockSpec((None, BLK, D), lambda s, n: (s, n, 0))
    kv_cur = pl.BlockSpec((None, BLK, 4 * N_KV * KV_W), lambda s, n: (s, n, 0))
    kv_prev = pl.BlockSpec((None, BLK, 4 * N_KV * KV_W), lambda s, n: (s, jnp.maximum(n - 1, 0), 0))
    sink = pl.BlockSpec(memory_space=pltpu.SMEM)
    return sink, q, kv_cur, kv_prev


def _stack_pairs(ref, g):
    return jnp.concatenate([ref[:, (PAIRS * g + j) * 128 : (PAIRS * g + j + 1) * 128] for j in range(PAIRS)], axis=0)


def _pair_heads(g, par):
    return [GROUP * g + 2 * j + par for j in range(PAIRS)]


def _head_col_load(ref, g, par):
    return jnp.concatenate([ref[:, h : h + 1] for h in _pair_heads(g, par)], axis=0)


def _head_col_store(ref, g, par, col):
    for j, h in enumerate(_pair_heads(g, par)):
        ref[:, h : h + 1] = col[j * BLK : (j + 1) * BLK]


def _sink_col(sink_ref, g, par):
    return jnp.concatenate([jnp.full((BLK, 1), sink_ref[h], F32) for h in _pair_heads(g, par)], axis=0)


def _kv2_block(kvc_ref, kvp_ref, has_prev, c):
    col = slice(c * KV_W, (c + 1) * KV_W)
    prev = jnp.where(has_prev, kvp_ref[:, col], jnp.zeros((BLK, KV_W), BF16))
    return jnp.concatenate([prev, kvc_ref[:, col]], axis=0)


def _attn_fwd(qr3, kv3, sinks):
    NS, S, _ = qr3.shape

    def body(sink_ref, q_ref, kvc_ref, kvp_ref, o_ref, lse_ref):
        has_prev = pl.program_id(1) > 0
        mask = _attn_mask(has_prev)[0:BLK]
        for g in range(N_KV):
            kv = [[_kv2_block(kvc_ref, kvp_ref, has_prev, _kv2_col(kind, g, par)) for par in range(2)] for kind in range(2)]
            for j in range(PAIRS):
                cols = slice((PAIRS * g + j) * 128, (PAIRS * g + j + 1) * 128)
                q2 = q_ref[:, cols]
                o_pair = None
                for par in range(2):
                    h = GROUP * g + 2 * j + par
                    s = jnp.where(mask, _dot_nt(q2, kv[0][par]), NEG_BIG)
                    sk = sink_ref[h]
                    mx = jnp.maximum(jnp.max(s, axis=-1, keepdims=True), sk)
                    e = jnp.exp(s - mx)
                    den = jnp.sum(e, axis=-1, keepdims=True) + jnp.exp(sk - mx)
                    pv = _dot(e.astype(BF16), kv[1][par]) * (1.0 / den)
                    o_pair = pv if o_pair is None else o_pair + pv
                    lse_ref[:, h : h + 1] = mx + jnp.log(den)
                o_ref[:, cols] = o_pair.astype(BF16)

    return pl.pallas_call(
        body, name="attn_fwd", grid=(NS, S // BLK),
        out_shape=[jax.ShapeDtypeStruct((NS, S, D), BF16), jax.ShapeDtypeStruct((NS, S, N_HEADS), F32)],
        in_specs=list(_attn_specs()),
        out_specs=[pl.BlockSpec((None, BLK, D), lambda s, n: (s, n, 0)),
                   pl.BlockSpec((None, BLK, N_HEADS), lambda s, n: (s, n, 0))],
        compiler_params=_cparams("parallel", "parallel"),
    )(sinks, qr3, kv3, kv3)


def _weight_spec(shape):
    return pl.BlockSpec(shape, lambda i: (0,) * len(shape), pipeline_mode=pl.Buffered(1))


def _dsilu(v, s):
    return s * (1.0 + v * (1.0 - s))


MID_TILE = 256


def _mid(cs, o, z2, x2, p2, tgt, ln_post, weights):
    T = cs.shape[0]
    tm = min(MID_TILE, T)
    nt = T // tm
    n_bf = 16

    def body(cs_ref, o_ref, cg_ref, ag_ref, gc_ref, ga_ref, x_ref, p_ref, t_ref, g_ref,
             wpw, wbrc, wbra, wout, wpg, wpp,
             dya0_ref, ya1_ref, dya_ref, yb0_ref, dyb_ref, m_ref, dmo_ref, x1_ref, dgl_ref, dpp_ref,
             dcs_ref, do_ref, dcg_ref, dag_ref, dgc_ref, dga_ref, dx1_ref, loss_ref, glp_ref, lacc):
        i = pl.program_id(0)

        @pl.when(i == 0)
        def _():
            lacc[...] = jnp.zeros_like(lacc)
            glp_ref[...] = jnp.zeros_like(glp_ref)

        cg = cg_ref[...].astype(F32)
        scg = _sig(cg)
        silu_c = cg * scg
        ya0 = _dot(cs_ref[...], wpw[...])
        ya1 = (ya0 * silu_c).astype(BF16)
        ya1_ref[...] = ya1
        ya = _dot(ya1, wbrc[...])
        ag = ag_ref[...].astype(F32)
        sag = _sig(ag)
        silu_a = ag * sag
        ov = o_ref[...].astype(F32)
        yb0 = (ov * silu_a).astype(BF16)
        yb0_ref[...] = yb0
        yb = _dot(yb0, wbra[...])
        sgc = _sig(gc_ref[...].astype(F32))
        sga = _sig(ga_ref[...].astype(F32))
        mb = (sgc * ya + sga * yb).astype(BF16)
        m_ref[...] = mb
        mo = _dot(mb, wout[...])
        r2 = lax.rsqrt(jnp.mean(mo * mo, axis=-1, keepdims=True) + EPS)
        nrm = mo * r2
        x1 = x_ref[...] + nrm * g_ref[...]
        x1b = x1.astype(BF16)
        x1_ref[...] = x1b
        gate = _sig(_dot(x1b, wpg[...]))
        pp = _dot(p_ref[...].astype(BF16), wpp[...])
        e = x1 + gate * pp - t_ref[...]
        lacc[...] += _rowsum8(e * e)
        dy = e * (1.0 / D)

        dgl = (dy * pp * gate * (1.0 - gate)).astype(BF16)
        dgl_ref[...] = dgl
        dpp_ref[...] = (dy * gate).astype(BF16)
        dx1 = dy + _dot_nt(dgl, wpg[...])
        dx1_ref[...] = dx1.astype(BF16)
        glp_ref[...] += _rowsum8(dx1 * nrm)
        dn = dx1 * g_ref[...]
        dmo = (r2 * (dn - nrm * jnp.mean(dn * nrm, axis=-1, keepdims=True))).astype(BF16)
        dmo_ref[...] = dmo
        dm = _dot_nt(dmo, wout[...])
        dya = (dm * sgc).astype(BF16)
        dyb = (dm * sga).astype(BF16)
        dya_ref[...] = dya
        dyb_ref[...] = dyb
        dgc_ref[...] = (dm * ya * sgc * (1.0 - sgc)).astype(BF16)
        dga_ref[...] = (dm * yb * sga * (1.0 - sga)).astype(BF16)
        dya1 = _dot_nt(dya, wbrc[...])
        dya0 = (dya1 * silu_c).astype(BF16)
        dya0_ref[...] = dya0
        dcg_ref[...] = (dya1 * ya0 * _dsilu(cg, scg)).astype(BF16)
        dcs_ref[...] = _dot_nt(dya0, wpw[...]).astype(BF16)
        dyb0 = _dot_nt(dyb, wbra[...])
        do_ref[...] = (dyb0 * silu_a).astype(BF16)
        dag_ref[...] = (dyb0 * ov * _dsilu(ag, sag)).astype(BF16)

        @pl.when(i == nt - 1)
        def _():
            loss_ref[...] = jnp.full(loss_ref.shape, jnp.sum(lacc[...]) * (0.5 / D), F32)

    row = pl.BlockSpec((tm, D), lambda i: (i, 0))

    def zcol(cb):
        return pl.BlockSpec((tm, D), lambda i: (i, cb))

    bf = jax.ShapeDtypeStruct((T, D), BF16)
    return pl.pallas_call(
        body, name="mid_fwd_bwd", grid=(nt,),
        out_shape=[bf] * (n_bf + 1) + [jax.ShapeDtypeStruct((8, 128), F32), jax.ShapeDtypeStruct((8, D), F32)],
        in_specs=[row, row, zcol(CB_CGATE), zcol(CB_AGATE), zcol(CB_GCONV), zcol(CB_GATTN), row,
                  pl.BlockSpec((tm, PLE), lambda i: (i, 0)), row, _weight_spec((1, D))]
        + [_weight_spec((D, D))] * 5 + [_weight_spec((PLE, D))],
        out_specs=[row] * (n_bf + 1) + [pl.BlockSpec((8, 128), lambda i: (0, 0)), pl.BlockSpec((8, D), lambda i: (0, 0))],
        scratch_shapes=[pltpu.VMEM((8, D), F32)],
        compiler_params=_cparams("arbitrary"),
    )(cs, o, z2, z2, z2, z2, x2, p2, tgt, ln_post, *weights)


def _conv_bwd(z3, c3, dcs3, w_dw, ln_g, ln_b, ex):
    NS, S, _ = z3.shape
    tm, nt, r = _conv_tiles(S)

    def body(*refs):
        ins, ex_in, outs, ex_out, scratch, ex_sems = _host_split(refs, 9, 3, 3, ex)
        val_ref, glu_ref, c_ref, dcs_ref, hc_ref, hdcs_ref, w_ref, g_ref, bb_ref = ins
        dz_ref, gw_ref, gvec_ref = outs
        dsh, ubuf, dubuf = scratch
        i = pl.program_id(1)
        first = (pl.program_id(0) == 0) & (i == 0)
        ex.carried(ex_in, ex_out, ex_sems, first, (pl.program_id(0) == NS - 1) & (i == nt - 1))

        @pl.when(first)
        def _():
            gw_ref[...] = jnp.zeros_like(gw_ref)
            gvec_ref[...] = jnp.zeros_like(gvec_ref)

        val = val_ref[...].astype(F32)
        sg = _sig(glu_ref[...].astype(F32))
        ubuf[...] = val * sg

        def ln_bwd(cv, dcs):
            cv = cv.astype(F32)
            mu = jnp.mean(cv, axis=-1, keepdims=True)
            xc = cv - mu
            rstd = lax.rsqrt(jnp.mean(xc * xc, axis=-1, keepdims=True) + EPS)
            xhat = xc * rstd
            cl = xhat * g_ref[...] + bb_ref[...]
            s = _sig(cl)
            dcl = dcs.astype(F32) * _dsilu(cl, s)
            dxh = dcl * g_ref[...]
            dc = rstd * (dxh - jnp.mean(dxh, axis=-1, keepdims=True) - xhat * jnp.mean(dxh * xhat, axis=-1, keepdims=True))
            return dc, dcl, xhat

        dc, dcl, xhat = ln_bwd(c_ref[...], dcs_ref[...])
        dsh[0, 0:tm, :] = dc
        dch, _, _ = ln_bwd(hc_ref[...], hdcs_ref[...])
        dsh[0, tm:, :] = jnp.where(i < nt - 1, dch, 0.0)
        gvec_ref[0:8, :] += _rowsum8(dcl * xhat)
        gvec_ref[8:16, :] += _rowsum8(dcl)
        gvec_ref[16:24, :] += _rowsum8(dc)
        _fill_shifted(dsh, tm + CONV_HALO - 8)

        def dc_ahead(r0, k):
            a, b = divmod(CONV_K - 1 - k, 8)
            return dsh[b, r0 + 8 * a : r0 + 8 * a + CONV_ROWS, :]

        for r0 in range(0, tm, CONV_ROWS):
            acc = jnp.zeros((CONV_ROWS, D), F32)
            for k in range(CONV_K):
                acc = acc + w_ref[k : k + 1, :] * dc_ahead(r0, k)
            dubuf[r0 : r0 + CONV_ROWS, :] = acc
        for r0 in range(0, tm, CONV_ROWS):
            ur = ubuf[r0 : r0 + CONV_ROWS, :]
            for k in range(CONV_K):
                gw_ref[8 * k : 8 * k + 8, :] += _rowsum8(ur * dc_ahead(r0, k))
        du = dubuf[...]
        dz_ref[:, 0:D] = (du * sg).astype(BF16)
        dz_ref[:, D:] = (du * val * sg * (1.0 - sg)).astype(BF16)

    def cur(cb):
        return pl.BlockSpec((None, tm, D), lambda s, i: (s, i, cb))

    nxt = pl.BlockSpec((None, CONV_HALO, D), lambda s, i: (s, jnp.minimum((i + 1) * r, S // CONV_HALO - 1), 0))
    vec = pl.BlockSpec((1, D), lambda s, i: (0, 0))
    return pl.pallas_call(
        body, name="conv_bwd", grid=(NS, nt),
        out_shape=[jax.ShapeDtypeStruct((NS, S, 2 * D), BF16), jax.ShapeDtypeStruct((CONV_HALO * 8, D), F32),
                   jax.ShapeDtypeStruct((24, D), F32)] + ex.out_shape,
        in_specs=[cur(CB_VAL), cur(CB_GLU), cur(0), cur(0), nxt, nxt,
                  pl.BlockSpec((CONV_HALO, D), lambda s, i: (0, 0)), vec, vec] + ex.specs,
        out_specs=[pl.BlockSpec((None, tm, 2 * D), lambda s, i: (s, i, 0)),
                   pl.BlockSpec((CONV_HALO * 8, D), lambda s, i: (0, 0)), pl.BlockSpec((24, D), lambda s, i: (0, 0))] + ex.specs,
        scratch_shapes=[pltpu.VMEM((8, tm + CONV_HALO, D), F32), pltpu.VMEM((tm, D), F32), pltpu.VMEM((tm, D), F32)] + ex.scratch,
        compiler_params=_cparams("arbitrary", "arbitrary"),
    )(z3, z3, c3, dcs3, c3, dcs3, w_dw, ln_g, ln_b, *ex.arrs)


def _attn_bwd(qr3, kv3, do3, o3, lse3, sinks, cos3, sa3, sb3):
    NS, S, _ = qr3.shape

    def body(sink_ref, q_ref, kvc_ref, kvp_ref, do_ref, o_ref, lse_ref, cos_ref, sa_ref, sb_ref,
             dq_ref, dkc_ref, dkp_ref, dvc_ref, dvp_ref, dsk_ref):
        has_prev = pl.program_id(1) > 0
        mask = _attn_mask(has_prev)
        tabs = cos_ref[...], sa_ref[...], sb_ref[...]
        low_q = lax.broadcasted_iota(jnp.int32, (QROWS, 128), 1) < HEAD_DIM
        dk_g, dv_g = [], []
        for g in range(N_KV):
            qs = _stack_pairs(q_ref, g)
            dos = _stack_pairs(do_ref, g)
            prod = dos.astype(F32) * _stack_pairs(o_ref, g).astype(F32)
            deltas = [jnp.sum(jnp.where(low_q, prod, 0.0), axis=-1, keepdims=True),
                      jnp.sum(jnp.where(low_q, 0.0, prod), axis=-1, keepdims=True)]
            dq_acc, dk_par, dv_par = None, [], []
            for par in range(2):
                k2 = _kv2_block(kvc_ref, kvp_ref, has_prev, _kv2_col(0, g, par))
                v2 = _kv2_block(kvc_ref, kvp_ref, has_prev, _kv2_col(1, g, par))
                lse = _head_col_load(lse_ref, g, par)
                p = jnp.exp(jnp.where(mask, _dot_nt(qs, k2), NEG_BIG) - lse)
                ds = (p * (_dot_nt(dos, v2) - deltas[par])).astype(BF16)
                dq = _dot(ds, k2)
                dq_acc = dq if dq_acc is None else dq_acc + dq
                dk_par.append(_dot_tn(ds, qs))
                dv_par.append(_dot_tn(p.astype(BF16), dos))
                _head_col_store(dsk_ref, g, par, -jnp.exp(_sink_col(sink_ref, g, par) - lse) * deltas[par])
            for j in range(PAIRS):
                dq_pair = _rope(dq_acc[j * BLK : (j + 1) * BLK], *tabs, sign=-1.0) * (HEAD_DIM ** -0.5)
                dq_ref[:, (PAIRS * g + j) * 128 : (PAIRS * g + j + 1) * 128] = dq_pair.astype(BF16)
            dk_g.append(dk_par[0] + pltpu.roll(dk_par[1], HEAD_DIM, 1))
            dv_g.append(dv_par[0] + pltpu.roll(dv_par[1], HEAD_DIM, 1))
        low_k = lax.broadcasted_iota(jnp.int32, (2 * BLK, KV_W), 1) < HEAD_DIM
        dk = jnp.where(low_k, dk_g[0], pltpu.roll(dk_g[1], HEAD_DIM, 1))
        dv = jnp.where(low_k, dv_g[0], pltpu.roll(dv_g[1], HEAD_DIM, 1))
        dkp_ref[...] = dk[0:BLK]
        dkc_ref[...] = dk[BLK:]
        dvp_ref[...] = dv[0:BLK]
        dvc_ref[...] = dv[BLK:]

    qspec = pl.BlockSpec((None, BLK, D), lambda s, n: (s, n, 0))
    kvspec = pl.BlockSpec((None, BLK, KV_W), lambda s, n: (s, n, 0))
    hspec = pl.BlockSpec((None, BLK, N_HEADS), lambda s, n: (s, n, 0))
    kv = jax.ShapeDtypeStruct((NS, S, KV_W), F32)
    return pl.pallas_call(
        body, name="attn_bwd", grid=(NS, S // BLK),
        out_shape=[jax.ShapeDtypeStruct((NS, S, D), BF16), kv, kv, kv, kv, jax.ShapeDtypeStruct((NS, S, N_HEADS), F32)],
        in_specs=list(_attn_specs()) + [qspec, qspec, hspec, kvspec, kvspec, kvspec],
        out_specs=[qspec, kvspec, kvspec, kvspec, kvspec, hspec],
        compiler_params=_cparams("parallel", "parallel"),
    )(sinks, qr3, kv3, kv3, do3, o3, lse3, cos3, sa3, sb3)


def _attn_post(dkc3, dkp3, dvc3, dvp3, cos3, sa3, sb3):
    NS, S, _ = dkc3.shape
    tm = min(16 * BLK, S)
    nt = S // tm

    def body(dkc_ref, dkp_ref, dkn_ref, dvc_ref, dvp_ref, dvn_ref, cos_ref, sa_ref, sb_ref, dkv_ref):
        has_next = pl.program_id(1) < nt - 1

        def join(cur_ref, prev_ref, next_ref):
            ahead = jnp.where(has_next, next_ref[...], 0.0)
            shifted = ahead if tm == BLK else jnp.concatenate([prev_ref[BLK:, :], ahead], axis=0)
            return cur_ref[...] + shifted

        tabs = cos_ref[...], sa_ref[...], sb_ref[...]
        dkv_ref[:, 0:KV_W] = _rope(join(dkc_ref, dkp_ref, dkn_ref), *tabs, sign=-1.0).astype(BF16)
        dkv_ref[:, KV_W:] = join(dvc_ref, dvp_ref, dvn_ref).astype(BF16)

    cur = pl.BlockSpec((None, tm, KV_W), lambda s, j: (s, j, 0))
    nxt = pl.BlockSpec((None, BLK, KV_W), lambda s, j: (s, jnp.minimum((j + 1) * (tm // BLK), S // BLK - 1), 0))
    return pl.pallas_call(
        body, name="attn_post", grid=(NS, nt),
        out_shape=jax.ShapeDtypeStruct((NS, S, 2 * KV_W), BF16),
        in_specs=[cur, cur, nxt, cur, cur, nxt, cur, cur, cur],
        out_specs=pl.BlockSpec((None, tm, 2 * KV_W), lambda s, j: (s, j, 0)),
        compiler_params=_cparams("parallel", "parallel"),
    )(dkc3, dkp3, dkp3, dvc3, dvp3, dvp3, cos3, sa3, sb3)


def _in_bwd(segs, w_in_t, x2, dx1, ln_pre, after):
    T = x2.shape[0]
    tm = min(ROW_TILE, T)
    nt = T // tm
    ns = len(segs)
    widths = [s.shape[1] for s in segs]

    def body(*refs):
        seg_refs = refs[:ns]
        w_ref, x_ref, dx1_ref, g_ref, _, gx_ref, glp_ref = refs[ns:]
        i = pl.program_id(0)

        @pl.when(i == 0)
        def _():
            glp_ref[...] = jnp.zeros_like(glp_ref)

        dh = None
        zc = 0
        for sref, w in zip(seg_refs, widths):
            part = _dot(sref[...], w_ref[_orig_col(zc) : _orig_col(zc) + w, :])
            dh = part if dh is None else dh + part
            zc += w
        xv = x_ref[...]
        r1 = lax.rsqrt(jnp.mean(xv * xv, axis=-1, keepdims=True) + EPS)
        xhat = xv * r1
        glp_ref[...] += _rowsum8(dh * xhat)
        dhg = dh * g_ref[...]
        gx_ref[...] = dx1_ref[...].astype(F32) + r1 * (dhg - xhat * jnp.mean(dhg * xhat, axis=-1, keepdims=True))

    row = pl.BlockSpec((tm, D), lambda i: (i, 0))
    return pl.pallas_call(
        body, name="in_bwd", grid=(nt,),
        out_shape=[jax.ShapeDtypeStruct((T, D), F32), jax.ShapeDtypeStruct((8, D), F32)],
        in_specs=[pl.BlockSpec((tm, w), lambda i: (i, 0)) for w in widths]
        + [_weight_spec((NW, D)), row, row, _weight_spec((1, D)), pl.BlockSpec(memory_space=pl.ANY)],
        out_specs=[row, pl.BlockSpec((8, D), lambda i: (0, 0))],
        compiler_params=_cparams("arbitrary"),
    )(*segs, w_in_t, x2, dx1, ln_pre, after)


def _grad_matmul(a, b, name, rows=None, into=None):
    T, M = a.shape
    N = b.shape[1]
    tk = min(2048, T)
    nk = T // tk

    def body(a_ref, b_ref, *rest):
        o_ref, acc = rest[-2:]
        k = pl.program_id(0)

        @pl.when(k == 0)
        def _():
            acc[...] = jnp.zeros_like(acc)

        acc[...] += _dot_tn(a_ref[...].astype(BF16), b_ref[...])

        @pl.when(k == nk - 1)
        def _():
            o_ref[...] = acc[...].astype(BF16)

    in_specs = [pl.BlockSpec((tk, M), lambda k: (k, 0)), pl.BlockSpec((tk, N), lambda k: (k, 0))]
    if rows is None:
        out_shape, out_spec = (M, N), pl.BlockSpec((M, N), lambda k: (0, 0))
    else:
        out_shape, out_spec = (rows[1], N), pl.BlockSpec((pl.Element(M), pl.Element(N)), lambda k: (rows[0], 0))
    operands = (a, b) if into is None else (a, b, into)
    return pl.pallas_call(
        body, name=name, grid=(nk,), out_shape=jax.ShapeDtypeStruct(out_shape, BF16),
        in_specs=in_specs if into is None else in_specs + [pl.BlockSpec(memory_space=pl.ANY)],
        out_specs=out_spec, input_output_aliases={} if into is None else {2: 0},
        scratch_shapes=[pltpu.VMEM((M, N), F32)],
        compiler_params=_cparams("arbitrary"),
    )(*operands)


def _pack_small(gw, gvec, glp_pre, glp_post, dsk):
    T = dsk.shape[0]

    def body(gw_ref, gvec_ref, pre_ref, post_ref, dsk_ref, gdw_ref, gs_ref):
        gwf = gw_ref[...].reshape(CONV_HALO, 8, D).sum(axis=1)
        for d in range(N_DEV):
            gdw_ref[d] = gwf[:, 128 * d : 128 * (d + 1)]
        gs_ref[...] = jnp.zeros_like(gs_ref)
        gs_ref[0:1, :] = jnp.sum(pre_ref[...], axis=0, keepdims=True)
        gs_ref[1:2, :] = jnp.sum(post_ref[...], axis=0, keepdims=True)
        gs_ref[2:3, :] = jnp.sum(gvec_ref[16:24, :], axis=0, keepdims=True)
        gs_ref[3:4, :] = jnp.sum(gvec_ref[0:8, :], axis=0, keepdims=True)
        gs_ref[4:5, :] = jnp.sum(gvec_ref[8:16, :], axis=0, keepdims=True)
        gs_ref[5:6, 0:N_HEADS] = jnp.sum(dsk_ref[...], axis=0, keepdims=True)

    return pl.pallas_call(
        body, name="pack_small",
        out_shape=[jax.ShapeDtypeStruct((N_DEV, CONV_HALO, 128), F32), jax.ShapeDtypeStruct((8, D), F32)],
        compiler_params=_cparams(),
    )(gw, gvec, glp_pre, glp_post, dsk)


def _adam_update(g, w_ref, m_ref, v_ref, g_ref, d_ref, nm_ref, nv_ref):
    nm = ADAM_B1 * m_ref[...] + (1.0 - ADAM_B1) * g
    nv = ADAM_B2 * v_ref[...] + (1.0 - ADAM_B2) * (g * g)
    m_hat = nm / (1.0 - ADAM_B1 ** ADAM_STEP)
    v_hat = nv / (1.0 - ADAM_B2 ** ADAM_STEP)
    g_ref[...] = g
    d_ref[...] = -ADAM_LR * (m_hat / (jnp.sqrt(v_hat) + ADAM_EPS) + ADAM_WD * w_ref[...])
    nm_ref[...] = nm
    nv_ref[...] = nv


def _adamw(parts, w, m, v, name):
    R, C = w.shape
    tr = R if R <= 256 else 128

    def body(p_ref, *rest):
        g = p_ref[0].astype(F32)
        for s in range(1, N_DEV):
            g = g + p_ref[s].astype(F32)
        _adam_update(g, *rest)

    blk = pl.BlockSpec((tr, C), lambda i: (i, 0))
    return pl.pallas_call(
        body, name=name, grid=(R // tr,), out_shape=[jax.ShapeDtypeStruct((R, C), F32)] * 4,
        in_specs=[pl.BlockSpec((N_DEV, tr, C), lambda i: (0, i, 0)), blk, blk, blk], out_specs=[blk] * 4,
        compiler_params=_cparams("parallel"),
    )(parts, w, m, v)


def _adamw_many(groups, name):
    n = len(groups)
    R, C = groups[0][1].shape
    tr = 32

    def body(*refs):
        ins, outs = refs[: 4 * n], refs[4 * n :]
        for j in range(n):
            p_ref = ins[4 * j]
            g = p_ref[0].astype(F32)
            for s in range(1, N_DEV):
                g = g + p_ref[s].astype(F32)
            _adam_update(g, *ins[4 * j + 1 : 4 * j + 4], *outs[4 * j : 4 * j + 4])

    blk = pl.BlockSpec((tr, C), lambda i: (i, 0))
    res = pl.pallas_call(
        body, name=name, grid=(R // tr,), out_shape=[jax.ShapeDtypeStruct((R, C), F32)] * (4 * n),
        in_specs=[pl.BlockSpec((N_DEV, tr, C), lambda i: (0, i, 0)), blk, blk, blk] * n, out_specs=[blk] * (4 * n),
        compiler_params=_cparams("parallel"),
    )(*[a for grp in groups for a in grp])
    return [res[4 * j : 4 * j + 4] for j in range(n)]


def _adamw_own(parts, own, me, w, m, v, name):
    R, C = w.shape
    tr = max(t for t in range(16, 513, 16) if R % t == 0)

    def body(me_ref, p_ref, own_ref, *rest):
        g = None
        for s in range(N_DEV):
            part = jnp.where(me_ref[0] == s, own_ref[...], p_ref[s]).astype(F32)
            g = part if g is None else g + part
        _adam_update(g, *rest)

    blk = pl.BlockSpec((tr, C), lambda i, me: (i, 0))
    return pl.pallas_call(
        body, name=name, out_shape=[jax.ShapeDtypeStruct((R, C), F32)] * 4,
        grid_spec=pltpu.PrefetchScalarGridSpec(
            num_scalar_prefetch=1, grid=(R // tr,),
            in_specs=[pl.BlockSpec((N_DEV, tr, C), lambda i, me: (0, i, 0)),
                      pl.BlockSpec((None, tr, C), lambda i, me: (me[0], i, 0)), blk, blk, blk],
            out_specs=[blk] * 4),
        compiler_params=_cparams("parallel"),
    )(me, parts, own, w, m, v)


def kernel(x, p, positions, w_in, ln_pre, ln_post, w_dw, b_dw, conv_ln_g, conv_ln_b, w_pw, sinks, w_br_conv, w_br_attn, w_out, w_ple_gate, w_ple_proj, loss_target, m_w_in, m_ln_pre, m_ln_post, m_w_dw, m_b_dw, m_conv_ln_g, m_conv_ln_b, m_w_pw, m_sinks, m_w_br_conv, m_w_br_attn, m_w_out, m_w_ple_gate, m_w_ple_proj, v_w_in, v_ln_pre, v_ln_post, v_w_dw, v_b_dw, v_conv_ln_g, v_conv_ln_b, v_w_pw, v_sinks, v_w_br_conv, v_w_br_attn, v_w_out, v_w_ple_gate, v_w_ple_proj):
    NS, S, _ = x.shape
    T = NS * S
    x2 = x.reshape(T, D)
    p2 = p.reshape(T, PLE)
    tgt = loss_target.reshape(T, D)
    pos = positions.reshape(T, 1)

    row_sharded = [w_pw[0], w_br_conv[0], w_br_attn[0], w_out[0], w_ple_gate[0]]
    sh_rows = D // N_DEV
    w_t, m_t, v_t = (jnp.swapaxes(a[0], 0, 1) for a in (w_in, m_w_in, v_w_in))
    cos, sa, sb, g_in = _rope_tables(pos, _TwoLevelGather([w_t.astype(BF16)]))
    w_in_f = g_in.reshape(NW, D)
    gather_rest = _Exchange([], [w.astype(BF16) for w in row_sharded] + [
        w_ple_proj[0].astype(BF16), jnp.pad(w_dw[0], ((0, CONV_HALO - CONV_K), (0, 0)))])

    in_out = _in_proj(x2, ln_pre, w_in_f, cos, sa, sb, gather_rest)
    z2, h, qr, kv2 = in_out[:4]
    g_rows, g_pp, g_dw = in_out[4:9], in_out[9], in_out[10]
    full = [g.reshape(D, D) for g in g_rows]
    w_pp_f = g_pp.transpose(1, 0, 2).reshape(PLE, D)
    w_dw_f = g_dw.transpose(1, 0, 2).reshape(CONV_HALO, D)
    z3 = z2.reshape(NS, S, NW)
    c3, cs3 = _conv_fwd(z3, w_dw_f, b_dw, conv_ln_g, conv_ln_b)
    qr3, kv3 = qr.reshape(NS, S, D), kv2.reshape(NS, S, 4 * N_KV * KV_W)
    sinks1 = sinks.reshape(N_HEADS)
    o3, lse3 = _attn_fwd(qr3, kv3, sinks1)
    o = o3.reshape(T, D)
    cs = cs3.reshape(T, D)
    (dya0, ya1, dya, yb0, dyb, m, dmo, x1, dgl, dpp, dcs, do, dcg, dag, dgc, dga, dx1, loss_blk, glp_post) = _mid(
        cs, o, z2, x2, p2, tgt, ln_post, full + [w_pp_f])

    gp_rows = [_grad_matmul(a, b, nm).reshape(N_DEV, sh_rows, D) for a, b, nm in (
        (cs, dya0, "grad_w_pw"), (ya1, dya, "grad_w_br_conv"), (yb0, dyb, "grad_w_br_attn"),
        (m, dmo, "grad_w_out"), (x1, dgl, "grad_w_ple_gate"))]
    gp_pp = _grad_matmul(p2, dpp, "grad_w_ple_proj").reshape(PLE, N_DEV, D // N_DEV).transpose(1, 0, 2)
    conv_out = _conv_bwd(z3, c3, dcs.reshape(NS, S, D), w_dw_f, conv_ln_g, conv_ln_b,
                         _Exchange(gp_rows + [gp_pp], [loss_blk]))
    dzvu3, gw, gvec, r_rows, r_pp, r_loss = conv_out[0], conv_out[1], conv_out[2], conv_out[3:8], conv_out[8], conv_out[9]
    loss = jnp.sum(r_loss[:, 0, 0])
    tab3 = [t.reshape(NS, S, 128) for t in (cos, sa, sb)]
    dq3, dkc3, dkp3, dvc3, dvp3, dsk3 = _attn_bwd(qr3, kv3, do.reshape(NS, S, D), o3, lse3, sinks1, *tab3)
    dkv3 = _attn_post(dkc3, dkp3, dvc3, dvp3, *tab3)
    segs = [dzvu3.reshape(T, 2 * D), dcg, dq3.reshape(T, D), dag, dgc, dga, dkv3.reshape(T, 2 * KV_W)]
    gp_in, zc = None, 0
    for j, s in enumerate(segs):
        gp_in = _grad_matmul(s, h, f"grad_w_in_{j}", rows=(_orig_col(zc), NW), into=gp_in)
        zc += s.shape[1]
    gp_in = gp_in.reshape(N_DEV, NW // N_DEV, D)
    in_send, in_recv, in_own, in_land, sent = _scatter_send(gp_in, "w_in_grad_send")
    grad_x2, glp_pre = _in_bwd(segs, w_in_f, x2, dx1, ln_pre, sent)
    gp_dw, gp_small = _pack_small(gw, gvec, glp_pre, glp_post, dsk3.reshape(T, N_HEADS))
    r_dw, r_small = _Exchange([gp_dw], [gp_small]).alone("small_grad_exchange")

    res = {}
    names_rows = ["w_pw", "w_br_conv", "w_br_attn", "w_out", "w_ple_gate"]
    wmv = {"w_pw": (w_pw, m_w_pw, v_w_pw), "w_br_conv": (w_br_conv, m_w_br_conv, v_w_br_conv),
           "w_br_attn": (w_br_attn, m_w_br_attn, v_w_br_attn), "w_out": (w_out, m_w_out, v_w_out),
           "w_ple_gate": (w_ple_gate, m_w_ple_gate, v_w_ple_gate)}
    rows_res = _adamw_many([(parts, *(a[0] for a in wmv[nm])) for nm, parts in zip(names_rows, r_rows)], "adamw_rows")
    res.update(zip(names_rows, rows_res))
    res["w_ple_proj"] = _adamw(r_pp, w_ple_proj[0], m_w_ple_proj[0], v_w_ple_proj[0], "adamw_w_ple_proj")
    pad_dw = lambda a: jnp.pad(a[0], ((0, CONV_HALO - CONV_K), (0, 0)))
    res["w_dw"] = [a[:CONV_K] for a in _adamw(r_dw, pad_dw(w_dw), pad_dw(m_w_dw), pad_dw(v_w_dw), "adamw_w_dw")]

    def stack_small(a_pre, a_post, a_b, a_g, a_bb, a_s):
        sk = jnp.pad(a_s, ((0, 0), (0, D - N_HEADS)))
        return jnp.concatenate([a_pre, a_post, a_b, a_g, a_bb, sk, jnp.zeros((2, D), F32)], axis=0)

    small = _adamw(
        r_small, stack_small(ln_pre, ln_post, b_dw, conv_ln_g, conv_ln_b, sinks),
        stack_small(m_ln_pre, m_ln_post, m_b_dw, m_conv_ln_g, m_conv_ln_b, m_sinks),
        stack_small(v_ln_pre, v_ln_post, v_b_dw, v_conv_ln_g, v_conv_ln_b, v_sinks), "adamw_small")
    for j, nm in enumerate(["ln_pre", "ln_post", "b_dw", "conv_ln_g", "conv_ln_b"]):
        res[nm] = [a[j] for a in small]
    res["sinks"] = [a[5, :N_HEADS] for a in small]
    in_own, in_land = _scatter_wait(in_send, in_recv, in_own, in_land, small[0], "w_in_grad_wait")
    me = _slot(*_my_place()).astype(jnp.int32).reshape(1)
    res["w_in"] = [jnp.swapaxes(a, 0, 1) for a in _adamw_own(in_land, in_own, me, w_t, m_t, v_t, "adamw_w_in")]

    order = ["w_in", "ln_pre", "ln_post", "w_dw", "b_dw", "conv_ln_g", "conv_ln_b", "w_pw", "sinks", "w_br_conv",
             "w_br_attn", "w_out", "w_ple_gate", "w_ple_proj"]
    outs = [loss, grad_x2.reshape(NS, S, D)]
    for kind in range(4):
        outs += [res[nm][kind][None] for nm in order]
    return tuple(outs)
```

```python
import numpy as np

import jax
import jax.numpy as jnp
from jax import lax
from jax.experimental import pallas as pl
from jax.experimental.pallas import tpu as pltpu

F32 = jnp.float32
BF16 = jnp.bfloat16

D = 1024
N_HEADS = 16
N_KV = 2
HEAD_DIM = 64
GROUP = N_HEADS // N_KV
KV_W = N_KV * HEAD_DIM
CONV_K = 31
CONV_HALO = 32
BLK = 128
ROPE_DIM = 16
ROPE_THETA = 500000.0
EPS = 1e-6
PLE = 256
NW = 7 * D + 2 * KV_W
N_DEV = 8

CB_VAL, CB_GLU, CB_CGATE, CB_Q, CB_AGATE, CB_GCONV, CB_GATTN = range(7)
CB_K = 7 * D // KV_W

ADAM_LR, ADAM_B1, ADAM_B2, ADAM_EPS, ADAM_WD, ADAM_STEP = 0.001, 0.9, 0.999, 1e-08, 0.01, 10

VMEM_LIMIT = 56 * 1024 * 1024
ROW_TILE = 512
CONV_TILE = 256


def _cparams(*sem):
    return pltpu.CompilerParams(dimension_semantics=sem if sem else None, vmem_limit_bytes=VMEM_LIMIT)


def _sig(v):
    return 0.5 * jnp.tanh(0.5 * v) + 0.5


def _rowsum8(a):
    return a.reshape(a.shape[0] // 8, 8, a.shape[1]).sum(axis=0)


def _dot(a, b):
    return jnp.dot(a, b, preferred_element_type=F32)


def _dot_nt(a, b):
    return lax.dot_general(a, b, (((1,), (1,)), ((), ())), preferred_element_type=F32)


def _dot_tn(a, b):
    return lax.dot_general(a, b, (((0,), (0,)), ((), ())), preferred_element_type=F32)


def _orig_col(zc):
    if zc < 4 * D:
        return zc
    return zc - 7 * D + 4 * D if zc >= 7 * D else zc + 2 * KV_W


def _my_place():
    return lax.axis_index("x"), lax.axis_index("y"), lax.axis_index("c")


def _slot(px, py, pc):
    return 4 * px + 2 * py + pc


class _TwoLevelGather:
    def __init__(self, shards):
        self.arrs = list(shards)
        self.n = len(self.arrs)
        self.out_shape = [jax.ShapeDtypeStruct((N_DEV,) + s.shape, s.dtype) for s in shards]
        self.specs = [pl.BlockSpec(memory_space=pl.ANY)] * self.n
        self.scratch = [pltpu.SemaphoreType.DMA((self.n, 7)), pltpu.SemaphoreType.DMA((self.n, 7)),
                        pltpu.SemaphoreType.DMA((self.n,))]

    def _plan(self, ins, outs, sems):
        send_sems, recv_sems, local_sems = sems
        x, y, c = _my_place()
        me, sibling = (x, y, c), (x, y, 1 - c)
        chips = [(1 - x, y), (x, 1 - y), (1 - x, 1 - y)]

        def copy(a, k, block, to, src=None):
            rows = outs[a].at[_slot(*block)]
            return pltpu.make_async_remote_copy(
                src_ref=rows if src is None else src, dst_ref=rows, send_sem=send_sems.at[a, k],
                recv_sem=recv_sems.at[a, k], device_id=to, device_id_type=pl.DeviceIdType.MESH)

        mine = [pltpu.make_async_copy(ins[a], outs[a].at[_slot(*me)], local_sems.at[a]) for a in range(self.n)]
        first = []
        for a in range(self.n):
            first.append(copy(a, 0, me, sibling, src=ins[a]))
            first += [copy(a, 1 + j, me, (*chips[j], c), src=ins[a]) for j in range(2)]
        return copy, mine, first, me, sibling, chips, c

    def start(self, ins, outs, sems):
        _, mine, first, *_ = self._plan(ins, outs, sems)
        for cp in mine + first:
            cp.start()

    def finish(self, ins, outs, sems):
        copy, mine, first, me, sibling, chips, c = self._plan(ins, outs, sems)

        def land_and_pass(a, j, relay_to=None):
            copy(a, 1 + j, (*chips[j], c), me).wait_recv()
            if relay_to is not None:
                copy(a, 3, (*chips[j], c), (*chips[relay_to], c)).start()
            copy(a, 4 + j, (*chips[j], c), sibling).start()

        for a in range(self.n):
            @pl.when(c == 0)
            def _():
                land_and_pass(a, 0, relay_to=1)
                land_and_pass(a, 1)

            @pl.when(c == 1)
            def _():
                land_and_pass(a, 1, relay_to=0)
                land_and_pass(a, 0)

            copy(a, 3, (*chips[2], c), me).wait_recv()
            copy(a, 6, (*chips[2], c), sibling).start()
        for a in range(self.n):
            copy(a, 0, sibling, me).wait_recv()
            for j in range(3):
                copy(a, 4 + j, (*chips[j], 1 - c), me).wait_recv()
        for cp in first:
            cp.wait_send()
        for a in range(self.n):
            for k in (3, 4, 5, 6):
                copy(a, k, me, me, src=ins[a]).wait_send()
        for cp in mine:
            cp.wait()

    def carried(self, refs_in, refs_out, sems, first, last):
        @pl.when(first)
        def _():
            self.start(refs_in, refs_out, sems)

        @pl.when(last)
        def _():
            self.finish(refs_in, refs_out, sems)


class _Exchange:
    def __init__(self, scatter, bcast):
        self.arrs = list(scatter) + list(bcast)
        self.n, self.n_sc = len(self.arrs), len(scatter)
        self.out_shape = [jax.ShapeDtypeStruct(a.shape, a.dtype) for a in scatter]
        self.out_shape += [jax.ShapeDtypeStruct((N_DEV,) + a.shape, a.dtype) for a in bcast]
        self.specs = [pl.BlockSpec(memory_space=pl.ANY)] * self.n
        self.scratch = [pltpu.SemaphoreType.DMA((self.n, 7)), pltpu.SemaphoreType.DMA((self.n, 7)),
                        pltpu.SemaphoreType.DMA((self.n,))]

    def _copies(self, ins, outs, sems):
        send_sems, recv_sems, local_sems = sems
        x, y, c = _my_place()
        me = _slot(x, y, c)
        peers = _peers(x, y, c)
        mine, sends, arrivals = [], [], []
        for a in range(self.n):
            src = ins[a].at[me] if a < self.n_sc else ins[a]
            mine.append(pltpu.make_async_copy(src, outs[a].at[me], local_sems.at[a]))
        for k, peer in enumerate(peers):
            for a in range(self.n):
                src = ins[a].at[_slot(*peer)] if a < self.n_sc else ins[a]
                sends.append(pltpu.make_async_remote_copy(
                    src_ref=src, dst_ref=outs[a].at[me], send_sem=send_sems.at[a, k], recv_sem=recv_sems.at[a, k],
                    device_id=peer, device_id_type=pl.DeviceIdType.MESH))
                rows = outs[a].at[_slot(*peer)]
                arrivals.append(pltpu.make_async_remote_copy(
                    src_ref=rows, dst_ref=rows, send_sem=send_sems.at[a, k], recv_sem=recv_sems.at[a, k],
                    device_id=peer, device_id_type=pl.DeviceIdType.MESH))
        return mine, sends, arrivals

    def start(self, ins, outs, sems):
        mine, sends, _ = self._copies(ins, outs, sems)
        for cp in mine + sends:
            cp.start()

    def finish(self, ins, outs, sems):
        mine, sends, arrivals = self._copies(ins, outs, sems)
        for cp in arrivals:
            cp.wait_recv()
        for cp in sends:
            cp.wait_send()
        for cp in mine:
            cp.wait()

    def carried(self, refs_in, refs_out, sems, first, last):
        @pl.when(first)
        def _():
            self.start(refs_in, refs_out, sems)

        @pl.when(last)
        def _():
            self.finish(refs_in, refs_out, sems)

    def alone(self, name):
        n = self.n

        def body(*refs):
            ins, outs, sems = refs[:n], refs[n : 2 * n], refs[2 * n :]
            self.start(ins, outs, sems)
            self.finish(ins, outs, sems)

        return pl.pallas_call(body, name=name, out_shape=self.out_shape, in_specs=self.specs, out_specs=self.specs,
                              scratch_shapes=self.scratch)(*self.arrs)


def _peers(x, y, c):
    return [(1 - x if k & 4 else x, 1 - y if k & 2 else y, 1 - c if k & 1 else c) for k in range(1, N_DEV)]


def _scatter_send(g, name):
    hbm = pl.BlockSpec(memory_space=pltpu.HBM)
    sem = pl.BlockSpec(memory_space=pltpu.SEMAPHORE)

    def body(g_ref, land_ref, send_sems, recv_sems, g_thru, land_thru, token):
        x, y, c = _my_place()
        me = _slot(x, y, c)
        for k, peer in enumerate(_peers(x, y, c)):
            pltpu.make_async_remote_copy(
                src_ref=g_ref.at[_slot(*peer)], dst_ref=land_ref.at[me], send_sem=send_sems.at[k], recv_sem=recv_sems.at[k],
                device_id=peer, device_id_type=pl.DeviceIdType.MESH).start()
        token[...] = jnp.zeros_like(token)

    return pl.pallas_call(
        body, name=name,
        out_shape=(pltpu.SemaphoreType.DMA((N_DEV - 1,)), pltpu.SemaphoreType.DMA((N_DEV - 1,)),
                   pltpu.HBM(g.shape, g.dtype), pltpu.HBM(g.shape, g.dtype), jax.ShapeDtypeStruct((8, 128), F32)),
        in_specs=(hbm, hbm), out_specs=(sem, sem, hbm, hbm, pl.BlockSpec(memory_space=pltpu.VMEM)),
        input_output_aliases={0: 2, 1: 3},
        compiler_params=pltpu.CompilerParams(has_side_effects=pltpu.SideEffectType.DATAFLOW_SIDE_EFFECTING),
    )(pltpu.with_memory_space_constraint(g, pltpu.HBM),
      pltpu.with_memory_space_constraint(lax.empty(g.shape, g.dtype), pltpu.HBM))


def _scatter_wait(send_sems, recv_sems, g_thru, land_thru, after, name):
    hbm = pl.BlockSpec(memory_space=pltpu.HBM)
    sem = pl.BlockSpec(memory_space=pltpu.SEMAPHORE)

    def body(g_ref, land_ref, send_sems, recv_sems, after_ref, g_out, land_out):
        x, y, c = _my_place()
        for k, peer in enumerate(_peers(x, y, c)):
            cp = pltpu.make_async_remote_copy(
                src_ref=g_ref.at[_slot(*peer)], dst_ref=land_ref.at[_slot(*peer)], send_sem=send_sems.at[k],
                recv_sem=recv_sems.at[k], device_id=peer, device_id_type=pl.DeviceIdType.MESH)
            cp.wait_send()
            cp.wait_recv()

    return pl.pallas_call(
        body, name=name,
        out_shape=(pltpu.HBM(g_thru.shape, g_thru.dtype), pltpu.HBM(land_thru.shape, land_thru.dtype)),
        in_specs=(hbm, hbm, sem, sem, pl.BlockSpec(memory_space=pl.ANY)), out_specs=(hbm, hbm),
        input_output_aliases={0: 0, 1: 1},
        compiler_params=pltpu.CompilerParams(has_side_effects=pltpu.SideEffectType.DATAFLOW_SIDE_EFFECTING),
    )(g_thru, land_thru, send_sems, recv_sems, after)


def _host_split(refs, n_in, n_out, n_scratch, ex):
    k = ex.n if ex is not None else 0
    a = n_in
    b = a + k
    c = b + n_out
    d = c + k
    e = d + n_scratch
    return refs[:a], refs[a:b], refs[b:c], refs[c:d], refs[d:e], refs[e:]


def _rope_tables(pos, ex):
    T = pos.shape[0]
    tm = min(1024, T)
    nt = T // tm
    lane = np.arange(128) % HEAD_DIM
    inv = np.power(np.float32(ROPE_THETA), -np.arange(0, ROPE_DIM, 2, dtype=np.float32) / np.float32(ROPE_DIM)).astype(np.float32)
    half = ROPE_DIM // 2
    invf = np.where(lane < ROPE_DIM, inv[lane % half], 0.0).astype(np.float32)[None, :]
    m_a = (lane < half).astype(np.float32)[None, :]
    m_b = ((lane >= half) & (lane < ROPE_DIM)).astype(np.float32)[None, :]

    def body(*refs):
        ins, ex_in, (cos_ref, sa_ref, sb_ref), ex_out, _, ex_sems = _host_split(refs, 4, 3, 0, ex)
        pos_ref, invf_ref, ma_ref, mb_ref = ins
        i = pl.program_id(0)
        ex.carried(ex_in, ex_out, ex_sems, i == 0, i == nt - 1)
        ang = pos_ref[...].astype(F32) * invf_ref[...]
        sn = jnp.sin(ang)
        cos_ref[...] = jnp.cos(ang)
        sa_ref[...] = -sn * ma_ref[...]
        sb_ref[...] = sn * mb_ref[...]

    row = pl.BlockSpec((tm, 128), lambda i: (i, 0))
    cst = pl.BlockSpec((1, 128), lambda i: (0, 0))
    return pl.pallas_call(
        body, name="rope_tables", grid=(nt,), out_shape=[jax.ShapeDtypeStruct((T, 128), F32)] * 3 + ex.out_shape,
        in_specs=[pl.BlockSpec((tm, 1), lambda i: (i, 0)), cst, cst, cst] + ex.specs, out_specs=[row] * 3 + ex.specs,
        scratch_shapes=ex.scratch,
        compiler_params=_cparams("arbitrary"),
    )(pos, jnp.asarray(invf), jnp.asarray(m_a), jnp.asarray(m_b), *ex.arrs)


def _rope(t, cos, sa, sb, sign=1.0):
    parts = []
    for i in range(t.shape[1] // 128):
        ti = t[:, 128 * i : 128 * (i + 1)]
        up = pltpu.roll(ti, 128 - ROPE_DIM // 2, 1)
        dn = pltpu.roll(ti, ROPE_DIM // 2, 1)
        parts.append(ti * cos + sign * (up * sa + dn * sb))
    return parts[0] if len(parts) == 1 else jnp.concatenate(parts, axis=-1)


def _in_proj(x2, ln_pre, w_in, cos, sa, sb, ex):
    T = x2.shape[0]
    tm = min(512, T)
    nt = T // tm
    chunk = D
    kv_cols = 4 * N_KV * KV_W

    def body(*refs):
        ins, ex_in, (z_ref, h_ref, qr_ref, kv_ref), ex_out, _, ex_sems = _host_split(refs, 6, 4, 0, ex)
        x_ref, g_ref, w_ref, cos_ref, sa_ref, sb_ref = ins
        i = pl.program_id(0)
        ex.carried(ex_in, ex_out, ex_sems, i == 0, i == nt - 1)
        xv = x_ref[...]
        r = lax.rsqrt(jnp.mean(xv * xv, axis=-1, keepdims=True) + EPS)
        h = (xv * r * g_ref[...]).astype(BF16)
        h_ref[...] = h
        tabs = cos_ref[...], sa_ref[...], sb_ref[...]
        for c0 in range(0, NW, chunk):
            cw = min(chunk, NW - c0)
            zc = _dot_nt(h, w_ref[_orig_col(c0) : _orig_col(c0) + cw, :])
            z_ref[:, c0 : c0 + cw] = zc.astype(BF16)
            if c0 == CB_Q * D:
                qr_ref[...] = (_rope(zc, *tabs) * (HEAD_DIM ** -0.5)).astype(BF16)
            if c0 == CB_K * KV_W:
                low = lax.broadcasted_iota(jnp.int32, (tm, KV_W), 1) < HEAD_DIM
                for kind, t in ((0, _rope(zc[:, 0:KV_W], *tabs)), (1, zc[:, KV_W:])):
                    swapped = pltpu.roll(t, HEAD_DIM, 1)
                    cols = {(0, 0): jnp.where(low, t, 0.0), (0, 1): jnp.where(low, 0.0, swapped),
                            (1, 0): jnp.where(low, swapped, 0.0), (1, 1): jnp.where(low, 0.0, t)}
                    for (g, par), val in cols.items():
                        c = _kv2_col(kind, g, par)
                        kv_ref[:, c * KV_W : (c + 1) * KV_W] = val.astype(BF16)

    def row(w):
        return pl.BlockSpec((tm, w), lambda i: (i, 0))

    bf = lambda w: jax.ShapeDtypeStruct((T, w), BF16)
    return pl.pallas_call(
        body, name="in_proj", grid=(nt,),
        out_shape=[bf(NW), bf(D), bf(D), bf(kv_cols)] + ex.out_shape,
        in_specs=[row(D), _weight_spec((1, D)), _weight_spec((NW, D)), row(128), row(128), row(128)] + ex.specs,
        out_specs=[row(NW), row(D), row(D), row(kv_cols)] + ex.specs,
        scratch_shapes=ex.scratch,
        compiler_params=_cparams("arbitrary"),
    )(x2, ln_pre, w_in, cos, sa, sb, *ex.arrs)


def _conv_tiles(S):
    tm = min(CONV_TILE, S)
    return tm, S // tm, tm // CONV_HALO


CONV_ROWS_FWD = 32
CONV_ROWS = 16


def _fill_shifted(sh, rows):
    for b in range(1, 8):
        sh[b, 0:rows, :] = sh[0, b : b + rows, :]


def _conv_fwd(z3, w_dw, b_dw, ln_g, ln_b):
    NS, S, _ = z3.shape
    tm, nt, r = _conv_tiles(S)

    def body(val_ref, glu_ref, hval_ref, hglu_ref, w_ref, b_ref, g_ref, bb_ref, c_ref, cs_ref, ush, cbuf):
        i = pl.program_id(1)
        ush[0, CONV_HALO:, :] = val_ref[...].astype(F32) * _sig(glu_ref[...].astype(F32))
        uh = hval_ref[...].astype(F32) * _sig(hglu_ref[...].astype(F32))
        ush[0, 0:CONV_HALO, :] = jnp.where(i > 0, uh, 0.0)
        _fill_shifted(ush, tm + CONV_HALO - 8)
        for r0 in range(0, tm, CONV_ROWS_FWD):
            acc = jnp.zeros((CONV_ROWS_FWD, D), F32)
            for k in range(CONV_K):
                a, b = divmod(CONV_HALO - (CONV_K - 1) + k, 8)
                acc = acc + w_ref[k : k + 1, :] * ush[b, r0 + 8 * a : r0 + 8 * a + CONV_ROWS_FWD, :]
            cbuf[r0 : r0 + CONV_ROWS_FWD, :] = acc + b_ref[...]
        cv = cbuf[...]
        mu = jnp.mean(cv, axis=-1, keepdims=True)
        xc = cv - mu
        var = jnp.mean(xc * xc, axis=-1, keepdims=True)
        cl = xc * lax.rsqrt(var + EPS) * g_ref[...] + bb_ref[...]
        c_ref[...] = cv.astype(BF16)
        cs_ref[...] = (cl * _sig(cl)).astype(BF16)

    def cur(cb):
        return pl.BlockSpec((None, tm, D), lambda s, i: (s, i, cb))

    def halo(cb):
        return pl.BlockSpec((None, CONV_HALO, D), lambda s, i: (s, jnp.maximum(i * r - 1, 0), cb))

    vec = pl.BlockSpec((1, D), lambda s, i: (0, 0))
    out = pl.BlockSpec((None, tm, D), lambda s, i: (s, i, 0))
    return pl.pallas_call(
        body, name="conv_fwd", grid=(NS, nt),
        out_shape=[jax.ShapeDtypeStruct((NS, S, D), BF16)] * 2,
        in_specs=[cur(CB_VAL), cur(CB_GLU), halo(CB_VAL), halo(CB_GLU),
                  pl.BlockSpec((CONV_HALO, D), lambda s, i: (0, 0)), vec, vec, vec],
        out_specs=[out, out],
        scratch_shapes=[pltpu.VMEM((8, tm + CONV_HALO, D), F32), pltpu.VMEM((tm, D), F32)],
        compiler_params=_cparams("parallel", "parallel"),
    )(z3, z3, z3, z3, w_dw, b_dw, ln_g, ln_b)


PAIRS = GROUP // 2
QROWS = PAIRS * BLK


def _kv2_col(kind, g, par):
    return kind * 2 * N_KV + g * 2 + par


def _attn_mask(has_prev):
    qi = lax.broadcasted_iota(jnp.int32, (QROWS, 2 * BLK), 0) & (BLK - 1)
    kj = lax.broadcasted_iota(jnp.int32, (QROWS, 2 * BLK), 1)
    first_key = jnp.where(has_prev, 0, BLK)
    return (kj > qi) & (kj <= qi + BLK) & (kj >= first_key)


NEG_BIG = -1e30


def _attn_specs():
    q = pl.BlockSpec((None, BLK, D), lambda s, n: (s, n, 0))
    kv_cur = pl.BlockSpec((None, BLK, 4 * N_KV * KV_W), lambda s, n: (s, n, 0))
    kv_prev = pl.BlockSpec((None, BLK, 4 * N_KV * KV_W), lambda s, n: (s, jnp.maximum(n - 1, 0), 0))
    sink = pl.BlockSpec(memory_space=pltpu.SMEM)
    return sink, q, kv_cur, kv_prev


def _stack_pairs(ref, g):
    return jnp.concatenate([ref[:, (PAIRS * g + j) * 128 : (PAIRS * g + j + 1) * 128] for j in range(PAIRS)], axis=0)


def _pair_heads(g, par):
    return [GROUP * g + 2 * j + par for j in range(PAIRS)]


def _head_col_load(ref, g, par):
    return jnp.concatenate([ref[:, h : h + 1] for h in _pair_heads(g, par)], axis=0)


def _head_col_store(ref, g, par, col):
    for j, h in enumerate(_pair_heads(g, par)):
        ref[:, h : h + 1] = col[j * BLK : (j + 1) * BLK]


def _sink_col(sink_ref, g, par):
    return jnp.concatenate([jnp.full((BLK, 1), sink_ref[h], F32) for h in _pair_heads(g, par)], axis=0)


def _kv2_block(kvc_ref, kvp_ref, has_prev, c):
    col = slice(c * KV_W, (c + 1) * KV_W)
    prev = jnp.where(has_prev, kvp_ref[:, col], jnp.zeros((BLK, KV_W), BF16))
    return jnp.concatenate([prev, kvc_ref[:, col]], axis=0)


def _attn_fwd(qr3, kv3, sinks):
    NS, S, _ = qr3.shape

    def body(sink_ref, q_ref, kvc_ref, kvp_ref, o_ref, lse_ref):
        has_prev = pl.program_id(1) > 0
        mask = _attn_mask(has_prev)[0:BLK]
        for g in range(N_KV):
            kv = [[_kv2_block(kvc_ref, kvp_ref, has_prev, _kv2_col(kind, g, par)) for par in range(2)] for kind in range(2)]
            for j in range(PAIRS):
                cols = slice((PAIRS * g + j) * 128, (PAIRS * g + j + 1) * 128)
                q2 = q_ref[:, cols]
                o_pair = None
                for par in range(2):
                    h = GROUP * g + 2 * j + par
                    s = jnp.where(mask, _dot_nt(q2, kv[0][par]), NEG_BIG)
                    sk = sink_ref[h]
                    mx = jnp.maximum(jnp.max(s, axis=-1, keepdims=True), sk)
                    e = jnp.exp(s - mx)
                    den = jnp.sum(e, axis=-1, keepdims=True) + jnp.exp(sk - mx)
                    pv = _dot(e.astype(BF16), kv[1][par]) * (1.0 / den)
                    o_pair = pv if o_pair is None else o_pair + pv
                    lse_ref[:, h : h + 1] = mx + jnp.log(den)
                o_ref[:, cols] = o_pair.astype(BF16)

    return pl.pallas_call(
        body, name="attn_fwd", grid=(NS, S // BLK),
        out_shape=[jax.ShapeDtypeStruct((NS, S, D), BF16), jax.ShapeDtypeStruct((NS, S, N_HEADS), F32)],
        in_specs=list(_attn_specs()),
        out_specs=[pl.BlockSpec((None, BLK, D), lambda s, n: (s, n, 0)),
                   pl.BlockSpec((None, BLK, N_HEADS), lambda s, n: (s, n, 0))],
        compiler_params=_cparams("parallel", "parallel"),
    )(sinks, qr3, kv3, kv3)


def _weight_spec(shape):
    return pl.BlockSpec(shape, lambda i: (0,) * len(shape), pipeline_mode=pl.Buffered(1))


def _dsilu(v, s):
    return s * (1.0 + v * (1.0 - s))


MID_TILE = 256


def _mid(cs, o, z2, x2, p2, tgt, ln_post, weights):
    T = cs.shape[0]
    tm = min(MID_TILE, T)
    nt = T // tm
    n_bf = 16

    def body(cs_ref, o_ref, cg_ref, ag_ref, gc_ref, ga_ref, x_ref, p_ref, t_ref, g_ref,
             wpw, wbrc, wbra, wout, wpg, wpp,
             dya0_ref, ya1_ref, dya_ref, yb0_ref, dyb_ref, m_ref, dmo_ref, x1_ref, dgl_ref, dpp_ref,
             dcs_ref, do_ref, dcg_ref, dag_ref, dgc_ref, dga_ref, dx1_ref, loss_ref, glp_ref, lacc):
        i = pl.program_id(0)

        @pl.when(i == 0)
        def _():
            lacc[...] = jnp.zeros_like(lacc)
            glp_ref[...] = jnp.zeros_like(glp_ref)

        cg = cg_ref[...].astype(F32)
        scg = _sig(cg)
        silu_c = cg * scg
        ya0 = _dot(cs_ref[...], wpw[...])
        ya1 = (ya0 * silu_c).astype(BF16)
        ya1_ref[...] = ya1
        ya = _dot(ya1, wbrc[...])
        ag = ag_ref[...].astype(F32)
        sag = _sig(ag)
        silu_a = ag * sag
        ov = o_ref[...].astype(F32)
        yb0 = (ov * silu_a).astype(BF16)
        yb0_ref[...] = yb0
        yb = _dot(yb0, wbra[...])
        sgc = _sig(gc_ref[...].astype(F32))
        sga = _sig(ga_ref[...].astype(F32))
        mb = (sgc * ya + sga * yb).astype(BF16)
        m_ref[...] = mb
        mo = _dot(mb, wout[...])
        r2 = lax.rsqrt(jnp.mean(mo * mo, axis=-1, keepdims=True) + EPS)
        nrm = mo * r2
        x1 = x_ref[...] + nrm * g_ref[...]
        x1b = x1.astype(BF16)
        x1_ref[...] = x1b
        gate = _sig(_dot(x1b, wpg[...]))
        pp = _dot(p_ref[...].astype(BF16), wpp[...])
        e = x1 + gate * pp - t_ref[...]
        lacc[...] += _rowsum8(e * e)
        dy = e * (1.0 / D)

        dgl = (dy * pp * gate * (1.0 - gate)).astype(BF16)
        dgl_ref[...] = dgl
        dpp_ref[...] = (dy * gate).astype(BF16)
        dx1 = dy + _dot_nt(dgl, wpg[...])
        dx1_ref[...] = dx1
        glp_ref[...] += _rowsum8(dx1 * nrm)
        dn = dx1 * g_ref[...]
        dmo = (r2 * (dn - nrm * jnp.mean(dn * nrm, axis=-1, keepdims=True))).astype(BF16)
        dmo_ref[...] = dmo
        dm = _dot_nt(dmo, wout[...])
        dya = (dm * sgc).astype(BF16)
        dyb = (dm * sga).astype(BF16)
        dya_ref[...] = dya
        dyb_ref[...] = dyb
        dgc_ref[...] = (dm * ya * sgc * (1.0 - sgc)).astype(BF16)
        dga_ref[...] = (dm * yb * sga * (1.0 - sga)).astype(BF16)
        dya1 = _dot_nt(dya, wbrc[...])
        dya0 = (dya1 * silu_c).astype(BF16)
        dya0_ref[...] = dya0
        dcg_ref[...] = (dya1 * ya0 * _dsilu(cg, scg)).astype(BF16)
        dcs_ref[...] = _dot_nt(dya0, wpw[...]).astype(BF16)
        dyb0 = _dot_nt(dyb, wbra[...])
        do_ref[...] = (dyb0 * silu_a).astype(BF16)
        dag_ref[...] = (dyb0 * ov * _dsilu(ag, sag)).astype(BF16)

        @pl.when(i == nt - 1)
        def _():
            loss_ref[...] = jnp.full(loss_ref.shape, jnp.sum(lacc[...]) * (0.5 / D), F32)

    row = pl.BlockSpec((tm, D), lambda i: (i, 0))

    def zcol(cb):
        return pl.BlockSpec((tm, D), lambda i: (i, cb))

    bf = jax.ShapeDtypeStruct((T, D), BF16)
    return pl.pallas_call(
        body, name="mid_fwd_bwd", grid=(nt,),
        out_shape=[bf] * n_bf + [jax.ShapeDtypeStruct((T, D), F32), jax.ShapeDtypeStruct((8, 128), F32),
                                 jax.ShapeDtypeStruct((8, D), F32)],
        in_specs=[row, row, zcol(CB_CGATE), zcol(CB_AGATE), zcol(CB_GCONV), zcol(CB_GATTN), row,
                  pl.BlockSpec((tm, PLE), lambda i: (i, 0)), row, _weight_spec((1, D))]
        + [_weight_spec((D, D))] * 5 + [_weight_spec((PLE, D))],
        out_specs=[row] * (n_bf + 1) + [pl.BlockSpec((8, 128), lambda i: (0, 0)), pl.BlockSpec((8, D), lambda i: (0, 0))],
        scratch_shapes=[pltpu.VMEM((8, D), F32)],
        compiler_params=_cparams("arbitrary"),
    )(cs, o, z2, z2, z2, z2, x2, p2, tgt, ln_post, *weights)


def _conv_bwd(z3, c3, dcs3, w_dw, ln_g, ln_b, ex):
    NS, S, _ = z3.shape
    tm, nt, r = _conv_tiles(S)

    def body(*refs):
        ins, ex_in, outs, ex_out, scratch, ex_sems = _host_split(refs, 9, 3, 3, ex)
        val_ref, glu_ref, c_ref, dcs_ref, hc_ref, hdcs_ref, w_ref, g_ref, bb_ref = ins
        dz_ref, gw_ref, gvec_ref = outs
        dsh, ubuf, dubuf = scratch
        i = pl.program_id(1)
        first = (pl.program_id(0) == 0) & (i == 0)
        ex.carried(ex_in, ex_out, ex_sems, first, (pl.program_id(0) == NS - 1) & (i == nt - 1))

        @pl.when(first)
        def _():
            gw_ref[...] = jnp.zeros_like(gw_ref)
            gvec_ref[...] = jnp.zeros_like(gvec_ref)

        val = val_ref[...].astype(F32)
        sg = _sig(glu_ref[...].astype(F32))
        ubuf[...] = val * sg

        def ln_bwd(cv, dcs):
            cv = cv.astype(F32)
            mu = jnp.mean(cv, axis=-1, keepdims=True)
            xc = cv - mu
            rstd = lax.rsqrt(jnp.mean(xc * xc, axis=-1, keepdims=True) + EPS)
            xhat = xc * rstd
            cl = xhat * g_ref[...] + bb_ref[...]
            s = _sig(cl)
            dcl = dcs.astype(F32) * _dsilu(cl, s)
            dxh = dcl * g_ref[...]
            dc = rstd * (dxh - jnp.mean(dxh, axis=-1, keepdims=True) - xhat * jnp.mean(dxh * xhat, axis=-1, keepdims=True))
            return dc, dcl, xhat

        dc, dcl, xhat = ln_bwd(c_ref[...], dcs_ref[...])
        dsh[0, 0:tm, :] = dc
        dch, _, _ = ln_bwd(hc_ref[...], hdcs_ref[...])
        dsh[0, tm:, :] = jnp.where(i < nt - 1, dch, 0.0)
        gvec_ref[0:8, :] += _rowsum8(dcl * xhat)
        gvec_ref[8:16, :] += _rowsum8(dcl)
        gvec_ref[16:24, :] += _rowsum8(dc)
        _fill_shifted(dsh, tm + CONV_HALO - 8)

        def dc_ahead(r0, k):
            a, b = divmod(CONV_K - 1 - k, 8)
            return dsh[b, r0 + 8 * a : r0 + 8 * a + CONV_ROWS, :]

        for r0 in range(0, tm, CONV_ROWS):
            acc = jnp.zeros((CONV_ROWS, D), F32)
            for k in range(CONV_K):
                acc = acc + w_ref[k : k + 1, :] * dc_ahead(r0, k)
            dubuf[r0 : r0 + CONV_ROWS, :] = acc
        for r0 in range(0, tm, CONV_ROWS):
            ur = ubuf[r0 : r0 + CONV_ROWS, :]
            for k in range(CONV_K):
                gw_ref[8 * k : 8 * k + 8, :] += _rowsum8(ur * dc_ahead(r0, k))
        du = dubuf[...]
        dz_ref[:, 0:D] = (du * sg).astype(BF16)
        dz_ref[:, D:] = (du * val * sg * (1.0 - sg)).astype(BF16)

    def cur(cb):
        return pl.BlockSpec((None, tm, D), lambda s, i: (s, i, cb))

    nxt = pl.BlockSpec((None, CONV_HALO, D), lambda s, i: (s, jnp.minimum((i + 1) * r, S // CONV_HALO - 1), 0))
    vec = pl.BlockSpec((1, D), lambda s, i: (0, 0))
    return pl.pallas_call(
        body, name="conv_bwd", grid=(NS, nt),
        out_shape=[jax.ShapeDtypeStruct((NS, S, 2 * D), BF16), jax.ShapeDtypeStruct((CONV_HALO * 8, D), F32),
                   jax.ShapeDtypeStruct((24, D), F32)] + ex.out_shape,
        in_specs=[cur(CB_VAL), cur(CB_GLU), cur(0), cur(0), nxt, nxt,
                  pl.BlockSpec((CONV_HALO, D), lambda s, i: (0, 0)), vec, vec] + ex.specs,
        out_specs=[pl.BlockSpec((None, tm, 2 * D), lambda s, i: (s, i, 0)),
                   pl.BlockSpec((CONV_HALO * 8, D), lambda s, i: (0, 0)), pl.BlockSpec((24, D), lambda s, i: (0, 0))] + ex.specs,
        scratch_shapes=[pltpu.VMEM((8, tm + CONV_HALO, D), F32), pltpu.VMEM((tm, D), F32), pltpu.VMEM((tm, D), F32)] + ex.scratch,
        compiler_params=_cparams("arbitrary", "arbitrary"),
    )(z3, z3, c3, dcs3, c3, dcs3, w_dw, ln_g, ln_b, *ex.arrs)


def _attn_bwd(qr3, kv3, do3, o3, lse3, sinks, cos3, sa3, sb3):
    NS, S, _ = qr3.shape

    def body(sink_ref, q_ref, kvc_ref, kvp_ref, do_ref, o_ref, lse_ref, cos_ref, sa_ref, sb_ref,
             dq_ref, dkc_ref, dkp_ref, dvc_ref, dvp_ref, dsk_ref):
        has_prev = pl.program_id(1) > 0
        mask = _attn_mask(has_prev)
        tabs = cos_ref[...], sa_ref[...], sb_ref[...]
        low_q = lax.broadcasted_iota(jnp.int32, (QROWS, 128), 1) < HEAD_DIM
        dk_g, dv_g = [], []
        for g in range(N_KV):
            qs = _stack_pairs(q_ref, g)
            dos = _stack_pairs(do_ref, g)
            prod = dos.astype(F32) * _stack_pairs(o_ref, g).astype(F32)
            deltas = [jnp.sum(jnp.where(low_q, prod, 0.0), axis=-1, keepdims=True),
                      jnp.sum(jnp.where(low_q, 0.0, prod), axis=-1, keepdims=True)]
            dq_acc, dk_par, dv_par = None, [], []
            for par in range(2):
                k2 = _kv2_block(kvc_ref, kvp_ref, has_prev, _kv2_col(0, g, par))
                v2 = _kv2_block(kvc_ref, kvp_ref, has_prev, _kv2_col(1, g, par))
                lse = _head_col_load(lse_ref, g, par)
                p = jnp.exp(jnp.where(mask, _dot_nt(qs, k2), NEG_BIG) - lse)
                ds = (p * (_dot_nt(dos, v2) - deltas[par])).astype(BF16)
                dq = _dot(ds, k2)
                dq_acc = dq if dq_acc is None else dq_acc + dq
                dk_par.append(_dot_tn(ds, qs))
                dv_par.append(_dot_tn(p.astype(BF16), dos))
                _head_col_store(dsk_ref, g, par, -jnp.exp(_sink_col(sink_ref, g, par) - lse) * deltas[par])
            for j in range(PAIRS):
                dq_pair = _rope(dq_acc[j * BLK : (j + 1) * BLK], *tabs, sign=-1.0) * (HEAD_DIM ** -0.5)
                dq_ref[:, (PAIRS * g + j) * 128 : (PAIRS * g + j + 1) * 128] = dq_pair.astype(BF16)
            dk_g.append(dk_par[0] + pltpu.roll(dk_par[1], HEAD_DIM, 1))
            dv_g.append(dv_par[0] + pltpu.roll(dv_par[1], HEAD_DIM, 1))
        low_k = lax.broadcasted_iota(jnp.int32, (2 * BLK, KV_W), 1) < HEAD_DIM
        dk = jnp.where(low_k, dk_g[0], pltpu.roll(dk_g[1], HEAD_DIM, 1))
        dv = jnp.where(low_k, dv_g[0], pltpu.roll(dv_g[1], HEAD_DIM, 1))
        dkp_ref[...] = dk[0:BLK]
        dkc_ref[...] = dk[BLK:]
        dvp_ref[...] = dv[0:BLK]
        dvc_ref[...] = dv[BLK:]

    qspec = pl.BlockSpec((None, BLK, D), lambda s, n: (s, n, 0))
    kvspec = pl.BlockSpec((None, BLK, KV_W), lambda s, n: (s, n, 0))
    hspec = pl.BlockSpec((None, BLK, N_HEADS), lambda s, n: (s, n, 0))
    kv = jax.ShapeDtypeStruct((NS, S, KV_W), F32)
    return pl.pallas_call(
        body, name="attn_bwd", grid=(NS, S // BLK),
        out_shape=[jax.ShapeDtypeStruct((NS, S, D), BF16), kv, kv, kv, kv, jax.ShapeDtypeStruct((NS, S, N_HEADS), F32)],
        in_specs=list(_attn_specs()) + [qspec, qspec, hspec, kvspec, kvspec, kvspec],
        out_specs=[qspec, kvspec, kvspec, kvspec, kvspec, hspec],
        compiler_params=_cparams("parallel", "parallel"),
    )(sinks, qr3, kv3, kv3, do3, o3, lse3, cos3, sa3, sb3)


def _attn_post(dkc3, dkp3, dvc3, dvp3, cos3, sa3, sb3):
    NS, S, _ = dkc3.shape
    tm = min(16 * BLK, S)
    nt = S // tm

    def body(dkc_ref, dkp_ref, dkn_ref, dvc_ref, dvp_ref, dvn_ref, cos_ref, sa_ref, sb_ref, dkv_ref):
        has_next = pl.program_id(1) < nt - 1

        def join(cur_ref, prev_ref, next_ref):
            ahead = jnp.where(has_next, next_ref[...], 0.0)
            shifted = ahead if tm == BLK else jnp.concatenate([prev_ref[BLK:, :], ahead], axis=0)
            return cur_ref[...] + shifted

        tabs = cos_ref[...], sa_ref[...], sb_ref[...]
        dkv_ref[:, 0:KV_W] = _rope(join(dkc_ref, dkp_ref, dkn_ref), *tabs, sign=-1.0).astype(BF16)
        dkv_ref[:, KV_W:] = join(dvc_ref, dvp_ref, dvn_ref).astype(BF16)

    cur = pl.BlockSpec((None, tm, KV_W), lambda s, j: (s, j, 0))
    nxt = pl.BlockSpec((None, BLK, KV_W), lambda s, j: (s, jnp.minimum((j + 1) * (tm // BLK), S // BLK - 1), 0))
    return pl.pallas_call(
        body, name="attn_post", grid=(NS, nt),
        out_shape=jax.ShapeDtypeStruct((NS, S, 2 * KV_W), BF16),
        in_specs=[cur, cur, nxt, cur, cur, nxt, cur, cur, cur],
        out_specs=pl.BlockSpec((None, tm, 2 * KV_W), lambda s, j: (s, j, 0)),
        compiler_params=_cparams("parallel", "parallel"),
    )(dkc3, dkp3, dkp3, dvc3, dvp3, dvp3, cos3, sa3, sb3)


def _in_bwd(segs, w_in_t, x2, dx1, ln_pre, after):
    T = x2.shape[0]
    tm = min(ROW_TILE, T)
    nt = T // tm
    ns = len(segs)
    widths = [s.shape[1] for s in segs]

    def body(*refs):
        seg_refs = refs[:ns]
        w_ref, x_ref, dx1_ref, g_ref, _, gx_ref, glp_ref = refs[ns:]
        i = pl.program_id(0)

        @pl.when(i == 0)
        def _():
            glp_ref[...] = jnp.zeros_like(glp_ref)

        dh = None
        zc = 0
        for sref, w in zip(seg_refs, widths):
            part = _dot(sref[...], w_ref[_orig_col(zc) : _orig_col(zc) + w, :])
            dh = part if dh is None else dh + part
            zc += w
        xv = x_ref[...]
        r1 = lax.rsqrt(jnp.mean(xv * xv, axis=-1, keepdims=True) + EPS)
        xhat = xv * r1
        glp_ref[...] += _rowsum8(dh * xhat)
        dhg = dh * g_ref[...]
        gx_ref[...] = dx1_ref[...] + r1 * (dhg - xhat * jnp.mean(dhg * xhat, axis=-1, keepdims=True))

    row = pl.BlockSpec((tm, D), lambda i: (i, 0))
    return pl.pallas_call(
        body, name="in_bwd", grid=(nt,),
        out_shape=[jax.ShapeDtypeStruct((T, D), F32), jax.ShapeDtypeStruct((8, D), F32)],
        in_specs=[pl.BlockSpec((tm, w), lambda i: (i, 0)) for w in widths]
        + [_weight_spec((NW, D)), row, row, _weight_spec((1, D)), pl.BlockSpec(memory_space=pl.ANY)],
        out_specs=[row, pl.BlockSpec((8, D), lambda i: (0, 0))],
        compiler_params=_cparams("arbitrary"),
    )(*segs, w_in_t, x2, dx1, ln_pre, after)


def _grad_matmul(a, b, name, rows=None, into=None):
    T, M = a.shape
    N = b.shape[1]
    tk = min(2048, T)
    nk = T // tk

    def body(a_ref, b_ref, *rest):
        o_ref, acc = rest[-2:]
        k = pl.program_id(0)

        @pl.when(k == 0)
        def _():
            acc[...] = jnp.zeros_like(acc)

        acc[...] += _dot_tn(a_ref[...].astype(BF16), b_ref[...])

        @pl.when(k == nk - 1)
        def _():
            o_ref[...] = acc[...].astype(BF16)

    in_specs = [pl.BlockSpec((tk, M), lambda k: (k, 0)), pl.BlockSpec((tk, N), lambda k: (k, 0))]
    if rows is None:
        out_shape, out_spec = (M, N), pl.BlockSpec((M, N), lambda k: (0, 0))
    else:
        out_shape, out_spec = (rows[1], N), pl.BlockSpec((pl.Element(M), pl.Element(N)), lambda k: (rows[0], 0))
    operands = (a, b) if into is None else (a, b, into)
    return pl.pallas_call(
        body, name=name, grid=(nk,), out_shape=jax.ShapeDtypeStruct(out_shape, BF16),
        in_specs=in_specs if into is None else in_specs + [pl.BlockSpec(memory_space=pl.ANY)],
        out_specs=out_spec, input_output_aliases={} if into is None else {2: 0},
        scratch_shapes=[pltpu.VMEM((M, N), F32)],
        compiler_params=_cparams("arbitrary"),
    )(*operands)


def _pack_small(gw, gvec, glp_pre, glp_post, dsk):
    T = dsk.shape[0]

    def body(gw_ref, gvec_ref, pre_ref, post_ref, dsk_ref, gdw_ref, gs_ref):
        gwf = gw_ref[...].reshape(CONV_HALO, 8, D).sum(axis=1)
        for d in range(N_DEV):
            gdw_ref[d] = gwf[:, 128 * d : 128 * (d + 1)]
        gs_ref[...] = jnp.zeros_like(gs_ref)
        gs_ref[0:1, :] = jnp.sum(pre_ref[...], axis=0, keepdims=True)
        gs_ref[1:2, :] = jnp.sum(post_ref[...], axis=0, keepdims=True)
        gs_ref[2:3, :] = jnp.sum(gvec_ref[16:24, :], axis=0, keepdims=True)
        gs_ref[3:4, :] = jnp.sum(gvec_ref[0:8, :], axis=0, keepdims=True)
        gs_ref[4:5, :] = jnp.sum(gvec_ref[8:16, :], axis=0, keepdims=True)
        gs_ref[5:6, 0:N_HEADS] = jnp.sum(dsk_ref[...], axis=0, keepdims=True)

    return pl.pallas_call(
        body, name="pack_small",
        out_shape=[jax.ShapeDtypeStruct((N_DEV, CONV_HALO, 128), F32), jax.ShapeDtypeStruct((8, D), F32)],
        compiler_params=_cparams(),
    )(gw, gvec, glp_pre, glp_post, dsk)


def _adam_update(g, w_ref, m_ref, v_ref, g_ref, d_ref, nm_ref, nv_ref):
    nm = ADAM_B1 * m_ref[...] + (1.0 - ADAM_B1) * g
    nv = ADAM_B2 * v_ref[...] + (1.0 - ADAM_B2) * (g * g)
    m_hat = nm / (1.0 - ADAM_B1 ** ADAM_STEP)
    v_hat = nv / (1.0 - ADAM_B2 ** ADAM_STEP)
    g_ref[...] = g
    d_ref[...] = -ADAM_LR * (m_hat / (jnp.sqrt(v_hat) + ADAM_EPS) + ADAM_WD * w_ref[...])
    nm_ref[...] = nm
    nv_ref[...] = nv


def _adamw(parts, w, m, v, name):
    R, C = w.shape
    tr = R if R <= 256 else 128

    def body(p_ref, *rest):
        g = p_ref[0].astype(F32)
        for s in range(1, N_DEV):
            g = g + p_ref[s].astype(F32)
        _adam_update(g, *rest)

    blk = pl.BlockSpec((tr, C), lambda i: (i, 0))
    return pl.pallas_call(
        body, name=name, grid=(R // tr,), out_shape=[jax.ShapeDtypeStruct((R, C), F32)] * 4,
        in_specs=[pl.BlockSpec((N_DEV, tr, C), lambda i: (0, i, 0)), blk, blk, blk], out_specs=[blk] * 4,
        compiler_params=_cparams("parallel"),
    )(parts, w, m, v)


def _adamw_many(groups, name):
    n = len(groups)
    R, C = groups[0][1].shape
    tr = 32

    def body(*refs):
        ins, outs = refs[: 4 * n], refs[4 * n :]
        for j in range(n):
            p_ref = ins[4 * j]
            g = p_ref[0].astype(F32)
            for s in range(1, N_DEV):
                g = g + p_ref[s].astype(F32)
            _adam_update(g, *ins[4 * j + 1 : 4 * j + 4], *outs[4 * j : 4 * j + 4])

    blk = pl.BlockSpec((tr, C), lambda i: (i, 0))
    res = pl.pallas_call(
        body, name=name, grid=(R // tr,), out_shape=[jax.ShapeDtypeStruct((R, C), F32)] * (4 * n),
        in_specs=[pl.BlockSpec((N_DEV, tr, C), lambda i: (0, i, 0)), blk, blk, blk] * n, out_specs=[blk] * (4 * n),
        compiler_params=_cparams("parallel"),
    )(*[a for grp in groups for a in grp])
    return [res[4 * j : 4 * j + 4] for j in range(n)]


def _adamw_own(parts, own, me, w, m, v, name):
    R, C = w.shape
    tr = max(t for t in range(16, 513, 16) if R % t == 0)

    def body(me_ref, p_ref, own_ref, *rest):
        g = None
        for s in range(N_DEV):
            part = jnp.where(me_ref[0] == s, own_ref[...], p_ref[s]).astype(F32)
            g = part if g is None else g + part
        _adam_update(g, *rest)

    blk = pl.BlockSpec((tr, C), lambda i, me: (i, 0))
    return pl.pallas_call(
        body, name=name, out_shape=[jax.ShapeDtypeStruct((R, C), F32)] * 4,
        grid_spec=pltpu.PrefetchScalarGridSpec(
            num_scalar_prefetch=1, grid=(R // tr,),
            in_specs=[pl.BlockSpec((N_DEV, tr, C), lambda i, me: (0, i, 0)),
                      pl.BlockSpec((None, tr, C), lambda i, me: (me[0], i, 0)), blk, blk, blk],
            out_specs=[blk] * 4),
        compiler_params=_cparams("parallel"),
    )(me, parts, own, w, m, v)


def kernel(x, p, positions, w_in, ln_pre, ln_post, w_dw, b_dw, conv_ln_g, conv_ln_b, w_pw, sinks, w_br_conv, w_br_attn, w_out, w_ple_gate, w_ple_proj, loss_target, m_w_in, m_ln_pre, m_ln_post, m_w_dw, m_b_dw, m_conv_ln_g, m_conv_ln_b, m_w_pw, m_sinks, m_w_br_conv, m_w_br_attn, m_w_out, m_w_ple_gate, m_w_ple_proj, v_w_in, v_ln_pre, v_ln_post, v_w_dw, v_b_dw, v_conv_ln_g, v_conv_ln_b, v_w_pw, v_sinks, v_w_br_conv, v_w_br_attn, v_w_out, v_w_ple_gate, v_w_ple_proj):
    NS, S, _ = x.shape
    T = NS * S
    x2 = x.reshape(T, D)
    p2 = p.reshape(T, PLE)
    tgt = loss_target.reshape(T, D)
    pos = positions.reshape(T, 1)

    row_sharded = [w_pw[0], w_br_conv[0], w_br_attn[0], w_out[0], w_ple_gate[0]]
    sh_rows = D // N_DEV
    w_t, m_t, v_t = (jnp.swapaxes(a[0], 0, 1) for a in (w_in, m_w_in, v_w_in))
    cos, sa, sb, g_in = _rope_tables(pos, _TwoLevelGather([w_t.astype(BF16)]))
    w_in_f = g_in.reshape(NW, D)
    gather_rest = _Exchange([], [w.astype(BF16) for w in row_sharded] + [
        w_ple_proj[0].astype(BF16), jnp.pad(w_dw[0], ((0, CONV_HALO - CONV_K), (0, 0)))])

    in_out = _in_proj(x2, ln_pre, w_in_f, cos, sa, sb, gather_rest)
    z2, h, qr, kv2 = in_out[:4]
    g_rows, g_pp, g_dw = in_out[4:9], in_out[9], in_out[10]
    full = [g.reshape(D, D) for g in g_rows]
    w_pp_f = g_pp.transpose(1, 0, 2).reshape(PLE, D)
    w_dw_f = g_dw.transpose(1, 0, 2).reshape(CONV_HALO, D)
    z3 = z2.reshape(NS, S, NW)
    c3, cs3 = _conv_fwd(z3, w_dw_f, b_dw, conv_ln_g, conv_ln_b)
    qr3, kv3 = qr.reshape(NS, S, D), kv2.reshape(NS, S, 4 * N_KV * KV_W)
    sinks1 = sinks.reshape(N_HEADS)
    o3, lse3 = _attn_fwd(qr3, kv3, sinks1)
    o = o3.reshape(T, D)
    cs = cs3.reshape(T, D)
    (dya0, ya1, dya, yb0, dyb, m, dmo, x1, dgl, dpp, dcs, do, dcg, dag, dgc, dga, dx1, loss_blk, glp_post) = _mid(
        cs, o, z2, x2, p2, tgt, ln_post, full + [w_pp_f])

    gp_rows = [_grad_matmul(a, b, nm).reshape(N_DEV, sh_rows, D) for a, b, nm in (
        (cs, dya0, "grad_w_pw"), (ya1, dya, "grad_w_br_conv"), (yb0, dyb, "grad_w_br_attn"),
        (m, dmo, "grad_w_out"), (x1, dgl, "grad_w_ple_gate"))]
    gp_pp = _grad_matmul(p2, dpp, "grad_w_ple_proj").reshape(PLE, N_DEV, D // N_DEV).transpose(1, 0, 2)
    conv_out = _conv_bwd(z3, c3, dcs.reshape(NS, S, D), w_dw_f, conv_ln_g, conv_ln_b,
                         _Exchange(gp_rows + [gp_pp], [loss_blk]))
    dzvu3, gw, gvec, r_rows, r_pp, r_loss = conv_out[0], conv_out[1], conv_out[2], conv_out[3:8], conv_out[8], conv_out[9]
    loss = jnp.sum(r_loss[:, 0, 0])
    tab3 = [t.reshape(NS, S, 128) for t in (cos, sa, sb)]
    dq3, dkc3, dkp3, dvc3, dvp3, dsk3 = _attn_bwd(qr3, kv3, do.reshape(NS, S, D), o3, lse3, sinks1, *tab3)
    dkv3 = _attn_post(dkc3, dkp3, dvc3, dvp3, *tab3)
    segs = [dzvu3.reshape(T, 2 * D), dcg, dq3.reshape(T, D), dag, dgc, dga, dkv3.reshape(T, 2 * KV_W)]
    gp_in, zc = None, 0
    for j, s in enumerate(segs):
        gp_in = _grad_matmul(s, h, f"grad_w_in_{j}", rows=(_orig_col(zc), NW), into=gp_in)
        zc += s.shape[1]
    gp_in = gp_in.reshape(N_DEV, NW // N_DEV, D)
    in_send, in_recv, in_own, in_land, sent = _scatter_send(gp_in, "w_in_grad_send")
    grad_x2, glp_pre = _in_bwd(segs, w_in_f, x2, dx1, ln_pre, sent)
    gp_dw, gp_small = _pack_small(gw, gvec, glp_pre, glp_post, dsk3.reshape(T, N_HEADS))
    r_dw, r_small = _Exchange([gp_dw], [gp_small]).alone("small_grad_exchange")

    res = {}
    names_rows = ["w_pw", "w_br_conv", "w_br_attn", "w_out", "w_ple_gate"]
    wmv = {"w_pw": (w_pw, m_w_pw, v_w_pw), "w_br_conv": (w_br_conv, m_w_br_conv, v_w_br_conv),
           "w_br_attn": (w_br_attn, m_w_br_attn, v_w_br_attn), "w_out": (w_out, m_w_out, v_w_out),
           "w_ple_gate": (w_ple_gate, m_w_ple_gate, v_w_ple_gate)}
    rows_res = _adamw_many([(parts, *(a[0] for a in wmv[nm])) for nm, parts in zip(names_rows, r_rows)], "adamw_rows")
    res.update(zip(names_rows, rows_res))
    res["w_ple_proj"] = _adamw(r_pp, w_ple_proj[0], m_w_ple_proj[0], v_w_ple_proj[0], "adamw_w_ple_proj")
    pad_dw = lambda a: jnp.pad(a[0], ((0, CONV_HALO - CONV_K), (0, 0)))
    res["w_dw"] = [a[:CONV_K] for a in _adamw(r_dw, pad_dw(w_dw), pad_dw(m_w_dw), pad_dw(v_w_dw), "adamw_w_dw")]

    def stack_small(a_pre, a_post, a_b, a_g, a_bb, a_s):
        sk = jnp.pad(a_s, ((0, 0), (0, D - N_HEADS)))
        return jnp.concatenate([a_pre, a_post, a_b, a_g, a_bb, sk, jnp.zeros((2, D), F32)], axis=0)

    small = _adamw(
        r_small, stack_small(ln_pre, ln_post, b_dw, conv_ln_g, conv_ln_b, sinks),
        stack_small(m_ln_pre, m_ln_post, m_b_dw, m_conv_ln_g, m_conv_ln_b, m_sinks),
        stack_small(v_ln_pre, v_ln_post, v_b_dw, v_conv_ln_g, v_conv_ln_b, v_sinks), "adamw_small")
    for j, nm in enumerate(["ln_pre", "ln_post", "b_dw", "conv_ln_g", "conv_ln_b"]):
        res[nm] = [a[j] for a in small]
    res["sinks"] = [a[5, :N_HEADS] for a in small]
    in_own, in_land = _scatter_wait(in_send, in_recv, in_own, in_land, small[0], "w_in_grad_wait")
    me = _slot(*_my_place()).astype(jnp.int32).reshape(1)
    res["w_in"] = [jnp.swapaxes(a, 0, 1) for a in _adamw_own(in_land, in_own, me, w_t, m_t, v_t, "adamw_w_in")]

    order = ["w_in", "ln_pre", "ln_post", "w_dw", "b_dw", "conv_ln_g", "conv_ln_b", "w_pw", "sinks", "w_br_conv",
             "w_br_attn", "w_out", "w_ple_gate", "w_ple_proj"]
    outs = [loss, grad_x2.reshape(NS, S, D)]
    for kind in range(4):
        outs += [res[nm][kind][None] for nm in order]
    return tuple(outs)
```
